```python
import math
import jax, jax.numpy as jnp
from jax import lax
import numpy as np

D_MODEL = 1024
BATCH = 8
SEQ = 2048
DEPTH = 1

HEAD_DIM = 128
DN_HEADS = D_MODEL // 256
DN_WIDTH = DN_HEADS * HEAD_DIM
CONV_WIDTH = 4
CHUNK = 64
SWA_GROUPS = ((128, 1), (512, 4), (2048, 16))
N_GROUPS = 3
SWA_HEADS = D_MODEL // 256
SWA_WIDTH = SWA_HEADS * HEAD_DIM
ROPE_DIM = HEAD_DIM // 4
ROPE_THETA = 500000.0
N_BRANCHES = 2
EPS = 1e-6
NEG_INF = -1e30

IN_SIZES = (3 * DN_WIDTH, DN_WIDTH, DN_HEADS, DN_HEADS,
            N_GROUPS * SWA_WIDTH, N_GROUPS * SWA_WIDTH, N_GROUPS * SWA_WIDTH, SWA_WIDTH,
            N_BRANCHES * D_MODEL)
IN_COLS = sum(IN_SIZES)

kernel_name = "hybrid_gated_deltanet_dilated_swa_block"


def rms_norm(x, w):
    xf = x.astype(jnp.float32)
    y = xf * lax.rsqrt(jnp.mean(xf * xf, axis=-1, keepdims=True) + EPS)
    return y.astype(x.dtype) * w.astype(x.dtype)


def l2_normalize(x):
    return x * lax.rsqrt(jnp.sum(x * x, axis=-1, keepdims=True) + EPS)


def causal_depthwise_conv(x, w):
    c = x.shape[-1]
    return lax.conv_general_dilated(x, w.astype(x.dtype), window_strides=(1,), padding=[(CONV_WIDTH - 1, 0)],
                                    dimension_numbers=('NWC', 'WIO', 'NWC'), feature_group_count=c)


def partial_rope(x, cos, sin):
    half = ROPE_DIM // 2
    x1, x2, xp = x[..., :half], x[..., half:ROPE_DIM], x[..., ROPE_DIM:]
    c = cos[:, None, None, :].astype(x.dtype)
    s = sin[:, None, None, :].astype(x.dtype)
    return jnp.concatenate([x1 * c - x2 * s, x2 * c + x1 * s, xp], axis=-1)


def gated_delta_rule_chunked(q, k, v, g, beta):
    B, H, T, Dk = q.shape
    Dv = v.shape[-1]
    N = T // CHUNK
    q = q.reshape(B, H, N, CHUNK, Dk)
    k = k.reshape(B, H, N, CHUNK, Dk)
    v = v.reshape(B, H, N, CHUNK, Dv)
    g = g.reshape(B, H, N, CHUNK)
    beta = beta.reshape(B, H, N, CHUNK)
    gc = jnp.cumsum(g, axis=-1)
    idx = jnp.arange(CHUNK)
    causal = idx[:, None] >= idx[None, :]
    strict = idx[:, None] > idx[None, :]
    decay = jnp.exp(jnp.where(causal, gc[..., :, None] - gc[..., None, :], -jnp.inf))
    kb = k * beta[..., None]
    lower = jnp.where(strict, jnp.einsum('bhnid,bhnjd->bhnij', kb, k) * decay, 0.0)
    eye = jnp.eye(CHUNK, dtype=q.dtype)
    tmat = lax.linalg.triangular_solve(eye + lower, jnp.broadcast_to(eye, lower.shape),
                                       left_side=True, lower=True, unit_diagonal=True)
    u = jnp.einsum('bhnij,bhnjd->bhnid', tmat, v * beta[..., None])
    w = jnp.einsum('bhnij,bhnjd->bhnid', tmat, kb * jnp.exp(gc)[..., None])
    a_intra = jnp.where(causal, jnp.einsum('bhnid,bhnjd->bhnij', q, k) * decay, 0.0)

    def step(S, inp):
        qi, ki, ui, wi, ai, gci = inp
        v_new = ui - jnp.einsum('bhck,bhkv->bhcv', wi, S)
        o = jnp.einsum('bhck,bhkv->bhcv', qi * jnp.exp(gci)[..., None], S) + jnp.einsum('bhij,bhjv->bhiv', ai, v_new)
        glast = gci[..., -1]
        S = S * jnp.exp(glast)[..., None, None] + jnp.einsum(
            'bhck,bhcv->bhkv', ki * jnp.exp(glast[..., None] - gci)[..., None], v_new)
        return S, o

    xs = tuple(jnp.moveaxis(a, 2, 0) for a in (q, k, u, w, a_intra, gc))
    s0 = jnp.zeros((B, H, Dk, Dv), q.dtype)
    _, o = lax.scan(step, s0, xs)
    return jnp.moveaxis(o, 0, 2).reshape(B, H, T, Dv)


def dilated_window_attention(q, k, v, window, dilation):
    B, T, H, Dh = q.shape
    L = T // dilation
    blk = window // dilation
    nb = -(-L // blk)
    Lp = nb * blk

    def to_sub(x):
        x = x.astype(jnp.float32).reshape(B, L, dilation, H, Dh).transpose(0, 2, 3, 1, 4)
        return jnp.pad(x, ((0, 0), (0, 0), (0, 0), (0, Lp - L), (0, 0)))

    def windows(x):
        xp = jnp.pad(x, ((0, 0), (0, 0), (0, 0), (blk, 0), (0, 0))).reshape(B, dilation, H, nb + 1, blk, Dh)
        return jnp.concatenate([xp[:, :, :, :-1], xp[:, :, :, 1:]], axis=4)

    qb = to_sub(q).reshape(B, dilation, H, nb, blk, Dh)
    kw = windows(to_sub(k))
    vw = windows(to_sub(v))
    s = jnp.einsum('brhnqd,brhnkd->brhnqk', qb, kw) * (Dh ** -0.5)
    i = jnp.arange(blk)[:, None]
    j = jnp.arange(2 * blk)[None, :]
    dist = blk + i - j
    key_pos = (jnp.arange(nb)[:, None, None] - 1) * blk + j[None]
    valid = (dist >= 0)[None] & (dist <= blk)[None] & (key_pos >= 0)
    s = jnp.where(valid, s, NEG_INF)
    lse = jax.nn.logsumexp(s, axis=-1)
    p = jnp.exp(s - lse[..., None])
    o = jnp.einsum('brhnqk,brhnkd->brhnqd', p, vw)
    o = o.reshape(B, dilation, H, Lp, Dh)[:, :, :, :L].transpose(0, 3, 1, 2, 4).reshape(B, T, H, Dh)
    lse = lse.reshape(B, dilation, H, Lp)[..., :L].transpose(0, 3, 1, 2).reshape(B, T, H)
    return o, lse


def setup_inputs(seed: int = 0) -> dict:
    key = jax.random.key(seed)
    ks = jax.random.split(key, 13)
    f32 = jnp.float32
    x = jax.random.normal(ks[0], (BATCH, SEQ, D_MODEL), f32)
    norm_w = 1.0 + 0.02 * jax.random.normal(ks[1], (DEPTH, D_MODEL), f32)
    w_in = jax.random.normal(ks[2], (DEPTH, D_MODEL, IN_COLS), f32) * D_MODEL ** -0.5
    conv_w = jax.random.normal(ks[3], (DEPTH, CONV_WIDTH, 1, 3 * DN_WIDTH), f32) * CONV_WIDTH ** -0.5
    dn_a_log = jnp.log(jax.random.uniform(ks[4], (DEPTH, DN_HEADS), f32, 1.0, 16.0))
    dn_dt_bias = 0.1 * jax.random.normal(ks[5], (DEPTH, DN_HEADS), f32)
    dn_norm_w = 1.0 + 0.02 * jax.random.normal(ks[6], (DEPTH, HEAD_DIM), f32)
    q_norm_w = 1.0 + 0.02 * jax.random.normal(ks[7], (DEPTH, N_GROUPS, HEAD_DIM), f32)
    k_norm_w = 1.0 + 0.02 * jax.random.normal(ks[8], (DEPTH, N_GROUPS, HEAD_DIM), f32)
    w_branch_dn = jax.random.normal(ks[9], (DEPTH, DN_WIDTH, D_MODEL), f32) * DN_WIDTH ** -0.5
    w_branch_swa = jax.random.normal(ks[10], (DEPTH, SWA_WIDTH, D_MODEL), f32) * SWA_WIDTH ** -0.5
    w_out = jax.random.normal(ks[11], (DEPTH, D_MODEL, D_MODEL), f32) * D_MODEL ** -0.5
    return {"x": x, "norm_w": norm_w, "w_in": w_in, "conv_w": conv_w, "dn_a_log": dn_a_log,
            "dn_dt_bias": dn_dt_bias, "dn_norm_w": dn_norm_w, "q_norm_w": q_norm_w, "k_norm_w": k_norm_w,
            "w_branch_dn": w_branch_dn, "w_branch_swa": w_branch_swa, "w_out": w_out}


def reference(x, norm_w, w_in, conv_w, dn_a_log, dn_dt_bias, dn_norm_w, q_norm_w, k_norm_w,
              w_branch_dn, w_branch_swa, w_out):
    B, T, _ = x.shape
    split_points = [int(p) for p in np.cumsum(IN_SIZES)[:-1]]
    pos = jnp.arange(T, dtype=jnp.float32)
    inv_freq = ROPE_THETA ** (-jnp.arange(0, ROPE_DIM, 2, dtype=jnp.float32) / ROPE_DIM)
    ang = pos[:, None] * inv_freq[None, :]
    cos, sin = jnp.cos(ang), jnp.sin(ang)

    for layer in range(DEPTH):
        h = rms_norm(x, norm_w[layer])
        proj = h @ w_in[layer]
        (dn_qkv, dn_z, dn_b, dn_a, swa_q, swa_k, swa_v, swa_z, gates) = jnp.split(proj, split_points, axis=-1)

        qkv = jax.nn.silu(causal_depthwise_conv(dn_qkv, conv_w[layer]))
        dq, dk, dv = jnp.split(qkv.astype(jnp.float32), 3, axis=-1)
        to_heads = lambda t: t.reshape(B, T, DN_HEADS, HEAD_DIM).transpose(0, 2, 1, 3)
        dq = l2_normalize(to_heads(dq)) * (HEAD_DIM ** -0.5)
        dk = l2_normalize(to_heads(dk))
        dv = to_heads(dv)
        beta = jax.nn.sigmoid(dn_b.astype(jnp.float32)).transpose(0, 2, 1)
        g = (-jnp.exp(dn_a_log[layer].astype(jnp.float32))
             * jax.nn.softplus(dn_a.astype(jnp.float32) + dn_dt_bias[layer].astype(jnp.float32))).transpose(0, 2, 1)
        o_dn = gated_delta_rule_chunked(dq, dk, dv, g, beta).transpose(0, 2, 1, 3).astype(x.dtype)
        o_dn = rms_norm(o_dn, dn_norm_w[layer]) * jax.nn.silu(dn_z.reshape(B, T, DN_HEADS, HEAD_DIM))
        y_dn = o_dn.reshape(B, T, DN_WIDTH) @ w_branch_dn[layer]

        grp = lambda t: t.reshape(B, T, N_GROUPS, SWA_HEADS, HEAD_DIM)
        sq = partial_rope(rms_norm(grp(swa_q), q_norm_w[layer][:, None, :]), cos, sin)
        sk = partial_rope(rms_norm(grp(swa_k), k_norm_w[layer][:, None, :]), cos, sin)
        sv = grp(swa_v)
        outs, lses = [], []
        for gi, (window, dilation) in enumerate(SWA_GROUPS):
            o_g, lse_g = dilated_window_attention(sq[:, :, gi], sk[:, :, gi], sv[:, :, gi], window, dilation)
            outs.append(o_g)
            lses.append(lse_g)
        alpha = jax.nn.softmax(jnp.stack(lses, axis=0), axis=0)
        o_swa = jnp.sum(alpha[..., None] * jnp.stack(outs, axis=0), axis=0).astype(x.dtype)
        o_swa = o_swa * jax.nn.silu(swa_z.reshape(B, T, SWA_HEADS, HEAD_DIM))
        y_swa = o_swa.reshape(B, T, SWA_WIDTH) @ w_branch_swa[layer]

        g_dn, g_swa = jnp.split(gates, 2, axis=-1)
        merged = jax.nn.sigmoid(g_dn) * y_dn + jax.nn.sigmoid(g_swa) * y_swa
        x = x + merged @ w_out[layer]
    return x
```

```python
import functools
import math

import jax
import jax.numpy as jnp
import numpy as np
from jax import lax
from jax.experimental import pallas as pl
from jax.experimental.pallas import tpu as pltpu

D_MODEL = 1024
HEAD_DIM = 128
DN_HEADS = 4
DN_WIDTH = DN_HEADS * HEAD_DIM
CONV_WIDTH = 4
CHUNK = 64
SWA_GROUPS = ((128, 1), (512, 4), (2048, 16))
N_GROUPS = 3
SWA_HEADS = 4
SWA_WIDTH = SWA_HEADS * HEAD_DIM
SWA_BLOCK = 128
ROPE_DIM = HEAD_DIM // 4
ROPE_HALF = ROPE_DIM // 2
ROPE_THETA = 500000.0
EPS = 1e-6
NEG_INF = -1e30

LANES = 128
VMEM_LIMIT_BYTES = 48 * 1024 * 1024
ROW_TILE = 512

F32 = jnp.float32
BF16 = jnp.bfloat16
_NT = (((1,), (1,)), ((), ()))
_TN = (((0,), (0,)), ((), ()))


def _sigmoid(x):
    return 1.0 / (1.0 + jnp.exp(-x))


def _silu(x):
    return x * _sigmoid(x)


def _softplus(x):
    return jnp.maximum(x, 0.0) + jnp.log(1.0 + jnp.exp(-jnp.abs(x)))


def _params(n_axes):
    return pltpu.CompilerParams(dimension_semantics=("parallel",) * n_axes,
                                vmem_limit_bytes=VMEM_LIMIT_BYTES)


def _norm_proj_kernel(x_ref, nw_ref, wa_ref, wab_ref, alog_ref, dt_ref, h_ref, a_ref, bg_ref):
    x = x_ref[...]
    h = (x * lax.rsqrt(jnp.mean(x * x, axis=-1, keepdims=True) + EPS)) * nw_ref[...]
    hb = h.astype(BF16)
    h_ref[...] = hb
    a_ref[...] = jnp.dot(hb, wa_ref[...], preferred_element_type=F32).astype(BF16)
    ab = jnp.dot(hb, wab_ref[...], preferred_element_type=F32)
    lane = lax.broadcasted_iota(jnp.int32, ab.shape, 1)
    beta = _sigmoid(ab)
    g = -jnp.exp(alog_ref[...]) * _softplus(ab + dt_ref[...])
    bg_ref[...] = jnp.where(lane < DN_HEADS, beta, g)


def _norm_proj(x2, norm_w, w_a, w_ab, alog_row, dt_row):
    n = x2.shape[0]
    na = w_a.shape[1]
    row = lambda i: (i, 0)
    fixed = lambda i: (0, 0)
    return pl.pallas_call(
        _norm_proj_kernel,
        grid=(n // ROW_TILE,),
        in_specs=[pl.BlockSpec((ROW_TILE, D_MODEL), row),
                  pl.BlockSpec((1, D_MODEL), fixed),
                  pl.BlockSpec((D_MODEL, na), fixed),
                  pl.BlockSpec((D_MODEL, LANES), fixed),
                  pl.BlockSpec((1, LANES), fixed),
                  pl.BlockSpec((1, LANES), fixed)],
        out_specs=[pl.BlockSpec((ROW_TILE, D_MODEL), row),
                   pl.BlockSpec((ROW_TILE, na), row),
                   pl.BlockSpec((ROW_TILE, LANES), row)],
        out_shape=[jax.ShapeDtypeStruct((n, D_MODEL), BF16),
                   jax.ShapeDtypeStruct((n, na), BF16),
                   jax.ShapeDtypeStruct((n, LANES), F32)],
        compiler_params=_params(1),
        name="norm_proj",
    )(x2, norm_w, w_a, w_ab, alog_row, dt_row)


def _matmul_kernel(h_ref, w_ref, o_ref):
    o_ref[...] = jnp.dot(h_ref[...], w_ref[...], preferred_element_type=F32).astype(o_ref.dtype)


def _matmul(h, w, name):
    n, k = h.shape
    nc = w.shape[1]
    return pl.pallas_call(
        _matmul_kernel,
        grid=(n // ROW_TILE,),
        in_specs=[pl.BlockSpec((ROW_TILE, k), lambda i: (i, 0)),
                  pl.BlockSpec((k, nc), lambda i: (0, 0))],
        out_specs=pl.BlockSpec((ROW_TILE, nc), lambda i: (i, 0)),
        out_shape=jax.ShapeDtypeStruct((n, nc), BF16),
        compiler_params=_params(1),
        name=name,
    )(h, w)


def _deltanet_kernel(a_ref, bg_ref, cw_ref, nw_ref, o_ref, s_ref):
    t_len = a_ref.shape[1]
    ii = lax.broadcasted_iota(jnp.int32, (CHUNK, CHUNK), 0)
    jj = lax.broadcasted_iota(jnp.int32, (CHUNK, CHUNK), 1)
    causal = ii >= jj
    strict = ii > jj
    eye = ii == jj
    ltri = causal.astype(F32)
    eye_f = eye.astype(F32)
    ones = jnp.ones((CHUNK, CHUNK), F32)
    s_ref[...] = jnp.zeros(s_ref.shape, F32)
    halo = 16

    def chunk_step(c, carry):
        r0 = pl.multiple_of(c * CHUNK, CHUNK)
        rp = pl.multiple_of(jnp.maximum(r0 - halo, 0), halo)
        cur = a_ref[0, pl.ds(r0, CHUNK), 0:3 * DN_WIDTH].astype(F32)
        prev = a_ref[0, pl.ds(rp, halo), 0:3 * DN_WIDTH].astype(F32)
        prev = jnp.where(c > 0, prev, 0.0)
        xw = jnp.concatenate([prev, cur], axis=0)
        conv = None
        for j in range(CONV_WIDTH):
            lo = halo - (CONV_WIDTH - 1) + j
            term = cw_ref[j:j + 1, :] * xw[lo:lo + CHUNK, :]
            conv = term if conv is None else conv + term
        act = _silu(conv)
        bg = bg_ref[0, pl.ds(r0, CHUNK), :]
        gc_all = jnp.dot(ltri, bg, preferred_element_type=F32, precision=lax.Precision.HIGHEST)
        z = a_ref[0, pl.ds(r0, CHUNK), 3 * DN_WIDTH:4 * DN_WIDTH].astype(F32)
        for h in range(DN_HEADS):
            hs = slice(h * HEAD_DIM, (h + 1) * HEAD_DIM)
            qh = act[:, h * HEAD_DIM:(h + 1) * HEAD_DIM]
            kh = act[:, DN_WIDTH + h * HEAD_DIM:DN_WIDTH + (h + 1) * HEAD_DIM]
            vh = act[:, 2 * DN_WIDTH + h * HEAD_DIM:2 * DN_WIDTH + (h + 1) * HEAD_DIM]
            qn = qh * lax.rsqrt(jnp.sum(qh * qh, axis=-1, keepdims=True) + EPS) * (HEAD_DIM ** -0.5)
            kn = kh * lax.rsqrt(jnp.sum(kh * kh, axis=-1, keepdims=True) + EPS)
            beta = bg[:, h:h + 1]
            gc = gc_all[:, DN_HEADS + h:DN_HEADS + h + 1]
            gc_row = jnp.dot(ones, jnp.where(eye, gc, 0.0), preferred_element_type=F32,
                             precision=lax.Precision.HIGHEST)
            decay = jnp.where(causal, jnp.exp(jnp.where(causal, gc - gc_row, 0.0)), 0.0)
            kb = kn * beta
            egc = jnp.exp(gc)
            knb = kn.astype(BF16)
            qk = lax.dot_general(jnp.concatenate([qn, kb], axis=0).astype(BF16), knb, _NT,
                                 preferred_element_type=F32)
            a_intra = jnp.where(causal, qk[:CHUNK] * decay, 0.0)
            lower = jnp.where(strict, qk[CHUNK:] * decay, 0.0)
            p = -lower
            tm = eye_f + p
            for _ in range(int(math.log2(CHUNK)) - 1):
                pb = p.astype(BF16)
                p = jnp.dot(pb, pb, preferred_element_type=F32)
                tm = tm + jnp.dot(tm.astype(BF16), p.astype(BF16), preferred_element_type=F32)
            rhs = jnp.concatenate([vh * beta, kb * egc], axis=1).astype(BF16)
            uw = jnp.dot(tm.astype(BF16), rhs, preferred_element_type=F32)
            u = uw[:, :HEAD_DIM]
            w = uw[:, HEAD_DIM:]
            s_old = s_ref[h]
            ws = jnp.dot(jnp.concatenate([w, qn * egc], axis=0).astype(BF16), s_old.astype(BF16),
                         preferred_element_type=F32)
            v_new = u - ws[:CHUNK]
            v_new_b = v_new.astype(BF16)
            o = ws[CHUNK:] + jnp.dot(a_intra.astype(BF16), v_new_b, preferred_element_type=F32)
            g_last = gc[CHUNK - 1:CHUNK, :]
            kd = (kn * jnp.exp(g_last - gc)).astype(BF16)
            s_ref[h] = s_old * jnp.exp(g_last) + lax.dot_general(kd, v_new_b, _TN,
                                                                 preferred_element_type=F32)
            on = o * lax.rsqrt(jnp.mean(o * o, axis=-1, keepdims=True) + EPS) * nw_ref[...]
            o_ref[0, pl.ds(r0, CHUNK), hs] = (on * _silu(z[:, hs])).astype(o_ref.dtype)
        return carry

    lax.fori_loop(0, t_len // CHUNK, chunk_step, 0)


def _deltanet(a3, bg3, conv_w2, dn_norm_row):
    b, t, wa = a3.shape
    return pl.pallas_call(
        _deltanet_kernel,
        grid=(b,),
        in_specs=[pl.BlockSpec((1, t, wa), lambda i: (i, 0, 0)),
                  pl.BlockSpec((1, t, LANES), lambda i: (i, 0, 0)),
                  pl.BlockSpec((CONV_WIDTH, 3 * DN_WIDTH), lambda i: (0, 0)),
                  pl.BlockSpec((1, HEAD_DIM), lambda i: (0, 0))],
        out_specs=pl.BlockSpec((1, t, DN_WIDTH), lambda i: (i, 0, 0)),
        out_shape=jax.ShapeDtypeStruct((b, t, DN_WIDTH), BF16),
        scratch_shapes=[pltpu.VMEM((DN_HEADS, HEAD_DIM, HEAD_DIM), F32)],
        compiler_params=_params(1),
        name="deltanet",
    )(a3, bg3, conv_w2, dn_norm_row)


def _swa_kernel(q_ref, k_ref, v_ref, cos_ref, sa_ref, sb_ref, qw_ref, kw_ref, o_ref, lse_ref, qs_ref, ks_ref):
    seq = q_ref.shape[1]
    nblk = seq // SWA_BLOCK

    def prep(n, carry):
        r0 = pl.multiple_of(n * SWA_BLOCK, SWA_BLOCK)
        rows = pl.ds(r0, SWA_BLOCK)
        cos = cos_ref[rows, :]
        sa = sa_ref[rows, :]
        sb = sb_ref[rows, :]
        for h in range(SWA_HEADS):
            hs = slice(h * HEAD_DIM, (h + 1) * HEAD_DIM)
            for src, w_ref, dst, scale in ((q_ref, qw_ref, qs_ref, HEAD_DIM ** -0.5), (k_ref, kw_ref, ks_ref, 1.0)):
                xh = src[0, rows, hs].astype(F32)
                y = xh * lax.rsqrt(jnp.mean(xh * xh, axis=-1, keepdims=True) + EPS) * w_ref[...]
                y = y * cos + pltpu.roll(y, ROPE_HALF, 1) * sa + pltpu.roll(y, HEAD_DIM - ROPE_HALF, 1) * sb
                dst[rows, hs] = (y * scale).astype(BF16)
        return carry

    lax.fori_loop(0, nblk, prep, 0)

    def attend(r0, with_prev):
        width = 2 * SWA_BLOCK if with_prev else SWA_BLOCK
        k0 = r0 - SWA_BLOCK if with_prev else r0
        krows = pl.ds(k0, width)
        qrows = pl.ds(r0, SWA_BLOCK)
        qi = lax.broadcasted_iota(jnp.int32, (SWA_BLOCK, width), 0)
        kj = lax.broadcasted_iota(jnp.int32, (SWA_BLOCK, width), 1)
        if with_prev:
            valid = (kj >= qi) & (kj <= qi + SWA_BLOCK)
        else:
            valid = kj <= qi
        lane = lax.broadcasted_iota(jnp.int32, (SWA_BLOCK, LANES), 1)
        lse_tile = jnp.zeros((SWA_BLOCK, LANES), F32)
        for h in range(SWA_HEADS):
            hs = slice(h * HEAD_DIM, (h + 1) * HEAD_DIM)
            s = lax.dot_general(qs_ref[qrows, hs], ks_ref[krows, hs], _NT, preferred_element_type=F32)
            s = jnp.where(valid, s, NEG_INF)
            m = jnp.max(s, axis=-1, keepdims=True)
            p = jnp.exp(s - m)
            l = jnp.sum(p, axis=-1, keepdims=True)
            o = jnp.dot(p.astype(BF16), v_ref[0, krows, hs], preferred_element_type=F32)
            o_ref[0, qrows, hs] = (o / l).astype(o_ref.dtype)
            lse_tile = jnp.where(lane == h, m + jnp.log(l), lse_tile)
        lse_ref[0, qrows, :] = lse_tile

    attend(0, False)
    if nblk > 1:
        def body(n, carry):
            attend(pl.multiple_of(n * SWA_BLOCK, SWA_BLOCK), True)
            return carry
        lax.fori_loop(1, nblk, body, 0)


def _swa_group(s3, cos_t, sa_t, sb_t, qw_row, kw_row, dilation):
    b, seq, _ = s3.shape
    blk = lambda which: pl.BlockSpec((1, seq, SWA_WIDTH), lambda r, i: (i, 0, 3 * r + which))
    tab = pl.BlockSpec((seq, LANES), lambda r, i: (0, r))
    wspec = pl.BlockSpec((1, HEAD_DIM), lambda r, i: (0, 0))
    return pl.pallas_call(
        _swa_kernel,
        grid=(dilation, b),
        in_specs=[blk(0), blk(1), blk(2), tab, tab, tab, wspec, wspec],
        out_specs=[pl.BlockSpec((1, seq, SWA_WIDTH), lambda r, i: (i, 0, r)),
                   pl.BlockSpec((1, seq, LANES), lambda r, i: (i, 0, r))],
        out_shape=[jax.ShapeDtypeStruct((b, seq, dilation * SWA_WIDTH), BF16),
                   jax.ShapeDtypeStruct((b, seq, dilation * LANES), F32)],
        scratch_shapes=[pltpu.VMEM((seq, SWA_WIDTH), BF16), pltpu.VMEM((seq, SWA_WIDTH), BF16)],
        compiler_params=_params(2),
        name=f"swa_d{dilation}",
    )(s3, s3, s3, cos_t, sa_t, sb_t, qw_row, kw_row)


def _merge_out_kernel(x_ref, odn_ref, o0_ref, o1_ref, o2_ref, l0_ref, l1_ref, l2_ref, zg_ref,
                      wdn_ref, wswa_ref, wout_ref, out_ref):
    l0 = l0_ref[...]
    l1 = l1_ref[...]
    l2 = l2_ref[...]
    m = jnp.maximum(jnp.maximum(l0, l1), l2)
    e0 = jnp.exp(l0 - m)
    e1 = jnp.exp(l1 - m)
    e2 = jnp.exp(l2 - m)
    inv = 1.0 / (e0 + e1 + e2)
    parts = []
    for h in range(SWA_HEADS):
        hs = slice(h * HEAD_DIM, (h + 1) * HEAD_DIM)
        col = slice(h, h + 1)
        oh = ((e0[:, col] * inv[:, col]) * o0_ref[:, hs].astype(F32)
              + (e1[:, col] * inv[:, col]) * o1_ref[:, hs].astype(F32)
              + (e2[:, col] * inv[:, col]) * o2_ref[:, hs].astype(F32))
        parts.append((oh * _silu(zg_ref[:, hs].astype(F32))).astype(BF16))
    o_swa = jnp.concatenate(parts, axis=1)
    y_swa = jnp.dot(o_swa, wswa_ref[...], preferred_element_type=F32)
    y_dn = jnp.dot(odn_ref[...], wdn_ref[...], preferred_element_type=F32)
    g_dn = zg_ref[:, SWA_WIDTH:SWA_WIDTH + D_MODEL].astype(F32)
    g_swa = zg_ref[:, SWA_WIDTH + D_MODEL:SWA_WIDTH + 2 * D_MODEL].astype(F32)
    merged = _sigmoid(g_dn) * y_dn + _sigmoid(g_swa) * y_swa
    out_ref[...] = x_ref[...] + jnp.dot(merged.astype(BF16), wout_ref[...], preferred_element_type=F32)


def _merge_out(x2, odn, o_list, lse_list, zg, w_dn, w_swa, w_out):
    n = x2.shape[0]
    row = lambda w: pl.BlockSpec((ROW_TILE, w), lambda i: (i, 0))
    full = lambda a: pl.BlockSpec(a.shape, lambda i: (0, 0))
    return pl.pallas_call(
        _merge_out_kernel,
        grid=(n // ROW_TILE,),
        in_specs=[row(D_MODEL), row(DN_WIDTH), row(SWA_WIDTH), row(SWA_WIDTH), row(SWA_WIDTH),
                  row(LANES), row(LANES), row(LANES), row(zg.shape[1]),
                  full(w_dn), full(w_swa), full(w_out)],
        out_specs=row(D_MODEL),
        out_shape=jax.ShapeDtypeStruct((n, D_MODEL), F32),
        compiler_params=_params(1),
        name="merge_out",
    )(x2, odn, *o_list, *lse_list, zg, w_dn, w_swa, w_out)


def _rope_tables(t_len):
    pos = jnp.arange(t_len, dtype=F32)
    inv_freq = ROPE_THETA ** (-jnp.arange(0, ROPE_DIM, 2, dtype=F32) / ROPE_DIM)
    ang = pos[:, None] * inv_freq[None, :]
    cos, sin = jnp.cos(ang), jnp.sin(ang)
    zeros = jnp.zeros_like(sin)
    tail = HEAD_DIM - ROPE_DIM
    cos_t = jnp.concatenate([cos, cos, jnp.ones((t_len, tail), F32)], axis=1)
    sa_t = jnp.concatenate([zeros, sin, jnp.zeros((t_len, tail), F32)], axis=1)
    sb_t = jnp.concatenate([-sin, zeros, jnp.zeros((t_len, tail), F32)], axis=1)
    return cos_t, sa_t, sb_t


def kernel(x, norm_w, w_in, conv_w, dn_a_log, dn_dt_bias, dn_norm_w, q_norm_w, k_norm_w,
           w_branch_dn, w_branch_swa, w_out):
    b, t, d = x.shape
    n = b * t
    layer = 0
    w = w_in[layer]
    c_qkv = 3 * DN_WIDTH
    c_z = c_qkv + DN_WIDTH
    c_ab = c_z + 2 * DN_HEADS
    c_q = c_ab
    c_k = c_q + N_GROUPS * SWA_WIDTH
    c_v = c_k + N_GROUPS * SWA_WIDTH
    c_sz = c_v + N_GROUPS * SWA_WIDTH

    w_a = w[:, :c_z].astype(BF16)
    w_ab = jnp.pad(w[:, c_z:c_ab], ((0, 0), (0, LANES - 2 * DN_HEADS))).astype(BF16)
    w_grp = [jnp.concatenate([w[:, c_q + g * SWA_WIDTH:c_q + (g + 1) * SWA_WIDTH],
                              w[:, c_k + g * SWA_WIDTH:c_k + (g + 1) * SWA_WIDTH],
                              w[:, c_v + g * SWA_WIDTH:c_v + (g + 1) * SWA_WIDTH]], axis=1).astype(BF16)
             for g in range(N_GROUPS)]
    w_zg = w[:, c_sz:].astype(BF16)
    pad_heads = lambda v: jnp.pad(v.astype(F32), (DN_HEADS, LANES - 2 * DN_HEADS))[None, :]
    alog_row = pad_heads(dn_a_log[layer])
    dt_row = pad_heads(dn_dt_bias[layer])

    x2 = x.reshape(n, d)
    h, a, bg = _norm_proj(x2, norm_w[layer][None, :], w_a, w_ab, alog_row, dt_row)

    o_dn = _deltanet(a.reshape(b, t, 4 * DN_WIDTH), bg.reshape(b, t, LANES),
                     conv_w[layer][:, 0, :], dn_norm_w[layer][None, :])

    cos_t, sa_t, sb_t = _rope_tables(t)
    o_list, lse_list = [], []
    for g, (window, dilation) in enumerate(SWA_GROUPS):
        assert window // dilation == SWA_BLOCK
        seq = t // dilation
        s_g = _matmul(h, w_grp[g], f"proj_swa{g}").reshape(b, seq, dilation * 3 * SWA_WIDTH)
        view = lambda tab: tab.reshape(seq, dilation * LANES)
        o_g, lse_g = _swa_group(s_g, view(cos_t), view(sa_t), view(sb_t),
                                q_norm_w[layer][g][None, :], k_norm_w[layer][g][None, :], dilation)
        o_list.append(o_g.reshape(n, SWA_WIDTH))
        lse_list.append(lse_g.reshape(n, LANES))

    zg = _matmul(h, w_zg, "proj_zg")
    out = _merge_out(x2, o_dn.reshape(n, DN_WIDTH), o_list, lse_list, zg,
                     w_branch_dn[layer].astype(BF16), w_branch_swa[layer].astype(BF16),
                     w_out[layer].astype(BF16))
    return out.reshape(b, t, d)
```

```python
import functools
import math

import jax
import jax.numpy as jnp
import numpy as np
from jax import lax
from jax.experimental import pallas as pl
from jax.experimental.pallas import tpu as pltpu

D_MODEL = 1024
HEAD_DIM = 128
DN_HEADS = 4
DN_WIDTH = DN_HEADS * HEAD_DIM
CONV_WIDTH = 4
CHUNK = 64
SWA_GROUPS = ((128, 1), (512, 4), (2048, 16))
N_GROUPS = 3
SWA_HEADS = 4
SWA_WIDTH = SWA_HEADS * HEAD_DIM
SWA_BLOCK = 128
ROPE_DIM = HEAD_DIM // 4
ROPE_HALF = ROPE_DIM // 2
ROPE_THETA = 500000.0
EPS = 1e-6
NEG_INF = -1e30

LANES = 128
VMEM_LIMIT_BYTES = 48 * 1024 * 1024
DN_VMEM_LIMIT_BYTES = 56 * 1024 * 1024
ROW_TILE = 512
BG_ROWS = 16

F32 = jnp.float32
BF16 = jnp.bfloat16
_NT = (((1,), (1,)), ((), ()))
_TN = (((0,), (0,)), ((), ()))


def _sigmoid(x):
    return 1.0 / (1.0 + jnp.exp(-x))


def _silu(x):
    return x * _sigmoid(x)


def _softplus(x):
    return jnp.maximum(x, 0.0) + jnp.log(1.0 + jnp.exp(-jnp.abs(x)))


def _params(n_axes):
    return pltpu.CompilerParams(dimension_semantics=("parallel",) * n_axes,
                                vmem_limit_bytes=VMEM_LIMIT_BYTES)


def _beta_and_log_decay(ab, a_log, dt_bias, head_index):
    beta = _sigmoid(ab)
    g = -jnp.exp(a_log) * _softplus(ab + dt_bias)
    return jnp.where(head_index < DN_HEADS, beta, g)


def _norm_proj_kernel(x_ref, nw_ref, wa_ref, wab_ref, wabt_ref, alog_ref, dt_ref, alogt_ref, dtt_ref,
                      h_ref, a_ref, bg_ref, bgt_ref):
    x = x_ref[...]
    h = (x * lax.rsqrt(jnp.mean(x * x, axis=-1, keepdims=True) + EPS)) * nw_ref[...]
    hb = h.astype(BF16)
    h_ref[...] = hb
    a_ref[...] = jnp.dot(hb, wa_ref[...], preferred_element_type=F32).astype(BF16)
    ab = jnp.dot(hb, wab_ref[...], preferred_element_type=F32)
    bg_ref[...] = _beta_and_log_decay(ab, alog_ref[...], dt_ref[...],
                                      lax.broadcasted_iota(jnp.int32, ab.shape, 1))
    abt = lax.dot_general(wabt_ref[...], hb, _NT, preferred_element_type=F32)
    bgt_ref[...] = _beta_and_log_decay(abt, alogt_ref[...], dtt_ref[...],
                                       lax.broadcasted_iota(jnp.int32, abt.shape, 0))


def _norm_proj(x2, norm_w, w_a, w_ab, w_abt, alog_row, dt_row):
    n = x2.shape[0]
    na = w_a.shape[1]
    nt = w_abt.shape[0]
    row = lambda i: (i, 0)
    fixed = lambda i: (0, 0)
    pad_col = lambda v: v[0, :nt][:, None]
    return pl.pallas_call(
        _norm_proj_kernel,
        grid=(n // ROW_TILE,),
        in_specs=[pl.BlockSpec((ROW_TILE, D_MODEL), row),
                  pl.BlockSpec((1, D_MODEL), fixed),
                  pl.BlockSpec((D_MODEL, na), fixed),
                  pl.BlockSpec((D_MODEL, LANES), fixed),
                  pl.BlockSpec((nt, D_MODEL), fixed),
                  pl.BlockSpec((1, LANES), fixed),
                  pl.BlockSpec((1, LANES), fixed),
                  pl.BlockSpec((nt, 1), fixed),
                  pl.BlockSpec((nt, 1), fixed)],
        out_specs=[pl.BlockSpec((ROW_TILE, D_MODEL), row),
                   pl.BlockSpec((ROW_TILE, na), row),
                   pl.BlockSpec((ROW_TILE, LANES), row),
                   pl.BlockSpec((nt, ROW_TILE), lambda i: (0, i))],
        out_shape=[jax.ShapeDtypeStruct((n, D_MODEL), BF16),
                   jax.ShapeDtypeStruct((n, na), BF16),
                   jax.ShapeDtypeStruct((n, LANES), F32),
                   jax.ShapeDtypeStruct((nt, n), F32)],
        compiler_params=_params(1),
        name="norm_proj",
    )(x2, norm_w, w_a, w_ab, w_abt, alog_row, dt_row, pad_col(alog_row), pad_col(dt_row))


def _matmul_kernel(h_ref, w_ref, o_ref):
    o_ref[...] = jnp.dot(h_ref[...], w_ref[...], preferred_element_type=F32).astype(o_ref.dtype)


def _matmul(h, w, name):
    n, k = h.shape
    nc = w.shape[1]
    return pl.pallas_call(
        _matmul_kernel,
        grid=(n // ROW_TILE,),
        in_specs=[pl.BlockSpec((ROW_TILE, k), lambda i: (i, 0)),
                  pl.BlockSpec((k, nc), lambda i: (0, 0))],
        out_specs=pl.BlockSpec((ROW_TILE, nc), lambda i: (i, 0)),
        out_shape=jax.ShapeDtypeStruct((n, nc), BF16),
        compiler_params=_params(1),
        name=name,
    )(h, w)


DN_GROUP = 4
DN_ROWS = DN_GROUP * CHUNK
DN_HALO = 16


def _deltanet_kernel(a_ref, bg_ref, bgt_ref, cw_ref, nw_ref, o_ref,
                     s_ref, u_ref, wq_ref, kd_ref, ai_ref, eg_ref):
    t_len = a_ref.shape[1]
    ii = lax.broadcasted_iota(jnp.int32, (CHUNK, CHUNK), 0)
    jj = lax.broadcasted_iota(jnp.int32, (CHUNK, CHUNK), 1)
    causal = ii >= jj
    strict = ii > jj
    eye_f = (ii == jj).astype(F32)
    bi = lax.broadcasted_iota(jnp.int32, (DN_ROWS, DN_ROWS), 0)
    bj = lax.broadcasted_iota(jnp.int32, (DN_ROWS, DN_ROWS), 1)
    shift = int(math.log2(CHUNK))
    same_chunk = jnp.right_shift(bi, shift) == jnp.right_shift(bj, shift)
    cum_lower = (same_chunk & (bi >= bj)).astype(F32)
    cum_upper = (same_chunk & (bi <= bj)).astype(F32)
    chunk_sum = same_chunk.astype(F32)
    hi = lax.Precision.HIGHEST
    problems = [(c, h) for c in range(DN_GROUP) for h in range(DN_HEADS)]

    def phase_a(gi, carry):
        r0 = pl.multiple_of(gi * DN_ROWS, DN_ROWS)
        rp = pl.multiple_of(jnp.maximum(r0 - DN_HALO, 0), DN_HALO)
        cur = a_ref[0, pl.ds(r0, DN_ROWS), 0:3 * DN_WIDTH].astype(F32)
        prev = a_ref[0, pl.ds(rp, DN_HALO), 0:3 * DN_WIDTH].astype(F32)
        prev = jnp.where(gi > 0, prev, 0.0)
        xw = jnp.concatenate([prev, cur], axis=0)
        conv = None
        for j in range(CONV_WIDTH):
            lo = DN_HALO - (CONV_WIDTH - 1) + j
            term = cw_ref[j:j + 1, :] * xw[lo:lo + DN_ROWS, :]
            conv = term if conv is None else conv + term
        act = _silu(conv)
        bg = bg_ref[0, pl.ds(r0, DN_ROWS), :]
        gc_all = jnp.dot(cum_lower, bg, preferred_element_type=F32, precision=hi)
        gl_all = jnp.dot(chunk_sum, bg, preferred_element_type=F32, precision=hi)
        gct_all = jnp.dot(bgt_ref[:, pl.ds(r0, DN_ROWS)], cum_upper, preferred_element_type=F32, precision=hi)
        eg_all = jnp.exp(gl_all)
        for c in range(DN_GROUP):
            eg_ref[gi * DN_GROUP + c] = eg_all[c * CHUNK:c * CHUNK + 8, :]

        qn, kn, kb, qg, rhs = [], [], [], [], []
        for h in range(DN_HEADS):
            qh = act[:, h * HEAD_DIM:(h + 1) * HEAD_DIM]
            kh = act[:, DN_WIDTH + h * HEAD_DIM:DN_WIDTH + (h + 1) * HEAD_DIM]
            vh = act[:, 2 * DN_WIDTH + h * HEAD_DIM:2 * DN_WIDTH + (h + 1) * HEAD_DIM]
            qn_h = qh * lax.rsqrt(jnp.sum(qh * qh, axis=-1, keepdims=True) + EPS) * (HEAD_DIM ** -0.5)
            kn_h = kh * lax.rsqrt(jnp.sum(kh * kh, axis=-1, keepdims=True) + EPS)
            beta = bg[:, h:h + 1]
            gc = gc_all[:, DN_HEADS + h:DN_HEADS + h + 1]
            gl = gl_all[:, DN_HEADS + h:DN_HEADS + h + 1]
            egc = jnp.exp(gc)
            kb_h = kn_h * beta
            kd_ref[h, pl.ds(r0, DN_ROWS), :] = (kn_h * jnp.exp(gl - gc)).astype(BF16)
            qn.append(qn_h)
            kn.append(kn_h.astype(BF16))
            kb.append(kb_h)
            qg.append((qn_h * egc).astype(BF16))
            rhs.append(jnp.concatenate([vh * beta, kb_h * egc], axis=1).astype(BF16))

        cs = lambda c: slice(c * CHUNK, (c + 1) * CHUNK)
        qk = [lax.dot_general(jnp.concatenate([qn[h][cs(c)], kb[h][cs(c)]], axis=0).astype(BF16),
                              kn[h][cs(c)], _NT, preferred_element_type=F32) for c, h in problems]
        decay = []
        for c, h in problems:
            diff = gc_all[cs(c), DN_HEADS + h:DN_HEADS + h + 1] - gct_all[DN_HEADS + h:DN_HEADS + h + 1, cs(c)]
            decay.append(jnp.where(causal, jnp.exp(jnp.where(causal, diff, 0.0)), 0.0))
        for i, (c, h) in enumerate(problems):
            a_intra = jnp.where(causal, qk[i][:CHUNK] * decay[i], 0.0)
            ai_ref[h, pl.ds(r0 + c * CHUNK, CHUNK), :] = a_intra.astype(BF16)
        ps = [-jnp.where(strict, qk[i][CHUNK:] * decay[i], 0.0) for i in range(len(problems))]
        tms = [eye_f + p for p in ps]
        for _ in range(shift - 1):
            pbs = [p.astype(BF16) for p in ps]
            ps = [jnp.dot(pb, pb, preferred_element_type=F32) for pb in pbs]
            tms = [tm + jnp.dot(tm.astype(BF16), p.astype(BF16), preferred_element_type=F32)
                   for tm, p in zip(tms, ps)]
        uw = [jnp.dot(tms[i].astype(BF16), rhs[h][cs(c)], preferred_element_type=F32)
              for i, (c, h) in enumerate(problems)]
        for i, (c, h) in enumerate(problems):
            u_ref[h, pl.ds(r0 + c * CHUNK, CHUNK), :] = uw[i][:, :HEAD_DIM]
            w0 = pl.multiple_of(2 * (r0 + c * CHUNK), 2 * CHUNK)
            wq_ref[h, pl.ds(w0, CHUNK), :] = uw[i][:, HEAD_DIM:].astype(BF16)
            wq_ref[h, pl.ds(w0 + CHUNK, CHUNK), :] = qg[h][cs(c)]
        return carry

    lax.fori_loop(0, t_len // DN_ROWS, phase_a, 0)

    s_ref[...] = jnp.zeros(s_ref.shape, F32)
    heads = range(DN_HEADS)

    def phase_b(c, carry):
        r0 = pl.multiple_of(c * CHUNK, CHUNK)
        rows = pl.ds(r0, CHUNK)
        wrows = pl.ds(pl.multiple_of(2 * r0, 2 * CHUNK), 2 * CHUNK)
        eg = eg_ref[c]
        s_old = [s_ref[h] for h in heads]
        ws = [jnp.dot(wq_ref[h, wrows, :], s_old[h].astype(BF16), preferred_element_type=F32) for h in heads]
        vn = [(u_ref[h, rows, :] - ws[h][:CHUNK]).astype(BF16) for h in heads]
        for h in heads:
            s_ref[h] = (s_old[h] * eg[0:1, DN_HEADS + h:DN_HEADS + h + 1]
                        + lax.dot_general(kd_ref[h, rows, :], vn[h], _TN, preferred_element_type=F32))
        o = [ws[h][CHUNK:] + jnp.dot(ai_ref[h, rows, :], vn[h], preferred_element_type=F32) for h in heads]
        for h in heads:
            hs = slice(h * HEAD_DIM, (h + 1) * HEAD_DIM)
            on = o[h] * lax.rsqrt(jnp.mean(o[h] * o[h], axis=-1, keepdims=True) + EPS) * nw_ref[...]
            z = a_ref[0, rows, 3 * DN_WIDTH + h * HEAD_DIM:3 * DN_WIDTH + (h + 1) * HEAD_DIM].astype(F32)
            o_ref[0, rows, hs] = (on * _silu(z)).astype(o_ref.dtype)
        return carry

    lax.fori_loop(0, t_len // CHUNK, phase_b, 0)


def _deltanet(a3, bg3, bgt, conv_w2, dn_norm_row):
    b, t, wa = a3.shape
    nt = bgt.shape[0]
    return pl.pallas_call(
        _deltanet_kernel,
        grid=(b,),
        in_specs=[pl.BlockSpec((1, t, wa), lambda i: (i, 0, 0)),
                  pl.BlockSpec((1, t, LANES), lambda i: (i, 0, 0)),
                  pl.BlockSpec((nt, t), lambda i: (0, i)),
                  pl.BlockSpec((CONV_WIDTH, 3 * DN_WIDTH), lambda i: (0, 0)),
                  pl.BlockSpec((1, HEAD_DIM), lambda i: (0, 0))],
        out_specs=pl.BlockSpec((1, t, DN_WIDTH), lambda i: (i, 0, 0)),
        out_shape=jax.ShapeDtypeStruct((b, t, DN_WIDTH), BF16),
        scratch_shapes=[pltpu.VMEM((DN_HEADS, HEAD_DIM, HEAD_DIM), F32),
                        pltpu.VMEM((DN_HEADS, t, HEAD_DIM), F32),
                        pltpu.VMEM((DN_HEADS, 2 * t, HEAD_DIM), BF16),
                        pltpu.VMEM((DN_HEADS, t, HEAD_DIM), BF16),
                        pltpu.VMEM((DN_HEADS, t, CHUNK), BF16),
                        pltpu.VMEM((t // CHUNK, 8, LANES), F32)],
        compiler_params=pltpu.CompilerParams(dimension_semantics=("parallel",),
                                             vmem_limit_bytes=DN_VMEM_LIMIT_BYTES),
        name="deltanet",
    )(a3, bg3, bgt, conv_w2, dn_norm_row)


def _swa_kernel(q_ref, k_ref, v_ref, cos_ref, sa_ref, sb_ref, qw_ref, kw_ref, o_ref, lse_ref, qs_ref, ks_ref):
    seq = q_ref.shape[1]
    nblk = seq // SWA_BLOCK

    def prep(n, carry):
        r0 = pl.multiple_of(n * SWA_BLOCK, SWA_BLOCK)
        rows = pl.ds(r0, SWA_BLOCK)
        cos = cos_ref[rows, :]
        sa = sa_ref[rows, :]
        sb = sb_ref[rows, :]
        for h in range(SWA_HEADS):
            hs = slice(h * HEAD_DIM, (h + 1) * HEAD_DIM)
            for src, w_ref, dst, scale in ((q_ref, qw_ref, qs_ref, HEAD_DIM ** -0.5), (k_ref, kw_ref, ks_ref, 1.0)):
                xh = src[0, rows, hs].astype(F32)
                y = xh * lax.rsqrt(jnp.mean(xh * xh, axis=-1, keepdims=True) + EPS) * w_ref[...]
                y = y * cos + pltpu.roll(y, ROPE_HALF, 1) * sa + pltpu.roll(y, HEAD_DIM - ROPE_HALF, 1) * sb
                dst[rows, hs] = (y * scale).astype(BF16)
        return carry

    lax.fori_loop(0, nblk, prep, 0)

    def attend(r0, with_prev):
        width = 2 * SWA_BLOCK if with_prev else SWA_BLOCK
        k0 = r0 - SWA_BLOCK if with_prev else r0
        krows = pl.ds(k0, width)
        qrows = pl.ds(r0, SWA_BLOCK)
        qi = lax.broadcasted_iota(jnp.int32, (SWA_BLOCK, width), 0)
        kj = lax.broadcasted_iota(jnp.int32, (SWA_BLOCK, width), 1)
        if with_prev:
            valid = (kj >= qi) & (kj <= qi + SWA_BLOCK)
        else:
            valid = kj <= qi
        lane = lax.broadcasted_iota(jnp.int32, (SWA_BLOCK, LANES), 1)
        lse_tile = jnp.zeros((SWA_BLOCK, LANES), F32)
        for h in range(SWA_HEADS):
            hs = slice(h * HEAD_DIM, (h + 1) * HEAD_DIM)
            s = lax.dot_general(qs_ref[qrows, hs], ks_ref[krows, hs], _NT, preferred_element_type=F32)
            s = jnp.where(valid, s, NEG_INF)
            m = jnp.max(s, axis=-1, keepdims=True)
            p = jnp.exp(s - m)
            l = jnp.sum(p, axis=-1, keepdims=True)
            o = jnp.dot(p.astype(BF16), v_ref[0, krows, hs], preferred_element_type=F32)
            o_ref[0, qrows, hs] = (o / l).astype(o_ref.dtype)
            lse_tile = jnp.where(lane == h, m + jnp.log(l), lse_tile)
        lse_ref[0, qrows, :] = lse_tile

    attend(0, False)
    if nblk > 1:
        def body(n, carry):
            attend(pl.multiple_of(n * SWA_BLOCK, SWA_BLOCK), True)
            return carry
        lax.fori_loop(1, nblk, body, 0)


def _swa_group(s3, cos_t, sa_t, sb_t, qw_row, kw_row, dilation):
    b, seq, _ = s3.shape
    blk = lambda which: pl.BlockSpec((1, seq, SWA_WIDTH), lambda r, i: (i, 0, 3 * r + which))
    tab = pl.BlockSpec((seq, LANES), lambda r, i: (0, r))
    wspec = pl.BlockSpec((1, HEAD_DIM), lambda r, i: (0, 0))
    return pl.pallas_call(
        _swa_kernel,
        grid=(dilation, b),
        in_specs=[blk(0), blk(1), blk(2), tab, tab, tab, wspec, wspec],
        out_specs=[pl.BlockSpec((1, seq, SWA_WIDTH), lambda r, i: (i, 0, r)),
                   pl.BlockSpec((1, seq, LANES), lambda r, i: (i, 0, r))],
        out_shape=[jax.ShapeDtypeStruct((b, seq, dilation * SWA_WIDTH), BF16),
                   jax.ShapeDtypeStruct((b, seq, dilation * LANES), F32)],
        scratch_shapes=[pltpu.VMEM((seq, SWA_WIDTH), BF16), pltpu.VMEM((seq, SWA_WIDTH), BF16)],
        compiler_params=_params(2),
        name=f"swa_d{dilation}",
    )(s3, s3, s3, cos_t, sa_t, sb_t, qw_row, kw_row)


def _merge_out_kernel(x_ref, odn_ref, o0_ref, o1_ref, o2_ref, l0_ref, l1_ref, l2_ref, zg_ref,
                      wdn_ref, wswa_ref, wout_ref, out_ref):
    l0 = l0_ref[...]
    l1 = l1_ref[...]
    l2 = l2_ref[...]
    m = jnp.maximum(jnp.maximum(l0, l1), l2)
    e0 = jnp.exp(l0 - m)
    e1 = jnp.exp(l1 - m)
    e2 = jnp.exp(l2 - m)
    inv = 1.0 / (e0 + e1 + e2)
    parts = []
    for h in range(SWA_HEADS):
        hs = slice(h * HEAD_DIM, (h + 1) * HEAD_DIM)
        col = slice(h, h + 1)
        oh = ((e0[:, col] * inv[:, col]) * o0_ref[:, hs].astype(F32)
              + (e1[:, col] * inv[:, col]) * o1_ref[:, hs].astype(F32)
              + (e2[:, col] * inv[:, col]) * o2_ref[:, hs].astype(F32))
        parts.append((oh * _silu(zg_ref[:, hs].astype(F32))).astype(BF16))
    o_swa = jnp.concatenate(parts, axis=1)
    y_swa = jnp.dot(o_swa, wswa_ref[...], preferred_element_type=F32)
    y_dn = jnp.dot(odn_ref[...], wdn_ref[...], preferred_element_type=F32)
    g_dn = zg_ref[:, SWA_WIDTH:SWA_WIDTH + D_MODEL].astype(F32)
    g_swa = zg_ref[:, SWA_WIDTH + D_MODEL:SWA_WIDTH + 2 * D_MODEL].astype(F32)
    merged = _sigmoid(g_dn) * y_dn + _sigmoid(g_swa) * y_swa
    out_ref[...] = x_ref[...] + jnp.dot(merged.astype(BF16), wout_ref[...], preferred_element_type=F32)


def _merge_out(x2, odn, o_list, lse_list, zg, w_dn, w_swa, w_out):
    n = x2.shape[0]
    row = lambda w: pl.BlockSpec((ROW_TILE, w), lambda i: (i, 0))
    full = lambda a: pl.BlockSpec(a.shape, lambda i: (0, 0))
    return pl.pallas_call(
        _merge_out_kernel,
        grid=(n // ROW_TILE,),
        in_specs=[row(D_MODEL), row(DN_WIDTH), row(SWA_WIDTH), row(SWA_WIDTH), row(SWA_WIDTH),
                  row(LANES), row(LANES), row(LANES), row(zg.shape[1]),
                  full(w_dn), full(w_swa), full(w_out)],
        out_specs=row(D_MODEL),
        out_shape=jax.ShapeDtypeStruct((n, D_MODEL), F32),
        compiler_params=_params(1),
        name="merge_out",
    )(x2, odn, *o_list, *lse_list, zg, w_dn, w_swa, w_out)


def _rope_tables(t_len):
    pos = jnp.arange(t_len, dtype=F32)
    inv_freq = ROPE_THETA ** (-jnp.arange(0, ROPE_DIM, 2, dtype=F32) / ROPE_DIM)
    ang = pos[:, None] * inv_freq[None, :]
    cos, sin = jnp.cos(ang), jnp.sin(ang)
    zeros = jnp.zeros_like(sin)
    tail = HEAD_DIM - ROPE_DIM
    cos_t = jnp.concatenate([cos, cos, jnp.ones((t_len, tail), F32)], axis=1)
    sa_t = jnp.concatenate([zeros, sin, jnp.zeros((t_len, tail), F32)], axis=1)
    sb_t = jnp.concatenate([-sin, zeros, jnp.zeros((t_len, tail), F32)], axis=1)
    return cos_t, sa_t, sb_t


def kernel(x, norm_w, w_in, conv_w, dn_a_log, dn_dt_bias, dn_norm_w, q_norm_w, k_norm_w,
           w_branch_dn, w_branch_swa, w_out):
    b, t, d = x.shape
    n = b * t
    layer = 0
    w = w_in[layer]
    c_qkv = 3 * DN_WIDTH
    c_z = c_qkv + DN_WIDTH
    c_ab = c_z + 2 * DN_HEADS
    c_q = c_ab
    c_k = c_q + N_GROUPS * SWA_WIDTH
    c_v = c_k + N_GROUPS * SWA_WIDTH
    c_sz = c_v + N_GROUPS * SWA_WIDTH

    w_a = w[:, :c_z].astype(BF16)
    w_ab = jnp.pad(w[:, c_z:c_ab], ((0, 0), (0, LANES - 2 * DN_HEADS))).astype(BF16)
    w_grp = [jnp.concatenate([w[:, c_q + g * SWA_WIDTH:c_q + (g + 1) * SWA_WIDTH],
                              w[:, c_k + g * SWA_WIDTH:c_k + (g + 1) * SWA_WIDTH],
                              w[:, c_v + g * SWA_WIDTH:c_v + (g + 1) * SWA_WIDTH]], axis=1).astype(BF16)
             for g in range(N_GROUPS)]
    w_zg = w[:, c_sz:].astype(BF16)
    pad_heads = lambda v: jnp.pad(v.astype(F32), (DN_HEADS, LANES - 2 * DN_HEADS))[None, :]
    alog_row = pad_heads(dn_a_log[layer])
    dt_row = pad_heads(dn_dt_bias[layer])

    w_abt = jnp.pad(w[:, c_z:c_ab].T, ((0, BG_ROWS - 2 * DN_HEADS), (0, 0))).astype(BF16)

    x2 = x.reshape(n, d)
    h, a, bg, bgt = _norm_proj(x2, norm_w[layer][None, :], w_a, w_ab, w_abt, alog_row, dt_row)

    o_dn = _deltanet(a.reshape(b, t, 4 * DN_WIDTH), bg.reshape(b, t, LANES), bgt,
                     conv_w[layer][:, 0, :], dn_norm_w[layer][None, :])

    cos_t, sa_t, sb_t = _rope_tables(t)
    o_list, lse_list = [], []
    for g, (window, dilation) in enumerate(SWA_GROUPS):
        assert window // dilation == SWA_BLOCK
        seq = t // dilation
        s_g = _matmul(h, w_grp[g], f"proj_swa{g}").reshape(b, seq, dilation * 3 * SWA_WIDTH)
        view = lambda tab: tab.reshape(seq, dilation * LANES)
        o_g, lse_g = _swa_group(s_g, view(cos_t), view(sa_t), view(sb_t),
                                q_norm_w[layer][g][None, :], k_norm_w[layer][g][None, :], dilation)
        o_list.append(o_g.reshape(n, SWA_WIDTH))
        lse_list.append(lse_g.reshape(n, LANES))

    zg = _matmul(h, w_zg, "proj_zg")
    out = _merge_out(x2, o_dn.reshape(n, DN_WIDTH), o_list, lse_list, zg,
                     w_branch_dn[layer].astype(BF16), w_branch_swa[layer].astype(BF16),
                     w_out[layer].astype(BF16))
    return out.reshape(b, t, d)
```

```python
import functools
import math

import jax
import jax.numpy as jnp
import numpy as np
from jax import lax
from jax.experimental import pallas as pl
from jax.experimental.pallas import tpu as pltpu

D_MODEL = 1024
HEAD_DIM = 128
DN_HEADS = 4
DN_WIDTH = DN_HEADS * HEAD_DIM
CONV_WIDTH = 4
CHUNK = 64
SWA_GROUPS = ((128, 1), (512, 4), (2048, 16))
N_GROUPS = 3
SWA_HEADS = 4
SWA_WIDTH = SWA_HEADS * HEAD_DIM
SWA_BLOCK = 128
ROPE_DIM = HEAD_DIM // 4
ROPE_HALF = ROPE_DIM // 2
ROPE_THETA = 500000.0
EPS = 1e-6
NEG_INF = -1e30

LANES = 128
VMEM_LIMIT_BYTES = 48 * 1024 * 1024
DN_VMEM_LIMIT_BYTES = 56 * 1024 * 1024
ROW_TILE = 512
BG_ROWS = 16

F32 = jnp.float32
BF16 = jnp.bfloat16
_NT = (((1,), (1,)), ((), ()))
_TN = (((0,), (0,)), ((), ()))


def _sigmoid(x):
    return 1.0 / (1.0 + jnp.exp(-x))


def _silu(x):
    return x * _sigmoid(x)


def _softplus(x):
    return jnp.maximum(x, 0.0) + jnp.log(1.0 + jnp.exp(-jnp.abs(x)))


def _params(n_axes):
    return pltpu.CompilerParams(dimension_semantics=("parallel",) * n_axes,
                                vmem_limit_bytes=VMEM_LIMIT_BYTES)


def _beta_and_log_decay(ab, a_log, dt_bias, head_index):
    beta = _sigmoid(ab)
    g = -jnp.exp(a_log) * _softplus(ab + dt_bias)
    return jnp.where(head_index < DN_HEADS, beta, g)


def _norm_proj_kernel(x_ref, nw_ref, wa_ref, wab_ref, wabt_ref, alog_ref, dt_ref, alogt_ref, dtt_ref,
                      h_ref, a_ref, bg_ref, bgt_ref):
    x = x_ref[...]
    h = (x * lax.rsqrt(jnp.mean(x * x, axis=-1, keepdims=True) + EPS)) * nw_ref[...]
    hb = h.astype(BF16)
    h_ref[...] = hb
    a_ref[...] = jnp.dot(hb, wa_ref[...], preferred_element_type=F32).astype(BF16)
    ab = jnp.dot(hb, wab_ref[...], preferred_element_type=F32)
    bg_ref[...] = _beta_and_log_decay(ab, alog_ref[...], dt_ref[...],
                                      lax.broadcasted_iota(jnp.int32, ab.shape, 1))
    abt = lax.dot_general(wabt_ref[...], hb, _NT, preferred_element_type=F32)
    bgt_ref[...] = _beta_and_log_decay(abt, alogt_ref[...], dtt_ref[...],
                                       lax.broadcasted_iota(jnp.int32, abt.shape, 0))


def _norm_proj(x2, norm_w, w_a, w_ab, w_abt, alog_row, dt_row):
    n = x2.shape[0]
    na = w_a.shape[1]
    nt = w_abt.shape[0]
    row = lambda i: (i, 0)
    fixed = lambda i: (0, 0)
    pad_col = lambda v: v[0, :nt][:, None]
    return pl.pallas_call(
        _norm_proj_kernel,
        grid=(n // ROW_TILE,),
        in_specs=[pl.BlockSpec((ROW_TILE, D_MODEL), row),
                  pl.BlockSpec((1, D_MODEL), fixed),
                  pl.BlockSpec((D_MODEL, na), fixed),
                  pl.BlockSpec((D_MODEL, LANES), fixed),
                  pl.BlockSpec((nt, D_MODEL), fixed),
                  pl.BlockSpec((1, LANES), fixed),
                  pl.BlockSpec((1, LANES), fixed),
                  pl.BlockSpec((nt, 1), fixed),
                  pl.BlockSpec((nt, 1), fixed)],
        out_specs=[pl.BlockSpec((ROW_TILE, D_MODEL), row),
                   pl.BlockSpec((ROW_TILE, na), row),
                   pl.BlockSpec((ROW_TILE, LANES), row),
                   pl.BlockSpec((nt, ROW_TILE), lambda i: (0, i))],
        out_shape=[jax.ShapeDtypeStruct((n, D_MODEL), BF16),
                   jax.ShapeDtypeStruct((n, na), BF16),
                   jax.ShapeDtypeStruct((n, LANES), F32),
                   jax.ShapeDtypeStruct((nt, n), F32)],
        compiler_params=_params(1),
        name="norm_proj",
    )(x2, norm_w, w_a, w_ab, w_abt, alog_row, dt_row, pad_col(alog_row), pad_col(dt_row))


def _matmul_kernel(h_ref, w_ref, o_ref):
    o_ref[...] = jnp.dot(h_ref[...], w_ref[...], preferred_element_type=F32).astype(o_ref.dtype)


def _matmul(h, w, name):
    n, k = h.shape
    nc = w.shape[1]
    return pl.pallas_call(
        _matmul_kernel,
        grid=(n // ROW_TILE,),
        in_specs=[pl.BlockSpec((ROW_TILE, k), lambda i: (i, 0)),
                  pl.BlockSpec((k, nc), lambda i: (0, 0))],
        out_specs=pl.BlockSpec((ROW_TILE, nc), lambda i: (i, 0)),
        out_shape=jax.ShapeDtypeStruct((n, nc), BF16),
        compiler_params=_params(1),
        name=name,
    )(h, w)


DN_GROUP = 4
DN_ROWS = DN_GROUP * CHUNK
DN_HALO = 16


def _deltanet_kernel(a_ref, bg_ref, bgt_ref, cw_ref, nw_ref, o_ref,
                     s_ref, u_ref, wq_ref, kd_ref, ai_ref, eg_ref):
    t_len = a_ref.shape[1]
    ii = lax.broadcasted_iota(jnp.int32, (CHUNK, CHUNK), 0)
    jj = lax.broadcasted_iota(jnp.int32, (CHUNK, CHUNK), 1)
    causal = ii >= jj
    strict = ii > jj
    eye_f = (ii == jj).astype(F32)
    bi = lax.broadcasted_iota(jnp.int32, (DN_ROWS, DN_ROWS), 0)
    bj = lax.broadcasted_iota(jnp.int32, (DN_ROWS, DN_ROWS), 1)
    shift = int(math.log2(CHUNK))
    same_chunk = jnp.right_shift(bi, shift) == jnp.right_shift(bj, shift)
    cum_lower = (same_chunk & (bi >= bj)).astype(F32)
    cum_upper = (same_chunk & (bi <= bj)).astype(F32)
    chunk_sum = same_chunk.astype(F32)
    hi = lax.Precision.HIGHEST
    problems = [(c, h) for c in range(DN_GROUP) for h in range(DN_HEADS)]

    def phase_a(gi, carry):
        r0 = pl.multiple_of(gi * DN_ROWS, DN_ROWS)
        rp = pl.multiple_of(jnp.maximum(r0 - DN_HALO, 0), DN_HALO)
        cur = a_ref[0, pl.ds(r0, DN_ROWS), 0:3 * DN_WIDTH].astype(F32)
        prev = a_ref[0, pl.ds(rp, DN_HALO), 0:3 * DN_WIDTH].astype(F32)
        prev = jnp.where(gi > 0, prev, 0.0)
        xw = jnp.concatenate([prev, cur], axis=0)
        conv = None
        for j in range(CONV_WIDTH):
            lo = DN_HALO - (CONV_WIDTH - 1) + j
            term = cw_ref[j:j + 1, :] * xw[lo:lo + DN_ROWS, :]
            conv = term if conv is None else conv + term
        act = _silu(conv)
        bg = bg_ref[0, pl.ds(r0, DN_ROWS), :]
        gc_all = jnp.dot(cum_lower, bg, preferred_element_type=F32, precision=hi)
        gl_all = jnp.dot(chunk_sum, bg, preferred_element_type=F32, precision=hi)
        gct_all = jnp.dot(bgt_ref[:, pl.ds(r0, DN_ROWS)], cum_upper, preferred_element_type=F32, precision=hi)
        eg_all = jnp.exp(gl_all)
        for c in range(DN_GROUP):
            eg_ref[gi * DN_GROUP + c] = eg_all[c * CHUNK:c * CHUNK + 8, :]

        qn, kn, kb, qg, rhs = [], [], [], [], []
        for h in range(DN_HEADS):
            qh = act[:, h * HEAD_DIM:(h + 1) * HEAD_DIM]
            kh = act[:, DN_WIDTH + h * HEAD_DIM:DN_WIDTH + (h + 1) * HEAD_DIM]
            vh = act[:, 2 * DN_WIDTH + h * HEAD_DIM:2 * DN_WIDTH + (h + 1) * HEAD_DIM]
            qn_h = qh * lax.rsqrt(jnp.sum(qh * qh, axis=-1, keepdims=True) + EPS) * (HEAD_DIM ** -0.5)
            kn_h = kh * lax.rsqrt(jnp.sum(kh * kh, axis=-1, keepdims=True) + EPS)
            beta = bg[:, h:h + 1]
            gc = gc_all[:, DN_HEADS + h:DN_HEADS + h + 1]
            gl = gl_all[:, DN_HEADS + h:DN_HEADS + h + 1]
            egc = jnp.exp(gc)
            kb_h = kn_h * beta
            kd_ref[h, pl.ds(r0, DN_ROWS), :] = (kn_h * jnp.exp(gl - gc)).astype(BF16)
            qn.append(qn_h)
            kn.append(kn_h.astype(BF16))
            kb.append(kb_h)
            qg.append((qn_h * egc).astype(BF16))
            rhs.append(jnp.concatenate([vh * beta, kb_h * egc], axis=1).astype(BF16))

        cs = lambda c: slice(c * CHUNK, (c + 1) * CHUNK)
        qk = [lax.dot_general(jnp.concatenate([qn[h][cs(c)], kb[h][cs(c)]], axis=0).astype(BF16),
                              kn[h][cs(c)], _NT, preferred_element_type=F32) for c, h in problems]
        decay = []
        for c, h in problems:
            diff = gc_all[cs(c), DN_HEADS + h:DN_HEADS + h + 1] - gct_all[DN_HEADS + h:DN_HEADS + h + 1, cs(c)]
            decay.append(jnp.where(causal, jnp.exp(jnp.where(causal, diff, 0.0)), 0.0))
        for i, (c, h) in enumerate(problems):
            a_intra = jnp.where(causal, qk[i][:CHUNK] * decay[i], 0.0)
            ai_ref[h, pl.ds(r0 + c * CHUNK, CHUNK), :] = a_intra.astype(BF16)
        ps = [-jnp.where(strict, qk[i][CHUNK:] * decay[i], 0.0) for i in range(len(problems))]
        tms = [eye_f + p for p in ps]
        for _ in range(shift - 1):
            pbs = [p.astype(BF16) for p in ps]
            ps = [jnp.dot(pb, pb, preferred_element_type=F32) for pb in pbs]
            tms = [tm + jnp.dot(tm.astype(BF16), p.astype(BF16), preferred_element_type=F32)
                   for tm, p in zip(tms, ps)]
        uw = [jnp.dot(tms[i].astype(BF16), rhs[h][cs(c)], preferred_element_type=F32)
              for i, (c, h) in enumerate(problems)]
        for i, (c, h) in enumerate(problems):
            u_ref[h, pl.ds(r0 + c * CHUNK, CHUNK), :] = uw[i][:, :HEAD_DIM]
            w0 = pl.multiple_of(2 * (r0 + c * CHUNK), 2 * CHUNK)
            wq_ref[h, pl.ds(w0, CHUNK), :] = uw[i][:, HEAD_DIM:].astype(BF16)
            wq_ref[h, pl.ds(w0 + CHUNK, CHUNK), :] = qg[h][cs(c)]
        return carry

    lax.fori_loop(0, t_len // DN_ROWS, phase_a, 0)

    s_ref[...] = jnp.zeros(s_ref.shape, F32)
    heads = range(DN_HEADS)

    def phase_b(c, carry):
        r0 = pl.multiple_of(c * CHUNK, CHUNK)
        rows = pl.ds(r0, CHUNK)
        wrows = pl.ds(pl.multiple_of(2 * r0, 2 * CHUNK), 2 * CHUNK)
        eg = eg_ref[c]
        s_old = [s_ref[h] for h in heads]
        ws = [jnp.dot(wq_ref[h, wrows, :], s_old[h].astype(BF16), preferred_element_type=F32) for h in heads]
        vn = [(u_ref[h, rows, :] - ws[h][:CHUNK]).astype(BF16) for h in heads]
        for h in heads:
            s_ref[h] = (s_old[h] * eg[0:1, DN_HEADS + h:DN_HEADS + h + 1]
                        + lax.dot_general(kd_ref[h, rows, :], vn[h], _TN, preferred_element_type=F32))
        o = [ws[h][CHUNK:] + jnp.dot(ai_ref[h, rows, :], vn[h], preferred_element_type=F32) for h in heads]
        for h in heads:
            hs = slice(h * HEAD_DIM, (h + 1) * HEAD_DIM)
            on = o[h] * lax.rsqrt(jnp.mean(o[h] * o[h], axis=-1, keepdims=True) + EPS) * nw_ref[...]
            z = a_ref[0, rows, 3 * DN_WIDTH + h * HEAD_DIM:3 * DN_WIDTH + (h + 1) * HEAD_DIM].astype(F32)
            o_ref[0, rows, hs] = (on * _silu(z)).astype(o_ref.dtype)
        return carry

    lax.fori_loop(0, t_len // CHUNK, phase_b, 0)


def _deltanet(a3, bg3, bgt, conv_w2, dn_norm_row):
    b, t, wa = a3.shape
    nt = bgt.shape[0]
    return pl.pallas_call(
        _deltanet_kernel,
        grid=(b,),
        in_specs=[pl.BlockSpec((1, t, wa), lambda i: (i, 0, 0)),
                  pl.BlockSpec((1, t, LANES), lambda i: (i, 0, 0)),
                  pl.BlockSpec((nt, t), lambda i: (0, i)),
                  pl.BlockSpec((CONV_WIDTH, 3 * DN_WIDTH), lambda i: (0, 0)),
                  pl.BlockSpec((1, HEAD_DIM), lambda i: (0, 0))],
        out_specs=pl.BlockSpec((1, t, DN_WIDTH), lambda i: (i, 0, 0)),
        out_shape=jax.ShapeDtypeStruct((b, t, DN_WIDTH), BF16),
        scratch_shapes=[pltpu.VMEM((DN_HEADS, HEAD_DIM, HEAD_DIM), F32),
                        pltpu.VMEM((DN_HEADS, t, HEAD_DIM), F32),
                        pltpu.VMEM((DN_HEADS, 2 * t, HEAD_DIM), BF16),
                        pltpu.VMEM((DN_HEADS, t, HEAD_DIM), BF16),
                        pltpu.VMEM((DN_HEADS, t, CHUNK), BF16),
                        pltpu.VMEM((t // CHUNK, 8, LANES), F32)],
        compiler_params=pltpu.CompilerParams(dimension_semantics=("parallel",),
                                             vmem_limit_bytes=DN_VMEM_LIMIT_BYTES),
        name="deltanet",
    )(a3, bg3, bgt, conv_w2, dn_norm_row)


SWA_PREP_ROWS = 256


def _swa_kernel(s_ref, cos_ref, sin_ref, qw_ref, kw_ref, o_ref, lse_ref,
                qs_ref, ks_ref, *dilated_scratch, dilation):
    t_len = s_ref.shape[1]
    d = dilation
    seq = t_len // d
    nblk = seq // SWA_BLOCK
    per = SWA_PREP_ROWS // d
    heads = range(SWA_HEADS)
    hsl = lambda h: slice(h * HEAD_DIM, (h + 1) * HEAD_DIM)
    if d > 1:
        vs_ref, stage_ref, ostage_ref = dilated_scratch
    ones_b = jnp.ones((HEAD_DIM, HEAD_DIM), BF16)
    src_lane = lax.broadcasted_iota(jnp.int32, (HEAD_DIM, HEAD_DIM), 0)
    dst_lane = lax.broadcasted_iota(jnp.int32, (HEAD_DIM, HEAD_DIM), 1)
    swap_b = (((dst_lane < ROPE_HALF) & (src_lane == dst_lane + ROPE_HALF))
              | ((dst_lane >= ROPE_HALF) & (dst_lane < ROPE_DIM) & (src_lane == dst_lane - ROPE_HALF))
              ).astype(BF16)

    def prep(n, carry):
        r0 = pl.multiple_of(n * SWA_PREP_ROWS, SWA_PREP_ROWS)
        rows = pl.ds(r0, SWA_PREP_ROWS)
        cos = cos_ref[rows, :]
        sin = sin_ref[rows, :]
        i0 = pl.multiple_of(n * per, per)

        def regroup(ys, dst, slab0):
            for h, y in enumerate(ys):
                stage_ref[slab0 + h] = y
            for h in heads:
                for r in range(d):
                    part = stage_ref[slab0 + h, pl.ds(r, per, stride=d), :]
                    dst[pl.ds(r * seq + i0, per), hsl(h)] = part.astype(BF16)

        for which, w_ref, dst, scale in ((0, qw_ref, qs_ref, HEAD_DIM ** -0.5), (1, kw_ref, ks_ref, 1.0)):
            xs = [s_ref[0, rows, which * SWA_WIDTH + h * HEAD_DIM:which * SWA_WIDTH + (h + 1) * HEAD_DIM].astype(F32)
                  for h in heads]
            ssq = [jnp.dot((x * x).astype(BF16), ones_b, preferred_element_type=F32) for x in xs]
            inv = [lax.rsqrt(s * (1.0 / HEAD_DIM) + EPS) * scale for s in ssq]
            xw = [x * w_ref[...] for x in xs]
            swapped = [jnp.dot(y.astype(BF16), swap_b, preferred_element_type=F32) for y in xw]
            ys = [(y * cos + sw * sin) * r for y, sw, r in zip(xw, swapped, inv)]
            if d == 1:
                for h in heads:
                    dst[rows, hsl(h)] = ys[h].astype(BF16)
            else:
                regroup(ys, dst, which * SWA_HEADS)
        if d > 1:
            vs = [s_ref[0, rows, 2 * SWA_WIDTH + h * HEAD_DIM:2 * SWA_WIDTH + (h + 1) * HEAD_DIM].astype(F32)
                  for h in heads]
            regroup(vs, vs_ref, 2 * SWA_HEADS)
        return carry

    lax.fori_loop(0, t_len // SWA_PREP_ROWS, prep, 0)

    def band_bias(width):
        qi = lax.broadcasted_iota(jnp.int32, (SWA_BLOCK, width), 0)
        kj = lax.broadcasted_iota(jnp.int32, (SWA_BLOCK, width), 1)
        if width == SWA_BLOCK:
            valid = kj <= qi
        else:
            valid = (kj >= qi) & (kj <= qi + SWA_BLOCK)
        return jnp.where(valid, 0.0, NEG_INF).astype(F32)

    lane = lax.broadcasted_iota(jnp.int32, (SWA_BLOCK, LANES), 1)

    def attend(r, n, with_prev):
        width = 2 * SWA_BLOCK if with_prev else SWA_BLOCK
        q0 = pl.multiple_of(r * seq + n * SWA_BLOCK, SWA_BLOCK)
        qrows = pl.ds(q0, SWA_BLOCK)
        krows = pl.ds(pl.multiple_of(q0 - SWA_BLOCK, SWA_BLOCK), width) if with_prev else qrows
        bias = band_bias(width)
        s = [lax.dot_general(qs_ref[qrows, hsl(h)], ks_ref[krows, hsl(h)], _NT, preferred_element_type=F32) + bias
             for h in heads]
        m = [jnp.max(x, axis=-1, keepdims=True) for x in s]
        p = [jnp.exp(x - mx) for x, mx in zip(s, m)]
        l = [jnp.sum(x, axis=-1, keepdims=True) for x in p]
        if d == 1:
            v = [s_ref[0, krows, 2 * SWA_WIDTH + h * HEAD_DIM:2 * SWA_WIDTH + (h + 1) * HEAD_DIM] for h in heads]
        else:
            v = [vs_ref[krows, hsl(h)] for h in heads]
        o = [jnp.dot(x.astype(BF16), vh, preferred_element_type=F32) * (1.0 / lx) for x, vh, lx in zip(p, v, l)]
        lse_tile = jnp.zeros((SWA_BLOCK, LANES), F32)
        for h in heads:
            lse_tile = jnp.where(lane == h, m[h] + jnp.log(l[h]), lse_tile)
        if d == 1:
            for h in heads:
                o_ref[0, qrows, hsl(h)] = o[h].astype(o_ref.dtype)
            lse_ref[0, qrows, :] = lse_tile
        else:
            trows = pl.ds(r + d * n * SWA_BLOCK, SWA_BLOCK, stride=d)
            for h in heads:
                ostage_ref[h, trows, :] = o[h]
            lse_ref[0, trows, :] = lse_tile

    def per_subsequence(r, carry):
        attend(r, 0, False)
        if nblk > 1:
            def body(n, c2):
                attend(r, n, True)
                return c2
            lax.fori_loop(1, nblk, body, 0)
        return carry

    lax.fori_loop(0, d, per_subsequence, 0)

    if d > 1:
        def emit(n, carry):
            rows = pl.ds(pl.multiple_of(n * SWA_PREP_ROWS, SWA_PREP_ROWS), SWA_PREP_ROWS)
            for h in heads:
                o_ref[0, rows, hsl(h)] = ostage_ref[h, rows, :].astype(o_ref.dtype)
            return carry
        lax.fori_loop(0, t_len // SWA_PREP_ROWS, emit, 0)


def _swa_group(s3, cos_t, sin_t, qw_row, kw_row, dilation):
    b, t, ws = s3.shape
    tab = pl.BlockSpec((t, LANES), lambda i: (0, 0))
    wspec = pl.BlockSpec((1, HEAD_DIM), lambda i: (0, 0))
    scratch = [pltpu.VMEM((t, SWA_WIDTH), BF16), pltpu.VMEM((t, SWA_WIDTH), BF16)]
    if dilation > 1:
        scratch += [pltpu.VMEM((t, SWA_WIDTH), BF16),
                    pltpu.VMEM((3 * SWA_HEADS, SWA_PREP_ROWS, LANES), F32),
                    pltpu.VMEM((SWA_HEADS, t, LANES), F32)]
    return pl.pallas_call(
        functools.partial(_swa_kernel, dilation=dilation),
        grid=(b,),
        in_specs=[pl.BlockSpec((1, t, ws), lambda i: (i, 0, 0)), tab, tab, wspec, wspec],
        out_specs=[pl.BlockSpec((1, t, SWA_WIDTH), lambda i: (i, 0, 0)),
                   pl.BlockSpec((1, t, LANES), lambda i: (i, 0, 0))],
        out_shape=[jax.ShapeDtypeStruct((b, t, SWA_WIDTH), BF16),
                   jax.ShapeDtypeStruct((b, t, LANES), F32)],
        scratch_shapes=scratch,
        compiler_params=_params(1),
        name=f"swa_d{dilation}",
    )(s3, cos_t, sin_t, qw_row, kw_row)


def _merge_out_kernel(x_ref, odn_ref, o0_ref, o1_ref, o2_ref, l0_ref, l1_ref, l2_ref, zg_ref,
                      wdn_ref, wswa_ref, wout_ref, out_ref):
    l0 = l0_ref[...]
    l1 = l1_ref[...]
    l2 = l2_ref[...]
    m = jnp.maximum(jnp.maximum(l0, l1), l2)
    e0 = jnp.exp(l0 - m)
    e1 = jnp.exp(l1 - m)
    e2 = jnp.exp(l2 - m)
    inv = 1.0 / (e0 + e1 + e2)
    parts = []
    for h in range(SWA_HEADS):
        hs = slice(h * HEAD_DIM, (h + 1) * HEAD_DIM)
        col = slice(h, h + 1)
        oh = ((e0[:, col] * inv[:, col]) * o0_ref[:, hs].astype(F32)
              + (e1[:, col] * inv[:, col]) * o1_ref[:, hs].astype(F32)
              + (e2[:, col] * inv[:, col]) * o2_ref[:, hs].astype(F32))
        parts.append((oh * _silu(zg_ref[:, hs].astype(F32))).astype(BF16))
    o_swa = jnp.concatenate(parts, axis=1)
    y_swa = jnp.dot(o_swa, wswa_ref[...], preferred_element_type=F32)
    y_dn = jnp.dot(odn_ref[...], wdn_ref[...], preferred_element_type=F32)
    g_dn = zg_ref[:, SWA_WIDTH:SWA_WIDTH + D_MODEL].astype(F32)
    g_swa = zg_ref[:, SWA_WIDTH + D_MODEL:SWA_WIDTH + 2 * D_MODEL].astype(F32)
    merged = _sigmoid(g_dn) * y_dn + _sigmoid(g_swa) * y_swa
    out_ref[...] = x_ref[...] + jnp.dot(merged.astype(BF16), wout_ref[...], preferred_element_type=F32)


def _merge_out(x2, odn, o_list, lse_list, zg, w_dn, w_swa, w_out):
    n = x2.shape[0]
    row = lambda w: pl.BlockSpec((ROW_TILE, w), lambda i: (i, 0))
    full = lambda a: pl.BlockSpec(a.shape, lambda i: (0, 0))
    return pl.pallas_call(
        _merge_out_kernel,
        grid=(n // ROW_TILE,),
        in_specs=[row(D_MODEL), row(DN_WIDTH), row(SWA_WIDTH), row(SWA_WIDTH), row(SWA_WIDTH),
                  row(LANES), row(LANES), row(LANES), row(zg.shape[1]),
                  full(w_dn), full(w_swa), full(w_out)],
        out_specs=row(D_MODEL),
        out_shape=jax.ShapeDtypeStruct((n, D_MODEL), F32),
        compiler_params=_params(1),
        name="merge_out",
    )(x2, odn, *o_list, *lse_list, zg, w_dn, w_swa, w_out)


def _rope_tables(t_len):
    pos = jnp.arange(t_len, dtype=F32)
    inv_freq = ROPE_THETA ** (-jnp.arange(0, ROPE_DIM, 2, dtype=F32) / ROPE_DIM)
    ang = pos[:, None] * inv_freq[None, :]
    cos, sin = jnp.cos(ang), jnp.sin(ang)
    tail = HEAD_DIM - ROPE_DIM
    cos_t = jnp.concatenate([cos, cos, jnp.ones((t_len, tail), F32)], axis=1)
    sin_t = jnp.concatenate([-sin, sin, jnp.zeros((t_len, tail), F32)], axis=1)
    return cos_t, sin_t


def kernel(x, norm_w, w_in, conv_w, dn_a_log, dn_dt_bias, dn_norm_w, q_norm_w, k_norm_w,
           w_branch_dn, w_branch_swa, w_out):
    b, t, d = x.shape
    n = b * t
    layer = 0
    w = w_in[layer]
    c_qkv = 3 * DN_WIDTH
    c_z = c_qkv + DN_WIDTH
    c_ab = c_z + 2 * DN_HEADS
    c_q = c_ab
    c_k = c_q + N_GROUPS * SWA_WIDTH
    c_v = c_k + N_GROUPS * SWA_WIDTH
    c_sz = c_v + N_GROUPS * SWA_WIDTH

    w_a = w[:, :c_z].astype(BF16)
    w_ab = jnp.pad(w[:, c_z:c_ab], ((0, 0), (0, LANES - 2 * DN_HEADS))).astype(BF16)
    w_grp = [jnp.concatenate([w[:, c_q + g * SWA_WIDTH:c_q + (g + 1) * SWA_WIDTH],
                              w[:, c_k + g * SWA_WIDTH:c_k + (g + 1) * SWA_WIDTH],
                              w[:, c_v + g * SWA_WIDTH:c_v + (g + 1) * SWA_WIDTH]], axis=1).astype(BF16)
             for g in range(N_GROUPS)]
    w_zg = w[:, c_sz:].astype(BF16)
    pad_heads = lambda v: jnp.pad(v.astype(F32), (DN_HEADS, LANES - 2 * DN_HEADS))[None, :]
    alog_row = pad_heads(dn_a_log[layer])
    dt_row = pad_heads(dn_dt_bias[layer])

    w_abt = jnp.pad(w[:, c_z:c_ab].T, ((0, BG_ROWS - 2 * DN_HEADS), (0, 0))).astype(BF16)

    x2 = x.reshape(n, d)
    h, a, bg, bgt = _norm_proj(x2, norm_w[layer][None, :], w_a, w_ab, w_abt, alog_row, dt_row)

    o_dn = _deltanet(a.reshape(b, t, 4 * DN_WIDTH), bg.reshape(b, t, LANES), bgt,
                     conv_w[layer][:, 0, :], dn_norm_w[layer][None, :])

    cos_t, sin_t = _rope_tables(t)
    o_list, lse_list = [], []
    for g, (window, dilation) in enumerate(SWA_GROUPS):
        assert window // dilation == SWA_BLOCK
        s_g = _matmul(h, w_grp[g], f"proj_swa{g}").reshape(b, t, 3 * SWA_WIDTH)
        o_g, lse_g = _swa_group(s_g, cos_t, sin_t,
                                q_norm_w[layer][g][None, :], k_norm_w[layer][g][None, :], dilation)
        o_list.append(o_g.reshape(n, SWA_WIDTH))
        lse_list.append(lse_g.reshape(n, LANES))

    zg = _matmul(h, w_zg, "proj_zg")
    out = _merge_out(x2, o_dn.reshape(n, DN_WIDTH), o_list, lse_list, zg,
                     w_branch_dn[layer].astype(BF16), w_branch_swa[layer].astype(BF16),
                     w_out[layer].astype(BF16))
    return out.reshape(b, t, d)
```

```python
import functools
import math

import jax
import jax.numpy as jnp
import numpy as np
from jax import lax
from jax.experimental import pallas as pl
from jax.experimental.pallas import tpu as pltpu

D_MODEL = 1024
HEAD_DIM = 128
DN_HEADS = 4
DN_WIDTH = DN_HEADS * HEAD_DIM
CONV_WIDTH = 4
CHUNK = 64
SWA_GROUPS = ((128, 1), (512, 4), (2048, 16))
N_GROUPS = 3
SWA_HEADS = 4
SWA_WIDTH = SWA_HEADS * HEAD_DIM
SWA_BLOCK = 128
ROPE_DIM = HEAD_DIM // 4
ROPE_HALF = ROPE_DIM // 2
ROPE_THETA = 500000.0
EPS = 1e-6
NEG_INF = -1e30

LANES = 128
VMEM_LIMIT_BYTES = 48 * 1024 * 1024
DN_VMEM_LIMIT_BYTES = 56 * 1024 * 1024
ROW_TILE = 512
BG_ROWS = 16

F32 = jnp.float32
BF16 = jnp.bfloat16
_NT = (((1,), (1,)), ((), ()))
_TN = (((0,), (0,)), ((), ()))


def _sigmoid(x):
    return 1.0 / (1.0 + jnp.exp(-x))


def _silu(x):
    return x * _sigmoid(x)


def _softplus(x):
    return jnp.maximum(x, 0.0) + jnp.log(1.0 + jnp.exp(-jnp.abs(x)))


def _params(n_axes):
    return pltpu.CompilerParams(dimension_semantics=("parallel",) * n_axes,
                                vmem_limit_bytes=VMEM_LIMIT_BYTES)


def _mxu(a, b):
    return jnp.dot(a, b, preferred_element_type=F32)


def _split3(x):
    hi = x.astype(BF16)
    r1 = x - hi.astype(F32)
    mid = r1.astype(BF16)
    lo = (r1 - mid.astype(F32)).astype(BF16)
    return hi, mid, lo


def _beta_and_log_decay(ab, a_log, dt_bias, head_index):
    beta = _sigmoid(ab)
    g = -jnp.exp(a_log) * _softplus(ab + dt_bias)
    return jnp.where(head_index < DN_HEADS, beta, g)


def _norm_proj_kernel(x_ref, nw_ref, wa_ref, wab_ref, wabt_ref, alog_ref, dt_ref, alogt_ref, dtt_ref,
                      h_ref, a_ref, bg_ref, bgt_ref):
    x = x_ref[...]
    h = (x * lax.rsqrt(jnp.mean(x * x, axis=-1, keepdims=True) + EPS)) * nw_ref[...]
    hb = h.astype(BF16)
    h_ref[...] = hb
    a_ref[...] = _mxu(hb, wa_ref[...]).astype(BF16)
    ab = _mxu(hb, wab_ref[...])
    bg_ref[...] = _beta_and_log_decay(ab, alog_ref[...], dt_ref[...],
                                      lax.broadcasted_iota(jnp.int32, ab.shape, 1))
    abt = lax.dot_general(wabt_ref[...], hb, _NT, preferred_element_type=F32)
    bgt_ref[...] = _beta_and_log_decay(abt, alogt_ref[...], dtt_ref[...],
                                       lax.broadcasted_iota(jnp.int32, abt.shape, 0))


def _norm_proj(x2, norm_w, w_a, w_ab, w_abt, alog_row, dt_row):
    n = x2.shape[0]
    na = w_a.shape[1]
    nt = w_abt.shape[0]
    row = lambda i: (i, 0)
    fixed = lambda i: (0, 0)
    pad_col = lambda v: v[0, :nt][:, None]
    return pl.pallas_call(
        _norm_proj_kernel,
        grid=(n // ROW_TILE,),
        in_specs=[pl.BlockSpec((ROW_TILE, D_MODEL), row),
                  pl.BlockSpec((1, D_MODEL), fixed),
                  pl.BlockSpec((D_MODEL, na), fixed),
                  pl.BlockSpec((D_MODEL, LANES), fixed),
                  pl.BlockSpec((nt, D_MODEL), fixed),
                  pl.BlockSpec((1, LANES), fixed),
                  pl.BlockSpec((1, LANES), fixed),
                  pl.BlockSpec((nt, 1), fixed),
                  pl.BlockSpec((nt, 1), fixed)],
        out_specs=[pl.BlockSpec((ROW_TILE, D_MODEL), row),
                   pl.BlockSpec((ROW_TILE, na), row),
                   pl.BlockSpec((ROW_TILE, LANES), row),
                   pl.BlockSpec((nt, ROW_TILE), lambda i: (0, i))],
        out_shape=[jax.ShapeDtypeStruct((n, D_MODEL), BF16),
                   jax.ShapeDtypeStruct((n, na), BF16),
                   jax.ShapeDtypeStruct((n, LANES), F32),
                   jax.ShapeDtypeStruct((nt, n), F32)],
        compiler_params=_params(1),
        name="norm_proj",
    )(x2, norm_w, w_a, w_ab, w_abt, alog_row, dt_row, pad_col(alog_row), pad_col(dt_row))


def _matmul_kernel(h_ref, w_ref, o_ref):
    o_ref[...] = _mxu(h_ref[...], w_ref[...]).astype(o_ref.dtype)


def _matmul(h, w, name):
    n, k = h.shape
    nc = w.shape[1]
    return pl.pallas_call(
        _matmul_kernel,
        grid=(n // ROW_TILE,),
        in_specs=[pl.BlockSpec((ROW_TILE, k), lambda i: (i, 0)),
                  pl.BlockSpec((k, nc), lambda i: (0, 0))],
        out_specs=pl.BlockSpec((ROW_TILE, nc), lambda i: (i, 0)),
        out_shape=jax.ShapeDtypeStruct((n, nc), BF16),
        compiler_params=_params(1),
        name=name,
    )(h, w)


DN_GROUP = 4
DN_ROWS = DN_GROUP * CHUNK
DN_HALO = 16


def _deltanet_kernel(a_ref, bg_ref, bgt_ref, cw_ref, nw_ref, o_ref,
                     u_ref, wq_ref, kd_ref, ai_ref, eg_ref):
    t_len = a_ref.shape[1]
    n_groups = t_len // DN_ROWS
    ii = lax.broadcasted_iota(jnp.int32, (CHUNK, CHUNK), 0)
    jj = lax.broadcasted_iota(jnp.int32, (CHUNK, CHUNK), 1)
    causal = ii >= jj
    strict = ii > jj
    eye_f = (ii == jj).astype(F32)
    bi = lax.broadcasted_iota(jnp.int32, (DN_ROWS, DN_ROWS), 0)
    bj = lax.broadcasted_iota(jnp.int32, (DN_ROWS, DN_ROWS), 1)
    shift = int(math.log2(CHUNK))
    same_chunk = jnp.right_shift(bi, shift) == jnp.right_shift(bj, shift)
    cum_lower = (same_chunk & (bi >= bj)).astype(BF16)
    cum_upper = (same_chunk & (bi <= bj)).astype(BF16)
    problems = [(c, h) for c in range(DN_GROUP) for h in range(DN_HEADS)]
    heads = range(DN_HEADS)
    hsl = lambda h: slice(h * HEAD_DIM, (h + 1) * HEAD_DIM)
    cs = lambda c: slice(c * CHUNK, (c + 1) * CHUNK)
    ones_b = jnp.ones((HEAD_DIM, HEAD_DIM), BF16)
    sel_src = lax.broadcasted_iota(jnp.int32, (LANES, DN_WIDTH), 0)
    sel_head = jnp.right_shift(lax.broadcasted_iota(jnp.int32, (LANES, DN_WIDTH), 1), int(math.log2(HEAD_DIM)))
    sel_beta = (sel_src == sel_head).astype(BF16)
    sel_g = (sel_src == sel_head + DN_HEADS).astype(BF16)
    lane = lax.broadcasted_iota(jnp.int32, (DN_ROWS, LANES), 1)
    g_lanes = (lane >= DN_HEADS) & (lane < 2 * DN_HEADS)

    pending = []

    def defer(ref, idx, value):
        pending.append((ref, idx, value))

    def flush():
        for ref, idx, value in pending:
            ref[idx] = value
        pending.clear()

    def conv_silu(gi, r0, col0):
        cols = slice(col0, col0 + HEAD_DIM)
        rp = pl.multiple_of(jnp.maximum(r0 - DN_HALO, 0), DN_HALO)
        cur = a_ref[0, pl.ds(r0, DN_ROWS), cols].astype(F32)
        prev = a_ref[0, pl.ds(rp, DN_HALO), cols].astype(F32)
        prev = jnp.where(gi > 0, prev, 0.0)
        xw = jnp.concatenate([prev, cur], axis=0)
        conv = None
        for j in range(CONV_WIDTH):
            lo = DN_HALO - (CONV_WIDTH - 1) + j
            term = cw_ref[j:j + 1, cols] * xw[lo:lo + DN_ROWS, :]
            conv = term if conv is None else conv + term
        return _silu(conv)

    def prepare(gi):
        r0 = pl.multiple_of(gi * DN_ROWS, DN_ROWS)
        bg = bg_ref[0, pl.ds(r0, DN_ROWS), :]
        gc_all = sum(_mxu(cum_lower, piece) for piece in _split3(bg))
        gct_all = sum(_mxu(piece, cum_upper) for piece in _split3(bgt_ref[:, pl.ds(r0, DN_ROWS)]))
        gl_all = jnp.concatenate([jnp.broadcast_to(gc_all[(c + 1) * CHUNK - 1:(c + 1) * CHUNK, :], (CHUNK, LANES))
                                  for c in range(DN_GROUP)], axis=0)
        eg_all = jnp.exp(gl_all)
        for c in range(DN_GROUP):
            defer(eg_ref, (gi * DN_GROUP + c,), eg_all[c * CHUNK:c * CHUNK + 8, :])
        gc_m = jnp.where(g_lanes, gc_all, 0.0)
        gl_m = jnp.where(g_lanes, gl_all, 0.0)
        beta_rep = _mxu(bg.astype(BF16), sel_beta)
        egc_rep = _mxu(jnp.exp(gc_m).astype(BF16), sel_g)
        ekd_rep = _mxu(jnp.exp(gl_m - gc_m).astype(BF16), sel_g)

        qn, knb, kb, rhs = [], [], [], []
        for h in heads:
            qh = conv_silu(gi, r0, h * HEAD_DIM)
            kh = conv_silu(gi, r0, DN_WIDTH + h * HEAD_DIM)
            vh = conv_silu(gi, r0, 2 * DN_WIDTH + h * HEAD_DIM)
            qn_h = qh * (lax.rsqrt(_mxu((qh * qh).astype(BF16), ones_b) + EPS) * (HEAD_DIM ** -0.5))
            kn_h = kh * lax.rsqrt(_mxu((kh * kh).astype(BF16), ones_b) + EPS)
            kb_h = kn_h * beta_rep[:, hsl(h)]
            defer(kd_ref, (h, pl.ds(r0, DN_ROWS), slice(None)), (kn_h * ekd_rep[:, hsl(h)]).astype(BF16))
            qg_h = (qn_h * egc_rep[:, hsl(h)]).astype(BF16)
            for c in range(DN_GROUP):
                w0 = pl.multiple_of(2 * (r0 + c * CHUNK), 2 * CHUNK)
                defer(wq_ref, (h, pl.ds(w0 + CHUNK, CHUNK), slice(None)), qg_h[cs(c)])
            qn.append(qn_h)
            knb.append(kn_h.astype(BF16))
            kb.append(kb_h)
            rhs.append(jnp.concatenate([vh * beta_rep[:, hsl(h)], kb_h * egc_rep[:, hsl(h)]], axis=1).astype(BF16))
            yield

        qk = [lax.dot_general(jnp.concatenate([qn[h][cs(c)], kb[h][cs(c)]], axis=0).astype(BF16),
                              knb[h][cs(c)], _NT, preferred_element_type=F32) for c, h in problems]
        decay = []
        for c, h in problems:
            diff = gc_all[cs(c), DN_HEADS + h:DN_HEADS + h + 1] - gct_all[DN_HEADS + h:DN_HEADS + h + 1, cs(c)]
            decay.append(jnp.where(causal, jnp.exp(jnp.where(causal, diff, 0.0)), 0.0))
        for i, (c, h) in enumerate(problems):
            a_intra = jnp.where(causal, qk[i][:CHUNK] * decay[i], 0.0)
            defer(ai_ref, (h, pl.ds(r0 + c * CHUNK, CHUNK), slice(None)), a_intra.astype(BF16))
        ps = [-jnp.where(strict, qk[i][CHUNK:] * decay[i], 0.0) for i in range(len(problems))]
        tms = [eye_f + p for p in ps]
        for _ in range(shift - 1):
            pbs = [p.astype(BF16) for p in ps]
            ps = [_mxu(pb, pb) for pb in pbs]
            tms = [tm + _mxu(tm.astype(BF16), p.astype(BF16)) for tm, p in zip(tms, ps)]
        uw = [_mxu(tms[i].astype(BF16), rhs[h][cs(c)]) for i, (c, h) in enumerate(problems)]
        for i, (c, h) in enumerate(problems):
            defer(u_ref, (h, pl.ds(r0 + c * CHUNK, CHUNK), slice(None)), uw[i][:, :HEAD_DIM])
            w0 = pl.multiple_of(2 * (r0 + c * CHUNK), 2 * CHUNK)
            defer(wq_ref, (h, pl.ds(w0, CHUNK), slice(None)), uw[i][:, HEAD_DIM:].astype(BF16))

    def recur(c, state):
        r0 = pl.multiple_of(c * CHUNK, CHUNK)
        rows = pl.ds(r0, CHUNK)
        wrows = pl.ds(pl.multiple_of(2 * r0, 2 * CHUNK), 2 * CHUNK)
        eg = eg_ref[c]
        ws = [_mxu(wq_ref[h, wrows, :], state[h].astype(BF16)) for h in heads]
        vn = [(u_ref[h, rows, :] - ws[h][:CHUNK]).astype(BF16) for h in heads]
        new_state = [state[h] * eg[0:1, DN_HEADS + h:DN_HEADS + h + 1]
                     + lax.dot_general(kd_ref[h, rows, :], vn[h], _TN, preferred_element_type=F32) for h in heads]
        o = [ws[h][CHUNK:] + _mxu(ai_ref[h, rows, :], vn[h]) for h in heads]
        ms = [_mxu((x * x).astype(BF16), ones_b) * (1.0 / HEAD_DIM) for x in o]
        for h in heads:
            on = o[h] * lax.rsqrt(ms[h] + EPS) * nw_ref[...]
            z = a_ref[0, rows, 3 * DN_WIDTH + h * HEAD_DIM:3 * DN_WIDTH + (h + 1) * HEAD_DIM].astype(F32)
            defer(o_ref, (0, rows, hsl(h)), (on * _silu(z)).astype(o_ref.dtype))
        return new_state

    for _ in prepare(0):
        pass
    flush()

    def pipelined(gi, state):
        state = list(state)
        stages = prepare(gi)
        for c in range(DN_GROUP):
            state = recur((gi - 1) * DN_GROUP + c, state)
            next(stages)
        for _ in stages:
            pass
        flush()
        return tuple(state)

    zero = jnp.zeros((HEAD_DIM, HEAD_DIM), F32)
    state = list(lax.fori_loop(1, n_groups, pipelined, (zero,) * DN_HEADS))
    for c in range(DN_GROUP):
        state = recur((n_groups - 1) * DN_GROUP + c, state)
    flush()


def _deltanet(a3, bg3, bgt, conv_w2, dn_norm_row):
    b, t, wa = a3.shape
    nt = bgt.shape[0]
    return pl.pallas_call(
        _deltanet_kernel,
        grid=(b,),
        in_specs=[pl.BlockSpec((1, t, wa), lambda i: (i, 0, 0)),
                  pl.BlockSpec((1, t, LANES), lambda i: (i, 0, 0)),
                  pl.BlockSpec((nt, t), lambda i: (0, i)),
                  pl.BlockSpec((CONV_WIDTH, 3 * DN_WIDTH), lambda i: (0, 0)),
                  pl.BlockSpec((1, HEAD_DIM), lambda i: (0, 0))],
        out_specs=pl.BlockSpec((1, t, DN_WIDTH), lambda i: (i, 0, 0)),
        out_shape=jax.ShapeDtypeStruct((b, t, DN_WIDTH), BF16),
        scratch_shapes=[pltpu.VMEM((DN_HEADS, t, HEAD_DIM), F32),
                        pltpu.VMEM((DN_HEADS, 2 * t, HEAD_DIM), BF16),
                        pltpu.VMEM((DN_HEADS, t, HEAD_DIM), BF16),
                        pltpu.VMEM((DN_HEADS, t, CHUNK), BF16),
                        pltpu.VMEM((t // CHUNK, 8, LANES), F32)],
        compiler_params=pltpu.CompilerParams(dimension_semantics=("parallel",),
                                             vmem_limit_bytes=DN_VMEM_LIMIT_BYTES),
        name="deltanet",
    )(a3, bg3, bgt, conv_w2, dn_norm_row)


SWA_PREP_ROWS = 256
SWA_UNITS = 2


def _swa_kernel(s_ref, tab_ref, o_ref, lse_ref, qs_ref, ks_ref, bias_ref, *dilated_scratch, dilation):
    t_len = s_ref.shape[1]
    d = dilation
    seq = t_len // d
    nblk = seq // SWA_BLOCK
    per = SWA_PREP_ROWS // d
    heads = range(SWA_HEADS)
    hsl = lambda h: slice(h * HEAD_DIM, (h + 1) * HEAD_DIM)
    if d > 1:
        vs_ref, stage_ref, ostage_ref = dilated_scratch
    mean_b = jnp.full((HEAD_DIM, HEAD_DIM), 1.0 / HEAD_DIM, BF16)
    src_lane = lax.broadcasted_iota(jnp.int32, (HEAD_DIM, HEAD_DIM), 0)
    dst_lane = lax.broadcasted_iota(jnp.int32, (HEAD_DIM, HEAD_DIM), 1)
    swap_b = (((dst_lane < ROPE_HALF) & (src_lane == dst_lane + ROPE_HALF))
              | ((dst_lane >= ROPE_HALF) & (dst_lane < ROPE_DIM) & (src_lane == dst_lane - ROPE_HALF))
              ).astype(BF16)

    def prep(n, carry):
        r0 = pl.multiple_of(n * SWA_PREP_ROWS, SWA_PREP_ROWS)
        rows = pl.ds(r0, SWA_PREP_ROWS)
        i0 = pl.multiple_of(n * per, per)

        def regroup(ys, dst, slab0):
            for h, y in enumerate(ys):
                stage_ref[slab0 + h] = y
            for h in heads:
                for r in range(d):
                    part = stage_ref[slab0 + h, pl.ds(r, per, stride=d), :]
                    dst[pl.ds(r * seq + i0, per), hsl(h)] = part.astype(BF16)

        for which, dst in ((0, qs_ref), (1, ks_ref)):
            plain = tab_ref[which, 0, rows, :]
            swapped_tab = tab_ref[which, 1, rows, :]
            xb = [s_ref[0, rows, which * SWA_WIDTH + h * HEAD_DIM:which * SWA_WIDTH + (h + 1) * HEAD_DIM]
                  for h in heads]
            ms = [_mxu(x * x, mean_b) for x in xb]
            sw = [_mxu(x, swap_b) for x in xb]
            ys = [(x.astype(F32) * plain + s * swapped_tab) * lax.rsqrt(m + EPS) for x, s, m in zip(xb, sw, ms)]
            if d == 1:
                for h in heads:
                    dst[rows, hsl(h)] = ys[h].astype(BF16)
            else:
                regroup(ys, dst, which * SWA_HEADS)
        if d > 1:
            vs = [s_ref[0, rows, 2 * SWA_WIDTH + h * HEAD_DIM:2 * SWA_WIDTH + (h + 1) * HEAD_DIM].astype(F32)
                  for h in heads]
            regroup(vs, vs_ref, 2 * SWA_HEADS)
        return carry

    lax.fori_loop(0, t_len // SWA_PREP_ROWS, prep, 0)

    qi = lax.broadcasted_iota(jnp.int32, (SWA_BLOCK, 2 * SWA_BLOCK), 0)
    kj = lax.broadcasted_iota(jnp.int32, (SWA_BLOCK, 2 * SWA_BLOCK), 1)
    bias_ref[0] = jnp.where(kj <= qi, 0.0, NEG_INF).astype(F32)
    bias_ref[1] = jnp.where((kj >= qi) & (kj <= qi + SWA_BLOCK), 0.0, NEG_INF).astype(F32)
    lane = lax.broadcasted_iota(jnp.int32, (SWA_BLOCK, LANES), 1)

    def attend(units):
        jobs = [(u, h) for u in range(len(units)) for h in heads]
        qrows = [pl.ds(q0, SWA_BLOCK) for q0, _, _, _, _ in units]
        krows = [pl.ds(k0, width) for _, k0, width, _, _ in units]
        s = [lax.dot_general(qs_ref[qrows[u], hsl(h)], ks_ref[krows[u], hsl(h)], _NT,
                             preferred_element_type=F32) + units[u][3] for u, h in jobs]
        m = [jnp.max(x, axis=-1, keepdims=True) for x in s]
        p = [jnp.exp(x - mx) for x, mx in zip(s, m)]
        l = [jnp.sum(x, axis=-1, keepdims=True) for x in p]
        if d == 1:
            v = [s_ref[0, krows[u], 2 * SWA_WIDTH + h * HEAD_DIM:2 * SWA_WIDTH + (h + 1) * HEAD_DIM] for u, h in jobs]
        else:
            v = [vs_ref[krows[u], hsl(h)] for u, h in jobs]
        o = [_mxu(x.astype(BF16), vh) * (1.0 / lx) for x, vh, lx in zip(p, v, l)]
        for u in range(len(units)):
            lse_tile = jnp.zeros((SWA_BLOCK, LANES), F32)
            for h in heads:
                j = u * SWA_HEADS + h
                lse_tile = jnp.where(lane == h, m[j] + jnp.log(l[j]), lse_tile)
            t0 = units[u][4]
            if d == 1:
                for h in heads:
                    o_ref[0, qrows[u], hsl(h)] = o[u * SWA_HEADS + h].astype(o_ref.dtype)
                lse_ref[0, qrows[u], :] = lse_tile
            else:
                trows = pl.ds(t0, SWA_BLOCK, stride=d)
                for h in heads:
                    ostage_ref[h, trows, :] = o[u * SWA_HEADS + h]
                lse_ref[0, trows, :] = lse_tile

    if nblk == 1:
        causal_bias = bias_ref[0][:, :SWA_BLOCK]

        def body(i, carry):
            units = []
            for u in range(SWA_UNITS):
                r = i * SWA_UNITS + u
                q0 = pl.multiple_of(r * seq, SWA_BLOCK)
                units.append((q0, q0, SWA_BLOCK, causal_bias, r))
            attend(units)
            return carry

        lax.fori_loop(0, d // SWA_UNITS, body, 0)
    else:
        pairs = nblk // SWA_UNITS

        def body(i, carry):
            r = i // pairs
            pp = i % pairs
            units = []
            for u in range(SWA_UNITS):
                n = pp * SWA_UNITS + u
                q0 = pl.multiple_of(r * seq + n * SWA_BLOCK, SWA_BLOCK)
                if u == 0:
                    first = (pp == 0).astype(jnp.int32)
                    k0 = pl.multiple_of(q0 - (1 - first) * SWA_BLOCK, SWA_BLOCK)
                    bias = bias_ref[1 - first]
                else:
                    k0 = pl.multiple_of(q0 - SWA_BLOCK, SWA_BLOCK)
                    bias = bias_ref[1]
                units.append((q0, k0, 2 * SWA_BLOCK, bias, r + d * n * SWA_BLOCK))
            attend(units)
            return carry

        lax.fori_loop(0, d * pairs, body, 0)

    if d > 1:
        def emit(n, carry):
            rows = pl.ds(pl.multiple_of(n * SWA_PREP_ROWS, SWA_PREP_ROWS), SWA_PREP_ROWS)
            for h in heads:
                o_ref[0, rows, hsl(h)] = ostage_ref[h, rows, :].astype(o_ref.dtype)
            return carry
        lax.fori_loop(0, t_len // SWA_PREP_ROWS, emit, 0)


def _swa_group(s3, tabs, group, dilation):
    b, t, ws = s3.shape
    assert (t // dilation) % SWA_BLOCK == 0 and SWA_PREP_ROWS % dilation == 0
    nblk = t // dilation // SWA_BLOCK
    assert (dilation % SWA_UNITS == 0) if nblk == 1 else (nblk % SWA_UNITS == 0)
    scratch = [pltpu.VMEM((t, SWA_WIDTH), BF16), pltpu.VMEM((t, SWA_WIDTH), BF16),
               pltpu.VMEM((2, SWA_BLOCK, 2 * SWA_BLOCK), F32)]
    if dilation > 1:
        scratch += [pltpu.VMEM((t, SWA_WIDTH), BF16),
                    pltpu.VMEM((3 * SWA_HEADS, SWA_PREP_ROWS, LANES), F32),
                    pltpu.VMEM((SWA_HEADS, t, LANES), F32)]
    return pl.pallas_call(
        functools.partial(_swa_kernel, dilation=dilation),
        grid=(b,),
        in_specs=[pl.BlockSpec((1, t, ws), lambda i: (i, 0, 0)),
                  pl.BlockSpec((None, 2, 2, t, LANES), lambda i: (group, 0, 0, 0, 0))],
        out_specs=[pl.BlockSpec((1, t, SWA_WIDTH), lambda i: (i, 0, 0)),
                   pl.BlockSpec((1, t, LANES), lambda i: (i, 0, 0))],
        out_shape=[jax.ShapeDtypeStruct((b, t, SWA_WIDTH), BF16),
                   jax.ShapeDtypeStruct((b, t, LANES), F32)],
        scratch_shapes=scratch,
        compiler_params=_params(1),
        name=f"swa_d{dilation}",
    )(s3, tabs)


def _merge_out_kernel(x_ref, odn_ref, o0_ref, o1_ref, o2_ref, l0_ref, l1_ref, l2_ref, zg_ref,
                      wdn_ref, wswa_ref, wout_ref, out_ref):
    l0 = l0_ref[...]
    l1 = l1_ref[...]
    l2 = l2_ref[...]
    m = jnp.maximum(jnp.maximum(l0, l1), l2)
    e0 = jnp.exp(l0 - m)
    e1 = jnp.exp(l1 - m)
    e2 = jnp.exp(l2 - m)
    inv = 1.0 / (e0 + e1 + e2)
    parts = []
    for h in range(SWA_HEADS):
        hs = slice(h * HEAD_DIM, (h + 1) * HEAD_DIM)
        col = slice(h, h + 1)
        oh = ((e0[:, col] * inv[:, col]) * o0_ref[:, hs].astype(F32)
              + (e1[:, col] * inv[:, col]) * o1_ref[:, hs].astype(F32)
              + (e2[:, col] * inv[:, col]) * o2_ref[:, hs].astype(F32))
        parts.append((oh * _silu(zg_ref[:, hs].astype(F32))).astype(BF16))
    o_swa = jnp.concatenate(parts, axis=1)
    y_swa = _mxu(o_swa, wswa_ref[...])
    y_dn = _mxu(odn_ref[...], wdn_ref[...])
    g_dn = zg_ref[:, SWA_WIDTH:SWA_WIDTH + D_MODEL].astype(F32)
    g_swa = zg_ref[:, SWA_WIDTH + D_MODEL:SWA_WIDTH + 2 * D_MODEL].astype(F32)
    merged = _sigmoid(g_dn) * y_dn + _sigmoid(g_swa) * y_swa
    out_ref[...] = x_ref[...] + _mxu(merged.astype(BF16), wout_ref[...])


def _merge_out(x2, odn, o_list, lse_list, zg, w_dn, w_swa, w_out):
    n = x2.shape[0]
    row = lambda w: pl.BlockSpec((ROW_TILE, w), lambda i: (i, 0))
    full = lambda a: pl.BlockSpec(a.shape, lambda i: (0, 0))
    return pl.pallas_call(
        _merge_out_kernel,
        grid=(n // ROW_TILE,),
        in_specs=[row(D_MODEL), row(DN_WIDTH), row(SWA_WIDTH), row(SWA_WIDTH), row(SWA_WIDTH),
                  row(LANES), row(LANES), row(LANES), row(zg.shape[1]),
                  full(w_dn), full(w_swa), full(w_out)],
        out_specs=row(D_MODEL),
        out_shape=jax.ShapeDtypeStruct((n, D_MODEL), F32),
        compiler_params=_params(1),
        name="merge_out",
    )(x2, odn, *o_list, *lse_list, zg, w_dn, w_swa, w_out)


def _rope_tables(t_len, q_norm_w, k_norm_w):
    pos = jnp.arange(t_len, dtype=F32)
    inv_freq = ROPE_THETA ** (-jnp.arange(0, ROPE_DIM, 2, dtype=F32) / ROPE_DIM)
    ang = pos[:, None] * inv_freq[None, :]
    cos, sin = jnp.cos(ang), jnp.sin(ang)
    tail = HEAD_DIM - ROPE_DIM
    cos_t = jnp.concatenate([cos, cos, jnp.ones((t_len, tail), F32)], axis=1)
    sin_t = jnp.concatenate([-sin, sin, jnp.zeros((t_len, tail), F32)], axis=1)
    w = jnp.stack([q_norm_w.astype(F32) * (HEAD_DIM ** -0.5), k_norm_w.astype(F32)], axis=1)
    w_swapped = jnp.concatenate([w[..., ROPE_HALF:ROPE_DIM], w[..., :ROPE_HALF], w[..., ROPE_DIM:]], axis=-1)
    plain = w[:, :, None, :] * cos_t[None, None]
    swapped = w_swapped[:, :, None, :] * sin_t[None, None]
    return jnp.stack([plain, swapped], axis=2)


def kernel(x, norm_w, w_in, conv_w, dn_a_log, dn_dt_bias, dn_norm_w, q_norm_w, k_norm_w,
           w_branch_dn, w_branch_swa, w_out):
    b, t, d = x.shape
    n = b * t
    layer = 0
    w = w_in[layer]
    c_qkv = 3 * DN_WIDTH
    c_z = c_qkv + DN_WIDTH
    c_ab = c_z + 2 * DN_HEADS
    c_q = c_ab
    c_k = c_q + N_GROUPS * SWA_WIDTH
    c_v = c_k + N_GROUPS * SWA_WIDTH
    c_sz = c_v + N_GROUPS * SWA_WIDTH

    w_a = w[:, :c_z].astype(BF16)
    w_ab = jnp.pad(w[:, c_z:c_ab], ((0, 0), (0, LANES - 2 * DN_HEADS))).astype(BF16)
    w_abt = jnp.pad(w[:, c_z:c_ab].T, ((0, BG_ROWS - 2 * DN_HEADS), (0, 0))).astype(BF16)
    w_grp = [jnp.concatenate([w[:, c_q + g * SWA_WIDTH:c_q + (g + 1) * SWA_WIDTH],
                              w[:, c_k + g * SWA_WIDTH:c_k + (g + 1) * SWA_WIDTH],
                              w[:, c_v + g * SWA_WIDTH:c_v + (g + 1) * SWA_WIDTH]], axis=1).astype(BF16)
             for g in range(N_GROUPS)]
    w_zg = w[:, c_sz:].astype(BF16)
    pad_heads = lambda v: jnp.pad(v.astype(F32), (DN_HEADS, LANES - 2 * DN_HEADS))[None, :]
    alog_row = pad_heads(dn_a_log[layer])
    dt_row = pad_heads(dn_dt_bias[layer])

    x2 = x.reshape(n, d)
    h, a, bg, bgt = _norm_proj(x2, norm_w[layer][None, :], w_a, w_ab, w_abt, alog_row, dt_row)

    o_dn = _deltanet(a.reshape(b, t, 4 * DN_WIDTH), bg.reshape(b, t, LANES), bgt,
                     conv_w[layer][:, 0, :], dn_norm_w[layer][None, :])

    tabs = _rope_tables(t, q_norm_w[layer], k_norm_w[layer])
    o_list, lse_list = [], []
    for g, (window, dilation) in enumerate(SWA_GROUPS):
        assert window // dilation == SWA_BLOCK
        s_g = _matmul(h, w_grp[g], f"proj_swa{g}").reshape(b, t, 3 * SWA_WIDTH)
        o_g, lse_g = _swa_group(s_g, tabs, g, dilation)
        o_list.append(o_g.reshape(n, SWA_WIDTH))
        lse_list.append(lse_g.reshape(n, LANES))

    zg = _matmul(h, w_zg, "proj_zg")
    out = _merge_out(x2, o_dn.reshape(n, DN_WIDTH), o_list, lse_list, zg,
                     w_branch_dn[layer].astype(BF16), w_branch_swa[layer].astype(BF16),
                     w_out[layer].astype(BF16))
    return out.reshape(b, t, d)
```

```python
import functools
import math

import jax
import jax.numpy as jnp
import numpy as np
from jax import lax
from jax.experimental import pallas as pl
from jax.experimental.pallas import tpu as pltpu

D_MODEL = 1024
HEAD_DIM = 128
DN_HEADS = 4
DN_WIDTH = DN_HEADS * HEAD_DIM
CONV_WIDTH = 4
CHUNK = 64
SWA_GROUPS = ((128, 1), (512, 4), (2048, 16))
N_GROUPS = 3
SWA_HEADS = 4
SWA_WIDTH = SWA_HEADS * HEAD_DIM
SWA_BLOCK = 128
ROPE_DIM = HEAD_DIM // 4
ROPE_HALF = ROPE_DIM // 2
ROPE_THETA = 500000.0
EPS = 1e-6
NEG_INF = -1e30

LANES = 128
VMEM_LIMIT_BYTES = 48 * 1024 * 1024
DN_VMEM_LIMIT_BYTES = 56 * 1024 * 1024
ROW_TILE = 512
BG_ROWS = 16

F32 = jnp.float32
BF16 = jnp.bfloat16
_NT = (((1,), (1,)), ((), ()))
_TN = (((0,), (0,)), ((), ()))


def _sigmoid(x):
    return 0.5 * jnp.tanh(0.5 * x) + 0.5


def _silu(x):
    return x * _sigmoid(x)


def _softplus(x):
    return jnp.maximum(x, 0.0) + jnp.log(1.0 + jnp.exp(-jnp.abs(x)))


def _params(n_axes):
    return pltpu.CompilerParams(dimension_semantics=("parallel",) * n_axes,
                                vmem_limit_bytes=VMEM_LIMIT_BYTES)


def _mxu(a, b):
    return jnp.dot(a, b, preferred_element_type=F32)


def _split3(x):
    hi = x.astype(BF16)
    r1 = x - hi.astype(F32)
    mid = r1.astype(BF16)
    lo = (r1 - mid.astype(F32)).astype(BF16)
    return hi, mid, lo


def _beta_and_log_decay(ab, a_log, dt_bias, head_index):
    beta = _sigmoid(ab)
    g = -jnp.exp(a_log) * _softplus(ab + dt_bias)
    return jnp.where(head_index < DN_HEADS, beta, g)


def _norm_proj_kernel(x_ref, nw_ref, wa_ref, wab_ref, wabt_ref, alog_ref, dt_ref, alogt_ref, dtt_ref,
                      h_ref, a_ref, bg_ref, bgt_ref):
    x = x_ref[...]
    h = (x * lax.rsqrt(jnp.mean(x * x, axis=-1, keepdims=True) + EPS)) * nw_ref[...]
    hb = h.astype(BF16)
    h_ref[...] = hb
    a_ref[...] = _mxu(hb, wa_ref[...]).astype(BF16)
    ab = _mxu(hb, wab_ref[...])
    bg_ref[...] = _beta_and_log_decay(ab, alog_ref[...], dt_ref[...],
                                      lax.broadcasted_iota(jnp.int32, ab.shape, 1))
    abt = lax.dot_general(wabt_ref[...], hb, _NT, preferred_element_type=F32)
    bgt_ref[...] = _beta_and_log_decay(abt, alogt_ref[...], dtt_ref[...],
                                       lax.broadcasted_iota(jnp.int32, abt.shape, 0))


def _norm_proj(x2, norm_w, w_a, w_ab, w_abt, alog_row, dt_row):
    n = x2.shape[0]
    na = w_a.shape[1]
    nt = w_abt.shape[0]
    row = lambda i: (i, 0)
    fixed = lambda i: (0, 0)
    pad_col = lambda v: v[0, :nt][:, None]
    return pl.pallas_call(
        _norm_proj_kernel,
        grid=(n // ROW_TILE,),
        in_specs=[pl.BlockSpec((ROW_TILE, D_MODEL), row),
                  pl.BlockSpec((1, D_MODEL), fixed),
                  pl.BlockSpec((D_MODEL, na), fixed),
                  pl.BlockSpec((D_MODEL, LANES), fixed),
                  pl.BlockSpec((nt, D_MODEL), fixed),
                  pl.BlockSpec((1, LANES), fixed),
                  pl.BlockSpec((1, LANES), fixed),
                  pl.BlockSpec((nt, 1), fixed),
                  pl.BlockSpec((nt, 1), fixed)],
        out_specs=[pl.BlockSpec((ROW_TILE, D_MODEL), row),
                   pl.BlockSpec((ROW_TILE, na), row),
                   pl.BlockSpec((ROW_TILE, LANES), row),
                   pl.BlockSpec((nt, ROW_TILE), lambda i: (0, i))],
        out_shape=[jax.ShapeDtypeStruct((n, D_MODEL), BF16),
                   jax.ShapeDtypeStruct((n, na), BF16),
                   jax.ShapeDtypeStruct((n, LANES), F32),
                   jax.ShapeDtypeStruct((nt, n), F32)],
        compiler_params=_params(1),
        name="norm_proj",
    )(x2, norm_w, w_a, w_ab, w_abt, alog_row, dt_row, pad_col(alog_row), pad_col(dt_row))


def _matmul_kernel(h_ref, *refs):
    w_refs, o_ref = refs[:-1], refs[-1]
    col = 0
    for w_ref in w_refs:
        width = w_ref.shape[1]
        o_ref[:, col:col + width] = _mxu(h_ref[...], w_ref[...]).astype(o_ref.dtype)
        col += width


def _matmul(h, w, col_blocks, name):
    n, k = h.shape
    nc = sum(width for width, _ in col_blocks)
    w_specs = [pl.BlockSpec((k, width), lambda i, j=j: (0, j)) for width, j in col_blocks]
    return pl.pallas_call(
        _matmul_kernel,
        grid=(n // ROW_TILE,),
        in_specs=[pl.BlockSpec((ROW_TILE, k), lambda i: (i, 0))] + w_specs,
        out_specs=pl.BlockSpec((ROW_TILE, nc), lambda i: (i, 0)),
        out_shape=jax.ShapeDtypeStruct((n, nc), BF16),
        compiler_params=_params(1),
        name=name,
    )(h, *([w] * len(col_blocks)))


DN_GROUP = 4
DN_ROWS = DN_GROUP * CHUNK
DN_HALO = 16


def _deltanet_kernel(a_ref, bg_ref, bgt_ref, cw_ref, nw_ref, o_ref,
                     u_ref, wq_ref, kd_ref, ai_ref, eg_ref):
    t_len = a_ref.shape[1]
    n_groups = t_len // DN_ROWS
    ii = lax.broadcasted_iota(jnp.int32, (CHUNK, CHUNK), 0)
    jj = lax.broadcasted_iota(jnp.int32, (CHUNK, CHUNK), 1)
    causal = ii >= jj
    strict = ii > jj
    eye_f = (ii == jj).astype(F32)
    bi = lax.broadcasted_iota(jnp.int32, (DN_ROWS, DN_ROWS), 0)
    bj = lax.broadcasted_iota(jnp.int32, (DN_ROWS, DN_ROWS), 1)
    shift = int(math.log2(CHUNK))
    same_chunk = jnp.right_shift(bi, shift) == jnp.right_shift(bj, shift)
    cum_lower = (same_chunk & (bi >= bj)).astype(BF16)
    cum_upper = (same_chunk & (bi <= bj)).astype(BF16)
    problems = [(c, h) for c in range(DN_GROUP) for h in range(DN_HEADS)]
    heads = range(DN_HEADS)
    hsl = lambda h: slice(h * HEAD_DIM, (h + 1) * HEAD_DIM)
    cs = lambda c: slice(c * CHUNK, (c + 1) * CHUNK)
    ones_b = jnp.ones((HEAD_DIM, HEAD_DIM), BF16)
    sel_src = lax.broadcasted_iota(jnp.int32, (LANES, DN_WIDTH), 0)
    sel_head = jnp.right_shift(lax.broadcasted_iota(jnp.int32, (LANES, DN_WIDTH), 1), int(math.log2(HEAD_DIM)))
    sel_beta = (sel_src == sel_head).astype(BF16)
    sel_g = (sel_src == sel_head + DN_HEADS).astype(BF16)
    lane = lax.broadcasted_iota(jnp.int32, (DN_ROWS, LANES), 1)
    g_lanes = (lane >= DN_HEADS) & (lane < 2 * DN_HEADS)

    pending = []

    def defer(ref, idx, value):
        pending.append((ref, idx, value))

    def flush():
        for ref, idx, value in pending:
            ref[idx] = value
        pending.clear()

    def conv_silu(gi, r0, col0):
        cols = slice(col0, col0 + HEAD_DIM)
        rp = pl.multiple_of(jnp.maximum(r0 - DN_HALO, 0), DN_HALO)
        cur = a_ref[0, pl.ds(r0, DN_ROWS), cols].astype(F32)
        prev = a_ref[0, pl.ds(rp, DN_HALO), cols].astype(F32)
        prev = jnp.where(gi > 0, prev, 0.0)
        xw = jnp.concatenate([prev, cur], axis=0)
        conv = None
        for j in range(CONV_WIDTH):
            lo = DN_HALO - (CONV_WIDTH - 1) + j
            term = cw_ref[j:j + 1, cols] * xw[lo:lo + DN_ROWS, :]
            conv = term if conv is None else conv + term
        return _silu(conv)

    def prepare(gi):
        r0 = pl.multiple_of(gi * DN_ROWS, DN_ROWS)
        bg = bg_ref[0, pl.ds(r0, DN_ROWS), :]
        gc_all = sum(_mxu(cum_lower, piece) for piece in _split3(bg))
        gct_all = sum(_mxu(piece, cum_upper) for piece in _split3(bgt_ref[:, pl.ds(r0, DN_ROWS)]))
        gl_all = jnp.concatenate([jnp.broadcast_to(gc_all[(c + 1) * CHUNK - 1:(c + 1) * CHUNK, :], (CHUNK, LANES))
                                  for c in range(DN_GROUP)], axis=0)
        eg_all = jnp.exp(gl_all)
        for c in range(DN_GROUP):
            defer(eg_ref, (gi * DN_GROUP + c,), eg_all[c * CHUNK:c * CHUNK + 8, :])
        gc_m = jnp.where(g_lanes, gc_all, 0.0)
        gl_m = jnp.where(g_lanes, gl_all, 0.0)
        beta_rep = _mxu(bg.astype(BF16), sel_beta)
        egc_rep = _mxu(jnp.exp(gc_m).astype(BF16), sel_g)
        ekd_rep = _mxu(jnp.exp(gl_m - gc_m).astype(BF16), sel_g)

        qn, knb, kb, rhs = [], [], [], []
        for h in heads:
            qh = conv_silu(gi, r0, h * HEAD_DIM)
            kh = conv_silu(gi, r0, DN_WIDTH + h * HEAD_DIM)
            vh = conv_silu(gi, r0, 2 * DN_WIDTH + h * HEAD_DIM)
            qn_h = qh * (lax.rsqrt(_mxu((qh * qh).astype(BF16), ones_b) + EPS) * (HEAD_DIM ** -0.5))
            kn_h = kh * lax.rsqrt(_mxu((kh * kh).astype(BF16), ones_b) + EPS)
            kb_h = kn_h * beta_rep[:, hsl(h)]
            defer(kd_ref, (h, pl.ds(r0, DN_ROWS), slice(None)), (kn_h * ekd_rep[:, hsl(h)]).astype(BF16))
            qg_h = (qn_h * egc_rep[:, hsl(h)]).astype(BF16)
            for c in range(DN_GROUP):
                w0 = pl.multiple_of(2 * (r0 + c * CHUNK), 2 * CHUNK)
                defer(wq_ref, (h, pl.ds(w0 + CHUNK, CHUNK), slice(None)), qg_h[cs(c)])
            qn.append(qn_h)
            knb.append(kn_h.astype(BF16))
            kb.append(kb_h)
            rhs.append(jnp.concatenate([vh * beta_rep[:, hsl(h)], kb_h * egc_rep[:, hsl(h)]], axis=1).astype(BF16))
            yield

        qk = [lax.dot_general(jnp.concatenate([qn[h][cs(c)], kb[h][cs(c)]], axis=0).astype(BF16),
                              knb[h][cs(c)], _NT, preferred_element_type=F32) for c, h in problems]
        decay = []
        for c, h in problems:
            diff = gc_all[cs(c), DN_HEADS + h:DN_HEADS + h + 1] - gct_all[DN_HEADS + h:DN_HEADS + h + 1, cs(c)]
            decay.append(jnp.where(causal, jnp.exp(jnp.where(causal, diff, 0.0)), 0.0))
        for i, (c, h) in enumerate(problems):
            a_intra = jnp.where(causal, qk[i][:CHUNK] * decay[i], 0.0)
            defer(ai_ref, (h, pl.ds(r0 + c * CHUNK, CHUNK), slice(None)), a_intra.astype(BF16))
        ps = [-jnp.where(strict, qk[i][CHUNK:] * decay[i], 0.0) for i in range(len(problems))]
        tms = [eye_f + p for p in ps]
        for _ in range(shift - 1):
            pbs = [p.astype(BF16) for p in ps]
            ps = [_mxu(pb, pb) for pb in pbs]
            tms = [tm + _mxu(tm.astype(BF16), p.astype(BF16)) for tm, p in zip(tms, ps)]
        uw = [_mxu(tms[i].astype(BF16), rhs[h][cs(c)]) for i, (c, h) in enumerate(problems)]
        for i, (c, h) in enumerate(problems):
            defer(u_ref, (h, pl.ds(r0 + c * CHUNK, CHUNK), slice(None)), uw[i][:, :HEAD_DIM])
            w0 = pl.multiple_of(2 * (r0 + c * CHUNK), 2 * CHUNK)
            defer(wq_ref, (h, pl.ds(w0, CHUNK), slice(None)), uw[i][:, HEAD_DIM:].astype(BF16))

    def recur(c, state):
        r0 = pl.multiple_of(c * CHUNK, CHUNK)
        rows = pl.ds(r0, CHUNK)
        wrows = pl.ds(pl.multiple_of(2 * r0, 2 * CHUNK), 2 * CHUNK)
        eg = eg_ref[c]
        ws = [_mxu(wq_ref[h, wrows, :], state[h].astype(BF16)) for h in heads]
        vn = [(u_ref[h, rows, :] - ws[h][:CHUNK]).astype(BF16) for h in heads]
        new_state = [state[h] * eg[0:1, DN_HEADS + h:DN_HEADS + h + 1]
                     + lax.dot_general(kd_ref[h, rows, :], vn[h], _TN, preferred_element_type=F32) for h in heads]
        o = [ws[h][CHUNK:] + _mxu(ai_ref[h, rows, :], vn[h]) for h in heads]
        ms = [_mxu((x * x).astype(BF16), ones_b) * (1.0 / HEAD_DIM) for x in o]
        for h in heads:
            on = o[h] * lax.rsqrt(ms[h] + EPS) * nw_ref[...]
            z = a_ref[0, rows, 3 * DN_WIDTH + h * HEAD_DIM:3 * DN_WIDTH + (h + 1) * HEAD_DIM].astype(F32)
            defer(o_ref, (0, rows, hsl(h)), (on * _silu(z)).astype(o_ref.dtype))
        return new_state

    for _ in prepare(0):
        pass
    flush()

    def pipelined(gi, state):
        state = list(state)
        stages = prepare(gi)
        for c in range(DN_GROUP):
            state = recur((gi - 1) * DN_GROUP + c, state)
            next(stages)
        for _ in stages:
            pass
        flush()
        return tuple(state)

    zero = jnp.zeros((HEAD_DIM, HEAD_DIM), F32)
    state = list(lax.fori_loop(1, n_groups, pipelined, (zero,) * DN_HEADS))
    for c in range(DN_GROUP):
        state = recur((n_groups - 1) * DN_GROUP + c, state)
    flush()


def _deltanet(a3, bg3, bgt, conv_w2, dn_norm_row):
    b, t, wa = a3.shape
    nt = bgt.shape[0]
    return pl.pallas_call(
        _deltanet_kernel,
        grid=(b,),
        in_specs=[pl.BlockSpec((1, t, wa), lambda i: (i, 0, 0)),
                  pl.BlockSpec((1, t, LANES), lambda i: (i, 0, 0)),
                  pl.BlockSpec((nt, t), lambda i: (0, i)),
                  pl.BlockSpec((CONV_WIDTH, 3 * DN_WIDTH), lambda i: (0, 0)),
                  pl.BlockSpec((1, HEAD_DIM), lambda i: (0, 0))],
        out_specs=pl.BlockSpec((1, t, DN_WIDTH), lambda i: (i, 0, 0)),
        out_shape=jax.ShapeDtypeStruct((b, t, DN_WIDTH), BF16),
        scratch_shapes=[pltpu.VMEM((DN_HEADS, t, HEAD_DIM), F32),
                        pltpu.VMEM((DN_HEADS, 2 * t, HEAD_DIM), BF16),
                        pltpu.VMEM((DN_HEADS, t, HEAD_DIM), BF16),
                        pltpu.VMEM((DN_HEADS, t, CHUNK), BF16),
                        pltpu.VMEM((t // CHUNK, 8, LANES), F32)],
        compiler_params=pltpu.CompilerParams(dimension_semantics=("parallel",),
                                             vmem_limit_bytes=DN_VMEM_LIMIT_BYTES),
        name="deltanet",
    )(a3, bg3, bgt, conv_w2, dn_norm_row)


SWA_PREP_ROWS = 256
SWA_UNITS = 2


def _swa_kernel(s_ref, tab_ref, o_ref, lse_ref, qs_ref, ks_ref, bias_ref, *dilated_scratch, dilation):
    t_len = s_ref.shape[1]
    d = dilation
    seq = t_len // d
    nblk = seq // SWA_BLOCK
    per = SWA_PREP_ROWS // d
    heads = range(SWA_HEADS)
    hsl = lambda h: slice(h * HEAD_DIM, (h + 1) * HEAD_DIM)
    if d > 1:
        vs_ref, stage_ref, ostage_ref = dilated_scratch
    mean_b = jnp.full((HEAD_DIM, HEAD_DIM), 1.0 / HEAD_DIM, BF16)
    src_lane = lax.broadcasted_iota(jnp.int32, (HEAD_DIM, HEAD_DIM), 0)
    dst_lane = lax.broadcasted_iota(jnp.int32, (HEAD_DIM, HEAD_DIM), 1)
    swap_b = (((dst_lane < ROPE_HALF) & (src_lane == dst_lane + ROPE_HALF))
              | ((dst_lane >= ROPE_HALF) & (dst_lane < ROPE_DIM) & (src_lane == dst_lane - ROPE_HALF))
              ).astype(BF16)

    def prep(n, carry):
        r0 = pl.multiple_of(n * SWA_PREP_ROWS, SWA_PREP_ROWS)
        rows = pl.ds(r0, SWA_PREP_ROWS)
        i0 = pl.multiple_of(n * per, per)

        def regroup(ys, dst, slab0):
            for h, y in enumerate(ys):
                stage_ref[slab0 + h] = y
            for h in heads:
                for r in range(d):
                    part = stage_ref[slab0 + h, pl.ds(r, per, stride=d), :]
                    dst[pl.ds(r * seq + i0, per), hsl(h)] = part.astype(BF16)

        for which, dst in ((0, qs_ref), (1, ks_ref)):
            plain = tab_ref[which, 0, rows, :]
            swapped_tab = tab_ref[which, 1, rows, :]
            xb = [s_ref[0, rows, which * SWA_WIDTH + h * HEAD_DIM:which * SWA_WIDTH + (h + 1) * HEAD_DIM]
                  for h in heads]
            ms = [_mxu(x * x, mean_b) for x in xb]
            sw = [_mxu(x, swap_b) for x in xb]
            ys = [(x.astype(F32) * plain + s * swapped_tab) * lax.rsqrt(m + EPS) for x, s, m in zip(xb, sw, ms)]
            if d == 1:
                for h in heads:
                    dst[rows, hsl(h)] = ys[h].astype(BF16)
            else:
                regroup(ys, dst, which * SWA_HEADS)
        if d > 1:
            vs = [s_ref[0, rows, 2 * SWA_WIDTH + h * HEAD_DIM:2 * SWA_WIDTH + (h + 1) * HEAD_DIM].astype(F32)
                  for h in heads]
            regroup(vs, vs_ref, 2 * SWA_HEADS)
        return carry

    lax.fori_loop(0, t_len // SWA_PREP_ROWS, prep, 0)

    qi = lax.broadcasted_iota(jnp.int32, (SWA_BLOCK, 2 * SWA_BLOCK), 0)
    kj = lax.broadcasted_iota(jnp.int32, (SWA_BLOCK, 2 * SWA_BLOCK), 1)
    bias_ref[0] = jnp.where(kj <= qi, 0.0, NEG_INF).astype(F32)
    bias_ref[1] = jnp.where((kj >= qi) & (kj <= qi + SWA_BLOCK), 0.0, NEG_INF).astype(F32)
    lane = lax.broadcasted_iota(jnp.int32, (SWA_BLOCK, LANES), 1)

    def attend(units):
        jobs = [(u, h) for u in range(len(units)) for h in heads]
        qrows = [pl.ds(q0, SWA_BLOCK) for q0, _, _, _, _ in units]
        krows = [pl.ds(k0, width) for _, k0, width, _, _ in units]
        s = [lax.dot_general(qs_ref[qrows[u], hsl(h)], ks_ref[krows[u], hsl(h)], _NT,
                             preferred_element_type=F32) + units[u][3] for u, h in jobs]
        m = [jnp.max(x, axis=-1, keepdims=True) for x in s]
        p = [jnp.exp(x - mx) for x, mx in zip(s, m)]
        l = [jnp.sum(x, axis=-1, keepdims=True) for x in p]
        if d == 1:
            v = [s_ref[0, krows[u], 2 * SWA_WIDTH + h * HEAD_DIM:2 * SWA_WIDTH + (h + 1) * HEAD_DIM] for u, h in jobs]
        else:
            v = [vs_ref[krows[u], hsl(h)] for u, h in jobs]
        o = [_mxu(x.astype(BF16), vh) * (1.0 / lx) for x, vh, lx in zip(p, v, l)]
        for u in range(len(units)):
            lse_tile = jnp.zeros((SWA_BLOCK, LANES), F32)
            for h in heads:
                j = u * SWA_HEADS + h
                lse_tile = jnp.where(lane == h, m[j] + jnp.log(l[j]), lse_tile)
            t0 = units[u][4]
            if d == 1:
                for h in heads:
                    o_ref[0, qrows[u], hsl(h)] = o[u * SWA_HEADS + h].astype(o_ref.dtype)
                lse_ref[0, qrows[u], :] = lse_tile
            else:
                trows = pl.ds(t0, SWA_BLOCK, stride=d)
                for h in heads:
                    ostage_ref[h, trows, :] = o[u * SWA_HEADS + h]
                lse_ref[0, trows, :] = lse_tile

    if nblk == 1:
        causal_bias = bias_ref[0][:, :SWA_BLOCK]

        def body(i, carry):
            units = []
            for u in range(SWA_UNITS):
                r = i * SWA_UNITS + u
                q0 = pl.multiple_of(r * seq, SWA_BLOCK)
                units.append((q0, q0, SWA_BLOCK, causal_bias, r))
            attend(units)
            return carry

        lax.fori_loop(0, d // SWA_UNITS, body, 0)
    else:
        pairs = nblk // SWA_UNITS

        def body(i, carry):
            r = i // pairs
            pp = i % pairs
            units = []
            for u in range(SWA_UNITS):
                n = pp * SWA_UNITS + u
                q0 = pl.multiple_of(r * seq + n * SWA_BLOCK, SWA_BLOCK)
                if u == 0:
                    first = (pp == 0).astype(jnp.int32)
                    k0 = pl.multiple_of(q0 - (1 - first) * SWA_BLOCK, SWA_BLOCK)
                    bias = bias_ref[1 - first]
                else:
                    k0 = pl.multiple_of(q0 - SWA_BLOCK, SWA_BLOCK)
                    bias = bias_ref[1]
                units.append((q0, k0, 2 * SWA_BLOCK, bias, r + d * n * SWA_BLOCK))
            attend(units)
            return carry

        lax.fori_loop(0, d * pairs, body, 0)

    if d > 1:
        def emit(n, carry):
            rows = pl.ds(pl.multiple_of(n * SWA_PREP_ROWS, SWA_PREP_ROWS), SWA_PREP_ROWS)
            for h in heads:
                o_ref[0, rows, hsl(h)] = ostage_ref[h, rows, :].astype(o_ref.dtype)
            return carry
        lax.fori_loop(0, t_len // SWA_PREP_ROWS, emit, 0)


def _swa_group(s3, tabs, group, dilation):
    b, t, ws = s3.shape
    assert (t // dilation) % SWA_BLOCK == 0 and SWA_PREP_ROWS % dilation == 0
    nblk = t // dilation // SWA_BLOCK
    assert (dilation % SWA_UNITS == 0) if nblk == 1 else (nblk % SWA_UNITS == 0)
    scratch = [pltpu.VMEM((t, SWA_WIDTH), BF16), pltpu.VMEM((t, SWA_WIDTH), BF16),
               pltpu.VMEM((2, SWA_BLOCK, 2 * SWA_BLOCK), F32)]
    if dilation > 1:
        scratch += [pltpu.VMEM((t, SWA_WIDTH), BF16),
                    pltpu.VMEM((3 * SWA_HEADS, SWA_PREP_ROWS, LANES), F32),
                    pltpu.VMEM((SWA_HEADS, t, LANES), F32)]
    return pl.pallas_call(
        functools.partial(_swa_kernel, dilation=dilation),
        grid=(b,),
        in_specs=[pl.BlockSpec((1, t, ws), lambda i: (i, 0, 0)),
                  pl.BlockSpec((None, 2, 2, t, LANES), lambda i: (group, 0, 0, 0, 0))],
        out_specs=[pl.BlockSpec((1, t, SWA_WIDTH), lambda i: (i, 0, 0)),
                   pl.BlockSpec((1, t, LANES), lambda i: (i, 0, 0))],
        out_shape=[jax.ShapeDtypeStruct((b, t, SWA_WIDTH), BF16),
                   jax.ShapeDtypeStruct((b, t, LANES), F32)],
        scratch_shapes=scratch,
        compiler_params=_params(1),
        name=f"swa_d{dilation}",
    )(s3, tabs)


MERGE_ROWS = 128


def _merge_out_kernel(x_ref, odn_ref, o0_ref, o1_ref, o2_ref, l0_ref, l1_ref, l2_ref, zg_ref,
                      wdn_ref, wswa_ref, wout_ref, out_ref):
    outs = []
    for j in range(ROW_TILE // MERGE_ROWS):
        rows = slice(j * MERGE_ROWS, (j + 1) * MERGE_ROWS)
        l0 = l0_ref[rows, :]
        l1 = l1_ref[rows, :]
        l2 = l2_ref[rows, :]
        m = jnp.maximum(jnp.maximum(l0, l1), l2)
        e0 = jnp.exp(l0 - m)
        e1 = jnp.exp(l1 - m)
        e2 = jnp.exp(l2 - m)
        inv = 1.0 / (e0 + e1 + e2)
        a0, a1, a2 = e0 * inv, e1 * inv, e2 * inv
        parts = []
        for h in range(SWA_HEADS):
            hs = slice(h * HEAD_DIM, (h + 1) * HEAD_DIM)
            col = slice(h, h + 1)
            oh = (a0[:, col] * o0_ref[rows, hs].astype(F32) + a1[:, col] * o1_ref[rows, hs].astype(F32)
                  + a2[:, col] * o2_ref[rows, hs].astype(F32))
            parts.append((oh * _silu(zg_ref[rows, hs].astype(F32))).astype(BF16))
        o_swa = jnp.concatenate(parts, axis=1)
        y_swa = _mxu(o_swa, wswa_ref[...])
        y_dn = _mxu(odn_ref[rows, :], wdn_ref[...])
        g_dn = zg_ref[rows, SWA_WIDTH:SWA_WIDTH + D_MODEL].astype(F32)
        g_swa = zg_ref[rows, SWA_WIDTH + D_MODEL:SWA_WIDTH + 2 * D_MODEL].astype(F32)
        merged = _sigmoid(g_dn) * y_dn + _sigmoid(g_swa) * y_swa
        outs.append((rows, x_ref[rows, :] + _mxu(merged.astype(BF16), wout_ref[...])))
    for rows, value in outs:
        out_ref[rows, :] = value


def _merge_out(x2, odn, o_list, lse_list, zg, w_dn, w_swa, w_out):
    n = x2.shape[0]
    row = lambda w: pl.BlockSpec((ROW_TILE, w), lambda i: (i, 0))
    full = lambda a: pl.BlockSpec(a.shape, lambda i: (0, 0))
    return pl.pallas_call(
        _merge_out_kernel,
        grid=(n // ROW_TILE,),
        in_specs=[row(D_MODEL), row(DN_WIDTH), row(SWA_WIDTH), row(SWA_WIDTH), row(SWA_WIDTH),
                  row(LANES), row(LANES), row(LANES), row(zg.shape[1]),
                  full(w_dn), full(w_swa), full(w_out)],
        out_specs=row(D_MODEL),
        out_shape=jax.ShapeDtypeStruct((n, D_MODEL), F32),
        compiler_params=_params(1),
        name="merge_out",
    )(x2, odn, *o_list, *lse_list, zg, w_dn, w_swa, w_out)


def _rope_tables(t_len, q_norm_w, k_norm_w):
    pos = jnp.arange(t_len, dtype=F32)
    inv_freq = ROPE_THETA ** (-jnp.arange(0, ROPE_DIM, 2, dtype=F32) / ROPE_DIM)
    ang = pos[:, None] * inv_freq[None, :]
    cos, sin = jnp.cos(ang), jnp.sin(ang)
    tail = HEAD_DIM - ROPE_DIM
    cos_t = jnp.concatenate([cos, cos, jnp.ones((t_len, tail), F32)], axis=1)
    sin_t = jnp.concatenate([-sin, sin, jnp.zeros((t_len, tail), F32)], axis=1)
    w = jnp.stack([q_norm_w.astype(F32) * (HEAD_DIM ** -0.5), k_norm_w.astype(F32)], axis=1)
    w_swapped = jnp.concatenate([w[..., ROPE_HALF:ROPE_DIM], w[..., :ROPE_HALF], w[..., ROPE_DIM:]], axis=-1)
    plain = w[:, :, None, :] * cos_t[None, None]
    swapped = w_swapped[:, :, None, :] * sin_t[None, None]
    return jnp.stack([plain, swapped], axis=2)


def kernel(x, norm_w, w_in, conv_w, dn_a_log, dn_dt_bias, dn_norm_w, q_norm_w, k_norm_w,
           w_branch_dn, w_branch_swa, w_out):
    b, t, d = x.shape
    n = b * t
    layer = 0
    w = w_in[layer]
    c_qkv = 3 * DN_WIDTH
    c_z = c_qkv + DN_WIDTH
    c_ab = c_z + 2 * DN_HEADS
    c_q = c_ab
    c_k = c_q + N_GROUPS * SWA_WIDTH
    c_v = c_k + N_GROUPS * SWA_WIDTH
    c_sz = c_v + N_GROUPS * SWA_WIDTH

    w_a = w[:, :c_z].astype(BF16)
    w_ab = jnp.pad(w[:, c_z:c_ab], ((0, 0), (0, LANES - 2 * DN_HEADS))).astype(BF16)
    w_abt = jnp.pad(w[:, c_z:c_ab].T, ((0, BG_ROWS - 2 * DN_HEADS), (0, 0))).astype(BF16)
    w_rest = w[:, c_q:].astype(BF16)
    grp_blocks = lambda g: [(SWA_WIDTH, which * N_GROUPS + g) for which in range(3)]
    gate0 = (c_sz + SWA_WIDTH - c_q) // D_MODEL
    assert (c_sz + SWA_WIDTH - c_q) % D_MODEL == 0
    zg_blocks = [(SWA_WIDTH, (c_sz - c_q) // SWA_WIDTH), (D_MODEL, gate0), (D_MODEL, gate0 + 1)]
    pad_heads = lambda v: jnp.pad(v.astype(F32), (DN_HEADS, LANES - 2 * DN_HEADS))[None, :]
    alog_row = pad_heads(dn_a_log[layer])
    dt_row = pad_heads(dn_dt_bias[layer])

    x2 = x.reshape(n, d)
    h, a, bg, bgt = _norm_proj(x2, norm_w[layer][None, :], w_a, w_ab, w_abt, alog_row, dt_row)

    o_dn = _deltanet(a.reshape(b, t, 4 * DN_WIDTH), bg.reshape(b, t, LANES), bgt,
                     conv_w[layer][:, 0, :], dn_norm_w[layer][None, :])

    tabs = _rope_tables(t, q_norm_w[layer], k_norm_w[layer])
    o_list, lse_list = [], []
    for g, (window, dilation) in enumerate(SWA_GROUPS):
        assert window // dilation == SWA_BLOCK
        s_g = _matmul(h, w_rest, grp_blocks(g), f"proj_swa{g}").reshape(b, t, 3 * SWA_WIDTH)
        o_g, lse_g = _swa_group(s_g, tabs, g, dilation)
        o_list.append(o_g.reshape(n, SWA_WIDTH))
        lse_list.append(lse_g.reshape(n, LANES))

    zg = _matmul(h, w_rest, zg_blocks, "proj_zg")
    out = _merge_out(x2, o_dn.reshape(n, DN_WIDTH), o_list, lse_list, zg,
                     w_branch_dn[layer].astype(BF16), w_branch_swa[layer].astype(BF16),
                     w_out[layer].astype(BF16))
    return out.reshape(b, t, d)
```

```python
import functools
import math

import jax
import jax.numpy as jnp
import numpy as np
from jax import lax
from jax.experimental import pallas as pl
from jax.experimental.pallas import tpu as pltpu

D_MODEL = 1024
HEAD_DIM = 128
DN_HEADS = 4
DN_WIDTH = DN_HEADS * HEAD_DIM
CONV_WIDTH = 4
CHUNK = 64
SWA_GROUPS = ((128, 1), (512, 4), (2048, 16))
N_GROUPS = 3
SWA_HEADS = 4
SWA_WIDTH = SWA_HEADS * HEAD_DIM
SWA_BLOCK = 128
ROPE_DIM = HEAD_DIM // 4
ROPE_HALF = ROPE_DIM // 2
ROPE_THETA = 500000.0
EPS = 1e-6
NEG_INF = -1e30

LANES = 128
VMEM_LIMIT_BYTES = 48 * 1024 * 1024
DN_VMEM_LIMIT_BYTES = 56 * 1024 * 1024
ROW_TILE = 512
BG_ROWS = 16

F32 = jnp.float32
BF16 = jnp.bfloat16
_NT = (((1,), (1,)), ((), ()))
_TN = (((0,), (0,)), ((), ()))


def _sigmoid(x):
    return 0.5 * jnp.tanh(0.5 * x) + 0.5


def _silu(x):
    return x * _sigmoid(x)


def _softplus(x):
    return jnp.maximum(x, 0.0) + jnp.log(1.0 + jnp.exp(-jnp.abs(x)))


def _params(n_axes):
    return pltpu.CompilerParams(dimension_semantics=("parallel",) * n_axes,
                                vmem_limit_bytes=VMEM_LIMIT_BYTES)


def _params_sequential():
    return pltpu.CompilerParams(dimension_semantics=("arbitrary",), vmem_limit_bytes=VMEM_LIMIT_BYTES)


def _mxu(a, b):
    return jnp.dot(a, b, preferred_element_type=F32)


def _split3(x):
    hi = x.astype(BF16)
    r1 = x - hi.astype(F32)
    mid = r1.astype(BF16)
    lo = (r1 - mid.astype(F32)).astype(BF16)
    return hi, mid, lo


def _beta_and_log_decay(ab, a_log, dt_bias, head_index):
    beta = _sigmoid(ab)
    g = -jnp.exp(a_log) * _softplus(ab + dt_bias)
    return jnp.where(head_index < DN_HEADS, beta, g)


def _weight_specs(row_blocks, k):
    return [pl.BlockSpec((pl.Element(rows), pl.Element(k)), lambda i, start=start: (start, 0),
                         pipeline_mode=pl.Buffered(1))
            for start, rows in row_blocks]


def _cast_weights_once(w_refs, w_scr):
    @pl.when(pl.program_id(0) == 0)
    def _():
        blocks = [w_ref[...] for w_ref in w_refs]
        pad = w_scr.shape[0] - sum(blk.shape[0] for blk in blocks)
        if pad:
            blocks.append(jnp.zeros((pad, w_scr.shape[1]), F32))
        w_scr[...] = (jnp.concatenate(blocks, axis=0) if len(blocks) > 1 else blocks[0]).astype(BF16)


def _norm_proj_kernel(x_ref, nw_ref, wa_ref, wab_ref, alog_ref, dt_ref, alogt_ref, dtt_ref,
                      h_ref, a_ref, bg_ref, bgt_ref, wa_scr, wab_scr):
    _cast_weights_once([wa_ref], wa_scr)
    _cast_weights_once([wab_ref], wab_scr)
    x = x_ref[...]
    h = (x * lax.rsqrt(jnp.mean(x * x, axis=-1, keepdims=True) + EPS)) * nw_ref[...]
    hb = h.astype(BF16)
    h_ref[...] = hb
    a_ref[...] = lax.dot_general(hb, wa_scr[...], _NT, preferred_element_type=F32).astype(BF16)
    ab = lax.dot_general(hb, wab_scr[...], _NT, preferred_element_type=F32)
    bg_ref[...] = _beta_and_log_decay(ab, alog_ref[...], dt_ref[...],
                                      lax.broadcasted_iota(jnp.int32, ab.shape, 1))
    abt = lax.dot_general(wab_scr[0:BG_ROWS, :], hb, _NT, preferred_element_type=F32)
    bgt_ref[...] = _beta_and_log_decay(abt, alogt_ref[...], dtt_ref[...],
                                       lax.broadcasted_iota(jnp.int32, abt.shape, 0))


def _norm_proj(x2, norm_w, wt, na, alog_row, dt_row):
    n = x2.shape[0]
    row = lambda i: (i, 0)
    fixed = lambda i: (0, 0)
    pad_col = lambda v: v[0, :BG_ROWS][:, None]
    return pl.pallas_call(
        _norm_proj_kernel,
        grid=(n // ROW_TILE,),
        in_specs=[pl.BlockSpec((ROW_TILE, D_MODEL), row),
                  pl.BlockSpec((1, D_MODEL), fixed)]
                 + _weight_specs([(0, na), (na, 2 * DN_HEADS)], D_MODEL)
                 + [pl.BlockSpec((1, LANES), fixed),
                    pl.BlockSpec((1, LANES), fixed),
                    pl.BlockSpec((BG_ROWS, 1), fixed),
                    pl.BlockSpec((BG_ROWS, 1), fixed)],
        out_specs=[pl.BlockSpec((ROW_TILE, D_MODEL), row),
                   pl.BlockSpec((ROW_TILE, na), row),
                   pl.BlockSpec((ROW_TILE, LANES), row),
                   pl.BlockSpec((BG_ROWS, ROW_TILE), lambda i: (0, i))],
        out_shape=[jax.ShapeDtypeStruct((n, D_MODEL), BF16),
                   jax.ShapeDtypeStruct((n, na), BF16),
                   jax.ShapeDtypeStruct((n, LANES), F32),
                   jax.ShapeDtypeStruct((BG_ROWS, n), F32)],
        scratch_shapes=[pltpu.VMEM((na, D_MODEL), BF16), pltpu.VMEM((LANES, D_MODEL), BF16)],
        compiler_params=_params_sequential(),
        name="norm_proj",
    )(x2, norm_w, wt, wt, alog_row, dt_row, pad_col(alog_row), pad_col(dt_row))


def _matmul_kernel(h_ref, *refs):
    w_refs, o_ref, w_scr = refs[:-2], refs[-2], refs[-1]
    _cast_weights_once(w_refs, w_scr)
    o_ref[...] = lax.dot_general(h_ref[...], w_scr[...], _NT, preferred_element_type=F32).astype(o_ref.dtype)


def _matmul(h, wt, row_blocks, name):
    n, k = h.shape
    nc = sum(rows for _, rows in row_blocks)
    return pl.pallas_call(
        _matmul_kernel,
        grid=(n // ROW_TILE,),
        in_specs=[pl.BlockSpec((ROW_TILE, k), lambda i: (i, 0))] + _weight_specs(row_blocks, k),
        out_specs=pl.BlockSpec((ROW_TILE, nc), lambda i: (i, 0)),
        out_shape=jax.ShapeDtypeStruct((n, nc), BF16),
        scratch_shapes=[pltpu.VMEM((nc, k), BF16)],
        compiler_params=_params_sequential(),
        name=name,
    )(h, *([wt] * len(row_blocks)))


DN_GROUP = 4
DN_ROWS = DN_GROUP * CHUNK
DN_HALO = 16


def _deltanet_kernel(a_ref, bg_ref, bgt_ref, cw_ref, nw_ref, o_ref,
                     u_ref, wq_ref, kd_ref, ai_ref, eg_ref):
    t_len = a_ref.shape[1]
    n_groups = t_len // DN_ROWS
    ii = lax.broadcasted_iota(jnp.int32, (CHUNK, CHUNK), 0)
    jj = lax.broadcasted_iota(jnp.int32, (CHUNK, CHUNK), 1)
    causal = ii >= jj
    strict = ii > jj
    eye_f = (ii == jj).astype(F32)
    bi = lax.broadcasted_iota(jnp.int32, (DN_ROWS, DN_ROWS), 0)
    bj = lax.broadcasted_iota(jnp.int32, (DN_ROWS, DN_ROWS), 1)
    shift = int(math.log2(CHUNK))
    same_chunk = jnp.right_shift(bi, shift) == jnp.right_shift(bj, shift)
    cum_lower = (same_chunk & (bi >= bj)).astype(BF16)
    cum_upper = (same_chunk & (bi <= bj)).astype(BF16)
    problems = [(c, h) for c in range(DN_GROUP) for h in range(DN_HEADS)]
    heads = range(DN_HEADS)
    hsl = lambda h: slice(h * HEAD_DIM, (h + 1) * HEAD_DIM)
    cs = lambda c: slice(c * CHUNK, (c + 1) * CHUNK)
    ones_b = jnp.ones((HEAD_DIM, HEAD_DIM), BF16)
    sel_src = lax.broadcasted_iota(jnp.int32, (LANES, DN_WIDTH), 0)
    sel_head = jnp.right_shift(lax.broadcasted_iota(jnp.int32, (LANES, DN_WIDTH), 1), int(math.log2(HEAD_DIM)))
    sel_beta = (sel_src == sel_head).astype(BF16)
    sel_g = (sel_src == sel_head + DN_HEADS).astype(BF16)
    lane = lax.broadcasted_iota(jnp.int32, (DN_ROWS, LANES), 1)
    g_lanes = (lane >= DN_HEADS) & (lane < 2 * DN_HEADS)

    pending = []

    def defer(ref, idx, value):
        pending.append((ref, idx, value))

    def flush():
        for ref, idx, value in pending:
            ref[idx] = value
        pending.clear()

    def conv_silu(gi, r0, col0):
        cols = slice(col0, col0 + HEAD_DIM)
        rp = pl.multiple_of(jnp.maximum(r0 - DN_HALO, 0), DN_HALO)
        cur = a_ref[0, pl.ds(r0, DN_ROWS), cols].astype(F32)
        prev = a_ref[0, pl.ds(rp, DN_HALO), cols].astype(F32)
        prev = jnp.where(gi > 0, prev, 0.0)
        xw = jnp.concatenate([prev, cur], axis=0)
        conv = None
        for j in range(CONV_WIDTH):
            lo = DN_HALO - (CONV_WIDTH - 1) + j
            term = cw_ref[j:j + 1, cols] * xw[lo:lo + DN_ROWS, :]
            conv = term if conv is None else conv + term
        return _silu(conv)

    def prepare(gi):
        r0 = pl.multiple_of(gi * DN_ROWS, DN_ROWS)
        bg = bg_ref[0, pl.ds(r0, DN_ROWS), :]
        gc_all = sum(_mxu(cum_lower, piece) for piece in _split3(bg))
        gct_all = sum(_mxu(piece, cum_upper) for piece in _split3(bgt_ref[:, pl.ds(r0, DN_ROWS)]))
        gl_all = jnp.concatenate([jnp.broadcast_to(gc_all[(c + 1) * CHUNK - 1:(c + 1) * CHUNK, :], (CHUNK, LANES))
                                  for c in range(DN_GROUP)], axis=0)
        eg_all = jnp.exp(gl_all)
        for c in range(DN_GROUP):
            defer(eg_ref, (gi * DN_GROUP + c,), eg_all[c * CHUNK:c * CHUNK + 8, :])
        gc_m = jnp.where(g_lanes, gc_all, 0.0)
        gl_m = jnp.where(g_lanes, gl_all, 0.0)
        beta_rep = _mxu(bg.astype(BF16), sel_beta)
        egc_rep = _mxu(jnp.exp(gc_m).astype(BF16), sel_g)
        ekd_rep = _mxu(jnp.exp(gl_m - gc_m).astype(BF16), sel_g)

        qn, knb, kb, rhs = [], [], [], []
        for h in heads:
            qh = conv_silu(gi, r0, h * HEAD_DIM)
            kh = conv_silu(gi, r0, DN_WIDTH + h * HEAD_DIM)
            vh = conv_silu(gi, r0, 2 * DN_WIDTH + h * HEAD_DIM)
            qn_h = qh * (lax.rsqrt(_mxu((qh * qh).astype(BF16), ones_b) + EPS) * (HEAD_DIM ** -0.5))
            kn_h = kh * lax.rsqrt(_mxu((kh * kh).astype(BF16), ones_b) + EPS)
            kb_h = kn_h * beta_rep[:, hsl(h)]
            defer(kd_ref, (h, pl.ds(r0, DN_ROWS), slice(None)), (kn_h * ekd_rep[:, hsl(h)]).astype(BF16))
            qg_h = (qn_h * egc_rep[:, hsl(h)]).astype(BF16)
            for c in range(DN_GROUP):
                w0 = pl.multiple_of(2 * (r0 + c * CHUNK), 2 * CHUNK)
                defer(wq_ref, (h, pl.ds(w0 + CHUNK, CHUNK), slice(None)), qg_h[cs(c)])
            qn.append(qn_h)
            knb.append(kn_h.astype(BF16))
            kb.append(kb_h)
            rhs.append(jnp.concatenate([vh * beta_rep[:, hsl(h)], kb_h * egc_rep[:, hsl(h)]], axis=1).astype(BF16))
            yield

        qk = [lax.dot_general(jnp.concatenate([qn[h][cs(c)], kb[h][cs(c)]], axis=0).astype(BF16),
                              knb[h][cs(c)], _NT, preferred_element_type=F32) for c, h in problems]
        decay = []
        for c, h in problems:
            diff = gc_all[cs(c), DN_HEADS + h:DN_HEADS + h + 1] - gct_all[DN_HEADS + h:DN_HEADS + h + 1, cs(c)]
            decay.append(jnp.where(causal, jnp.exp(jnp.where(causal, diff, 0.0)), 0.0))
        for i, (c, h) in enumerate(problems):
            a_intra = jnp.where(causal, qk[i][:CHUNK] * decay[i], 0.0)
            defer(ai_ref, (h, pl.ds(r0 + c * CHUNK, CHUNK), slice(None)), a_intra.astype(BF16))
        ps = [-jnp.where(strict, qk[i][CHUNK:] * decay[i], 0.0) for i in range(len(problems))]
        tms = [eye_f + p for p in ps]
        for _ in range(shift - 1):
            pbs = [p.astype(BF16) for p in ps]
            ps = [_mxu(pb, pb) for pb in pbs]
            tms = [tm + _mxu(tm.astype(BF16), p.astype(BF16)) for tm, p in zip(tms, ps)]
        uw = [_mxu(tms[i].astype(BF16), rhs[h][cs(c)]) for i, (c, h) in enumerate(problems)]
        for i, (c, h) in enumerate(problems):
            defer(u_ref, (h, pl.ds(r0 + c * CHUNK, CHUNK), slice(None)), uw[i][:, :HEAD_DIM])
            w0 = pl.multiple_of(2 * (r0 + c * CHUNK), 2 * CHUNK)
            defer(wq_ref, (h, pl.ds(w0, CHUNK), slice(None)), uw[i][:, HEAD_DIM:].astype(BF16))

    def recur(c, state):
        r0 = pl.multiple_of(c * CHUNK, CHUNK)
        rows = pl.ds(r0, CHUNK)
        wrows = pl.ds(pl.multiple_of(2 * r0, 2 * CHUNK), 2 * CHUNK)
        eg = eg_ref[c]
        ws = [_mxu(wq_ref[h, wrows, :], state[h].astype(BF16)) for h in heads]
        vn = [(u_ref[h, rows, :] - ws[h][:CHUNK]).astype(BF16) for h in heads]
        new_state = [state[h] * eg[0:1, DN_HEADS + h:DN_HEADS + h + 1]
                     + lax.dot_general(kd_ref[h, rows, :], vn[h], _TN, preferred_element_type=F32) for h in heads]
        o = [ws[h][CHUNK:] + _mxu(ai_ref[h, rows, :], vn[h]) for h in heads]
        ms = [_mxu((x * x).astype(BF16), ones_b) * (1.0 / HEAD_DIM) for x in o]
        for h in heads:
            on = o[h] * lax.rsqrt(ms[h] + EPS) * nw_ref[...]
            z = a_ref[0, rows, 3 * DN_WIDTH + h * HEAD_DIM:3 * DN_WIDTH + (h + 1) * HEAD_DIM].astype(F32)
            defer(o_ref, (0, rows, hsl(h)), (on * _silu(z)).astype(o_ref.dtype))
        return new_state

    for _ in prepare(0):
        pass
    flush()

    def pipelined(gi, state):
        state = list(state)
        stages = prepare(gi)
        for c in range(DN_GROUP):
            state = recur((gi - 1) * DN_GROUP + c, state)
            next(stages)
        for _ in stages:
            pass
        flush()
        return tuple(state)

    zero = jnp.zeros((HEAD_DIM, HEAD_DIM), F32)
    state = list(lax.fori_loop(1, n_groups, pipelined, (zero,) * DN_HEADS))
    for c in range(DN_GROUP):
        state = recur((n_groups - 1) * DN_GROUP + c, state)
    flush()


def _deltanet(a3, bg3, bgt, conv_w2, dn_norm_row):
    b, t, wa = a3.shape
    nt = bgt.shape[0]
    return pl.pallas_call(
        _deltanet_kernel,
        grid=(b,),
        in_specs=[pl.BlockSpec((1, t, wa), lambda i: (i, 0, 0)),
                  pl.BlockSpec((1, t, LANES), lambda i: (i, 0, 0)),
                  pl.BlockSpec((nt, t), lambda i: (0, i)),
                  pl.BlockSpec((CONV_WIDTH, 3 * DN_WIDTH), lambda i: (0, 0)),
                  pl.BlockSpec((1, HEAD_DIM), lambda i: (0, 0))],
        out_specs=pl.BlockSpec((1, t, DN_WIDTH), lambda i: (i, 0, 0)),
        out_shape=jax.ShapeDtypeStruct((b, t, DN_WIDTH), BF16),
        scratch_shapes=[pltpu.VMEM((DN_HEADS, t, HEAD_DIM), F32),
                        pltpu.VMEM((DN_HEADS, 2 * t, HEAD_DIM), BF16),
                        pltpu.VMEM((DN_HEADS, t, HEAD_DIM), BF16),
                        pltpu.VMEM((DN_HEADS, t, CHUNK), BF16),
                        pltpu.VMEM((t // CHUNK, 8, LANES), F32)],
        compiler_params=pltpu.CompilerParams(dimension_semantics=("parallel",),
                                             vmem_limit_bytes=DN_VMEM_LIMIT_BYTES),
        name="deltanet",
    )(a3, bg3, bgt, conv_w2, dn_norm_row)


SWA_PREP_ROWS = 256
SWA_UNITS = 2


def _swa_kernel(s_ref, tab_ref, o_ref, lse_ref, qs_ref, ks_ref, bias_ref, *dilated_scratch, dilation):
    t_len = s_ref.shape[1]
    d = dilation
    seq = t_len // d
    nblk = seq // SWA_BLOCK
    per = SWA_PREP_ROWS // d
    heads = range(SWA_HEADS)
    hsl = lambda h: slice(h * HEAD_DIM, (h + 1) * HEAD_DIM)
    if d > 1:
        vs_ref, stage_ref, ostage_ref = dilated_scratch
    mean_b = jnp.full((HEAD_DIM, HEAD_DIM), 1.0 / HEAD_DIM, BF16)
    src_lane = lax.broadcasted_iota(jnp.int32, (HEAD_DIM, HEAD_DIM), 0)
    dst_lane = lax.broadcasted_iota(jnp.int32, (HEAD_DIM, HEAD_DIM), 1)
    swap_b = (((dst_lane < ROPE_HALF) & (src_lane == dst_lane + ROPE_HALF))
              | ((dst_lane >= ROPE_HALF) & (dst_lane < ROPE_DIM) & (src_lane == dst_lane - ROPE_HALF))
              ).astype(BF16)

    def prep(n, carry):
        r0 = pl.multiple_of(n * SWA_PREP_ROWS, SWA_PREP_ROWS)
        rows = pl.ds(r0, SWA_PREP_ROWS)
        i0 = pl.multiple_of(n * per, per)

        def regroup(ys, dst, slab0):
            for h, y in enumerate(ys):
                stage_ref[slab0 + h] = y
            for h in heads:
                for r in range(d):
                    part = stage_ref[slab0 + h, pl.ds(r, per, stride=d), :]
                    dst[pl.ds(r * seq + i0, per), hsl(h)] = part.astype(BF16)

        for which, dst in ((0, qs_ref), (1, ks_ref)):
            plain = tab_ref[which, 0, rows, :]
            swapped_tab = tab_ref[which, 1, rows, :]
            xb = [s_ref[0, rows, which * SWA_WIDTH + h * HEAD_DIM:which * SWA_WIDTH + (h + 1) * HEAD_DIM]
                  for h in heads]
            ms = [_mxu(x * x, mean_b) for x in xb]
            sw = [_mxu(x, swap_b) for x in xb]
            ys = [(x.astype(F32) * plain + s * swapped_tab) * lax.rsqrt(m + EPS) for x, s, m in zip(xb, sw, ms)]
            if d == 1:
                for h in heads:
                    dst[rows, hsl(h)] = ys[h].astype(BF16)
            else:
                regroup(ys, dst, which * SWA_HEADS)
        if d > 1:
            vs = [s_ref[0, rows, 2 * SWA_WIDTH + h * HEAD_DIM:2 * SWA_WIDTH + (h + 1) * HEAD_DIM].astype(F32)
                  for h in heads]
            regroup(vs, vs_ref, 2 * SWA_HEADS)
        return carry

    lax.fori_loop(0, t_len // SWA_PREP_ROWS, prep, 0)

    qi = lax.broadcasted_iota(jnp.int32, (SWA_BLOCK, 2 * SWA_BLOCK), 0)
    kj = lax.broadcasted_iota(jnp.int32, (SWA_BLOCK, 2 * SWA_BLOCK), 1)
    bias_ref[0] = jnp.where(kj <= qi, 0.0, NEG_INF).astype(F32)
    bias_ref[1] = jnp.where((kj >= qi) & (kj <= qi + SWA_BLOCK), 0.0, NEG_INF).astype(F32)
    lane = lax.broadcasted_iota(jnp.int32, (SWA_BLOCK, LANES), 1)

    def attend(units):
        jobs = [(u, h) for u in range(len(units)) for h in heads]
        qrows = [pl.ds(q0, SWA_BLOCK) for q0, _, _, _, _ in units]
        krows = [pl.ds(k0, width) for _, k0, width, _, _ in units]
        s = [lax.dot_general(qs_ref[qrows[u], hsl(h)], ks_ref[krows[u], hsl(h)], _NT,
                             preferred_element_type=F32) + units[u][3] for u, h in jobs]
        m = [jnp.max(x, axis=-1, keepdims=True) for x in s]
        p = [jnp.exp(x - mx) for x, mx in zip(s, m)]
        l = [jnp.sum(x, axis=-1, keepdims=True) for x in p]
        if d == 1:
            v = [s_ref[0, krows[u], 2 * SWA_WIDTH + h * HEAD_DIM:2 * SWA_WIDTH + (h + 1) * HEAD_DIM] for u, h in jobs]
        else:
            v = [vs_ref[krows[u], hsl(h)] for u, h in jobs]
        o = [_mxu(x.astype(BF16), vh) * (1.0 / lx) for x, vh, lx in zip(p, v, l)]
        for u in range(len(units)):
            lse_tile = jnp.zeros((SWA_BLOCK, LANES), F32)
            for h in heads:
                j = u * SWA_HEADS + h
                lse_tile = jnp.where(lane == h, m[j] + jnp.log(l[j]), lse_tile)
            t0 = units[u][4]
            if d == 1:
                for h in heads:
                    o_ref[0, qrows[u], hsl(h)] = o[u * SWA_HEADS + h].astype(o_ref.dtype)
                lse_ref[0, qrows[u], :] = lse_tile
            else:
                trows = pl.ds(t0, SWA_BLOCK, stride=d)
                for h in heads:
                    ostage_ref[h, trows, :] = o[u * SWA_HEADS + h]
                lse_ref[0, trows, :] = lse_tile

    if nblk == 1:
        causal_bias = bias_ref[0][:, :SWA_BLOCK]

        def body(i, carry):
            units = []
            for u in range(SWA_UNITS):
                r = i * SWA_UNITS + u
                q0 = pl.multiple_of(r * seq, SWA_BLOCK)
                units.append((q0, q0, SWA_BLOCK, causal_bias, r))
            attend(units)
            return carry

        lax.fori_loop(0, d // SWA_UNITS, body, 0)
    else:
        pairs = nblk // SWA_UNITS

        def body(i, carry):
            r = i // pairs
            pp = i % pairs
            units = []
            for u in range(SWA_UNITS):
                n = pp * SWA_UNITS + u
                q0 = pl.multiple_of(r * seq + n * SWA_BLOCK, SWA_BLOCK)
                if u == 0:
                    first = (pp == 0).astype(jnp.int32)
                    k0 = pl.multiple_of(q0 - (1 - first) * SWA_BLOCK, SWA_BLOCK)
                    bias = bias_ref[1 - first]
                else:
                    k0 = pl.multiple_of(q0 - SWA_BLOCK, SWA_BLOCK)
                    bias = bias_ref[1]
                units.append((q0, k0, 2 * SWA_BLOCK, bias, r + d * n * SWA_BLOCK))
            attend(units)
            return carry

        lax.fori_loop(0, d * pairs, body, 0)

    if d > 1:
        def emit(n, carry):
            rows = pl.ds(pl.multiple_of(n * SWA_PREP_ROWS, SWA_PREP_ROWS), SWA_PREP_ROWS)
            for h in heads:
                o_ref[0, rows, hsl(h)] = ostage_ref[h, rows, :].astype(o_ref.dtype)
            return carry
        lax.fori_loop(0, t_len // SWA_PREP_ROWS, emit, 0)


def _swa_group(s3, tabs, group, dilation):
    b, t, ws = s3.shape
    assert (t // dilation) % SWA_BLOCK == 0 and SWA_PREP_ROWS % dilation == 0
    nblk = t // dilation // SWA_BLOCK
    assert (dilation % SWA_UNITS == 0) if nblk == 1 else (nblk % SWA_UNITS == 0)
    scratch = [pltpu.VMEM((t, SWA_WIDTH), BF16), pltpu.VMEM((t, SWA_WIDTH), BF16),
               pltpu.VMEM((2, SWA_BLOCK, 2 * SWA_BLOCK), F32)]
    if dilation > 1:
        scratch += [pltpu.VMEM((t, SWA_WIDTH), BF16),
                    pltpu.VMEM((3 * SWA_HEADS, SWA_PREP_ROWS, LANES), F32),
                    pltpu.VMEM((SWA_HEADS, t, LANES), F32)]
    return pl.pallas_call(
        functools.partial(_swa_kernel, dilation=dilation),
        grid=(b,),
        in_specs=[pl.BlockSpec((1, t, ws), lambda i: (i, 0, 0)),
                  pl.BlockSpec((None, 2, 2, t, LANES), lambda i: (group, 0, 0, 0, 0))],
        out_specs=[pl.BlockSpec((1, t, SWA_WIDTH), lambda i: (i, 0, 0)),
                   pl.BlockSpec((1, t, LANES), lambda i: (i, 0, 0))],
        out_shape=[jax.ShapeDtypeStruct((b, t, SWA_WIDTH), BF16),
                   jax.ShapeDtypeStruct((b, t, LANES), F32)],
        scratch_shapes=scratch,
        compiler_params=_params(1),
        name=f"swa_d{dilation}",
    )(s3, tabs)


MERGE_ROWS = 256


def _merge_out_kernel(x_ref, h_ref, odn_ref, o0_ref, o1_ref, o2_ref, l0_ref, l1_ref, l2_ref,
                      wz_ref, wgdn_ref, wgswa_ref, wdn_ref, wswa_ref, wout_ref, out_ref, wzg_scr):
    _cast_weights_once([wz_ref, wgdn_ref, wgswa_ref], wzg_scr)
    zg_all = lax.dot_general(h_ref[...], wzg_scr[...], _NT, preferred_element_type=F32)
    outs = []
    for j in range(ROW_TILE // MERGE_ROWS):
        rows = slice(j * MERGE_ROWS, (j + 1) * MERGE_ROWS)
        zg = zg_all[rows, :]
        l0 = l0_ref[rows, :]
        l1 = l1_ref[rows, :]
        l2 = l2_ref[rows, :]
        m = jnp.maximum(jnp.maximum(l0, l1), l2)
        e0 = jnp.exp(l0 - m)
        e1 = jnp.exp(l1 - m)
        e2 = jnp.exp(l2 - m)
        inv = 1.0 / (e0 + e1 + e2)
        a0, a1, a2 = e0 * inv, e1 * inv, e2 * inv
        parts = []
        for h in range(SWA_HEADS):
            hs = slice(h * HEAD_DIM, (h + 1) * HEAD_DIM)
            col = slice(h, h + 1)
            oh = (a0[:, col] * o0_ref[rows, hs].astype(F32) + a1[:, col] * o1_ref[rows, hs].astype(F32)
                  + a2[:, col] * o2_ref[rows, hs].astype(F32))
            parts.append((oh * _silu(zg[:, hs])).astype(BF16))
        o_swa = jnp.concatenate(parts, axis=1)
        y_swa = _mxu(o_swa, wswa_ref[...])
        y_dn = _mxu(odn_ref[rows, :], wdn_ref[...])
        g_dn = zg[:, SWA_WIDTH:SWA_WIDTH + D_MODEL]
        g_swa = zg[:, SWA_WIDTH + D_MODEL:SWA_WIDTH + 2 * D_MODEL]
        merged = _sigmoid(g_dn) * y_dn + _sigmoid(g_swa) * y_swa
        outs.append((rows, x_ref[rows, :] + _mxu(merged.astype(BF16), wout_ref[...])))
    for rows, value in outs:
        out_ref[rows, :] = value


def _merge_out(x2, h, odn, o_list, lse_list, wt, zg_row_blocks, w_dn, w_swa, w_out):
    n = x2.shape[0]
    row = lambda w: pl.BlockSpec((ROW_TILE, w), lambda i: (i, 0))
    full = lambda a: pl.BlockSpec(a.shape, lambda i: (0, 0), pipeline_mode=pl.Buffered(1))
    nzg = sum(rows for _, rows in zg_row_blocks)
    return pl.pallas_call(
        _merge_out_kernel,
        grid=(n // ROW_TILE,),
        in_specs=[row(D_MODEL), row(D_MODEL), row(DN_WIDTH), row(SWA_WIDTH), row(SWA_WIDTH), row(SWA_WIDTH),
                  row(LANES), row(LANES), row(LANES)]
                 + _weight_specs(zg_row_blocks, D_MODEL)
                 + [full(w_dn), full(w_swa), full(w_out)],
        out_specs=row(D_MODEL),
        out_shape=jax.ShapeDtypeStruct((n, D_MODEL), F32),
        scratch_shapes=[pltpu.VMEM((nzg, D_MODEL), BF16)],
        compiler_params=_params_sequential(),
        name="merge_out",
    )(x2, h, odn, *o_list, *lse_list, wt, wt, wt, w_dn, w_swa, w_out)


def _rope_tables(t_len, q_norm_w, k_norm_w):
    pos = jnp.arange(t_len, dtype=F32)
    inv_freq = ROPE_THETA ** (-jnp.arange(0, ROPE_DIM, 2, dtype=F32) / ROPE_DIM)
    ang = pos[:, None] * inv_freq[None, :]
    cos, sin = jnp.cos(ang), jnp.sin(ang)
    tail = HEAD_DIM - ROPE_DIM
    cos_t = jnp.concatenate([cos, cos, jnp.ones((t_len, tail), F32)], axis=1)
    sin_t = jnp.concatenate([-sin, sin, jnp.zeros((t_len, tail), F32)], axis=1)
    w = jnp.stack([q_norm_w.astype(F32) * (HEAD_DIM ** -0.5), k_norm_w.astype(F32)], axis=1)
    w_swapped = jnp.concatenate([w[..., ROPE_HALF:ROPE_DIM], w[..., :ROPE_HALF], w[..., ROPE_DIM:]], axis=-1)
    plain = w[:, :, None, :] * cos_t[None, None]
    swapped = w_swapped[:, :, None, :] * sin_t[None, None]
    return jnp.stack([plain, swapped], axis=2)


def kernel(x, norm_w, w_in, conv_w, dn_a_log, dn_dt_bias, dn_norm_w, q_norm_w, k_norm_w,
           w_branch_dn, w_branch_swa, w_out):
    b, t, d = x.shape
    n = b * t
    layer = 0
    wt = jnp.swapaxes(w_in[layer], 0, 1)
    c_z = 4 * DN_WIDTH
    c_q = c_z + 2 * DN_HEADS
    c_k = c_q + N_GROUPS * SWA_WIDTH
    c_v = c_k + N_GROUPS * SWA_WIDTH
    c_sz = c_v + N_GROUPS * SWA_WIDTH
    c_g = c_sz + SWA_WIDTH
    grp_blocks = lambda g: [(c0 + g * SWA_WIDTH, SWA_WIDTH) for c0 in (c_q, c_k, c_v)]
    zg_blocks = [(c_sz, SWA_WIDTH), (c_g, D_MODEL), (c_g + D_MODEL, D_MODEL)]
    pad_heads = lambda v: jnp.pad(v.astype(F32), (DN_HEADS, LANES - 2 * DN_HEADS))[None, :]
    alog_row = pad_heads(dn_a_log[layer])
    dt_row = pad_heads(dn_dt_bias[layer])

    x2 = x.reshape(n, d)
    h, a, bg, bgt = _norm_proj(x2, norm_w[layer][None, :], wt, c_z, alog_row, dt_row)

    o_dn = _deltanet(a.reshape(b, t, 4 * DN_WIDTH), bg.reshape(b, t, LANES), bgt,
                     conv_w[layer][:, 0, :], dn_norm_w[layer][None, :])

    tabs = _rope_tables(t, q_norm_w[layer], k_norm_w[layer])
    o_list, lse_list = [], []
    for g, (window, dilation) in enumerate(SWA_GROUPS):
        assert window // dilation == SWA_BLOCK
        s_g = _matmul(h, wt, grp_blocks(g), f"proj_swa{g}").reshape(b, t, 3 * SWA_WIDTH)
        o_g, lse_g = _swa_group(s_g, tabs, g, dilation)
        o_list.append(o_g.reshape(n, SWA_WIDTH))
        lse_list.append(lse_g.reshape(n, LANES))

    out = _merge_out(x2, h, o_dn.reshape(n, DN_WIDTH), o_list, lse_list, wt, zg_blocks,
                     w_branch_dn[layer].astype(BF16), w_branch_swa[layer].astype(BF16),
                     w_out[layer].astype(BF16))
    return out.reshape(b, t, d)
```

```python
import functools
import math

import jax
import jax.numpy as jnp
import numpy as np
from jax import lax
from jax.experimental import pallas as pl
from jax.experimental.pallas import tpu as pltpu

D_MODEL = 1024
HEAD_DIM = 128
DN_HEADS = 4
DN_WIDTH = DN_HEADS * HEAD_DIM
CONV_WIDTH = 4
CHUNK = 64
SWA_GROUPS = ((128, 1), (512, 4), (2048, 16))
N_GROUPS = 3
SWA_HEADS = 4
SWA_WIDTH = SWA_HEADS * HEAD_DIM
SWA_BLOCK = 128
ROPE_DIM = HEAD_DIM // 4
ROPE_HALF = ROPE_DIM // 2
ROPE_THETA = 500000.0
EPS = 1e-6
NEG_INF = -1e30

LANES = 128
VMEM_LIMIT_BYTES = 48 * 1024 * 1024
DN_VMEM_LIMIT_BYTES = 56 * 1024 * 1024
ROW_TILE = 512
BG_ROWS = 16

F32 = jnp.float32
BF16 = jnp.bfloat16
_NT = (((1,), (1,)), ((), ()))
_TN = (((0,), (0,)), ((), ()))


def _sigmoid(x):
    return 0.5 * jnp.tanh(0.5 * x) + 0.5


def _silu(x):
    return x * _sigmoid(x)


def _softplus(x):
    return jnp.maximum(x, 0.0) + jnp.log(1.0 + jnp.exp(-jnp.abs(x)))


def _params(n_axes):
    return pltpu.CompilerParams(dimension_semantics=("parallel",) * n_axes,
                                vmem_limit_bytes=VMEM_LIMIT_BYTES)


def _params_sequential():
    return pltpu.CompilerParams(dimension_semantics=("arbitrary",), vmem_limit_bytes=VMEM_LIMIT_BYTES)


def _mxu(a, b):
    return jnp.dot(a, b, preferred_element_type=F32)


def _split3(x):
    hi = x.astype(BF16)
    r1 = x - hi.astype(F32)
    mid = r1.astype(BF16)
    lo = (r1 - mid.astype(F32)).astype(BF16)
    return hi, mid, lo


def _beta_and_log_decay(ab, a_log, dt_bias, head_index):
    beta = _sigmoid(ab)
    g = -jnp.exp(a_log) * _softplus(ab + dt_bias)
    return jnp.where(head_index < DN_HEADS, beta, g)


def _weight_specs(row_blocks, k):
    return [pl.BlockSpec((pl.Element(rows), pl.Element(k)), lambda i, start=start: (start, 0),
                         pipeline_mode=pl.Buffered(1))
            for start, rows in row_blocks]


def _cast_weights_once(w_refs, w_scr):
    @pl.when(pl.program_id(0) == 0)
    def _():
        blocks = [w_ref[...] for w_ref in w_refs]
        pad = w_scr.shape[0] - sum(blk.shape[0] for blk in blocks)
        if pad:
            blocks.append(jnp.zeros((pad, w_scr.shape[1]), F32))
        w_scr[...] = (jnp.concatenate(blocks, axis=0) if len(blocks) > 1 else blocks[0]).astype(BF16)


CONV_TILE = 256
CONV_HALO = 8
CONV_ROWS = 64
CONV_EARLY_COLS = 2 * DN_WIDTH


def _norm_proj_kernel(x_ref, nw_ref, wa_ref, wab_ref, cw_ref, alog_ref, dt_ref, alogt_ref, dtt_ref,
                      h_ref, a_ref, bg_ref, bgt_ref, wa_scr, wab_scr, tail_ref, res_a, res_b, *, tiles_per_seq):
    res_scr = (res_a, res_b)
    _cast_weights_once([wa_ref], wa_scr)
    _cast_weights_once([wab_ref], wab_scr)
    x = x_ref[...]
    h = (x * lax.rsqrt(jnp.mean(x * x, axis=-1, keepdims=True) + EPS)) * nw_ref[...]
    hb = h.astype(BF16)
    h_ref[...] = hb
    n_conv = CONV_EARLY_COLS
    seq_start = pl.program_id(0) % tiles_per_seq == 0
    tail = jnp.where(seq_start, 0.0, tail_ref[...])
    stores = []
    tiles = list(range(0, n_conv, CONV_TILE))
    base = pl.multiple_of((pl.program_id(0) >> 20) * CONV_HALO, CONV_HALO)

    def project(k):
        c0 = tiles[k]
        buf = res_scr[k % 2]
        buf[pl.ds(base, CONV_HALO), :] = tail[:, c0:c0 + CONV_TILE]
        buf[pl.ds(base + CONV_HALO, ROW_TILE), :] = lax.dot_general(hb, wa_scr[c0:c0 + CONV_TILE, :], _NT, preferred_element_type=F32)

    if tiles:
        project(0)
    for k, c0 in enumerate(tiles):
        cols = slice(c0, c0 + CONV_TILE)
        if k + 1 < len(tiles):
            project(k + 1)
        buf = res_scr[k % 2]
        acts = []
        for rb in range(0, ROW_TILE, CONV_ROWS):
            blk = buf[pl.ds(base + rb, CONV_HALO + CONV_ROWS), :]
            conv = None
            for j in range(CONV_WIDTH):
                lo = CONV_HALO - (CONV_WIDTH - 1) + j
                term = cw_ref[j:j + 1, cols] * blk[lo:lo + CONV_ROWS, :]
                conv = term if conv is None else conv + term
            acts.append(_silu(conv).astype(BF16))
        stores.append((a_ref, (slice(None), cols), jnp.concatenate(acts, axis=0)))
        stores.append((tail_ref, (slice(None), cols), buf[pl.ds(base + ROW_TILE, CONV_HALO), :]))
    z = lax.dot_general(hb, wa_scr[n_conv:, :], _NT, preferred_element_type=F32)
    stores.append((a_ref, (slice(None), slice(n_conv, None)), z.astype(BF16)))
    for ref, idx, value in stores:
        ref[idx] = value
    ab = lax.dot_general(hb, wab_scr[...], _NT, preferred_element_type=F32)
    bg_ref[...] = _beta_and_log_decay(ab, alog_ref[...], dt_ref[...],
                                      lax.broadcasted_iota(jnp.int32, ab.shape, 1))
    abt = lax.dot_general(wab_scr[0:BG_ROWS, :], hb, _NT, preferred_element_type=F32)
    bgt_ref[...] = _beta_and_log_decay(abt, alogt_ref[...], dtt_ref[...],
                                       lax.broadcasted_iota(jnp.int32, abt.shape, 0))


def _norm_proj(x2, norm_w, wt, na, conv_w2, alog_row, dt_row, seq_len):
    n = x2.shape[0]
    assert seq_len % ROW_TILE == 0 and CONV_EARLY_COLS % CONV_TILE == 0
    row = lambda i: (i, 0)
    fixed = lambda i: (0, 0)
    pad_col = lambda v: v[0, :BG_ROWS][:, None]
    return pl.pallas_call(
        functools.partial(_norm_proj_kernel, tiles_per_seq=seq_len // ROW_TILE),
        grid=(n // ROW_TILE,),
        in_specs=[pl.BlockSpec((ROW_TILE, D_MODEL), row),
                  pl.BlockSpec((1, D_MODEL), fixed)]
                 + _weight_specs([(0, na), (na, 2 * DN_HEADS)], D_MODEL)
                 + [pl.BlockSpec((CONV_WIDTH, 3 * DN_WIDTH), fixed),
                    pl.BlockSpec((1, LANES), fixed),
                    pl.BlockSpec((1, LANES), fixed),
                    pl.BlockSpec((BG_ROWS, 1), fixed),
                    pl.BlockSpec((BG_ROWS, 1), fixed)],
        out_specs=[pl.BlockSpec((ROW_TILE, D_MODEL), row),
                   pl.BlockSpec((ROW_TILE, na), row),
                   pl.BlockSpec((ROW_TILE, LANES), row),
                   pl.BlockSpec((BG_ROWS, ROW_TILE), lambda i: (0, i))],
        out_shape=[jax.ShapeDtypeStruct((n, D_MODEL), BF16),
                   jax.ShapeDtypeStruct((n, na), BF16),
                   jax.ShapeDtypeStruct((n, LANES), F32),
                   jax.ShapeDtypeStruct((BG_ROWS, n), F32)],
        scratch_shapes=[pltpu.VMEM((na, D_MODEL), BF16), pltpu.VMEM((LANES, D_MODEL), BF16),
                        pltpu.VMEM((CONV_HALO, max(CONV_EARLY_COLS, LANES)), F32),
                        pltpu.VMEM((CONV_HALO + ROW_TILE, CONV_TILE), F32),
                        pltpu.VMEM((CONV_HALO + ROW_TILE, CONV_TILE), F32)],
        compiler_params=_params_sequential(),
        name="norm_proj",
    )(x2, norm_w, wt, wt, conv_w2, alog_row, dt_row, pad_col(alog_row), pad_col(dt_row))


def _matmul_kernel(h_ref, *refs):
    w_refs, o_ref, w_scr = refs[:-2], refs[-2], refs[-1]
    _cast_weights_once(w_refs, w_scr)
    o_ref[...] = lax.dot_general(h_ref[...], w_scr[...], _NT, preferred_element_type=F32).astype(o_ref.dtype)


def _matmul(h, wt, row_blocks, name):
    n, k = h.shape
    nc = sum(rows for _, rows in row_blocks)
    return pl.pallas_call(
        _matmul_kernel,
        grid=(n // ROW_TILE,),
        in_specs=[pl.BlockSpec((ROW_TILE, k), lambda i: (i, 0))] + _weight_specs(row_blocks, k),
        out_specs=pl.BlockSpec((ROW_TILE, nc), lambda i: (i, 0)),
        out_shape=jax.ShapeDtypeStruct((n, nc), BF16),
        scratch_shapes=[pltpu.VMEM((nc, k), BF16)],
        compiler_params=_params_sequential(),
        name=name,
    )(h, *([wt] * len(row_blocks)))


DN_GROUP = 4
DN_ROWS = DN_GROUP * CHUNK
DN_HALO = 16


def _deltanet_kernel(a_ref, bg_ref, bgt_ref, cw_ref, nw_ref, o_ref,
                     u_ref, wq_ref, kd_ref, ai_ref, eg_ref):
    t_len = a_ref.shape[1]
    n_groups = t_len // DN_ROWS
    ii = lax.broadcasted_iota(jnp.int32, (CHUNK, CHUNK), 0)
    jj = lax.broadcasted_iota(jnp.int32, (CHUNK, CHUNK), 1)
    causal = ii >= jj
    strict = ii > jj
    eye_f = (ii == jj).astype(F32)
    bi = lax.broadcasted_iota(jnp.int32, (DN_ROWS, DN_ROWS), 0)
    bj = lax.broadcasted_iota(jnp.int32, (DN_ROWS, DN_ROWS), 1)
    shift = int(math.log2(CHUNK))
    same_chunk = jnp.right_shift(bi, shift) == jnp.right_shift(bj, shift)
    cum_lower = (same_chunk & (bi >= bj)).astype(BF16)
    cum_upper = (same_chunk & (bi <= bj)).astype(BF16)
    problems = [(c, h) for c in range(DN_GROUP) for h in range(DN_HEADS)]
    heads = range(DN_HEADS)
    hsl = lambda h: slice(h * HEAD_DIM, (h + 1) * HEAD_DIM)
    cs = lambda c: slice(c * CHUNK, (c + 1) * CHUNK)
    ones_b = jnp.ones((HEAD_DIM, HEAD_DIM), BF16)
    sel_src = lax.broadcasted_iota(jnp.int32, (LANES, DN_WIDTH), 0)
    sel_head = jnp.right_shift(lax.broadcasted_iota(jnp.int32, (LANES, DN_WIDTH), 1), int(math.log2(HEAD_DIM)))
    sel_beta = (sel_src == sel_head).astype(BF16)
    sel_g = (sel_src == sel_head + DN_HEADS).astype(BF16)
    lane = lax.broadcasted_iota(jnp.int32, (DN_ROWS, LANES), 1)
    g_lanes = (lane >= DN_HEADS) & (lane < 2 * DN_HEADS)

    pending = []

    def defer(ref, idx, value):
        pending.append((ref, idx, value))

    def flush():
        for ref, idx, value in pending:
            ref[idx] = value
        pending.clear()

    def conv_silu(gi, r0, col0):
        cols = slice(col0, col0 + HEAD_DIM)
        rp = pl.multiple_of(jnp.maximum(r0 - DN_HALO, 0), DN_HALO)
        cur = a_ref[0, pl.ds(r0, DN_ROWS), cols].astype(F32)
        prev = a_ref[0, pl.ds(rp, DN_HALO), cols].astype(F32)
        prev = jnp.where(gi > 0, prev, 0.0)
        xw = jnp.concatenate([prev, cur], axis=0)
        conv = None
        for j in range(CONV_WIDTH):
            lo = DN_HALO - (CONV_WIDTH - 1) + j
            term = cw_ref[j:j + 1, cols] * xw[lo:lo + DN_ROWS, :]
            conv = term if conv is None else conv + term
        return _silu(conv)

    def elementwise(gi, handoff):
        r0 = pl.multiple_of(gi * DN_ROWS, DN_ROWS)
        def act_of(h):
            tiles = []
            for base in (0, DN_WIDTH, 2 * DN_WIDTH):
                col0 = base + h * HEAD_DIM
                if col0 < CONV_EARLY_COLS:
                    tiles.append(a_ref[0, pl.ds(r0, DN_ROWS), col0:col0 + HEAD_DIM].astype(F32))
                else:
                    tiles.append(conv_silu(gi, r0, col0))
            return tiles
        bg = bg_ref[0, pl.ds(r0, DN_ROWS), :]
        gc_all = sum(_mxu(cum_lower, piece) for piece in _split3(bg))
        gct_all = sum(_mxu(piece, cum_upper) for piece in _split3(bgt_ref[:, pl.ds(r0, DN_ROWS)]))
        gl_all = jnp.concatenate([jnp.broadcast_to(gc_all[(c + 1) * CHUNK - 1:(c + 1) * CHUNK, :], (CHUNK, LANES))
                                  for c in range(DN_GROUP)], axis=0)
        eg_all = jnp.exp(gl_all)
        for c in range(DN_GROUP):
            defer(eg_ref, (gi * DN_GROUP + c,), eg_all[c * CHUNK:c * CHUNK + 8, :])
        gc_m = jnp.where(g_lanes, gc_all, 0.0)
        gl_m = jnp.where(g_lanes, gl_all, 0.0)
        beta_rep = _mxu(bg.astype(BF16), sel_beta)
        egc_rep = _mxu(jnp.exp(gc_m).astype(BF16), sel_g)
        ekd_rep = _mxu(jnp.exp(gl_m - gc_m).astype(BF16), sel_g)

        qn, knb, kb, rhs = [], [], [], []
        for h in heads:
            qh, kh, vh = act_of(h)
            qn_h = qh * (lax.rsqrt(_mxu((qh * qh).astype(BF16), ones_b) + EPS) * (HEAD_DIM ** -0.5))
            kn_h = kh * lax.rsqrt(_mxu((kh * kh).astype(BF16), ones_b) + EPS)
            kb_h = kn_h * beta_rep[:, hsl(h)]
            defer(kd_ref, (h, pl.ds(r0, DN_ROWS), slice(None)), (kn_h * ekd_rep[:, hsl(h)]).astype(BF16))
            qg_h = (qn_h * egc_rep[:, hsl(h)]).astype(BF16)
            for c in range(DN_GROUP):
                w0 = pl.multiple_of(2 * (r0 + c * CHUNK), 2 * CHUNK)
                defer(wq_ref, (h, pl.ds(w0 + CHUNK, CHUNK), slice(None)), qg_h[cs(c)])
            qn.append(qn_h)
            knb.append(kn_h.astype(BF16))
            kb.append(kb_h)
            rhs.append(jnp.concatenate([vh * beta_rep[:, hsl(h)], kb_h * egc_rep[:, hsl(h)]], axis=1).astype(BF16))
            yield

        qk = [lax.dot_general(jnp.concatenate([qn[h][cs(c)], kb[h][cs(c)]], axis=0).astype(BF16),
                              knb[h][cs(c)], _NT, preferred_element_type=F32) for c, h in problems]
        decay = []
        for c, h in problems:
            diff = gc_all[cs(c), DN_HEADS + h:DN_HEADS + h + 1] - gct_all[DN_HEADS + h:DN_HEADS + h + 1, cs(c)]
            decay.append(jnp.where(causal, jnp.exp(jnp.where(causal, diff, 0.0)), 0.0))
        for i, (c, h) in enumerate(problems):
            a_intra = jnp.where(causal, qk[i][:CHUNK] * decay[i], 0.0)
            defer(ai_ref, (h, pl.ds(r0 + c * CHUNK, CHUNK), slice(None)), a_intra.astype(BF16))
        handoff["neg_lower"] = [-jnp.where(strict, qk[i][CHUNK:] * decay[i], 0.0) for i in range(len(problems))]
        handoff["rhs"] = rhs

    def solve(gi, handoff):
        r0 = pl.multiple_of(gi * DN_ROWS, DN_ROWS)
        ps = handoff["neg_lower"]
        rhs = handoff["rhs"]
        tms = [eye_f + p for p in ps]
        for _ in range(shift - 1):
            pbs = [p.astype(BF16) for p in ps]
            ps = [_mxu(pb, pb) for pb in pbs]
            tms = [tm + _mxu(tm.astype(BF16), p.astype(BF16)) for tm, p in zip(tms, ps)]
        uw = [_mxu(tms[i].astype(BF16), rhs[h][cs(c)]) for i, (c, h) in enumerate(problems)]
        for i, (c, h) in enumerate(problems):
            defer(u_ref, (h, pl.ds(r0 + c * CHUNK, CHUNK), slice(None)), uw[i][:, :HEAD_DIM])
            w0 = pl.multiple_of(2 * (r0 + c * CHUNK), 2 * CHUNK)
            defer(wq_ref, (h, pl.ds(w0, CHUNK), slice(None)), uw[i][:, HEAD_DIM:].astype(BF16))

    def recur(c, state):
        r0 = pl.multiple_of(c * CHUNK, CHUNK)
        rows = pl.ds(r0, CHUNK)
        wrows = pl.ds(pl.multiple_of(2 * r0, 2 * CHUNK), 2 * CHUNK)
        eg = eg_ref[c]
        ws = [_mxu(wq_ref[h, wrows, :], state[h].astype(BF16)) for h in heads]
        vn = [(u_ref[h, rows, :] - ws[h][:CHUNK]).astype(BF16) for h in heads]
        new_state = [state[h] * eg[0:1, DN_HEADS + h:DN_HEADS + h + 1]
                     + lax.dot_general(kd_ref[h, rows, :], vn[h], _TN, preferred_element_type=F32) for h in heads]
        o = [ws[h][CHUNK:] + _mxu(ai_ref[h, rows, :], vn[h]) for h in heads]
        ms = [_mxu((x * x).astype(BF16), ones_b) * (1.0 / HEAD_DIM) for x in o]
        for h in heads:
            on = o[h] * lax.rsqrt(ms[h] + EPS) * nw_ref[...]
            z = a_ref[0, rows, 3 * DN_WIDTH + h * HEAD_DIM:3 * DN_WIDTH + (h + 1) * HEAD_DIM].astype(F32)
            defer(o_ref, (0, rows, hsl(h)), (on * _silu(z)).astype(o_ref.dtype))
        return new_state

    def pipeline_step(g_prep, g_recur, state):
        handoff = {}
        elem = elementwise(g_prep, handoff) if g_prep is not None else iter(())
        for c in range(DN_GROUP):
            if g_recur is not None:
                state = recur(g_recur * DN_GROUP + c, state)
            next(elem, None)
        for _ in elem:
            pass
        if g_prep is not None:
            solve(g_prep, handoff)
        flush()
        return state

    zero = jnp.zeros((HEAD_DIM, HEAD_DIM), F32)
    state = pipeline_step(0, None, [zero] * DN_HEADS)
    state = list(lax.fori_loop(
        1, n_groups, lambda gi, st: tuple(pipeline_step(gi, gi - 1, list(st))), tuple(state)))
    pipeline_step(None, n_groups - 1, state)


def _deltanet(a3, bg3, bgt, conv_w2, dn_norm_row):
    b, t, wa = a3.shape
    nt = bgt.shape[0]
    return pl.pallas_call(
        _deltanet_kernel,
        grid=(b,),
        in_specs=[pl.BlockSpec((1, t, wa), lambda i: (i, 0, 0)),
                  pl.BlockSpec((1, t, LANES), lambda i: (i, 0, 0)),
                  pl.BlockSpec((nt, t), lambda i: (0, i)),
                  pl.BlockSpec((CONV_WIDTH, 3 * DN_WIDTH), lambda i: (0, 0)),
                  pl.BlockSpec((1, HEAD_DIM), lambda i: (0, 0))],
        out_specs=pl.BlockSpec((1, t, DN_WIDTH), lambda i: (i, 0, 0)),
        out_shape=jax.ShapeDtypeStruct((b, t, DN_WIDTH), BF16),
        scratch_shapes=[pltpu.VMEM((DN_HEADS, t, HEAD_DIM), F32),
                        pltpu.VMEM((DN_HEADS, 2 * t, HEAD_DIM), BF16),
                        pltpu.VMEM((DN_HEADS, t, HEAD_DIM), BF16),
                        pltpu.VMEM((DN_HEADS, t, CHUNK), BF16),
                        pltpu.VMEM((t // CHUNK, 8, LANES), F32)],
        compiler_params=pltpu.CompilerParams(dimension_semantics=("parallel",),
                                             vmem_limit_bytes=DN_VMEM_LIMIT_BYTES),
        name="deltanet",
    )(a3, bg3, bgt, conv_w2, dn_norm_row)


SWA_PREP_ROWS = 256
SWA_UNITS = 2


def _swa_kernel(s_ref, tab_ref, o_ref, lse_ref, qs_ref, ks_ref, bias_ref, *dilated_scratch, dilation):
    t_len = s_ref.shape[1]
    d = dilation
    seq = t_len // d
    nblk = seq // SWA_BLOCK
    per = SWA_PREP_ROWS // d
    heads = range(SWA_HEADS)
    hsl = lambda h: slice(h * HEAD_DIM, (h + 1) * HEAD_DIM)
    if d > 1:
        vs_ref, stage_ref, ostage_ref = dilated_scratch
    mean_b = jnp.full((HEAD_DIM, HEAD_DIM), 1.0 / HEAD_DIM, BF16)
    src_lane = lax.broadcasted_iota(jnp.int32, (HEAD_DIM, HEAD_DIM), 0)
    dst_lane = lax.broadcasted_iota(jnp.int32, (HEAD_DIM, HEAD_DIM), 1)
    swap_b = (((dst_lane < ROPE_HALF) & (src_lane == dst_lane + ROPE_HALF))
              | ((dst_lane >= ROPE_HALF) & (dst_lane < ROPE_DIM) & (src_lane == dst_lane - ROPE_HALF))
              ).astype(BF16)

    def prep(n, carry):
        r0 = pl.multiple_of(n * SWA_PREP_ROWS, SWA_PREP_ROWS)
        rows = pl.ds(r0, SWA_PREP_ROWS)
        i0 = pl.multiple_of(n * per, per)

        def regroup(ys, dst, slab0):
            for h, y in enumerate(ys):
                stage_ref[slab0 + h] = y
            for h in heads:
                for r in range(d):
                    part = stage_ref[slab0 + h, pl.ds(r, per, stride=d), :]
                    dst[pl.ds(r * seq + i0, per), hsl(h)] = part.astype(BF16)

        for which, dst in ((0, qs_ref), (1, ks_ref)):
            plain = tab_ref[which, 0, rows, :]
            swapped_tab = tab_ref[which, 1, rows, :]
            xb = [s_ref[0, rows, which * SWA_WIDTH + h * HEAD_DIM:which * SWA_WIDTH + (h + 1) * HEAD_DIM]
                  for h in heads]
            ms = [_mxu(x * x, mean_b) for x in xb]
            sw = [_mxu(x, swap_b) for x in xb]
            ys = [(x.astype(F32) * plain + s * swapped_tab) * lax.rsqrt(m + EPS) for x, s, m in zip(xb, sw, ms)]
            if d == 1:
                for h in heads:
                    dst[rows, hsl(h)] = ys[h].astype(BF16)
            else:
                regroup(ys, dst, which * SWA_HEADS)
        if d > 1:
            vs = [s_ref[0, rows, 2 * SWA_WIDTH + h * HEAD_DIM:2 * SWA_WIDTH + (h + 1) * HEAD_DIM].astype(F32)
                  for h in heads]
            regroup(vs, vs_ref, 2 * SWA_HEADS)
        return carry

    lax.fori_loop(0, t_len // SWA_PREP_ROWS, prep, 0)

    qi = lax.broadcasted_iota(jnp.int32, (SWA_BLOCK, 2 * SWA_BLOCK), 0)
    kj = lax.broadcasted_iota(jnp.int32, (SWA_BLOCK, 2 * SWA_BLOCK), 1)
    bias_ref[0] = jnp.where(kj <= qi, 0.0, NEG_INF).astype(F32)
    bias_ref[1] = jnp.where((kj >= qi) & (kj <= qi + SWA_BLOCK), 0.0, NEG_INF).astype(F32)
    lane = lax.broadcasted_iota(jnp.int32, (SWA_BLOCK, LANES), 1)

    def attend(units):
        jobs = [(u, h) for u in range(len(units)) for h in heads]
        qrows = [pl.ds(q0, SWA_BLOCK) for q0, _, _, _, _ in units]
        krows = [pl.ds(k0, width) for _, k0, width, _, _ in units]
        s = [lax.dot_general(qs_ref[qrows[u], hsl(h)], ks_ref[krows[u], hsl(h)], _NT,
                             preferred_element_type=F32) + units[u][3] for u, h in jobs]
        m = [jnp.max(x, axis=-1, keepdims=True) for x in s]
        p = [jnp.exp(x - mx) for x, mx in zip(s, m)]
        l = [jnp.sum(x, axis=-1, keepdims=True) for x in p]
        if d == 1:
            v = [s_ref[0, krows[u], 2 * SWA_WIDTH + h * HEAD_DIM:2 * SWA_WIDTH + (h + 1) * HEAD_DIM] for u, h in jobs]
        else:
            v = [vs_ref[krows[u], hsl(h)] for u, h in jobs]
        o = [_mxu(x.astype(BF16), vh) * (1.0 / lx) for x, vh, lx in zip(p, v, l)]
        for u in range(len(units)):
            lse_tile = jnp.zeros((SWA_BLOCK, LANES), F32)
            for h in heads:
                j = u * SWA_HEADS + h
                lse_tile = jnp.where(lane == h, m[j] + jnp.log(l[j]), lse_tile)
            t0 = units[u][4]
            if d == 1:
                for h in heads:
                    o_ref[0, qrows[u], hsl(h)] = o[u * SWA_HEADS + h].astype(o_ref.dtype)
                lse_ref[0, qrows[u], :] = lse_tile
            else:
                trows = pl.ds(t0, SWA_BLOCK, stride=d)
                for h in heads:
                    ostage_ref[h, trows, :] = o[u * SWA_HEADS + h]
                lse_ref[0, trows, :] = lse_tile

    if nblk == 1:
        causal_bias = bias_ref[0][:, :SWA_BLOCK]

        def body(i, carry):
            units = []
            for u in range(SWA_UNITS):
                r = i * SWA_UNITS + u
                q0 = pl.multiple_of(r * seq, SWA_BLOCK)
                units.append((q0, q0, SWA_BLOCK, causal_bias, r))
            attend(units)
            return carry

        lax.fori_loop(0, d // SWA_UNITS, body, 0)
    else:
        pairs = nblk // SWA_UNITS

        def body(i, carry):
            r = i // pairs
            pp = i % pairs
            units = []
            for u in range(SWA_UNITS):
                n = pp * SWA_UNITS + u
                q0 = pl.multiple_of(r * seq + n * SWA_BLOCK, SWA_BLOCK)
                if u == 0:
                    first = jnp.asarray(pp == 0).astype(jnp.int32)
                    k0 = pl.multiple_of(q0 - (1 - first) * SWA_BLOCK, SWA_BLOCK)
                    bias = bias_ref[1 - first]
                else:
                    k0 = pl.multiple_of(q0 - SWA_BLOCK, SWA_BLOCK)
                    bias = bias_ref[1]
                units.append((q0, k0, 2 * SWA_BLOCK, bias, r + d * n * SWA_BLOCK))
            attend(units)
            return carry

        lax.fori_loop(0, d * pairs, body, 0)

    if d > 1:
        def emit(n, carry):
            rows = pl.ds(pl.multiple_of(n * SWA_PREP_ROWS, SWA_PREP_ROWS), SWA_PREP_ROWS)
            for h in heads:
                o_ref[0, rows, hsl(h)] = ostage_ref[h, rows, :].astype(o_ref.dtype)
            return carry
        lax.fori_loop(0, t_len // SWA_PREP_ROWS, emit, 0)


def _swa_group(s3, tabs, group, dilation):
    b, t, ws = s3.shape
    assert (t // dilation) % SWA_BLOCK == 0 and SWA_PREP_ROWS % dilation == 0
    nblk = t // dilation // SWA_BLOCK
    assert (dilation % SWA_UNITS == 0) if nblk == 1 else (nblk % SWA_UNITS == 0)
    scratch = [pltpu.VMEM((t, SWA_WIDTH), BF16), pltpu.VMEM((t, SWA_WIDTH), BF16),
               pltpu.VMEM((2, SWA_BLOCK, 2 * SWA_BLOCK), F32)]
    if dilation > 1:
        scratch += [pltpu.VMEM((t, SWA_WIDTH), BF16),
                    pltpu.VMEM((3 * SWA_HEADS, SWA_PREP_ROWS, LANES), F32),
                    pltpu.VMEM((SWA_HEADS, t, LANES), F32)]
    return pl.pallas_call(
        functools.partial(_swa_kernel, dilation=dilation),
        grid=(b,),
        in_specs=[pl.BlockSpec((1, t, ws), lambda i: (i, 0, 0)),
                  pl.BlockSpec((None, 2, 2, t, LANES), lambda i: (group, 0, 0, 0, 0))],
        out_specs=[pl.BlockSpec((1, t, SWA_WIDTH), lambda i: (i, 0, 0)),
                   pl.BlockSpec((1, t, LANES), lambda i: (i, 0, 0))],
        out_shape=[jax.ShapeDtypeStruct((b, t, SWA_WIDTH), BF16),
                   jax.ShapeDtypeStruct((b, t, LANES), F32)],
        scratch_shapes=scratch,
        compiler_params=_params(1),
        name=f"swa_d{dilation}",
    )(s3, tabs)


MERGE_ROWS = 256


def _merge_out_kernel(x_ref, h_ref, odn_ref, o0_ref, o1_ref, o2_ref, l0_ref, l1_ref, l2_ref,
                      wz_ref, wgdn_ref, wgswa_ref, wdn_ref, wswa_ref, wout_ref, out_ref, wzg_scr):
    _cast_weights_once([wz_ref, wgdn_ref, wgswa_ref], wzg_scr)
    zg_all = lax.dot_general(h_ref[...], wzg_scr[...], _NT, preferred_element_type=F32)
    outs = []
    for j in range(ROW_TILE // MERGE_ROWS):
        rows = slice(j * MERGE_ROWS, (j + 1) * MERGE_ROWS)
        zg = zg_all[rows, :]
        l0 = l0_ref[rows, :]
        l1 = l1_ref[rows, :]
        l2 = l2_ref[rows, :]
        m = jnp.maximum(jnp.maximum(l0, l1), l2)
        e0 = jnp.exp(l0 - m)
        e1 = jnp.exp(l1 - m)
        e2 = jnp.exp(l2 - m)
        inv = 1.0 / (e0 + e1 + e2)
        a0, a1, a2 = e0 * inv, e1 * inv, e2 * inv
        parts = []
        for h in range(SWA_HEADS):
            hs = slice(h * HEAD_DIM, (h + 1) * HEAD_DIM)
            col = slice(h, h + 1)
            oh = (a0[:, col] * o0_ref[rows, hs].astype(F32) + a1[:, col] * o1_ref[rows, hs].astype(F32)
                  + a2[:, col] * o2_ref[rows, hs].astype(F32))
            parts.append((oh * _silu(zg[:, hs])).astype(BF16))
        o_swa = jnp.concatenate(parts, axis=1)
        y_swa = _mxu(o_swa, wswa_ref[...])
        y_dn = _mxu(odn_ref[rows, :], wdn_ref[...])
        g_dn = zg[:, SWA_WIDTH:SWA_WIDTH + D_MODEL]
        g_swa = zg[:, SWA_WIDTH + D_MODEL:SWA_WIDTH + 2 * D_MODEL]
        merged = _sigmoid(g_dn) * y_dn + _sigmoid(g_swa) * y_swa
        outs.append((rows, x_ref[rows, :] + _mxu(merged.astype(BF16), wout_ref[...])))
    for rows, value in outs:
        out_ref[rows, :] = value


def _merge_out(x2, h, odn, o_list, lse_list, wt, zg_row_blocks, w_dn, w_swa, w_out):
    n = x2.shape[0]
    row = lambda w: pl.BlockSpec((ROW_TILE, w), lambda i: (i, 0))
    full = lambda a: pl.BlockSpec(a.shape, lambda i: (0, 0), pipeline_mode=pl.Buffered(1))
    nzg = sum(rows for _, rows in zg_row_blocks)
    return pl.pallas_call(
        _merge_out_kernel,
        grid=(n // ROW_TILE,),
        in_specs=[row(D_MODEL), row(D_MODEL), row(DN_WIDTH), row(SWA_WIDTH), row(SWA_WIDTH), row(SWA_WIDTH),
                  row(LANES), row(LANES), row(LANES)]
                 + _weight_specs(zg_row_blocks, D_MODEL)
                 + [full(w_dn), full(w_swa), full(w_out)],
        out_specs=row(D_MODEL),
        out_shape=jax.ShapeDtypeStruct((n, D_MODEL), F32),
        scratch_shapes=[pltpu.VMEM((nzg, D_MODEL), BF16)],
        compiler_params=_params_sequential(),
        name="merge_out",
    )(x2, h, odn, *o_list, *lse_list, wt, wt, wt, w_dn, w_swa, w_out)


def _rope_tables(t_len, q_norm_w, k_norm_w):
    pos = jnp.arange(t_len, dtype=F32)
    inv_freq = ROPE_THETA ** (-jnp.arange(0, ROPE_DIM, 2, dtype=F32) / ROPE_DIM)
    ang = pos[:, None] * inv_freq[None, :]
    cos, sin = jnp.cos(ang), jnp.sin(ang)
    tail = HEAD_DIM - ROPE_DIM
    cos_t = jnp.concatenate([cos, cos, jnp.ones((t_len, tail), F32)], axis=1)
    sin_t = jnp.concatenate([-sin, sin, jnp.zeros((t_len, tail), F32)], axis=1)
    w = jnp.stack([q_norm_w.astype(F32) * (HEAD_DIM ** -0.5), k_norm_w.astype(F32)], axis=1)
    w_swapped = jnp.concatenate([w[..., ROPE_HALF:ROPE_DIM], w[..., :ROPE_HALF], w[..., ROPE_DIM:]], axis=-1)
    plain = w[:, :, None, :] * cos_t[None, None]
    swapped = w_swapped[:, :, None, :] * sin_t[None, None]
    return jnp.stack([plain, swapped], axis=2)


def kernel(x, norm_w, w_in, conv_w, dn_a_log, dn_dt_bias, dn_norm_w, q_norm_w, k_norm_w,
           w_branch_dn, w_branch_swa, w_out):
    b, t, d = x.shape
    n = b * t
    layer = 0
    wt = jnp.swapaxes(w_in[layer], 0, 1)
    c_z = 4 * DN_WIDTH
    c_q = c_z + 2 * DN_HEADS
    c_k = c_q + N_GROUPS * SWA_WIDTH
    c_v = c_k + N_GROUPS * SWA_WIDTH
    c_sz = c_v + N_GROUPS * SWA_WIDTH
    c_g = c_sz + SWA_WIDTH
    grp_blocks = lambda g: [(c0 + g * SWA_WIDTH, SWA_WIDTH) for c0 in (c_q, c_k, c_v)]
    zg_blocks = [(c_sz, SWA_WIDTH), (c_g, D_MODEL), (c_g + D_MODEL, D_MODEL)]
    pad_heads = lambda v: jnp.pad(v.astype(F32), (DN_HEADS, LANES - 2 * DN_HEADS))[None, :]
    alog_row = pad_heads(dn_a_log[layer])
    dt_row = pad_heads(dn_dt_bias[layer])

    x2 = x.reshape(n, d)
    h, a, bg, bgt = _norm_proj(x2, norm_w[layer][None, :], wt, c_z, conv_w[layer][:, 0, :], alog_row, dt_row, t)

    o_dn = _deltanet(a.reshape(b, t, 4 * DN_WIDTH), bg.reshape(b, t, LANES), bgt,
                     conv_w[layer][:, 0, :], dn_norm_w[layer][None, :])

    tabs = _rope_tables(t, q_norm_w[layer], k_norm_w[layer])
    o_list, lse_list = [], []
    for g, (window, dilation) in enumerate(SWA_GROUPS):
        assert window // dilation == SWA_BLOCK
        s_g = _matmul(h, wt, grp_blocks(g), f"proj_swa{g}").reshape(b, t, 3 * SWA_WIDTH)
        o_g, lse_g = _swa_group(s_g, tabs, g, dilation)
        o_list.append(o_g.reshape(n, SWA_WIDTH))
        lse_list.append(lse_g.reshape(n, LANES))

    out = _merge_out(x2, h, o_dn.reshape(n, DN_WIDTH), o_list, lse_list, wt, zg_blocks,
                     w_branch_dn[layer].astype(BF16), w_branch_swa[layer].astype(BF16),
                     w_out[layer].astype(BF16))
    return out.reshape(b, t, d)
```

```python
import functools
import math

import jax
import jax.numpy as jnp
import numpy as np
from jax import lax
from jax.experimental import pallas as pl
from jax.experimental.pallas import tpu as pltpu

D_MODEL = 1024
HEAD_DIM = 128
DN_HEADS = 4
DN_WIDTH = DN_HEADS * HEAD_DIM
CONV_WIDTH = 4
CHUNK = 64
SWA_GROUPS = ((128, 1), (512, 4), (2048, 16))
N_GROUPS = 3
SWA_HEADS = 4
SWA_WIDTH = SWA_HEADS * HEAD_DIM
SWA_BLOCK = 128
ROPE_DIM = HEAD_DIM // 4
ROPE_HALF = ROPE_DIM // 2
ROPE_THETA = 500000.0
EPS = 1e-6
NEG_INF = -1e30

LANES = 128
VMEM_LIMIT_BYTES = 48 * 1024 * 1024
DN_VMEM_LIMIT_BYTES = 56 * 1024 * 1024
ROW_TILE = 512
BG_ROWS = 16

F32 = jnp.float32
BF16 = jnp.bfloat16
_NT = (((1,), (1,)), ((), ()))
_TN = (((0,), (0,)), ((), ()))


def _sigmoid(x):
    return 0.5 * jnp.tanh(0.5 * x) + 0.5


def _silu(x):
    return x * _sigmoid(x)


def _softplus(x):
    return jnp.maximum(x, 0.0) + jnp.log(1.0 + jnp.exp(-jnp.abs(x)))


def _params(n_axes):
    return pltpu.CompilerParams(dimension_semantics=("parallel",) * n_axes,
                                vmem_limit_bytes=VMEM_LIMIT_BYTES)


def _params_sequential():
    return pltpu.CompilerParams(dimension_semantics=("arbitrary",), vmem_limit_bytes=VMEM_LIMIT_BYTES)


def _mxu(a, b):
    return jnp.dot(a, b, preferred_element_type=F32)


def _split3(x):
    hi = x.astype(BF16)
    r1 = x - hi.astype(F32)
    mid = r1.astype(BF16)
    lo = (r1 - mid.astype(F32)).astype(BF16)
    return hi, mid, lo


def _beta_and_log_decay(ab, a_log, dt_bias, head_index):
    beta = _sigmoid(ab)
    g = -jnp.exp(a_log) * _softplus(ab + dt_bias)
    return jnp.where(head_index < DN_HEADS, beta, g)


def _weight_specs(row_blocks, k):
    return [pl.BlockSpec((pl.Element(rows), pl.Element(k)), lambda i, start=start: (start, 0),
                         pipeline_mode=pl.Buffered(1))
            for start, rows in row_blocks]


def _cast_weights_once(w_refs, w_scr):
    @pl.when(pl.program_id(0) == 0)
    def _():
        blocks = [w_ref[...] for w_ref in w_refs]
        pad = w_scr.shape[0] - sum(blk.shape[0] for blk in blocks)
        if pad:
            blocks.append(jnp.zeros((pad, w_scr.shape[1]), F32))
        w_scr[...] = (jnp.concatenate(blocks, axis=0) if len(blocks) > 1 else blocks[0]).astype(BF16)


CONV_TILE = 256
CONV_HALO = 8
CONV_ROWS = 64
CONV_EARLY_COLS = 2 * DN_WIDTH


def _norm_proj_kernel(x_ref, nw_ref, wa_ref, wab_ref, cw_ref, alog_ref, dt_ref, alogt_ref, dtt_ref,
                      h_ref, a_ref, bg_ref, bgt_ref, wa_scr, wab_scr, tail_ref, res_a, res_b, *, tiles_per_seq):
    res_scr = (res_a, res_b)
    _cast_weights_once([wa_ref], wa_scr)
    _cast_weights_once([wab_ref], wab_scr)
    x = x_ref[...]
    h = (x * lax.rsqrt(jnp.mean(x * x, axis=-1, keepdims=True) + EPS)) * nw_ref[...]
    hb = h.astype(BF16)
    h_ref[...] = hb
    n_conv = CONV_EARLY_COLS
    seq_start = pl.program_id(0) % tiles_per_seq == 0
    tail = jnp.where(seq_start, 0.0, tail_ref[...])
    stores = []
    tiles = list(range(0, n_conv, CONV_TILE))
    base = pl.multiple_of((pl.program_id(0) >> 20) * CONV_HALO, CONV_HALO)

    def project(k):
        c0 = tiles[k]
        buf = res_scr[k % 2]
        buf[pl.ds(base, CONV_HALO), :] = tail[:, c0:c0 + CONV_TILE]
        buf[pl.ds(base + CONV_HALO, ROW_TILE), :] = lax.dot_general(hb, wa_scr[c0:c0 + CONV_TILE, :], _NT, preferred_element_type=F32)

    if tiles:
        project(0)
    for k, c0 in enumerate(tiles):
        cols = slice(c0, c0 + CONV_TILE)
        if k + 1 < len(tiles):
            project(k + 1)
        buf = res_scr[k % 2]
        acts = []
        for rb in range(0, ROW_TILE, CONV_ROWS):
            blk = buf[pl.ds(base + rb, CONV_HALO + CONV_ROWS), :]
            conv = None
            for j in range(CONV_WIDTH):
                lo = CONV_HALO - (CONV_WIDTH - 1) + j
                term = cw_ref[j:j + 1, cols] * blk[lo:lo + CONV_ROWS, :]
                conv = term if conv is None else conv + term
            acts.append(_silu(conv).astype(BF16))
        stores.append((a_ref, (slice(None), cols), jnp.concatenate(acts, axis=0)))
        stores.append((tail_ref, (slice(None), cols), buf[pl.ds(base + ROW_TILE, CONV_HALO), :]))
    z = lax.dot_general(hb, wa_scr[n_conv:, :], _NT, preferred_element_type=F32)
    stores.append((a_ref, (slice(None), slice(n_conv, None)), z.astype(BF16)))
    for ref, idx, value in stores:
        ref[idx] = value
    ab = lax.dot_general(hb, wab_scr[...], _NT, preferred_element_type=F32)
    bg_ref[...] = _beta_and_log_decay(ab, alog_ref[...], dt_ref[...],
                                      lax.broadcasted_iota(jnp.int32, ab.shape, 1))
    abt = lax.dot_general(wab_scr[0:BG_ROWS, :], hb, _NT, preferred_element_type=F32)
    bgt_ref[...] = _beta_and_log_decay(abt, alogt_ref[...], dtt_ref[...],
                                       lax.broadcasted_iota(jnp.int32, abt.shape, 0))


def _norm_proj(x2, norm_w, wt, na, conv_w2, alog_row, dt_row, seq_len):
    n = x2.shape[0]
    assert seq_len % ROW_TILE == 0 and CONV_EARLY_COLS % CONV_TILE == 0
    row = lambda i: (i, 0)
    fixed = lambda i: (0, 0)
    pad_col = lambda v: v[0, :BG_ROWS][:, None]
    return pl.pallas_call(
        functools.partial(_norm_proj_kernel, tiles_per_seq=seq_len // ROW_TILE),
        grid=(n // ROW_TILE,),
        in_specs=[pl.BlockSpec((ROW_TILE, D_MODEL), row),
                  pl.BlockSpec((1, D_MODEL), fixed)]
                 + _weight_specs([(0, na), (na, 2 * DN_HEADS)], D_MODEL)
                 + [pl.BlockSpec((CONV_WIDTH, 3 * DN_WIDTH), fixed),
                    pl.BlockSpec((1, LANES), fixed),
                    pl.BlockSpec((1, LANES), fixed),
                    pl.BlockSpec((BG_ROWS, 1), fixed),
                    pl.BlockSpec((BG_ROWS, 1), fixed)],
        out_specs=[pl.BlockSpec((ROW_TILE, D_MODEL), row),
                   pl.BlockSpec((ROW_TILE, na), row),
                   pl.BlockSpec((ROW_TILE, LANES), row),
                   pl.BlockSpec((BG_ROWS, ROW_TILE), lambda i: (0, i))],
        out_shape=[jax.ShapeDtypeStruct((n, D_MODEL), BF16),
                   jax.ShapeDtypeStruct((n, na), BF16),
                   jax.ShapeDtypeStruct((n, LANES), F32),
                   jax.ShapeDtypeStruct((BG_ROWS, n), F32)],
        scratch_shapes=[pltpu.VMEM((na, D_MODEL), BF16), pltpu.VMEM((LANES, D_MODEL), BF16),
                        pltpu.VMEM((CONV_HALO, max(CONV_EARLY_COLS, LANES)), F32),
                        pltpu.VMEM((CONV_HALO + ROW_TILE, CONV_TILE), F32),
                        pltpu.VMEM((CONV_HALO + ROW_TILE, CONV_TILE), F32)],
        compiler_params=_params_sequential(),
        name="norm_proj",
    )(x2, norm_w, wt, wt, conv_w2, alog_row, dt_row, pad_col(alog_row), pad_col(dt_row))


REGROUP_STRIDE = 4


def _matmul_kernel(h_ref, *refs, dilation):
    n_scratch = 1 if dilation == 1 else 3
    w_refs, o_ref, w_scr = refs[:-n_scratch - 1], refs[-n_scratch - 1], refs[-n_scratch]
    _cast_weights_once(w_refs, w_scr)
    if dilation == 1:
        o_ref[...] = lax.dot_general(h_ref[...], w_scr[...], _NT, preferred_element_type=F32).astype(o_ref.dtype)
        return
    stage_ref, stage2_ref = refs[-2], refs[-1]
    s1 = min(dilation, REGROUP_STRIDE)
    s2 = dilation // s1
    rows1 = ROW_TILE // s1
    per = ROW_TILE // dilation
    col = 0
    for w_ref in w_refs:
        width = w_ref.shape[0]
        res = lax.dot_general(h_ref[...], w_scr[col:col + width, :], _NT, preferred_element_type=F32)
        slabs = range(col // LANES, (col + width) // LANES)
        for c in slabs:
            stage_ref[c] = res[:, (c * LANES - col):(c * LANES - col) + LANES]
        for c in slabs:
            for a in range(s1):
                first = stage_ref[c, pl.ds(a, rows1, stride=s1), :]
                if s2 == 1:
                    o_ref[0, a, :, c * LANES:(c + 1) * LANES] = first.astype(o_ref.dtype)
                else:
                    stage2_ref[c, a * rows1:(a + 1) * rows1, :] = first
        if s2 > 1:
            for c in slabs:
                for a in range(s1):
                    for b in range(s2):
                        o_ref[0, b * s1 + a, :, c * LANES:(c + 1) * LANES] = (
                            stage2_ref[c, pl.ds(a * rows1 + b, per, stride=s2), :].astype(o_ref.dtype))
        col += width


def _matmul(h, wt, row_blocks, name, dilation=1, seq_len=None):
    n, k = h.shape
    nc = sum(rows for _, rows in row_blocks)
    scratch = [pltpu.VMEM((nc, k), BF16)]
    if dilation == 1:
        out_spec = pl.BlockSpec((ROW_TILE, nc), lambda i: (i, 0))
        out_shape = jax.ShapeDtypeStruct((n, nc), BF16)
    else:
        tiles = seq_len // ROW_TILE
        per = ROW_TILE // dilation
        assert seq_len % ROW_TILE == 0 and ROW_TILE % dilation == 0 and per % 16 == 0 and nc % LANES == 0
        out_spec = pl.BlockSpec((1, dilation, per, nc), lambda i: (i // tiles, 0, i % tiles, 0))
        out_shape = jax.ShapeDtypeStruct((n // seq_len, dilation, seq_len // dilation, nc), BF16)
        assert dilation % min(dilation, REGROUP_STRIDE) == 0
        scratch += [pltpu.VMEM((nc // LANES, ROW_TILE, LANES), F32)] * 2
    return pl.pallas_call(
        functools.partial(_matmul_kernel, dilation=dilation),
        grid=(n // ROW_TILE,),
        in_specs=[pl.BlockSpec((ROW_TILE, k), lambda i: (i, 0))] + _weight_specs(row_blocks, k),
        out_specs=out_spec,
        out_shape=out_shape,
        scratch_shapes=scratch,
        compiler_params=_params_sequential(),
        name=name,
    )(h, *([wt] * len(row_blocks)))


DN_GROUP = 4
DN_ROWS = DN_GROUP * CHUNK
DN_HALO = 16


def _deltanet_kernel(a_ref, bg_ref, bgt_ref, cw_ref, nw_ref, o_ref,
                     u_ref, wq_ref, kd_ref, ai_ref, eg_ref):
    t_len = a_ref.shape[1]
    n_groups = t_len // DN_ROWS
    ii = lax.broadcasted_iota(jnp.int32, (CHUNK, CHUNK), 0)
    jj = lax.broadcasted_iota(jnp.int32, (CHUNK, CHUNK), 1)
    causal = ii >= jj
    strict = ii > jj
    eye_f = (ii == jj).astype(F32)
    bi = lax.broadcasted_iota(jnp.int32, (DN_ROWS, DN_ROWS), 0)
    bj = lax.broadcasted_iota(jnp.int32, (DN_ROWS, DN_ROWS), 1)
    shift = int(math.log2(CHUNK))
    same_chunk = jnp.right_shift(bi, shift) == jnp.right_shift(bj, shift)
    cum_lower = (same_chunk & (bi >= bj)).astype(BF16)
    cum_upper = (same_chunk & (bi <= bj)).astype(BF16)
    problems = [(c, h) for c in range(DN_GROUP) for h in range(DN_HEADS)]
    heads = range(DN_HEADS)
    hsl = lambda h: slice(h * HEAD_DIM, (h + 1) * HEAD_DIM)
    cs = lambda c: slice(c * CHUNK, (c + 1) * CHUNK)
    ones_b = jnp.ones((HEAD_DIM, HEAD_DIM), BF16)
    sel_src = lax.broadcasted_iota(jnp.int32, (LANES, DN_WIDTH), 0)
    sel_head = jnp.right_shift(lax.broadcasted_iota(jnp.int32, (LANES, DN_WIDTH), 1), int(math.log2(HEAD_DIM)))
    sel_beta = (sel_src == sel_head).astype(BF16)
    sel_g = (sel_src == sel_head + DN_HEADS).astype(BF16)
    lane = lax.broadcasted_iota(jnp.int32, (DN_ROWS, LANES), 1)
    g_lanes = (lane >= DN_HEADS) & (lane < 2 * DN_HEADS)

    pending = []

    def defer(ref, idx, value):
        pending.append((ref, idx, value))

    def flush():
        for ref, idx, value in pending:
            ref[idx] = value
        pending.clear()

    def conv_silu(gi, r0, col0):
        cols = slice(col0, col0 + HEAD_DIM)
        rp = pl.multiple_of(jnp.maximum(r0 - DN_HALO, 0), DN_HALO)
        cur = a_ref[0, pl.ds(r0, DN_ROWS), cols].astype(F32)
        prev = a_ref[0, pl.ds(rp, DN_HALO), cols].astype(F32)
        prev = jnp.where(gi > 0, prev, 0.0)
        xw = jnp.concatenate([prev, cur], axis=0)
        conv = None
        for j in range(CONV_WIDTH):
            lo = DN_HALO - (CONV_WIDTH - 1) + j
            term = cw_ref[j:j + 1, cols] * xw[lo:lo + DN_ROWS, :]
            conv = term if conv is None else conv + term
        return _silu(conv)

    def elementwise(gi, handoff):
        r0 = pl.multiple_of(gi * DN_ROWS, DN_ROWS)
        def act_of(h):
            tiles = []
            for base in (0, DN_WIDTH, 2 * DN_WIDTH):
                col0 = base + h * HEAD_DIM
                if col0 < CONV_EARLY_COLS:
                    tiles.append(a_ref[0, pl.ds(r0, DN_ROWS), col0:col0 + HEAD_DIM].astype(F32))
                else:
                    tiles.append(conv_silu(gi, r0, col0))
            return tiles
        bg = bg_ref[0, pl.ds(r0, DN_ROWS), :]
        gc_all = sum(_mxu(cum_lower, piece) for piece in _split3(bg))
        gct_all = sum(_mxu(piece, cum_upper) for piece in _split3(bgt_ref[:, pl.ds(r0, DN_ROWS)]))
        gl_all = jnp.concatenate([jnp.broadcast_to(gc_all[(c + 1) * CHUNK - 1:(c + 1) * CHUNK, :], (CHUNK, LANES))
                                  for c in range(DN_GROUP)], axis=0)
        eg_all = jnp.exp(gl_all)
        for c in range(DN_GROUP):
            defer(eg_ref, (gi * DN_GROUP + c,), eg_all[c * CHUNK:c * CHUNK + 8, :])
        gc_m = jnp.where(g_lanes, gc_all, 0.0)
        gl_m = jnp.where(g_lanes, gl_all, 0.0)
        beta_rep = _mxu(bg.astype(BF16), sel_beta)
        egc_rep = _mxu(jnp.exp(gc_m).astype(BF16), sel_g)
        ekd_rep = _mxu(jnp.exp(gl_m - gc_m).astype(BF16), sel_g)

        qn, knb, kb, rhs = [], [], [], []
        for h in heads:
            qh, kh, vh = act_of(h)
            qn_h = qh * (lax.rsqrt(_mxu((qh * qh).astype(BF16), ones_b) + EPS) * (HEAD_DIM ** -0.5))
            kn_h = kh * lax.rsqrt(_mxu((kh * kh).astype(BF16), ones_b) + EPS)
            kb_h = kn_h * beta_rep[:, hsl(h)]
            defer(kd_ref, (h, pl.ds(r0, DN_ROWS), slice(None)), (kn_h * ekd_rep[:, hsl(h)]).astype(BF16))
            qg_h = (qn_h * egc_rep[:, hsl(h)]).astype(BF16)
            for c in range(DN_GROUP):
                w0 = pl.multiple_of(2 * (r0 + c * CHUNK), 2 * CHUNK)
                defer(wq_ref, (h, pl.ds(w0 + CHUNK, CHUNK), slice(None)), qg_h[cs(c)])
            qn.append(qn_h)
            knb.append(kn_h.astype(BF16))
            kb.append(kb_h)
            rhs.append(jnp.concatenate([vh * beta_rep[:, hsl(h)], kb_h * egc_rep[:, hsl(h)]], axis=1).astype(BF16))
            yield

        qk = [lax.dot_general(jnp.concatenate([qn[h][cs(c)], kb[h][cs(c)]], axis=0).astype(BF16),
                              knb[h][cs(c)], _NT, preferred_element_type=F32) for c, h in problems]
        decay = []
        for c, h in problems:
            diff = gc_all[cs(c), DN_HEADS + h:DN_HEADS + h + 1] - gct_all[DN_HEADS + h:DN_HEADS + h + 1, cs(c)]
            decay.append(jnp.where(causal, jnp.exp(jnp.where(causal, diff, 0.0)), 0.0))
        for i, (c, h) in enumerate(problems):
            a_intra = jnp.where(causal, qk[i][:CHUNK] * decay[i], 0.0)
            defer(ai_ref, (h, pl.ds(r0 + c * CHUNK, CHUNK), slice(None)), a_intra.astype(BF16))
        handoff["neg_lower"] = [-jnp.where(strict, qk[i][CHUNK:] * decay[i], 0.0) for i in range(len(problems))]
        handoff["rhs"] = rhs

    def solve(gi, handoff):
        r0 = pl.multiple_of(gi * DN_ROWS, DN_ROWS)
        ps = handoff["neg_lower"]
        rhs = handoff["rhs"]
        tms = [eye_f + p for p in ps]
        for _ in range(shift - 1):
            pbs = [p.astype(BF16) for p in ps]
            ps = [_mxu(pb, pb) for pb in pbs]
            tms = [tm + _mxu(tm.astype(BF16), p.astype(BF16)) for tm, p in zip(tms, ps)]
        uw = [_mxu(tms[i].astype(BF16), rhs[h][cs(c)]) for i, (c, h) in enumerate(problems)]
        for i, (c, h) in enumerate(problems):
            defer(u_ref, (h, pl.ds(r0 + c * CHUNK, CHUNK), slice(None)), uw[i][:, :HEAD_DIM])
            w0 = pl.multiple_of(2 * (r0 + c * CHUNK), 2 * CHUNK)
            defer(wq_ref, (h, pl.ds(w0, CHUNK), slice(None)), uw[i][:, HEAD_DIM:].astype(BF16))

    def recur(c, state):
        r0 = pl.multiple_of(c * CHUNK, CHUNK)
        rows = pl.ds(r0, CHUNK)
        wrows = pl.ds(pl.multiple_of(2 * r0, 2 * CHUNK), 2 * CHUNK)
        eg = eg_ref[c]
        ws = [_mxu(wq_ref[h, wrows, :], state[h].astype(BF16)) for h in heads]
        vn = [(u_ref[h, rows, :] - ws[h][:CHUNK]).astype(BF16) for h in heads]
        new_state = [state[h] * eg[0:1, DN_HEADS + h:DN_HEADS + h + 1]
                     + lax.dot_general(kd_ref[h, rows, :], vn[h], _TN, preferred_element_type=F32) for h in heads]
        o = [ws[h][CHUNK:] + _mxu(ai_ref[h, rows, :], vn[h]) for h in heads]
        ms = [_mxu((x * x).astype(BF16), ones_b) * (1.0 / HEAD_DIM) for x in o]
        for h in heads:
            on = o[h] * lax.rsqrt(ms[h] + EPS) * nw_ref[...]
            z = a_ref[0, rows, 3 * DN_WIDTH + h * HEAD_DIM:3 * DN_WIDTH + (h + 1) * HEAD_DIM].astype(F32)
            defer(o_ref, (0, rows, hsl(h)), (on * _silu(z)).astype(o_ref.dtype))
        return new_state

    def pipeline_step(g_prep, g_recur, state):
        handoff = {}
        elem = elementwise(g_prep, handoff) if g_prep is not None else iter(())
        for c in range(DN_GROUP):
            if g_recur is not None:
                state = recur(g_recur * DN_GROUP + c, state)
            next(elem, None)
        for _ in elem:
            pass
        if g_prep is not None:
            solve(g_prep, handoff)
        flush()
        return state

    zero = jnp.zeros((HEAD_DIM, HEAD_DIM), F32)
    state = pipeline_step(0, None, [zero] * DN_HEADS)
    state = list(lax.fori_loop(
        1, n_groups, lambda gi, st: tuple(pipeline_step(gi, gi - 1, list(st))), tuple(state)))
    pipeline_step(None, n_groups - 1, state)


def _deltanet(a3, bg3, bgt, conv_w2, dn_norm_row):
    b, t, wa = a3.shape
    nt = bgt.shape[0]
    return pl.pallas_call(
        _deltanet_kernel,
        grid=(b,),
        in_specs=[pl.BlockSpec((1, t, wa), lambda i: (i, 0, 0)),
                  pl.BlockSpec((1, t, LANES), lambda i: (i, 0, 0)),
                  pl.BlockSpec((nt, t), lambda i: (0, i)),
                  pl.BlockSpec((CONV_WIDTH, 3 * DN_WIDTH), lambda i: (0, 0)),
                  pl.BlockSpec((1, HEAD_DIM), lambda i: (0, 0))],
        out_specs=pl.BlockSpec((1, t, DN_WIDTH), lambda i: (i, 0, 0)),
        out_shape=jax.ShapeDtypeStruct((b, t, DN_WIDTH), BF16),
        scratch_shapes=[pltpu.VMEM((DN_HEADS, t, HEAD_DIM), F32),
                        pltpu.VMEM((DN_HEADS, 2 * t, HEAD_DIM), BF16),
                        pltpu.VMEM((DN_HEADS, t, HEAD_DIM), BF16),
                        pltpu.VMEM((DN_HEADS, t, CHUNK), BF16),
                        pltpu.VMEM((t // CHUNK, 8, LANES), F32)],
        compiler_params=pltpu.CompilerParams(dimension_semantics=("parallel",),
                                             vmem_limit_bytes=DN_VMEM_LIMIT_BYTES),
        name="deltanet",
    )(a3, bg3, bgt, conv_w2, dn_norm_row)


SWA_PREP_ROWS = 256
SWA_UNITS = 2


def _swa_kernel(s_ref, plain_ref, swapped_ref, o_ref, lse_ref, qs_ref, ks_ref, bias_ref, *dilated_scratch, dilation):
    t_len = s_ref.shape[1]
    d = dilation
    seq = t_len // d
    nblk = seq // SWA_BLOCK
    heads = range(SWA_HEADS)
    hsl = lambda h: slice(h * HEAD_DIM, (h + 1) * HEAD_DIM)
    if d > 1:
        (ostage_ref,) = dilated_scratch
    mean_b = jnp.full((HEAD_DIM, HEAD_DIM), 1.0 / HEAD_DIM, BF16)
    src_lane = lax.broadcasted_iota(jnp.int32, (HEAD_DIM, HEAD_DIM), 0)
    dst_lane = lax.broadcasted_iota(jnp.int32, (HEAD_DIM, HEAD_DIM), 1)
    swap_b = (((dst_lane < ROPE_HALF) & (src_lane == dst_lane + ROPE_HALF))
              | ((dst_lane >= ROPE_HALF) & (dst_lane < ROPE_DIM) & (src_lane == dst_lane - ROPE_HALF))
              ).astype(BF16)

    def prep(n, carry):
        r0 = pl.multiple_of(n * SWA_PREP_ROWS, SWA_PREP_ROWS)
        rows = pl.ds(r0, SWA_PREP_ROWS)
        for which, dst in ((0, qs_ref), (1, ks_ref)):
            plain = plain_ref[which, rows, :]
            swapped_tab = swapped_ref[which, rows, :]
            xb = [s_ref[0, rows, which * SWA_WIDTH + h * HEAD_DIM:which * SWA_WIDTH + (h + 1) * HEAD_DIM]
                  for h in heads]
            ms = [_mxu(x * x, mean_b) for x in xb]
            sw = [_mxu(x, swap_b) for x in xb]
            ys = [(x.astype(F32) * plain + s * swapped_tab) * lax.rsqrt(m + EPS) for x, s, m in zip(xb, sw, ms)]
            for h in heads:
                dst[rows, hsl(h)] = ys[h].astype(BF16)
        return carry

    lax.fori_loop(0, t_len // SWA_PREP_ROWS, prep, 0)

    qi = lax.broadcasted_iota(jnp.int32, (SWA_BLOCK, 2 * SWA_BLOCK), 0)
    kj = lax.broadcasted_iota(jnp.int32, (SWA_BLOCK, 2 * SWA_BLOCK), 1)
    bias_ref[0] = jnp.where(kj <= qi, 0.0, NEG_INF).astype(F32)
    bias_ref[1] = jnp.where((kj >= qi) & (kj <= qi + SWA_BLOCK), 0.0, NEG_INF).astype(F32)
    lane = lax.broadcasted_iota(jnp.int32, (SWA_BLOCK, LANES), 1)

    def attend(units):
        jobs = [(u, h) for u in range(len(units)) for h in heads]
        qrows = [pl.ds(q0, SWA_BLOCK) for q0, _, _, _, _ in units]
        krows = [pl.ds(k0, width) for _, k0, width, _, _ in units]
        s = [lax.dot_general(qs_ref[qrows[u], hsl(h)], ks_ref[krows[u], hsl(h)], _NT,
                             preferred_element_type=F32) + units[u][3] for u, h in jobs]
        m = [jnp.max(x, axis=-1, keepdims=True) for x in s]
        p = [jnp.exp(x - mx) for x, mx in zip(s, m)]
        l = [jnp.sum(x, axis=-1, keepdims=True) for x in p]
        v = [s_ref[0, krows[u], 2 * SWA_WIDTH + h * HEAD_DIM:2 * SWA_WIDTH + (h + 1) * HEAD_DIM] for u, h in jobs]
        o =[_mxu(x.astype(BF16), vh) * (1.0 / lx) for x, vh, lx in zip(p, v, l)]
        for u in range(len(units)):
            lse_tile = jnp.zeros((SWA_BLOCK, LANES), F32)
            for h in heads:
                j = u * SWA_HEADS + h
                lse_tile = jnp.where(lane == h, m[j] + jnp.log(l[j]), lse_tile)
            t0 = units[u][4]
            if d == 1:
                for h in heads:
                    o_ref[0, qrows[u], hsl(h)] = o[u * SWA_HEADS + h].astype(o_ref.dtype)
                lse_ref[0, qrows[u], :] = lse_tile
            else:
                trows = pl.ds(t0, SWA_BLOCK, stride=d)
                for h in heads:
                    ostage_ref[h, trows, :] = o[u * SWA_HEADS + h]
                lse_ref[0, trows, :] = lse_tile

    if nblk == 1:
        causal_bias = bias_ref[0][:, :SWA_BLOCK]

        def body(i, carry):
            units = []
            for u in range(SWA_UNITS):
                r = i * SWA_UNITS + u
                q0 = pl.multiple_of(r * seq, SWA_BLOCK)
                units.append((q0, q0, SWA_BLOCK, causal_bias, r))
            attend(units)
            return carry

        lax.fori_loop(0, d // SWA_UNITS, body, 0)
    else:
        pairs = nblk // SWA_UNITS

        def body(i, carry):
            r = i // pairs
            pp = i % pairs
            units = []
            for u in range(SWA_UNITS):
                n = pp * SWA_UNITS + u
                q0 = pl.multiple_of(r * seq + n * SWA_BLOCK, SWA_BLOCK)
                if u == 0:
                    first = jnp.asarray(pp == 0).astype(jnp.int32)
                    k0 = pl.multiple_of(q0 - (1 - first) * SWA_BLOCK, SWA_BLOCK)
                    bias = bias_ref[1 - first]
                else:
                    k0 = pl.multiple_of(q0 - SWA_BLOCK, SWA_BLOCK)
                    bias = bias_ref[1]
                units.append((q0, k0, 2 * SWA_BLOCK, bias, r + d * n * SWA_BLOCK))
            attend(units)
            return carry

        lax.fori_loop(0, d * pairs, body, 0)

    if d > 1:
        def emit(n, carry):
            rows = pl.ds(pl.multiple_of(n * SWA_PREP_ROWS, SWA_PREP_ROWS), SWA_PREP_ROWS)
            for h in heads:
                o_ref[0, rows, hsl(h)] = ostage_ref[h, rows, :].astype(o_ref.dtype)
            return carry
        lax.fori_loop(0, t_len // SWA_PREP_ROWS, emit, 0)


def _swa_group(s3, plain, swapped, group, dilation):
    b, t, ws = s3.shape
    assert (t // dilation) % SWA_BLOCK == 0 and t % SWA_PREP_ROWS == 0
    nblk = t // dilation // SWA_BLOCK
    assert (dilation % SWA_UNITS == 0) if nblk == 1 else (nblk % SWA_UNITS == 0)
    scratch = [pltpu.VMEM((t, SWA_WIDTH), BF16), pltpu.VMEM((t, SWA_WIDTH), BF16),
               pltpu.VMEM((2, SWA_BLOCK, 2 * SWA_BLOCK), F32)]
    if dilation > 1:
        scratch += [pltpu.VMEM((SWA_HEADS, t, LANES), F32)]
    tab_spec = pl.BlockSpec((None, 2, t, LANES), lambda i: (group, 0, 0, 0))
    return pl.pallas_call(
        functools.partial(_swa_kernel, dilation=dilation),
        grid=(b,),
        in_specs=[pl.BlockSpec((1, t, ws), lambda i: (i, 0, 0)), tab_spec, tab_spec],
        out_specs=[pl.BlockSpec((1, t, SWA_WIDTH), lambda i: (i, 0, 0)),
                   pl.BlockSpec((1, t, LANES), lambda i: (i, 0, 0))],
        out_shape=[jax.ShapeDtypeStruct((b, t, SWA_WIDTH), BF16),
                   jax.ShapeDtypeStruct((b, t, LANES), F32)],
        scratch_shapes=scratch,
        compiler_params=_params(1),
        name=f"swa_d{dilation}",
    )(s3, plain, swapped)


MERGE_ROWS = 256


def _merge_out_kernel(x_ref, h_ref, odn_ref, o0_ref, o1_ref, o2_ref, l0_ref, l1_ref, l2_ref,
                      wz_ref, wgdn_ref, wgswa_ref, wdn_ref, wswa_ref, wout_ref, out_ref, wzg_scr):
    _cast_weights_once([wz_ref, wgdn_ref, wgswa_ref], wzg_scr)
    zg_all = lax.dot_general(h_ref[...], wzg_scr[...], _NT, preferred_element_type=F32)
    outs = []
    for j in range(ROW_TILE // MERGE_ROWS):
        rows = slice(j * MERGE_ROWS, (j + 1) * MERGE_ROWS)
        zg = zg_all[rows, :]
        l0 = l0_ref[rows, :]
        l1 = l1_ref[rows, :]
        l2 = l2_ref[rows, :]
        m = jnp.maximum(jnp.maximum(l0, l1), l2)
        e0 = jnp.exp(l0 - m)
        e1 = jnp.exp(l1 - m)
        e2 = jnp.exp(l2 - m)
        inv = 1.0 / (e0 + e1 + e2)
        a0, a1, a2 = e0 * inv, e1 * inv, e2 * inv
        parts = []
        for h in range(SWA_HEADS):
            hs = slice(h * HEAD_DIM, (h + 1) * HEAD_DIM)
            col = slice(h, h + 1)
            oh = (a0[:, col] * o0_ref[rows, hs].astype(F32) + a1[:, col] * o1_ref[rows, hs].astype(F32)
                  + a2[:, col] * o2_ref[rows, hs].astype(F32))
            parts.append((oh * _silu(zg[:, hs])).astype(BF16))
        o_swa = jnp.concatenate(parts, axis=1)
        y_swa = _mxu(o_swa, wswa_ref[...])
        y_dn = _mxu(odn_ref[rows, :], wdn_ref[...])
        g_dn = zg[:, SWA_WIDTH:SWA_WIDTH + D_MODEL]
        g_swa = zg[:, SWA_WIDTH + D_MODEL:SWA_WIDTH + 2 * D_MODEL]
        merged = _sigmoid(g_dn) * y_dn + _sigmoid(g_swa) * y_swa
        outs.append((rows, x_ref[rows, :] + _mxu(merged.astype(BF16), wout_ref[...])))
    for rows, value in outs:
        out_ref[rows, :] = value


def _merge_out(x2, h, odn, o_list, lse_list, wt, zg_row_blocks, w_dn, w_swa, w_out):
    n = x2.shape[0]
    row = lambda w: pl.BlockSpec((ROW_TILE, w), lambda i: (i, 0))
    full = lambda a: pl.BlockSpec(a.shape, lambda i: (0, 0), pipeline_mode=pl.Buffered(1))
    nzg = sum(rows for _, rows in zg_row_blocks)
    return pl.pallas_call(
        _merge_out_kernel,
        grid=(n // ROW_TILE,),
        in_specs=[row(D_MODEL), row(D_MODEL), row(DN_WIDTH), row(SWA_WIDTH), row(SWA_WIDTH), row(SWA_WIDTH),
                  row(LANES), row(LANES), row(LANES)]
                 + _weight_specs(zg_row_blocks, D_MODEL)
                 + [full(w_dn), full(w_swa), full(w_out)],
        out_specs=row(D_MODEL),
        out_shape=jax.ShapeDtypeStruct((n, D_MODEL), F32),
        scratch_shapes=[pltpu.VMEM((nzg, D_MODEL), BF16)],
        compiler_params=_params_sequential(),
        name="merge_out",
    )(x2, h, odn, *o_list, *lse_list, wt, wt, wt, w_dn, w_swa, w_out)


def _rope_tables(t_len, q_norm_w, k_norm_w):
    j = jnp.arange(t_len, dtype=jnp.int32)
    pos = jnp.stack([((j % (t_len // d)) * d + j // (t_len // d)).astype(F32) for _, d in SWA_GROUPS])
    inv_freq = ROPE_THETA ** (-jnp.arange(0, ROPE_DIM, 2, dtype=F32) / ROPE_DIM)
    ang = pos[:, :, None] * inv_freq[None, None, :]
    cos, sin = jnp.cos(ang), jnp.sin(ang)
    tail = (N_GROUPS, t_len, HEAD_DIM - ROPE_DIM)
    cos_t = jnp.concatenate([cos, cos, jnp.ones(tail, F32)], axis=-1)
    sin_t = jnp.concatenate([-sin, sin, jnp.zeros(tail, F32)], axis=-1)
    w = jnp.stack([q_norm_w.astype(F32) * (HEAD_DIM ** -0.5), k_norm_w.astype(F32)], axis=1)
    w_swapped = jnp.concatenate([w[..., ROPE_HALF:ROPE_DIM], w[..., :ROPE_HALF], w[..., ROPE_DIM:]], axis=-1)
    return w[:, :, None, :] * cos_t[:, None], w_swapped[:, :, None, :] * sin_t[:, None]


def kernel(x, norm_w, w_in, conv_w, dn_a_log, dn_dt_bias, dn_norm_w, q_norm_w, k_norm_w,
           w_branch_dn, w_branch_swa, w_out):
    b, t, d = x.shape
    n = b * t
    layer = 0
    wt = jnp.swapaxes(w_in[layer], 0, 1)
    c_z = 4 * DN_WIDTH
    c_q = c_z + 2 * DN_HEADS
    c_k = c_q + N_GROUPS * SWA_WIDTH
    c_v = c_k + N_GROUPS * SWA_WIDTH
    c_sz = c_v + N_GROUPS * SWA_WIDTH
    c_g = c_sz + SWA_WIDTH
    grp_blocks = lambda g: [(c0 + g * SWA_WIDTH, SWA_WIDTH) for c0 in (c_q, c_k, c_v)]
    zg_blocks = [(c_sz, SWA_WIDTH), (c_g, D_MODEL), (c_g + D_MODEL, D_MODEL)]
    pad_heads = lambda v: jnp.pad(v.astype(F32), (DN_HEADS, LANES - 2 * DN_HEADS))[None, :]
    alog_row = pad_heads(dn_a_log[layer])
    dt_row = pad_heads(dn_dt_bias[layer])

    x2 = x.reshape(n, d)
    h, a, bg, bgt = _norm_proj(x2, norm_w[layer][None, :], wt, c_z, conv_w[layer][:, 0, :], alog_row, dt_row, t)

    o_dn = _deltanet(a.reshape(b, t, 4 * DN_WIDTH), bg.reshape(b, t, LANES), bgt,
                     conv_w[layer][:, 0, :], dn_norm_w[layer][None, :])

    plain, swapped = _rope_tables(t, q_norm_w[layer], k_norm_w[layer])
    o_list, lse_list = [], []
    for g, (window, dilation) in enumerate(SWA_GROUPS):
        assert window // dilation == SWA_BLOCK
        s_g = _matmul(h, wt, grp_blocks(g), f"proj_swa{g}", dilation, t).reshape(b, t, 3 * SWA_WIDTH)
        o_g, lse_g = _swa_group(s_g, plain, swapped, g, dilation)
        o_list.append(o_g.reshape(n, SWA_WIDTH))
        lse_list.append(lse_g.reshape(n, LANES))

    out = _merge_out(x2, h, o_dn.reshape(n, DN_WIDTH), o_list, lse_list, wt, zg_blocks,
                     w_branch_dn[layer].astype(BF16), w_branch_swa[layer].astype(BF16),
                     w_out[layer].astype(BF16))
    return out.reshape(b, t, d)
```

```python
import functools
import math

import jax
import jax.numpy as jnp
import numpy as np
from jax import lax
from jax.experimental import pallas as pl
from jax.experimental.pallas import tpu as pltpu

D_MODEL = 1024
HEAD_DIM = 128
DN_HEADS = 4
DN_WIDTH = DN_HEADS * HEAD_DIM
CONV_WIDTH = 4
CHUNK = 64
SWA_GROUPS = ((128, 1), (512, 4), (2048, 16))
N_GROUPS = 3
SWA_HEADS = 4
SWA_WIDTH = SWA_HEADS * HEAD_DIM
SWA_BLOCK = 128
ROPE_DIM = HEAD_DIM // 4
ROPE_HALF = ROPE_DIM // 2
ROPE_THETA = 500000.0
EPS = 1e-6
NEG_INF = -1e30

LANES = 128
VMEM_LIMIT_BYTES = 48 * 1024 * 1024
DN_VMEM_LIMIT_BYTES = 56 * 1024 * 1024
ROW_TILE = 512
BG_ROWS = 16

F32 = jnp.float32
BF16 = jnp.bfloat16
_NT = (((1,), (1,)), ((), ()))
_TN = (((0,), (0,)), ((), ()))


def _sigmoid(x):
    return 0.5 * jnp.tanh(0.5 * x) + 0.5


def _silu(x):
    return x * _sigmoid(x)


def _softplus(x):
    return jnp.maximum(x, 0.0) + jnp.log(1.0 + jnp.exp(-jnp.abs(x)))


def _params(n_axes):
    return pltpu.CompilerParams(dimension_semantics=("parallel",) * n_axes,
                                vmem_limit_bytes=VMEM_LIMIT_BYTES)


def _params_sequential():
    return pltpu.CompilerParams(dimension_semantics=("arbitrary",), vmem_limit_bytes=VMEM_LIMIT_BYTES)


def _mxu(a, b):
    return jnp.dot(a, b, preferred_element_type=F32)


def _split3(x):
    hi = x.astype(BF16)
    r1 = x - hi.astype(F32)
    mid = r1.astype(BF16)
    lo = (r1 - mid.astype(F32)).astype(BF16)
    return hi, mid, lo


def _beta_and_log_decay(ab, a_log, dt_bias, head_index):
    beta = _sigmoid(ab)
    g = -jnp.exp(a_log) * _softplus(ab + dt_bias)
    return jnp.where(head_index < DN_HEADS, beta, g)


def _weight_specs(row_blocks, k):
    return [pl.BlockSpec((pl.Element(rows), pl.Element(k)), lambda i, start=start: (start, 0),
                         pipeline_mode=pl.Buffered(1))
            for start, rows in row_blocks]


def _cast_weights_once(w_refs, w_scr):
    @pl.when(pl.program_id(0) == 0)
    def _():
        blocks = [w_ref[...] for w_ref in w_refs]
        pad = w_scr.shape[0] - sum(blk.shape[0] for blk in blocks)
        if pad:
            blocks.append(jnp.zeros((pad, w_scr.shape[1]), F32))
        w_scr[...] = (jnp.concatenate(blocks, axis=0) if len(blocks) > 1 else blocks[0]).astype(BF16)


CONV_TILE = 256
CONV_HALO = 8
CONV_ROWS = 64
CONV_EARLY_COLS = 2 * DN_WIDTH


def _norm_proj_kernel(x_ref, nw_ref, wa_ref, wab_ref, cw_ref, alog_ref, dt_ref, alogt_ref, dtt_ref,
                      h_ref, a_ref, bg_ref, bgt_ref, wa_scr, wab_scr, tail_ref, res_a, res_b, *, tiles_per_seq):
    res_scr = (res_a, res_b)
    _cast_weights_once([wa_ref], wa_scr)
    _cast_weights_once([wab_ref], wab_scr)
    x = x_ref[...]
    h = (x * lax.rsqrt(jnp.mean(x * x, axis=-1, keepdims=True) + EPS)) * nw_ref[...]
    hb = h.astype(BF16)
    h_ref[...] = hb
    n_conv = CONV_EARLY_COLS
    seq_start = pl.program_id(0) % tiles_per_seq == 0
    tail = jnp.where(seq_start, 0.0, tail_ref[...])
    stores = []
    tiles = list(range(0, n_conv, CONV_TILE))
    base = pl.multiple_of((pl.program_id(0) >> 20) * CONV_HALO, CONV_HALO)

    def project(k):
        c0 = tiles[k]
        buf = res_scr[k % 2]
        buf[pl.ds(base, CONV_HALO), :] = tail[:, c0:c0 + CONV_TILE]
        buf[pl.ds(base + CONV_HALO, ROW_TILE), :] = lax.dot_general(hb, wa_scr[c0:c0 + CONV_TILE, :], _NT, preferred_element_type=F32)

    if tiles:
        project(0)
    for k, c0 in enumerate(tiles):
        cols = slice(c0, c0 + CONV_TILE)
        if k + 1 < len(tiles):
            project(k + 1)
        buf = res_scr[k % 2]
        acts = []
        for rb in range(0, ROW_TILE, CONV_ROWS):
            blk = buf[pl.ds(base + rb, CONV_HALO + CONV_ROWS), :]
            conv = None
            for j in range(CONV_WIDTH):
                lo = CONV_HALO - (CONV_WIDTH - 1) + j
                term = cw_ref[j:j + 1, cols] * blk[lo:lo + CONV_ROWS, :]
                conv = term if conv is None else conv + term
            acts.append(_silu(conv).astype(BF16))
        stores.append((a_ref, (slice(None), cols), jnp.concatenate(acts, axis=0)))
        stores.append((tail_ref, (slice(None), cols), buf[pl.ds(base + ROW_TILE, CONV_HALO), :]))
    z = lax.dot_general(hb, wa_scr[n_conv:, :], _NT, preferred_element_type=F32)
    stores.append((a_ref, (slice(None), slice(n_conv, None)), z.astype(BF16)))
    for ref, idx, value in stores:
        ref[idx] = value
    ab = lax.dot_general(hb, wab_scr[...], _NT, preferred_element_type=F32)
    bg_ref[...] = _beta_and_log_decay(ab, alog_ref[...], dt_ref[...],
                                      lax.broadcasted_iota(jnp.int32, ab.shape, 1))
    abt = lax.dot_general(wab_scr[0:BG_ROWS, :], hb, _NT, preferred_element_type=F32)
    bgt_ref[...] = _beta_and_log_decay(abt, alogt_ref[...], dtt_ref[...],
                                       lax.broadcasted_iota(jnp.int32, abt.shape, 0))


def _norm_proj(x2, norm_w, wt, na, conv_w2, alog_row, dt_row, seq_len):
    n = x2.shape[0]
    assert seq_len % ROW_TILE == 0 and CONV_EARLY_COLS % CONV_TILE == 0
    row = lambda i: (i, 0)
    fixed = lambda i: (0, 0)
    pad_col = lambda v: v[0, :BG_ROWS][:, None]
    return pl.pallas_call(
        functools.partial(_norm_proj_kernel, tiles_per_seq=seq_len // ROW_TILE),
        grid=(n // ROW_TILE,),
        in_specs=[pl.BlockSpec((ROW_TILE, D_MODEL), row),
                  pl.BlockSpec((1, D_MODEL), fixed)]
                 + _weight_specs([(0, na), (na, 2 * DN_HEADS)], D_MODEL)
                 + [pl.BlockSpec((CONV_WIDTH, 3 * DN_WIDTH), fixed),
                    pl.BlockSpec((1, LANES), fixed),
                    pl.BlockSpec((1, LANES), fixed),
                    pl.BlockSpec((BG_ROWS, 1), fixed),
                    pl.BlockSpec((BG_ROWS, 1), fixed)],
        out_specs=[pl.BlockSpec((ROW_TILE, D_MODEL), row),
                   pl.BlockSpec((ROW_TILE, na), row),
                   pl.BlockSpec((ROW_TILE, LANES), row),
                   pl.BlockSpec((BG_ROWS, ROW_TILE), lambda i: (0, i))],
        out_shape=[jax.ShapeDtypeStruct((n, D_MODEL), BF16),
                   jax.ShapeDtypeStruct((n, na), BF16),
                   jax.ShapeDtypeStruct((n, LANES), F32),
                   jax.ShapeDtypeStruct((BG_ROWS, n), F32)],
        scratch_shapes=[pltpu.VMEM((na, D_MODEL), BF16), pltpu.VMEM((LANES, D_MODEL), BF16),
                        pltpu.VMEM((CONV_HALO, max(CONV_EARLY_COLS, LANES)), F32),
                        pltpu.VMEM((CONV_HALO + ROW_TILE, CONV_TILE), F32),
                        pltpu.VMEM((CONV_HALO + ROW_TILE, CONV_TILE), F32)],
        compiler_params=_params_sequential(),
        name="norm_proj",
    )(x2, norm_w, wt, wt, conv_w2, alog_row, dt_row, pad_col(alog_row), pad_col(dt_row))


REGROUP_STRIDE = 4


def _matmul_kernel(h_ref, *refs, dilation):
    n_scratch = 1 if dilation == 1 else 3
    w_refs, o_ref, w_scr = refs[:-n_scratch - 1], refs[-n_scratch - 1], refs[-n_scratch]
    _cast_weights_once(w_refs, w_scr)
    if dilation == 1:
        o_ref[...] = lax.dot_general(h_ref[...], w_scr[...], _NT, preferred_element_type=F32).astype(o_ref.dtype)
        return
    stage_ref, stage2_ref = refs[-2], refs[-1]
    s1 = min(dilation, REGROUP_STRIDE)
    s2 = dilation // s1
    rows1 = ROW_TILE // s1
    per = ROW_TILE // dilation
    col = 0
    for w_ref in w_refs:
        width = w_ref.shape[0]
        res = lax.dot_general(h_ref[...], w_scr[col:col + width, :], _NT, preferred_element_type=F32)
        slabs = range(col // LANES, (col + width) // LANES)
        for c in slabs:
            stage_ref[c] = res[:, (c * LANES - col):(c * LANES - col) + LANES]
        for c in slabs:
            for a in range(s1):
                first = stage_ref[c, pl.ds(a, rows1, stride=s1), :]
                if s2 == 1:
                    o_ref[0, a, :, c * LANES:(c + 1) * LANES] = first.astype(o_ref.dtype)
                else:
                    stage2_ref[c, a * rows1:(a + 1) * rows1, :] = first
        if s2 > 1:
            for c in slabs:
                for a in range(s1):
                    for b in range(s2):
                        o_ref[0, b * s1 + a, :, c * LANES:(c + 1) * LANES] = (
                            stage2_ref[c, pl.ds(a * rows1 + b, per, stride=s2), :].astype(o_ref.dtype))
        col += width


def _matmul(h, wt, row_blocks, name, dilation=1, seq_len=None):
    n, k = h.shape
    nc = sum(rows for _, rows in row_blocks)
    scratch = [pltpu.VMEM((nc, k), BF16)]
    if dilation == 1:
        out_spec = pl.BlockSpec((ROW_TILE, nc), lambda i: (i, 0))
        out_shape = jax.ShapeDtypeStruct((n, nc), BF16)
    else:
        tiles = seq_len // ROW_TILE
        per = ROW_TILE // dilation
        assert seq_len % ROW_TILE == 0 and ROW_TILE % dilation == 0 and per % 16 == 0 and nc % LANES == 0
        out_spec = pl.BlockSpec((1, dilation, per, nc), lambda i: (i // tiles, 0, i % tiles, 0))
        out_shape = jax.ShapeDtypeStruct((n // seq_len, dilation, seq_len // dilation, nc), BF16)
        assert dilation % min(dilation, REGROUP_STRIDE) == 0
        scratch += [pltpu.VMEM((nc // LANES, ROW_TILE, LANES), F32)] * 2
    return pl.pallas_call(
        functools.partial(_matmul_kernel, dilation=dilation),
        grid=(n // ROW_TILE,),
        in_specs=[pl.BlockSpec((ROW_TILE, k), lambda i: (i, 0))] + _weight_specs(row_blocks, k),
        out_specs=out_spec,
        out_shape=out_shape,
        scratch_shapes=scratch,
        compiler_params=_params_sequential(),
        name=name,
    )(h, *([wt] * len(row_blocks)))


DN_GROUP = 8
DN_ROWS = DN_GROUP * CHUNK
DN_HALO = 16


def _deltanet_kernel(a_ref, bg_ref, bgt_ref, cw_ref, nw_ref, o_ref,
                     u_ref, wq_ref, kd_ref, ai_ref, eg_ref):
    t_len = a_ref.shape[1]
    n_groups = t_len // DN_ROWS
    ii = lax.broadcasted_iota(jnp.int32, (CHUNK, CHUNK), 0)
    jj = lax.broadcasted_iota(jnp.int32, (CHUNK, CHUNK), 1)
    causal = ii >= jj
    strict = ii > jj
    eye_f = (ii == jj).astype(F32)
    bi = lax.broadcasted_iota(jnp.int32, (DN_ROWS, DN_ROWS), 0)
    bj = lax.broadcasted_iota(jnp.int32, (DN_ROWS, DN_ROWS), 1)
    shift = int(math.log2(CHUNK))
    same_chunk = jnp.right_shift(bi, shift) == jnp.right_shift(bj, shift)
    cum_lower = (same_chunk & (bi >= bj)).astype(BF16)
    cum_upper = (same_chunk & (bi <= bj)).astype(BF16)
    problems = [(c, h) for c in range(DN_GROUP) for h in range(DN_HEADS)]
    heads = range(DN_HEADS)
    hsl = lambda h: slice(h * HEAD_DIM, (h + 1) * HEAD_DIM)
    cs = lambda c: slice(c * CHUNK, (c + 1) * CHUNK)
    ones_b = jnp.ones((HEAD_DIM, HEAD_DIM), BF16)
    sel_src = lax.broadcasted_iota(jnp.int32, (LANES, DN_WIDTH), 0)
    sel_head = jnp.right_shift(lax.broadcasted_iota(jnp.int32, (LANES, DN_WIDTH), 1), int(math.log2(HEAD_DIM)))
    sel_beta = (sel_src == sel_head).astype(BF16)
    sel_g = (sel_src == sel_head + DN_HEADS).astype(BF16)
    lane = lax.broadcasted_iota(jnp.int32, (DN_ROWS, LANES), 1)
    g_lanes = (lane >= DN_HEADS) & (lane < 2 * DN_HEADS)

    pending = []

    def defer(ref, idx, value):
        pending.append((ref, idx, value))

    def flush():
        for ref, idx, value in pending:
            ref[idx] = value
        pending.clear()

    def conv_silu(gi, r0, col0):
        cols = slice(col0, col0 + HEAD_DIM)
        rp = pl.multiple_of(jnp.maximum(r0 - DN_HALO, 0), DN_HALO)
        cur = a_ref[0, pl.ds(r0, DN_ROWS), cols].astype(F32)
        prev = a_ref[0, pl.ds(rp, DN_HALO), cols].astype(F32)
        prev = jnp.where(gi > 0, prev, 0.0)
        xw = jnp.concatenate([prev, cur], axis=0)
        conv = None
        for j in range(CONV_WIDTH):
            lo = DN_HALO - (CONV_WIDTH - 1) + j
            term = cw_ref[j:j + 1, cols] * xw[lo:lo + DN_ROWS, :]
            conv = term if conv is None else conv + term
        return _silu(conv)

    def elementwise(gi, handoff):
        r0 = pl.multiple_of(gi * DN_ROWS, DN_ROWS)
        def act_of(h):
            tiles = []
            for base in (0, DN_WIDTH, 2 * DN_WIDTH):
                col0 = base + h * HEAD_DIM
                if col0 < CONV_EARLY_COLS:
                    tiles.append(a_ref[0, pl.ds(r0, DN_ROWS), col0:col0 + HEAD_DIM].astype(F32))
                else:
                    tiles.append(conv_silu(gi, r0, col0))
            return tiles
        bg = bg_ref[0, pl.ds(r0, DN_ROWS), :]
        gc_all = sum(_mxu(cum_lower, piece) for piece in _split3(bg))
        gct_all = sum(_mxu(piece, cum_upper) for piece in _split3(bgt_ref[:, pl.ds(r0, DN_ROWS)]))
        gl_all = jnp.concatenate([jnp.broadcast_to(gc_all[(c + 1) * CHUNK - 1:(c + 1) * CHUNK, :], (CHUNK, LANES))
                                  for c in range(DN_GROUP)], axis=0)
        eg_all = jnp.exp(gl_all)
        for c in range(DN_GROUP):
            defer(eg_ref, (gi * DN_GROUP + c,), eg_all[c * CHUNK:c * CHUNK + 8, :])
        gc_m = jnp.where(g_lanes, gc_all, 0.0)
        gl_m = jnp.where(g_lanes, gl_all, 0.0)
        beta_rep = _mxu(bg.astype(BF16), sel_beta)
        egc_rep = _mxu(jnp.exp(gc_m).astype(BF16), sel_g)
        ekd_rep = _mxu(jnp.exp(gl_m - gc_m).astype(BF16), sel_g)

        qn, knb, kb, rhs = [], [], [], []
        for h in heads:
            qh, kh, vh = act_of(h)
            qn_h = qh * (lax.rsqrt(_mxu((qh * qh).astype(BF16), ones_b) + EPS) * (HEAD_DIM ** -0.5))
            kn_h = kh * lax.rsqrt(_mxu((kh * kh).astype(BF16), ones_b) + EPS)
            kb_h = kn_h * beta_rep[:, hsl(h)]
            defer(kd_ref, (h, pl.ds(r0, DN_ROWS), slice(None)), (kn_h * ekd_rep[:, hsl(h)]).astype(BF16))
            qg_h = (qn_h * egc_rep[:, hsl(h)]).astype(BF16)
            for c in range(DN_GROUP):
                w0 = pl.multiple_of(2 * (r0 + c * CHUNK), 2 * CHUNK)
                defer(wq_ref, (h, pl.ds(w0 + CHUNK, CHUNK), slice(None)), qg_h[cs(c)])
            qn.append(qn_h)
            knb.append(kn_h.astype(BF16))
            kb.append(kb_h)
            rhs.append(jnp.concatenate([vh * beta_rep[:, hsl(h)], kb_h * egc_rep[:, hsl(h)]], axis=1).astype(BF16))
            yield

        qk = [lax.dot_general(jnp.concatenate([qn[h][cs(c)], kb[h][cs(c)]], axis=0).astype(BF16),
                              knb[h][cs(c)], _NT, preferred_element_type=F32) for c, h in problems]
        decay = []
        for c, h in problems:
            diff = gc_all[cs(c), DN_HEADS + h:DN_HEADS + h + 1] - gct_all[DN_HEADS + h:DN_HEADS + h + 1, cs(c)]
            decay.append(jnp.where(causal, jnp.exp(jnp.where(causal, diff, 0.0)), 0.0))
        for i, (c, h) in enumerate(problems):
            a_intra = jnp.where(causal, qk[i][:CHUNK] * decay[i], 0.0)
            defer(ai_ref, (h, pl.ds(r0 + c * CHUNK, CHUNK), slice(None)), a_intra.astype(BF16))
        handoff["neg_lower"] = [-jnp.where(strict, qk[i][CHUNK:] * decay[i], 0.0) for i in range(len(problems))]
        handoff["rhs"] = rhs

    def solve(gi, handoff):
        r0 = pl.multiple_of(gi * DN_ROWS, DN_ROWS)
        ps = handoff["neg_lower"]
        rhs = handoff["rhs"]
        tms = [eye_f + p for p in ps]
        for _ in range(shift - 1):
            pbs = [p.astype(BF16) for p in ps]
            ps = [_mxu(pb, pb) for pb in pbs]
            tms = [tm + _mxu(tm.astype(BF16), p.astype(BF16)) for tm, p in zip(tms, ps)]
        uw = [_mxu(tms[i].astype(BF16), rhs[h][cs(c)]) for i, (c, h) in enumerate(problems)]
        for i, (c, h) in enumerate(problems):
            defer(u_ref, (h, pl.ds(r0 + c * CHUNK, CHUNK), slice(None)), uw[i][:, :HEAD_DIM])
            w0 = pl.multiple_of(2 * (r0 + c * CHUNK), 2 * CHUNK)
            defer(wq_ref, (h, pl.ds(w0, CHUNK), slice(None)), uw[i][:, HEAD_DIM:].astype(BF16))

    def recur(c, state):
        r0 = pl.multiple_of(c * CHUNK, CHUNK)
        rows = pl.ds(r0, CHUNK)
        wrows = pl.ds(pl.multiple_of(2 * r0, 2 * CHUNK), 2 * CHUNK)
        eg = eg_ref[c]
        ws = [_mxu(wq_ref[h, wrows, :], state[h].astype(BF16)) for h in heads]
        vn = [(u_ref[h, rows, :] - ws[h][:CHUNK]).astype(BF16) for h in heads]
        new_state = [state[h] * eg[0:1, DN_HEADS + h:DN_HEADS + h + 1]
                     + lax.dot_general(kd_ref[h, rows, :], vn[h], _TN, preferred_element_type=F32) for h in heads]
        o = [ws[h][CHUNK:] + _mxu(ai_ref[h, rows, :], vn[h]) for h in heads]
        ms = [_mxu((x * x).astype(BF16), ones_b) * (1.0 / HEAD_DIM) for x in o]
        for h in heads:
            on = o[h] * lax.rsqrt(ms[h] + EPS) * nw_ref[...]
            z = a_ref[0, rows, 3 * DN_WIDTH + h * HEAD_DIM:3 * DN_WIDTH + (h + 1) * HEAD_DIM].astype(F32)
            defer(o_ref, (0, rows, hsl(h)), (on * _silu(z)).astype(o_ref.dtype))
        return new_state

    def pipeline_step(g_prep, g_recur, state):
        handoff = {}
        elem = elementwise(g_prep, handoff) if g_prep is not None else iter(())
        for c in range(DN_GROUP):
            if g_recur is not None:
                state = recur(g_recur * DN_GROUP + c, state)
            next(elem, None)
        for _ in elem:
            pass
        if g_prep is not None:
            solve(g_prep, handoff)
        flush()
        return state

    zero = jnp.zeros((HEAD_DIM, HEAD_DIM), F32)
    state = pipeline_step(0, None, [zero] * DN_HEADS)
    state = list(lax.fori_loop(
        1, n_groups, lambda gi, st: tuple(pipeline_step(gi, gi - 1, list(st))), tuple(state)))
    pipeline_step(None, n_groups - 1, state)


def _deltanet(a3, bg3, bgt, conv_w2, dn_norm_row):
    b, t, wa = a3.shape
    nt = bgt.shape[0]
    return pl.pallas_call(
        _deltanet_kernel,
        grid=(b,),
        in_specs=[pl.BlockSpec((1, t, wa), lambda i: (i, 0, 0)),
                  pl.BlockSpec((1, t, LANES), lambda i: (i, 0, 0)),
                  pl.BlockSpec((nt, t), lambda i: (0, i)),
                  pl.BlockSpec((CONV_WIDTH, 3 * DN_WIDTH), lambda i: (0, 0)),
                  pl.BlockSpec((1, HEAD_DIM), lambda i: (0, 0))],
        out_specs=pl.BlockSpec((1, t, DN_WIDTH), lambda i: (i, 0, 0)),
        out_shape=jax.ShapeDtypeStruct((b, t, DN_WIDTH), BF16),
        scratch_shapes=[pltpu.VMEM((DN_HEADS, t, HEAD_DIM), F32),
                        pltpu.VMEM((DN_HEADS, 2 * t, HEAD_DIM), BF16),
                        pltpu.VMEM((DN_HEADS, t, HEAD_DIM), BF16),
                        pltpu.VMEM((DN_HEADS, t, CHUNK), BF16),
                        pltpu.VMEM((t // CHUNK, 8, LANES), F32)],
        compiler_params=pltpu.CompilerParams(dimension_semantics=("parallel",),
                                             vmem_limit_bytes=DN_VMEM_LIMIT_BYTES),
        name="deltanet",
    )(a3, bg3, bgt, conv_w2, dn_norm_row)


SWA_PREP_ROWS = 256
SWA_UNITS = 4


def _swa_kernel(s_ref, plain_ref, swapped_ref, o_ref, lse_ref, qs_ref, ks_ref, bias_ref, *dilated_scratch, dilation):
    t_len = s_ref.shape[1]
    d = dilation
    seq = t_len // d
    nblk = seq // SWA_BLOCK
    heads = range(SWA_HEADS)
    hsl = lambda h: slice(h * HEAD_DIM, (h + 1) * HEAD_DIM)
    if d > 1:
        (ostage_ref,) = dilated_scratch
    mean_b = jnp.full((HEAD_DIM, HEAD_DIM), 1.0 / HEAD_DIM, BF16)
    src_lane = lax.broadcasted_iota(jnp.int32, (HEAD_DIM, HEAD_DIM), 0)
    dst_lane = lax.broadcasted_iota(jnp.int32, (HEAD_DIM, HEAD_DIM), 1)
    swap_b = (((dst_lane < ROPE_HALF) & (src_lane == dst_lane + ROPE_HALF))
              | ((dst_lane >= ROPE_HALF) & (dst_lane < ROPE_DIM) & (src_lane == dst_lane - ROPE_HALF))
              ).astype(BF16)

    def prep(n, carry):
        r0 = pl.multiple_of(n * SWA_PREP_ROWS, SWA_PREP_ROWS)
        rows = pl.ds(r0, SWA_PREP_ROWS)
        for which, dst in ((0, qs_ref), (1, ks_ref)):
            plain = plain_ref[which, rows, :]
            swapped_tab = swapped_ref[which, rows, :]
            xb = [s_ref[0, rows, which * SWA_WIDTH + h * HEAD_DIM:which * SWA_WIDTH + (h + 1) * HEAD_DIM]
                  for h in heads]
            ms = [_mxu(x * x, mean_b) for x in xb]
            sw = [_mxu(x, swap_b) for x in xb]
            ys = [(x.astype(F32) * plain + s * swapped_tab) * lax.rsqrt(m + EPS) for x, s, m in zip(xb, sw, ms)]
            for h in heads:
                dst[rows, hsl(h)] = ys[h].astype(BF16)
        return carry

    lax.fori_loop(0, t_len // SWA_PREP_ROWS, prep, 0)

    qi = lax.broadcasted_iota(jnp.int32, (SWA_BLOCK, 2 * SWA_BLOCK), 0)
    kj = lax.broadcasted_iota(jnp.int32, (SWA_BLOCK, 2 * SWA_BLOCK), 1)
    bias_ref[0] = jnp.where(kj <= qi, 0.0, NEG_INF).astype(F32)
    bias_ref[1] = jnp.where((kj >= qi) & (kj <= qi + SWA_BLOCK), 0.0, NEG_INF).astype(F32)
    lane = lax.broadcasted_iota(jnp.int32, (SWA_BLOCK, LANES), 1)

    def attend(units):
        jobs = [(u, h) for u in range(len(units)) for h in heads]
        qrows = [pl.ds(q0, SWA_BLOCK) for q0, _, _, _, _ in units]
        krows = [pl.ds(k0, width) for _, k0, width, _, _ in units]
        s = [lax.dot_general(qs_ref[qrows[u], hsl(h)], ks_ref[krows[u], hsl(h)], _NT,
                             preferred_element_type=F32) + units[u][3] for u, h in jobs]
        m = [jnp.max(x, axis=-1, keepdims=True) for x in s]
        p = [jnp.exp(x - mx) for x, mx in zip(s, m)]
        l = [jnp.sum(x, axis=-1, keepdims=True) for x in p]
        v = [s_ref[0, krows[u], 2 * SWA_WIDTH + h * HEAD_DIM:2 * SWA_WIDTH + (h + 1) * HEAD_DIM] for u, h in jobs]
        o =[_mxu(x.astype(BF16), vh) * (1.0 / lx) for x, vh, lx in zip(p, v, l)]
        for u in range(len(units)):
            lse_tile = jnp.zeros((SWA_BLOCK, LANES), F32)
            for h in heads:
                j = u * SWA_HEADS + h
                lse_tile = jnp.where(lane == h, m[j] + jnp.log(l[j]), lse_tile)
            t0 = units[u][4]
            if d == 1:
                for h in heads:
                    o_ref[0, qrows[u], hsl(h)] = o[u * SWA_HEADS + h].astype(o_ref.dtype)
                lse_ref[0, qrows[u], :] = lse_tile
            else:
                trows = pl.ds(t0, SWA_BLOCK, stride=d)
                for h in heads:
                    ostage_ref[h, trows, :] = o[u * SWA_HEADS + h]
                lse_ref[0, trows, :] = lse_tile

    if nblk == 1:
        causal_bias = bias_ref[0][:, :SWA_BLOCK]

        def body(i, carry):
            units = []
            for u in range(SWA_UNITS):
                r = i * SWA_UNITS + u
                q0 = pl.multiple_of(r * seq, SWA_BLOCK)
                units.append((q0, q0, SWA_BLOCK, causal_bias, r))
            attend(units)
            return carry

        lax.fori_loop(0, d // SWA_UNITS, body, 0)
    else:
        pairs = nblk // SWA_UNITS

        def body(i, carry):
            r = i // pairs
            pp = i % pairs
            units = []
            for u in range(SWA_UNITS):
                n = pp * SWA_UNITS + u
                q0 = pl.multiple_of(r * seq + n * SWA_BLOCK, SWA_BLOCK)
                if u == 0:
                    first = jnp.asarray(pp == 0).astype(jnp.int32)
                    k0 = pl.multiple_of(q0 - (1 - first) * SWA_BLOCK, SWA_BLOCK)
                    bias = bias_ref[1 - first]
                else:
                    k0 = pl.multiple_of(q0 - SWA_BLOCK, SWA_BLOCK)
                    bias = bias_ref[1]
                units.append((q0, k0, 2 * SWA_BLOCK, bias, r + d * n * SWA_BLOCK))
            attend(units)
            return carry

        lax.fori_loop(0, d * pairs, body, 0)

    if d > 1:
        def emit(n, carry):
            rows = pl.ds(pl.multiple_of(n * SWA_PREP_ROWS, SWA_PREP_ROWS), SWA_PREP_ROWS)
            for h in heads:
                o_ref[0, rows, hsl(h)] = ostage_ref[h, rows, :].astype(o_ref.dtype)
            return carry
        lax.fori_loop(0, t_len // SWA_PREP_ROWS, emit, 0)


def _swa_group(s3, plain, swapped, group, dilation):
    b, t, ws = s3.shape
    assert (t // dilation) % SWA_BLOCK == 0 and t % SWA_PREP_ROWS == 0
    nblk = t // dilation // SWA_BLOCK
    assert (dilation % SWA_UNITS == 0) if nblk == 1 else (nblk % SWA_UNITS == 0)
    scratch = [pltpu.VMEM((t, SWA_WIDTH), BF16), pltpu.VMEM((t, SWA_WIDTH), BF16),
               pltpu.VMEM((2, SWA_BLOCK, 2 * SWA_BLOCK), F32)]
    if dilation > 1:
        scratch += [pltpu.VMEM((SWA_HEADS, t, LANES), F32)]
    tab_spec = pl.BlockSpec((None, 2, t, LANES), lambda i: (group, 0, 0, 0))
    return pl.pallas_call(
        functools.partial(_swa_kernel, dilation=dilation),
        grid=(b,),
        in_specs=[pl.BlockSpec((1, t, ws), lambda i: (i, 0, 0)), tab_spec, tab_spec],
        out_specs=[pl.BlockSpec((1, t, SWA_WIDTH), lambda i: (i, 0, 0)),
                   pl.BlockSpec((1, t, LANES), lambda i: (i, 0, 0))],
        out_shape=[jax.ShapeDtypeStruct((b, t, SWA_WIDTH), BF16),
                   jax.ShapeDtypeStruct((b, t, LANES), F32)],
        scratch_shapes=scratch,
        compiler_params=_params(1),
        name=f"swa_d{dilation}",
    )(s3, plain, swapped)


MERGE_ROWS = 256


def _merge_out_kernel(x_ref, h_ref, odn_ref, o0_ref, o1_ref, o2_ref, l0_ref, l1_ref, l2_ref,
                      wz_ref, wgdn_ref, wgswa_ref, wdn_ref, wswa_ref, wout_ref, out_ref, wzg_scr):
    _cast_weights_once([wz_ref, wgdn_ref, wgswa_ref], wzg_scr)
    zg_all = lax.dot_general(h_ref[...], wzg_scr[...], _NT, preferred_element_type=F32)
    outs = []
    for j in range(ROW_TILE // MERGE_ROWS):
        rows = slice(j * MERGE_ROWS, (j + 1) * MERGE_ROWS)
        zg = zg_all[rows, :]
        l0 = l0_ref[rows, :]
        l1 = l1_ref[rows, :]
        l2 = l2_ref[rows, :]
        m = jnp.maximum(jnp.maximum(l0, l1), l2)
        e0 = jnp.exp(l0 - m)
        e1 = jnp.exp(l1 - m)
        e2 = jnp.exp(l2 - m)
        inv = 1.0 / (e0 + e1 + e2)
        a0, a1, a2 = e0 * inv, e1 * inv, e2 * inv
        parts = []
        for h in range(SWA_HEADS):
            hs = slice(h * HEAD_DIM, (h + 1) * HEAD_DIM)
            col = slice(h, h + 1)
            oh = (a0[:, col] * o0_ref[rows, hs].astype(F32) + a1[:, col] * o1_ref[rows, hs].astype(F32)
                  + a2[:, col] * o2_ref[rows, hs].astype(F32))
            parts.append((oh * _silu(zg[:, hs])).astype(BF16))
        o_swa = jnp.concatenate(parts, axis=1)
        y_swa = _mxu(o_swa, wswa_ref[...])
        y_dn = _mxu(odn_ref[rows, :], wdn_ref[...])
        g_dn = zg[:, SWA_WIDTH:SWA_WIDTH + D_MODEL]
        g_swa = zg[:, SWA_WIDTH + D_MODEL:SWA_WIDTH + 2 * D_MODEL]
        merged = _sigmoid(g_dn) * y_dn + _sigmoid(g_swa) * y_swa
        outs.append((rows, x_ref[rows, :] + _mxu(merged.astype(BF16), wout_ref[...])))
    for rows, value in outs:
        out_ref[rows, :] = value


def _merge_out(x2, h, odn, o_list, lse_list, wt, zg_row_blocks, w_dn, w_swa, w_out):
    n = x2.shape[0]
    row = lambda w: pl.BlockSpec((ROW_TILE, w), lambda i: (i, 0))
    full = lambda a: pl.BlockSpec(a.shape, lambda i: (0, 0), pipeline_mode=pl.Buffered(1))
    nzg = sum(rows for _, rows in zg_row_blocks)
    return pl.pallas_call(
        _merge_out_kernel,
        grid=(n // ROW_TILE,),
        in_specs=[row(D_MODEL), row(D_MODEL), row(DN_WIDTH), row(SWA_WIDTH), row(SWA_WIDTH), row(SWA_WIDTH),
                  row(LANES), row(LANES), row(LANES)]
                 + _weight_specs(zg_row_blocks, D_MODEL)
                 + [full(w_dn), full(w_swa), full(w_out)],
        out_specs=row(D_MODEL),
        out_shape=jax.ShapeDtypeStruct((n, D_MODEL), F32),
        scratch_shapes=[pltpu.VMEM((nzg, D_MODEL), BF16)],
        compiler_params=_params_sequential(),
        name="merge_out",
    )(x2, h, odn, *o_list, *lse_list, wt, wt, wt, w_dn, w_swa, w_out)


def _rope_tables(t_len, q_norm_w, k_norm_w):
    j = jnp.arange(t_len, dtype=jnp.int32)
    pos = jnp.stack([((j % (t_len // d)) * d + j // (t_len // d)).astype(F32) for _, d in SWA_GROUPS])
    inv_freq = ROPE_THETA ** (-jnp.arange(0, ROPE_DIM, 2, dtype=F32) / ROPE_DIM)
    ang = pos[:, :, None] * inv_freq[None, None, :]
    cos, sin = jnp.cos(ang), jnp.sin(ang)
    tail = (N_GROUPS, t_len, HEAD_DIM - ROPE_DIM)
    cos_t = jnp.concatenate([cos, cos, jnp.ones(tail, F32)], axis=-1)
    sin_t = jnp.concatenate([-sin, sin, jnp.zeros(tail, F32)], axis=-1)
    w = jnp.stack([q_norm_w.astype(F32) * (HEAD_DIM ** -0.5), k_norm_w.astype(F32)], axis=1)
    w_swapped = jnp.concatenate([w[..., ROPE_HALF:ROPE_DIM], w[..., :ROPE_HALF], w[..., ROPE_DIM:]], axis=-1)
    return w[:, :, None, :] * cos_t[:, None], w_swapped[:, :, None, :] * sin_t[:, None]


def kernel(x, norm_w, w_in, conv_w, dn_a_log, dn_dt_bias, dn_norm_w, q_norm_w, k_norm_w,
           w_branch_dn, w_branch_swa, w_out):
    b, t, d = x.shape
    n = b * t
    layer = 0
    wt = jnp.swapaxes(w_in[layer], 0, 1)
    c_z = 4 * DN_WIDTH
    c_q = c_z + 2 * DN_HEADS
    c_k = c_q + N_GROUPS * SWA_WIDTH
    c_v = c_k + N_GROUPS * SWA_WIDTH
    c_sz = c_v + N_GROUPS * SWA_WIDTH
    c_g = c_sz + SWA_WIDTH
    grp_blocks = lambda g: [(c0 + g * SWA_WIDTH, SWA_WIDTH) for c0 in (c_q, c_k, c_v)]
    zg_blocks = [(c_sz, SWA_WIDTH), (c_g, D_MODEL), (c_g + D_MODEL, D_MODEL)]
    pad_heads = lambda v: jnp.pad(v.astype(F32), (DN_HEADS, LANES - 2 * DN_HEADS))[None, :]
    alog_row = pad_heads(dn_a_log[layer])
    dt_row = pad_heads(dn_dt_bias[layer])

    x2 = x.reshape(n, d)
    h, a, bg, bgt = _norm_proj(x2, norm_w[layer][None, :], wt, c_z, conv_w[layer][:, 0, :], alog_row, dt_row, t)

    o_dn = _deltanet(a.reshape(b, t, 4 * DN_WIDTH), bg.reshape(b, t, LANES), bgt,
                     conv_w[layer][:, 0, :], dn_norm_w[layer][None, :])

    plain, swapped = _rope_tables(t, q_norm_w[layer], k_norm_w[layer])
    o_list, lse_list = [], []
    for g, (window, dilation) in enumerate(SWA_GROUPS):
        assert window // dilation == SWA_BLOCK
        s_g = _matmul(h, wt, grp_blocks(g), f"proj_swa{g}", dilation, t).reshape(b, t, 3 * SWA_WIDTH)
        o_g, lse_g = _swa_group(s_g, plain, swapped, g, dilation)
        o_list.append(o_g.reshape(n, SWA_WIDTH))
        lse_list.append(lse_g.reshape(n, LANES))

    out = _merge_out(x2, h, o_dn.reshape(n, DN_WIDTH), o_list, lse_list, wt, zg_blocks,
                     w_branch_dn[layer].astype(BF16), w_branch_swa[layer].astype(BF16),
                     w_out[layer].astype(BF16))
    return out.reshape(b, t, d)
```

```python
import functools
import math

import jax
import jax.numpy as jnp
import numpy as np
from jax import lax
from jax.experimental import pallas as pl
from jax.experimental.pallas import tpu as pltpu

D_MODEL = 1024
HEAD_DIM = 128
DN_HEADS = 4
DN_WIDTH = DN_HEADS * HEAD_DIM
CONV_WIDTH = 4
CHUNK = 64
SWA_GROUPS = ((128, 1), (512, 4), (2048, 16))
N_GROUPS = 3
SWA_HEADS = 4
SWA_WIDTH = SWA_HEADS * HEAD_DIM
SWA_BLOCK = 128
ROPE_DIM = HEAD_DIM // 4
ROPE_HALF = ROPE_DIM // 2
ROPE_THETA = 500000.0
EPS = 1e-6
NEG_INF = -1e30

LANES = 128
VMEM_LIMIT_BYTES = 48 * 1024 * 1024
DN_VMEM_LIMIT_BYTES = 56 * 1024 * 1024
ROW_TILE = 512
PROJ_ROW_TILE = 1024
BG_ROWS = 16

F32 = jnp.float32
BF16 = jnp.bfloat16
_NT = (((1,), (1,)), ((), ()))
_TN = (((0,), (0,)), ((), ()))


def _sigmoid(x):
    return 0.5 * jnp.tanh(0.5 * x) + 0.5


def _silu(x):
    return x * _sigmoid(x)


def _softplus(x):
    return jnp.maximum(x, 0.0) + jnp.log(1.0 + jnp.exp(-jnp.abs(x)))


def _params(n_axes):
    return pltpu.CompilerParams(dimension_semantics=("parallel",) * n_axes,
                                vmem_limit_bytes=VMEM_LIMIT_BYTES)


def _params_sequential():
    return pltpu.CompilerParams(dimension_semantics=("arbitrary",), vmem_limit_bytes=VMEM_LIMIT_BYTES)


def _mxu(a, b):
    return jnp.dot(a, b, preferred_element_type=F32)


def _split3(x):
    hi = x.astype(BF16)
    r1 = x - hi.astype(F32)
    mid = r1.astype(BF16)
    lo = (r1 - mid.astype(F32)).astype(BF16)
    return hi, mid, lo


def _beta_and_log_decay(ab, a_log, dt_bias, head_index):
    beta = _sigmoid(ab)
    g = -jnp.exp(a_log) * _softplus(ab + dt_bias)
    return jnp.where(head_index < DN_HEADS, beta, g)


def _weight_specs(row_blocks, k):
    return [pl.BlockSpec((pl.Element(rows), pl.Element(k)), lambda i, start=start: (start, 0),
                         pipeline_mode=pl.Buffered(1))
            for start, rows in row_blocks]


def _cast_weights_once(w_refs, w_scr):
    @pl.when(pl.program_id(0) == 0)
    def _():
        blocks = [w_ref[...] for w_ref in w_refs]
        pad = w_scr.shape[0] - sum(blk.shape[0] for blk in blocks)
        if pad:
            blocks.append(jnp.zeros((pad, w_scr.shape[1]), F32))
        w_scr[...] = (jnp.concatenate(blocks, axis=0) if len(blocks) > 1 else blocks[0]).astype(BF16)


CONV_TILE = 256
CONV_HALO = 8
CONV_ROWS = 64
CONV_EARLY_COLS = 2 * DN_WIDTH


def _norm_proj_kernel(x_ref, nw_ref, wa_ref, wab_ref, cw_ref, alog_ref, dt_ref, alogt_ref, dtt_ref,
                      h_ref, a_ref, bg_ref, bgt_ref, wa_scr, wab_scr, tail_ref, res_a, res_b, *, tiles_per_seq):
    res_scr = (res_a, res_b)
    _cast_weights_once([wa_ref], wa_scr)
    _cast_weights_once([wab_ref], wab_scr)
    x = x_ref[...]
    h = (x * lax.rsqrt(jnp.mean(x * x, axis=-1, keepdims=True) + EPS)) * nw_ref[...]
    hb = h.astype(BF16)
    h_ref[...] = hb
    n_conv = CONV_EARLY_COLS
    seq_start = pl.program_id(0) % tiles_per_seq == 0
    tail = jnp.where(seq_start, 0.0, tail_ref[...])
    stores = []
    tiles = list(range(0, n_conv, CONV_TILE))
    base = pl.multiple_of((pl.program_id(0) >> 20) * CONV_HALO, CONV_HALO)

    def project(k):
        c0 = tiles[k]
        buf = res_scr[k % 2]
        buf[pl.ds(base, CONV_HALO), :] = tail[:, c0:c0 + CONV_TILE]
        buf[pl.ds(base + CONV_HALO, ROW_TILE), :] = lax.dot_general(hb, wa_scr[c0:c0 + CONV_TILE, :], _NT, preferred_element_type=F32)

    if tiles:
        project(0)
    for k, c0 in enumerate(tiles):
        cols = slice(c0, c0 + CONV_TILE)
        if k + 1 < len(tiles):
            project(k + 1)
        buf = res_scr[k % 2]
        acts = []
        for rb in range(0, ROW_TILE, CONV_ROWS):
            blk = buf[pl.ds(base + rb, CONV_HALO + CONV_ROWS), :]
            conv = None
            for j in range(CONV_WIDTH):
                lo = CONV_HALO - (CONV_WIDTH - 1) + j
                term = cw_ref[j:j + 1, cols] * blk[lo:lo + CONV_ROWS, :]
                conv = term if conv is None else conv + term
            acts.append(_silu(conv).astype(BF16))
        stores.append((a_ref, (slice(None), cols), jnp.concatenate(acts, axis=0)))
        stores.append((tail_ref, (slice(None), cols), buf[pl.ds(base + ROW_TILE, CONV_HALO), :]))
    z = lax.dot_general(hb, wa_scr[n_conv:, :], _NT, preferred_element_type=F32)
    stores.append((a_ref, (slice(None), slice(n_conv, None)), z.astype(BF16)))
    for ref, idx, value in stores:
        ref[idx] = value
    ab = lax.dot_general(hb, wab_scr[...], _NT, preferred_element_type=F32)
    bg_ref[...] = _beta_and_log_decay(ab, alog_ref[...], dt_ref[...],
                                      lax.broadcasted_iota(jnp.int32, ab.shape, 1))
    abt = lax.dot_general(wab_scr[0:BG_ROWS, :], hb, _NT, preferred_element_type=F32)
    bgt_ref[...] = _beta_and_log_decay(abt, alogt_ref[...], dtt_ref[...],
                                       lax.broadcasted_iota(jnp.int32, abt.shape, 0))


def _norm_proj(x2, norm_w, wt, na, conv_w2, alog_row, dt_row, seq_len):
    n = x2.shape[0]
    assert seq_len % ROW_TILE == 0 and CONV_EARLY_COLS % CONV_TILE == 0
    row = lambda i: (i, 0)
    fixed = lambda i: (0, 0)
    pad_col = lambda v: v[0, :BG_ROWS][:, None]
    return pl.pallas_call(
        functools.partial(_norm_proj_kernel, tiles_per_seq=seq_len // ROW_TILE),
        grid=(n // ROW_TILE,),
        in_specs=[pl.BlockSpec((ROW_TILE, D_MODEL), row),
                  pl.BlockSpec((1, D_MODEL), fixed)]
                 + _weight_specs([(0, na), (na, 2 * DN_HEADS)], D_MODEL)
                 + [pl.BlockSpec((CONV_WIDTH, 3 * DN_WIDTH), fixed),
                    pl.BlockSpec((1, LANES), fixed),
                    pl.BlockSpec((1, LANES), fixed),
                    pl.BlockSpec((BG_ROWS, 1), fixed),
                    pl.BlockSpec((BG_ROWS, 1), fixed)],
        out_specs=[pl.BlockSpec((ROW_TILE, D_MODEL), row),
                   pl.BlockSpec((ROW_TILE, na), row),
                   pl.BlockSpec((ROW_TILE, LANES), row),
                   pl.BlockSpec((BG_ROWS, ROW_TILE), lambda i: (0, i))],
        out_shape=[jax.ShapeDtypeStruct((n, D_MODEL), BF16),
                   jax.ShapeDtypeStruct((n, na), BF16),
                   jax.ShapeDtypeStruct((n, LANES), F32),
                   jax.ShapeDtypeStruct((BG_ROWS, n), F32)],
        scratch_shapes=[pltpu.VMEM((na, D_MODEL), BF16), pltpu.VMEM((LANES, D_MODEL), BF16),
                        pltpu.VMEM((CONV_HALO, max(CONV_EARLY_COLS, LANES)), F32),
                        pltpu.VMEM((CONV_HALO + ROW_TILE, CONV_TILE), F32),
                        pltpu.VMEM((CONV_HALO + ROW_TILE, CONV_TILE), F32)],
        compiler_params=_params_sequential(),
        name="norm_proj",
    )(x2, norm_w, wt, wt, conv_w2, alog_row, dt_row, pad_col(alog_row), pad_col(dt_row))


REGROUP_STRIDE = 4


def _matmul_kernel(h_ref, *refs, dilation):
    n_scratch = 1 if dilation == 1 else 3
    w_refs, o_ref, w_scr = refs[:-n_scratch - 1], refs[-n_scratch - 1], refs[-n_scratch]
    _cast_weights_once(w_refs, w_scr)
    if dilation == 1:
        o_ref[...] = lax.dot_general(h_ref[...], w_scr[...], _NT, preferred_element_type=F32).astype(o_ref.dtype)
        return
    stage_ref, stage2_ref = refs[-2], refs[-1]
    s1 = min(dilation, REGROUP_STRIDE)
    s2 = dilation // s1
    rows1 = h_ref.shape[0] // s1
    per = h_ref.shape[0] // dilation
    col = 0
    for w_ref in w_refs:
        width = w_ref.shape[0]
        res = lax.dot_general(h_ref[...], w_scr[col:col + width, :], _NT, preferred_element_type=F32)
        slabs = range(col // LANES, (col + width) // LANES)
        for c in slabs:
            stage_ref[c] = res[:, (c * LANES - col):(c * LANES - col) + LANES]
        for c in slabs:
            for a in range(s1):
                first = stage_ref[c, pl.ds(a, rows1, stride=s1), :]
                if s2 == 1:
                    o_ref[0, a, :, c * LANES:(c + 1) * LANES] = first.astype(o_ref.dtype)
                else:
                    stage2_ref[c, a * rows1:(a + 1) * rows1, :] = first
        if s2 > 1:
            for c in slabs:
                for a in range(s1):
                    for b in range(s2):
                        o_ref[0, b * s1 + a, :, c * LANES:(c + 1) * LANES] = (
                            stage2_ref[c, pl.ds(a * rows1 + b, per, stride=s2), :].astype(o_ref.dtype))
        col += width


def _matmul(h, wt, row_blocks, name, dilation=1, seq_len=None):
    n, k = h.shape
    nc = sum(rows for _, rows in row_blocks)
    scratch = [pltpu.VMEM((nc, k), BF16)]
    tile = PROJ_ROW_TILE
    if dilation == 1:
        out_spec = pl.BlockSpec((tile, nc), lambda i: (i, 0))
        out_shape = jax.ShapeDtypeStruct((n, nc), BF16)
    else:
        tiles = seq_len // tile
        per = tile // dilation
        assert seq_len % tile == 0 and tile % dilation == 0 and per % 16 == 0 and nc % LANES == 0
        out_spec = pl.BlockSpec((1, dilation, per, nc), lambda i: (i // tiles, 0, i % tiles, 0))
        out_shape = jax.ShapeDtypeStruct((n // seq_len, dilation, seq_len // dilation, nc), BF16)
        assert dilation % min(dilation, REGROUP_STRIDE) == 0
        scratch += [pltpu.VMEM((nc // LANES, tile, LANES), F32)] * 2
    return pl.pallas_call(
        functools.partial(_matmul_kernel, dilation=dilation),
        grid=(n // tile,),
        in_specs=[pl.BlockSpec((tile, k), lambda i: (i, 0))] + _weight_specs(row_blocks, k),
        out_specs=out_spec,
        out_shape=out_shape,
        scratch_shapes=scratch,
        compiler_params=_params_sequential(),
        name=name,
    )(h, *([wt] * len(row_blocks)))


DN_GROUP = 8
DN_ROWS = DN_GROUP * CHUNK
DN_HALO = 16


def _deltanet_kernel(a_ref, bg_ref, bgt_ref, cw_ref, nw_ref, o_ref,
                     u_ref, wq_ref, kd_ref, ai_ref, eg_ref):
    t_len = a_ref.shape[1]
    n_groups = t_len // DN_ROWS
    ii = lax.broadcasted_iota(jnp.int32, (CHUNK, CHUNK), 0)
    jj = lax.broadcasted_iota(jnp.int32, (CHUNK, CHUNK), 1)
    causal = ii >= jj
    strict = ii > jj
    eye_f = (ii == jj).astype(F32)
    bi = lax.broadcasted_iota(jnp.int32, (DN_ROWS, DN_ROWS), 0)
    bj = lax.broadcasted_iota(jnp.int32, (DN_ROWS, DN_ROWS), 1)
    shift = int(math.log2(CHUNK))
    same_chunk = jnp.right_shift(bi, shift) == jnp.right_shift(bj, shift)
    cum_lower = (same_chunk & (bi >= bj)).astype(BF16)
    cum_upper = (same_chunk & (bi <= bj)).astype(BF16)
    problems = [(c, h) for c in range(DN_GROUP) for h in range(DN_HEADS)]
    heads = range(DN_HEADS)
    hsl = lambda h: slice(h * HEAD_DIM, (h + 1) * HEAD_DIM)
    cs = lambda c: slice(c * CHUNK, (c + 1) * CHUNK)
    ones_b = jnp.ones((HEAD_DIM, HEAD_DIM), BF16)
    sel_src = lax.broadcasted_iota(jnp.int32, (LANES, DN_WIDTH), 0)
    sel_head = jnp.right_shift(lax.broadcasted_iota(jnp.int32, (LANES, DN_WIDTH), 1), int(math.log2(HEAD_DIM)))
    sel_beta = (sel_src == sel_head).astype(BF16)
    sel_g = (sel_src == sel_head + DN_HEADS).astype(BF16)
    lane = lax.broadcasted_iota(jnp.int32, (DN_ROWS, LANES), 1)
    g_lanes = (lane >= DN_HEADS) & (lane < 2 * DN_HEADS)

    pending = []

    def defer(ref, idx, value):
        pending.append((ref, idx, value))

    def flush():
        for ref, idx, value in pending:
            ref[idx] = value
        pending.clear()

    def conv_silu(gi, r0, col0):
        cols = slice(col0, col0 + HEAD_DIM)
        rp = pl.multiple_of(jnp.maximum(r0 - DN_HALO, 0), DN_HALO)
        cur = a_ref[0, pl.ds(r0, DN_ROWS), cols].astype(F32)
        prev = a_ref[0, pl.ds(rp, DN_HALO), cols].astype(F32)
        prev = jnp.where(gi > 0, prev, 0.0)
        xw = jnp.concatenate([prev, cur], axis=0)
        conv = None
        for j in range(CONV_WIDTH):
            lo = DN_HALO - (CONV_WIDTH - 1) + j
            term = cw_ref[j:j + 1, cols] * xw[lo:lo + DN_ROWS, :]
            conv = term if conv is None else conv + term
        return _silu(conv)

    def elementwise(gi, handoff):
        r0 = pl.multiple_of(gi * DN_ROWS, DN_ROWS)
        def act_of(h):
            tiles = []
            for base in (0, DN_WIDTH, 2 * DN_WIDTH):
                col0 = base + h * HEAD_DIM
                if col0 < CONV_EARLY_COLS:
                    tiles.append(a_ref[0, pl.ds(r0, DN_ROWS), col0:col0 + HEAD_DIM].astype(F32))
                else:
                    tiles.append(conv_silu(gi, r0, col0))
            return tiles
        acts = []

        def head_norms(h):
            qh, kh, vh = act_of(h)
            qn_h = qh * (lax.rsqrt(_mxu((qh * qh).astype(BF16), ones_b) + EPS) * (HEAD_DIM ** -0.5))
            kn_h = kh * lax.rsqrt(_mxu((kh * kh).astype(BF16), ones_b) + EPS)
            acts.append((qn_h, kn_h, vh))

        head_norms(0)
        yield
        bg = bg_ref[0, pl.ds(r0, DN_ROWS), :]
        gc_all = sum(_mxu(cum_lower, piece) for piece in _split3(bg))
        gct_all = sum(_mxu(piece, cum_upper) for piece in _split3(bgt_ref[:, pl.ds(r0, DN_ROWS)]))
        gl_all = jnp.concatenate([jnp.broadcast_to(gc_all[(c + 1) * CHUNK - 1:(c + 1) * CHUNK, :], (CHUNK, LANES))
                                  for c in range(DN_GROUP)], axis=0)
        eg_all = jnp.exp(gl_all)
        for c in range(DN_GROUP):
            defer(eg_ref, (gi * DN_GROUP + c,), eg_all[c * CHUNK:c * CHUNK + 8, :])
        gc_m = jnp.where(g_lanes, gc_all, 0.0)
        gl_m = jnp.where(g_lanes, gl_all, 0.0)
        beta_rep = _mxu(bg.astype(BF16), sel_beta)
        egc_rep = _mxu(jnp.exp(gc_m).astype(BF16), sel_g)
        ekd_rep = _mxu(jnp.exp(gl_m - gc_m).astype(BF16), sel_g)

        for h in range(1, DN_HEADS):
            head_norms(h)
            yield

        qn, knb, kb, rhs = [], [], [], []
        for h in heads:
            qn_h, kn_h, vh = acts[h]
            kb_h = kn_h * beta_rep[:, hsl(h)]
            defer(kd_ref, (h, pl.ds(r0, DN_ROWS), slice(None)), (kn_h * ekd_rep[:, hsl(h)]).astype(BF16))
            qg_h = (qn_h * egc_rep[:, hsl(h)]).astype(BF16)
            for c in range(DN_GROUP):
                w0 = pl.multiple_of(2 * (r0 + c * CHUNK), 2 * CHUNK)
                defer(wq_ref, (h, pl.ds(w0 + CHUNK, CHUNK), slice(None)), qg_h[cs(c)])
            qn.append(qn_h)
            knb.append(kn_h.astype(BF16))
            kb.append(kb_h)
            rhs.append(jnp.concatenate([vh * beta_rep[:, hsl(h)], kb_h * egc_rep[:, hsl(h)]], axis=1).astype(BF16))
            yield

        qk = [lax.dot_general(jnp.concatenate([qn[h][cs(c)], kb[h][cs(c)]], axis=0).astype(BF16),
                              knb[h][cs(c)], _NT, preferred_element_type=F32) for c, h in problems]
        decay = []
        for c, h in problems:
            diff = gc_all[cs(c), DN_HEADS + h:DN_HEADS + h + 1] - gct_all[DN_HEADS + h:DN_HEADS + h + 1, cs(c)]
            decay.append(jnp.where(causal, jnp.exp(jnp.where(causal, diff, 0.0)), 0.0))
        for i, (c, h) in enumerate(problems):
            a_intra = jnp.where(causal, qk[i][:CHUNK] * decay[i], 0.0)
            defer(ai_ref, (h, pl.ds(r0 + c * CHUNK, CHUNK), slice(None)), a_intra.astype(BF16))
        handoff["neg_lower"] = [-jnp.where(strict, qk[i][CHUNK:] * decay[i], 0.0) for i in range(len(problems))]
        handoff["rhs"] = rhs

    def solve(gi, handoff):
        r0 = pl.multiple_of(gi * DN_ROWS, DN_ROWS)
        ps = handoff["neg_lower"]
        rhs = handoff["rhs"]
        tms = [eye_f + p for p in ps]
        for _ in range(shift - 1):
            pbs = [p.astype(BF16) for p in ps]
            ps = [_mxu(pb, pb) for pb in pbs]
            tms = [tm + _mxu(tm.astype(BF16), p.astype(BF16)) for tm, p in zip(tms, ps)]
            yield
        uw = [_mxu(tms[i].astype(BF16), rhs[h][cs(c)]) for i, (c, h) in enumerate(problems)]
        for i, (c, h) in enumerate(problems):
            defer(u_ref, (h, pl.ds(r0 + c * CHUNK, CHUNK), slice(None)), uw[i][:, :HEAD_DIM])
            w0 = pl.multiple_of(2 * (r0 + c * CHUNK), 2 * CHUNK)
            defer(wq_ref, (h, pl.ds(w0, CHUNK), slice(None)), uw[i][:, HEAD_DIM:].astype(BF16))

    def recur(c, state):
        r0 = pl.multiple_of(c * CHUNK, CHUNK)
        rows = pl.ds(r0, CHUNK)
        wrows = pl.ds(pl.multiple_of(2 * r0, 2 * CHUNK), 2 * CHUNK)
        eg = eg_ref[c]
        ws = [_mxu(wq_ref[h, wrows, :], state[h].astype(BF16)) for h in heads]
        vn = [(u_ref[h, rows, :] - ws[h][:CHUNK]).astype(BF16) for h in heads]
        new_state = [state[h] * eg[0:1, DN_HEADS + h:DN_HEADS + h + 1]
                     + lax.dot_general(kd_ref[h, rows, :], vn[h], _TN, preferred_element_type=F32) for h in heads]
        o = [ws[h][CHUNK:] + _mxu(ai_ref[h, rows, :], vn[h]) for h in heads]
        ms = [_mxu((x * x).astype(BF16), ones_b) * (1.0 / HEAD_DIM) for x in o]
        for h in heads:
            on = o[h] * lax.rsqrt(ms[h] + EPS) * nw_ref[...]
            z = a_ref[0, rows, 3 * DN_WIDTH + h * HEAD_DIM:3 * DN_WIDTH + (h + 1) * HEAD_DIM].astype(F32)
            defer(o_ref, (0, rows, hsl(h)), (on * _silu(z)).astype(o_ref.dtype))
        return new_state

    def pipeline_step(g_prep, g_recur, state):
        def stages():
            if g_prep is not None:
                handoff = {}
                yield from elementwise(g_prep, handoff)
                yield
                yield from solve(g_prep, handoff)

        work = stages()
        for c in range(DN_GROUP):
            if g_recur is not None:
                state = recur(g_recur * DN_GROUP + c, state)
            next(work, None)
        for _ in work:
            pass
        flush()
        return state

    zero = jnp.zeros((HEAD_DIM, HEAD_DIM), F32)
    state = pipeline_step(0, None, [zero] * DN_HEADS)
    state = list(lax.fori_loop(
        1, n_groups, lambda gi, st: tuple(pipeline_step(gi, gi - 1, list(st))), tuple(state)))
    pipeline_step(None, n_groups - 1, state)


def _deltanet(a3, bg3, bgt, conv_w2, dn_norm_row):
    b, t, wa = a3.shape
    nt = bgt.shape[0]
    return pl.pallas_call(
        _deltanet_kernel,
        grid=(b,),
        in_specs=[pl.BlockSpec((1, t, wa), lambda i: (i, 0, 0)),
                  pl.BlockSpec((1, t, LANES), lambda i: (i, 0, 0)),
                  pl.BlockSpec((nt, t), lambda i: (0, i)),
                  pl.BlockSpec((CONV_WIDTH, 3 * DN_WIDTH), lambda i: (0, 0)),
                  pl.BlockSpec((1, HEAD_DIM), lambda i: (0, 0))],
        out_specs=pl.BlockSpec((1, t, DN_WIDTH), lambda i: (i, 0, 0)),
        out_shape=jax.ShapeDtypeStruct((b, t, DN_WIDTH), BF16),
        scratch_shapes=[pltpu.VMEM((DN_HEADS, t, HEAD_DIM), F32),
                        pltpu.VMEM((DN_HEADS, 2 * t, HEAD_DIM), BF16),
                        pltpu.VMEM((DN_HEADS, t, HEAD_DIM), BF16),
                        pltpu.VMEM((DN_HEADS, t, CHUNK), BF16),
                        pltpu.VMEM((t // CHUNK, 8, LANES), F32)],
        compiler_params=pltpu.CompilerParams(dimension_semantics=("parallel",),
                                             vmem_limit_bytes=DN_VMEM_LIMIT_BYTES),
        name="deltanet",
    )(a3, bg3, bgt, conv_w2, dn_norm_row)


SWA_PREP_ROWS = 256
SWA_UNITS = 4


def _swa_kernel(s_ref, plain_ref, swapped_ref, o_ref, lse_ref, qs_ref, ks_ref, bias_ref, *dilated_scratch, dilation):
    t_len = s_ref.shape[1]
    d = dilation
    seq = t_len // d
    nblk = seq // SWA_BLOCK
    heads = range(SWA_HEADS)
    hsl = lambda h: slice(h * HEAD_DIM, (h + 1) * HEAD_DIM)
    if d > 1:
        (ostage_ref,) = dilated_scratch
    mean_b = jnp.full((HEAD_DIM, HEAD_DIM), 1.0 / HEAD_DIM, BF16)
    src_lane = lax.broadcasted_iota(jnp.int32, (HEAD_DIM, HEAD_DIM), 0)
    dst_lane = lax.broadcasted_iota(jnp.int32, (HEAD_DIM, HEAD_DIM), 1)
    swap_b = (((dst_lane < ROPE_HALF) & (src_lane == dst_lane + ROPE_HALF))
              | ((dst_lane >= ROPE_HALF) & (dst_lane < ROPE_DIM) & (src_lane == dst_lane - ROPE_HALF))
              ).astype(BF16)

    def prep(n, carry):
        r0 = pl.multiple_of(n * SWA_PREP_ROWS, SWA_PREP_ROWS)
        rows = pl.ds(r0, SWA_PREP_ROWS)
        for which, dst in ((0, qs_ref), (1, ks_ref)):
            plain = plain_ref[which, rows, :]
            swapped_tab = swapped_ref[which, rows, :]
            xb = [s_ref[0, rows, which * SWA_WIDTH + h * HEAD_DIM:which * SWA_WIDTH + (h + 1) * HEAD_DIM]
                  for h in heads]
            ms = [_mxu(x * x, mean_b) for x in xb]
            sw = [_mxu(x, swap_b) for x in xb]
            ys = [(x.astype(F32) * plain + s * swapped_tab) * lax.rsqrt(m + EPS) for x, s, m in zip(xb, sw, ms)]
            for h in heads:
                dst[rows, hsl(h)] = ys[h].astype(BF16)
        return carry

    lax.fori_loop(0, t_len // SWA_PREP_ROWS, prep, 0)

    qi = lax.broadcasted_iota(jnp.int32, (SWA_BLOCK, 2 * SWA_BLOCK), 0)
    kj = lax.broadcasted_iota(jnp.int32, (SWA_BLOCK, 2 * SWA_BLOCK), 1)
    bias_ref[0] = jnp.where(kj <= qi, 0.0, NEG_INF).astype(F32)
    bias_ref[1] = jnp.where((kj >= qi) & (kj <= qi + SWA_BLOCK), 0.0, NEG_INF).astype(F32)
    lane = lax.broadcasted_iota(jnp.int32, (SWA_BLOCK, LANES), 1)

    def attend(units):
        jobs = [(u, h) for u in range(len(units)) for h in heads]
        qrows = [pl.ds(q0, SWA_BLOCK) for q0, _, _, _, _ in units]
        krows = [pl.ds(k0, width) for _, k0, width, _, _ in units]
        s = [lax.dot_general(qs_ref[qrows[u], hsl(h)], ks_ref[krows[u], hsl(h)], _NT,
                             preferred_element_type=F32) + units[u][3] for u, h in jobs]
        m = [jnp.max(x, axis=-1, keepdims=True) for x in s]
        p = [jnp.exp(x - mx) for x, mx in zip(s, m)]
        l = [jnp.sum(x, axis=-1, keepdims=True) for x in p]
        v = [s_ref[0, krows[u], 2 * SWA_WIDTH + h * HEAD_DIM:2 * SWA_WIDTH + (h + 1) * HEAD_DIM] for u, h in jobs]
        o =[_mxu(x.astype(BF16), vh) * (1.0 / lx) for x, vh, lx in zip(p, v, l)]
        for u in range(len(units)):
            lse_tile = jnp.zeros((SWA_BLOCK, LANES), F32)
            for h in heads:
                j = u * SWA_HEADS + h
                lse_tile = jnp.where(lane == h, m[j] + jnp.log(l[j]), lse_tile)
            t0 = units[u][4]
            if d == 1:
                for h in heads:
                    o_ref[0, qrows[u], hsl(h)] = o[u * SWA_HEADS + h].astype(o_ref.dtype)
                lse_ref[0, qrows[u], :] = lse_tile
            else:
                trows = pl.ds(t0, SWA_BLOCK, stride=d)
                for h in heads:
                    ostage_ref[h, trows, :] = o[u * SWA_HEADS + h]
                lse_ref[0, trows, :] = lse_tile

    if nblk == 1:
        causal_bias = bias_ref[0][:, :SWA_BLOCK]

        def body(i, carry):
            units = []
            for u in range(SWA_UNITS):
                r = i * SWA_UNITS + u
                q0 = pl.multiple_of(r * seq, SWA_BLOCK)
                units.append((q0, q0, SWA_BLOCK, causal_bias, r))
            attend(units)
            return carry

        lax.fori_loop(0, d // SWA_UNITS, body, 0)
    else:
        pairs = nblk // SWA_UNITS

        def body(i, carry):
            r = i // pairs
            pp = i % pairs
            units = []
            for u in range(SWA_UNITS):
                n = pp * SWA_UNITS + u
                q0 = pl.multiple_of(r * seq + n * SWA_BLOCK, SWA_BLOCK)
                if u == 0:
                    first = jnp.asarray(pp == 0).astype(jnp.int32)
                    k0 = pl.multiple_of(q0 - (1 - first) * SWA_BLOCK, SWA_BLOCK)
                    bias = bias_ref[1 - first]
                else:
                    k0 = pl.multiple_of(q0 - SWA_BLOCK, SWA_BLOCK)
                    bias = bias_ref[1]
                units.append((q0, k0, 2 * SWA_BLOCK, bias, r + d * n * SWA_BLOCK))
            attend(units)
            return carry

        lax.fori_loop(0, d * pairs, body, 0)

    if d > 1:
        def emit(n, carry):
            rows = pl.ds(pl.multiple_of(n * SWA_PREP_ROWS, SWA_PREP_ROWS), SWA_PREP_ROWS)
            for h in heads:
                o_ref[0, rows, hsl(h)] = ostage_ref[h, rows, :].astype(o_ref.dtype)
            return carry
        lax.fori_loop(0, t_len // SWA_PREP_ROWS, emit, 0)


def _swa_group(s3, plain, swapped, group, dilation):
    b, t, ws = s3.shape
    assert (t // dilation) % SWA_BLOCK == 0 and t % SWA_PREP_ROWS == 0
    nblk = t // dilation // SWA_BLOCK
    assert (dilation % SWA_UNITS == 0) if nblk == 1 else (nblk % SWA_UNITS == 0)
    scratch = [pltpu.VMEM((t, SWA_WIDTH), BF16), pltpu.VMEM((t, SWA_WIDTH), BF16),
               pltpu.VMEM((2, SWA_BLOCK, 2 * SWA_BLOCK), F32)]
    if dilation > 1:
        scratch += [pltpu.VMEM((SWA_HEADS, t, LANES), F32)]
    tab_spec = pl.BlockSpec((None, 2, t, LANES), lambda i: (group, 0, 0, 0))
    return pl.pallas_call(
        functools.partial(_swa_kernel, dilation=dilation),
        grid=(b,),
        in_specs=[pl.BlockSpec((1, t, ws), lambda i: (i, 0, 0)), tab_spec, tab_spec],
        out_specs=[pl.BlockSpec((1, t, SWA_WIDTH), lambda i: (i, 0, 0)),
                   pl.BlockSpec((1, t, LANES), lambda i: (i, 0, 0))],
        out_shape=[jax.ShapeDtypeStruct((b, t, SWA_WIDTH), BF16),
                   jax.ShapeDtypeStruct((b, t, LANES), F32)],
        scratch_shapes=scratch,
        compiler_params=_params(1),
        name=f"swa_d{dilation}",
    )(s3, plain, swapped)


MERGE_ROWS = 256


def _merge_out_kernel(x_ref, h_ref, odn_ref, o0_ref, o1_ref, o2_ref, l0_ref, l1_ref, l2_ref,
                      wz_ref, wgdn_ref, wgswa_ref, wdn32_ref, wswa32_ref, wout32_ref, out_ref,
                      wzg_scr, wdn_ref, wswa_ref, wout_ref):
    _cast_weights_once([wz_ref, wgdn_ref, wgswa_ref], wzg_scr)
    _cast_weights_once([wdn32_ref], wdn_ref)
    _cast_weights_once([wswa32_ref], wswa_ref)
    _cast_weights_once([wout32_ref], wout_ref)
    zg_all = lax.dot_general(h_ref[...], wzg_scr[...], _NT, preferred_element_type=F32)
    outs = []
    for j in range(ROW_TILE // MERGE_ROWS):
        rows = slice(j * MERGE_ROWS, (j + 1) * MERGE_ROWS)
        zg = zg_all[rows, :]
        l0 = l0_ref[rows, :]
        l1 = l1_ref[rows, :]
        l2 = l2_ref[rows, :]
        m = jnp.maximum(jnp.maximum(l0, l1), l2)
        e0 = jnp.exp(l0 - m)
        e1 = jnp.exp(l1 - m)
        e2 = jnp.exp(l2 - m)
        inv = 1.0 / (e0 + e1 + e2)
        a0, a1, a2 = e0 * inv, e1 * inv, e2 * inv
        parts = []
        for h in range(SWA_HEADS):
            hs = slice(h * HEAD_DIM, (h + 1) * HEAD_DIM)
            col = slice(h, h + 1)
            oh = (a0[:, col] * o0_ref[rows, hs].astype(F32) + a1[:, col] * o1_ref[rows, hs].astype(F32)
                  + a2[:, col] * o2_ref[rows, hs].astype(F32))
            parts.append((oh * _silu(zg[:, hs])).astype(BF16))
        o_swa = jnp.concatenate(parts, axis=1)
        y_swa = _mxu(o_swa, wswa_ref[...])
        y_dn = _mxu(odn_ref[rows, :], wdn_ref[...])
        g_dn = zg[:, SWA_WIDTH:SWA_WIDTH + D_MODEL]
        g_swa = zg[:, SWA_WIDTH + D_MODEL:SWA_WIDTH + 2 * D_MODEL]
        merged = _sigmoid(g_dn) * y_dn + _sigmoid(g_swa) * y_swa
        outs.append((rows, x_ref[rows, :] + _mxu(merged.astype(BF16), wout_ref[...])))
    for rows, value in outs:
        out_ref[rows, :] = value


def _merge_out(x2, h, odn, o_list, lse_list, wt, zg_row_blocks, w_dn, w_swa, w_out):
    n = x2.shape[0]
    row = lambda w: pl.BlockSpec((ROW_TILE, w), lambda i: (i, 0))
    full = lambda a: pl.BlockSpec(a.shape, lambda i: (0, 0), pipeline_mode=pl.Buffered(1))
    nzg = sum(rows for _, rows in zg_row_blocks)
    return pl.pallas_call(
        _merge_out_kernel,
        grid=(n // ROW_TILE,),
        in_specs=[row(D_MODEL), row(D_MODEL), row(DN_WIDTH), row(SWA_WIDTH), row(SWA_WIDTH), row(SWA_WIDTH),
                  row(LANES), row(LANES), row(LANES)]
                 + _weight_specs(zg_row_blocks, D_MODEL)
                 + [full(w_dn), full(w_swa), full(w_out)],
        out_specs=row(D_MODEL),
        out_shape=jax.ShapeDtypeStruct((n, D_MODEL), F32),
        scratch_shapes=[pltpu.VMEM((nzg, D_MODEL), BF16)] + [pltpu.VMEM(w.shape, BF16) for w in (w_dn, w_swa, w_out)],
        compiler_params=_params_sequential(),
        name="merge_out",
    )(x2, h, odn, *o_list, *lse_list, wt, wt, wt, w_dn, w_swa, w_out)


def _rope_tables(t_len, q_norm_w, k_norm_w):
    j = jnp.arange(t_len, dtype=jnp.int32)
    pos = jnp.stack([((j % (t_len // d)) * d + j // (t_len // d)).astype(F32) for _, d in SWA_GROUPS])
    inv_freq = ROPE_THETA ** (-jnp.arange(0, ROPE_DIM, 2, dtype=F32) / ROPE_DIM)
    ang = pos[:, :, None] * inv_freq[None, None, :]
    cos, sin = jnp.cos(ang), jnp.sin(ang)
    tail = (N_GROUPS, t_len, HEAD_DIM - ROPE_DIM)
    cos_t = jnp.concatenate([cos, cos, jnp.ones(tail, F32)], axis=-1)
    sin_t = jnp.concatenate([-sin, sin, jnp.zeros(tail, F32)], axis=-1)
    w = jnp.stack([q_norm_w.astype(F32) * (HEAD_DIM ** -0.5), k_norm_w.astype(F32)], axis=1)
    w_swapped = jnp.concatenate([w[..., ROPE_HALF:ROPE_DIM], w[..., :ROPE_HALF], w[..., ROPE_DIM:]], axis=-1)
    return w[:, :, None, :] * cos_t[:, None], w_swapped[:, :, None, :] * sin_t[:, None]


def kernel(x, norm_w, w_in, conv_w, dn_a_log, dn_dt_bias, dn_norm_w, q_norm_w, k_norm_w,
           w_branch_dn, w_branch_swa, w_out):
    b, t, d = x.shape
    n = b * t
    layer = 0
    wt = jnp.swapaxes(w_in[layer], 0, 1)
    c_z = 4 * DN_WIDTH
    c_q = c_z + 2 * DN_HEADS
    c_k = c_q + N_GROUPS * SWA_WIDTH
    c_v = c_k + N_GROUPS * SWA_WIDTH
    c_sz = c_v + N_GROUPS * SWA_WIDTH
    c_g = c_sz + SWA_WIDTH
    grp_blocks = lambda g: [(c0 + g * SWA_WIDTH, SWA_WIDTH) for c0 in (c_q, c_k, c_v)]
    zg_blocks = [(c_sz, SWA_WIDTH), (c_g, D_MODEL), (c_g + D_MODEL, D_MODEL)]
    pad_heads = lambda v: jnp.pad(v.astype(F32), (DN_HEADS, LANES - 2 * DN_HEADS))[None, :]
    alog_row = pad_heads(dn_a_log[layer])
    dt_row = pad_heads(dn_dt_bias[layer])

    x2 = x.reshape(n, d)
    h, a, bg, bgt = _norm_proj(x2, norm_w[layer][None, :], wt, c_z, conv_w[layer][:, 0, :], alog_row, dt_row, t)

    o_dn = _deltanet(a.reshape(b, t, 4 * DN_WIDTH), bg.reshape(b, t, LANES), bgt,
                     conv_w[layer][:, 0, :], dn_norm_w[layer][None, :])

    plain, swapped = _rope_tables(t, q_norm_w[layer], k_norm_w[layer])
    o_list, lse_list = [], []
    for g, (window, dilation) in enumerate(SWA_GROUPS):
        assert window // dilation == SWA_BLOCK
        s_g = _matmul(h, wt, grp_blocks(g), f"proj_swa{g}", dilation, t).reshape(b, t, 3 * SWA_WIDTH)
        o_g, lse_g = _swa_group(s_g, plain, swapped, g, dilation)
        o_list.append(o_g.reshape(n, SWA_WIDTH))
        lse_list.append(lse_g.reshape(n, LANES))

    out = _merge_out(x2, h, o_dn.reshape(n, DN_WIDTH), o_list, lse_list, wt, zg_blocks,
                     w_branch_dn[layer], w_branch_swa[layer], w_out[layer])
    return out.reshape(b, t, d)
```

```python
import functools
import math

import jax
import jax.numpy as jnp
import numpy as np
from jax import lax
from jax.experimental import pallas as pl
from jax.experimental.pallas import tpu as pltpu

D_MODEL = 1024
HEAD_DIM = 128
DN_HEADS = 4
DN_WIDTH = DN_HEADS * HEAD_DIM
CONV_WIDTH = 4
CHUNK = 64
SWA_GROUPS = ((128, 1), (512, 4), (2048, 16))
N_GROUPS = 3
SWA_HEADS = 4
SWA_WIDTH = SWA_HEADS * HEAD_DIM
SWA_BLOCK = 128
ROPE_DIM = HEAD_DIM // 4
ROPE_HALF = ROPE_DIM // 2
ROPE_THETA = 500000.0
EPS = 1e-6
NEG_INF = -1e30

LANES = 128
VMEM_LIMIT_BYTES = 48 * 1024 * 1024
DN_VMEM_LIMIT_BYTES = 56 * 1024 * 1024
ROW_TILE = 512
NORM_ROW_TILE = 512
PROJ_ROW_TILE = 1024
BG_ROWS = 16

F32 = jnp.float32
BF16 = jnp.bfloat16
_NT = (((1,), (1,)), ((), ()))
_TN = (((0,), (0,)), ((), ()))


def _sigmoid(x):
    return 0.5 * jnp.tanh(0.5 * x) + 0.5


def _silu(x):
    return x * _sigmoid(x)


def _softplus(x):
    return jnp.maximum(x, 0.0) + jnp.log(1.0 + jnp.exp(-jnp.abs(x)))


def _params(n_axes):
    return pltpu.CompilerParams(dimension_semantics=("parallel",) * n_axes,
                                vmem_limit_bytes=VMEM_LIMIT_BYTES)


def _params_sequential():
    return pltpu.CompilerParams(dimension_semantics=("arbitrary",), vmem_limit_bytes=VMEM_LIMIT_BYTES)


def _mxu(a, b):
    return jnp.dot(a, b, preferred_element_type=F32)


def _split3(x):
    hi = x.astype(BF16)
    r1 = x - hi.astype(F32)
    mid = r1.astype(BF16)
    lo = (r1 - mid.astype(F32)).astype(BF16)
    return hi, mid, lo


def _beta_and_log_decay(ab, a_log, dt_bias, head_index):
    beta = _sigmoid(ab)
    g = -jnp.exp(a_log) * _softplus(ab + dt_bias)
    return jnp.where(head_index < DN_HEADS, beta, g)


def _weight_specs(row_blocks, k):
    return [pl.BlockSpec((pl.Element(rows), pl.Element(k)), lambda i, start=start: (start, 0),
                         pipeline_mode=pl.Buffered(1))
            for start, rows in row_blocks]


def _cast_weights_once(w_refs, w_scr):
    @pl.when(pl.program_id(0) == 0)
    def _():
        blocks = [w_ref[...] for w_ref in w_refs]
        pad = w_scr.shape[0] - sum(blk.shape[0] for blk in blocks)
        if pad:
            blocks.append(jnp.zeros((pad, w_scr.shape[1]), F32))
        w_scr[...] = (jnp.concatenate(blocks, axis=0) if len(blocks) > 1 else blocks[0]).astype(BF16)


CONV_TILE = 256
CONV_HALO = 8
CONV_ROWS = 64
CONV_EARLY_COLS = 2 * DN_WIDTH


def _norm_proj_kernel(x_ref, nw_ref, wa_ref, wab_ref, cw_ref, alog_ref, dt_ref, alogt_ref, dtt_ref,
                      h_ref, a_ref, bg_ref, bgt_ref, wa_scr, wab_scr, tail_ref, res_a, res_b, *, tiles_per_seq):
    res_scr = (res_a, res_b)
    tile_rows = x_ref.shape[0]
    _cast_weights_once([wa_ref], wa_scr)
    _cast_weights_once([wab_ref], wab_scr)
    x = x_ref[...]
    h = (x * lax.rsqrt(jnp.mean(x * x, axis=-1, keepdims=True) + EPS)) * nw_ref[...]
    hb = h.astype(BF16)
    h_ref[...] = hb
    n_conv = CONV_EARLY_COLS
    seq_start = pl.program_id(0) % tiles_per_seq == 0
    tail = jnp.where(seq_start, 0.0, tail_ref[...])
    stores = []
    tiles = list(range(0, n_conv, CONV_TILE))
    base = pl.multiple_of((pl.program_id(0) >> 20) * CONV_HALO, CONV_HALO)

    def project(k):
        c0 = tiles[k]
        buf = res_scr[k % 2]
        buf[pl.ds(base, CONV_HALO), :] = tail[:, c0:c0 + CONV_TILE]
        buf[pl.ds(base + CONV_HALO, tile_rows), :] = lax.dot_general(hb, wa_scr[c0:c0 + CONV_TILE, :], _NT,
                                                                    preferred_element_type=F32)

    if tiles:
        project(0)
    for k, c0 in enumerate(tiles):
        cols = slice(c0, c0 + CONV_TILE)
        if k + 1 < len(tiles):
            project(k + 1)
        buf = res_scr[k % 2]
        acts = []
        for rb in range(0, tile_rows, CONV_ROWS):
            blk = buf[pl.ds(base + rb, CONV_HALO + CONV_ROWS), :]
            conv = None
            for j in range(CONV_WIDTH):
                lo = CONV_HALO - (CONV_WIDTH - 1) + j
                term = cw_ref[j:j + 1, cols] * blk[lo:lo + CONV_ROWS, :]
                conv = term if conv is None else conv + term
            acts.append(_silu(conv).astype(BF16))
        stores.append((a_ref, (slice(None), cols), jnp.concatenate(acts, axis=0)))
        stores.append((tail_ref, (slice(None), cols), buf[pl.ds(base + tile_rows, CONV_HALO), :]))
    z = lax.dot_general(hb, wa_scr[n_conv:, :], _NT, preferred_element_type=F32)
    stores.append((a_ref, (slice(None), slice(n_conv, None)), z.astype(BF16)))
    for ref, idx, value in stores:
        ref[idx] = value
    ab = lax.dot_general(hb, wab_scr[...], _NT, preferred_element_type=F32)
    bg_ref[...] = _beta_and_log_decay(ab, alog_ref[...], dt_ref[...],
                                      lax.broadcasted_iota(jnp.int32, ab.shape, 1))
    abt = lax.dot_general(wab_scr[0:BG_ROWS, :], hb, _NT, preferred_element_type=F32)
    bgt_ref[...] = _beta_and_log_decay(abt, alogt_ref[...], dtt_ref[...],
                                       lax.broadcasted_iota(jnp.int32, abt.shape, 0))


def _norm_proj(x2, norm_w, wt, na, conv_w2, alog_row, dt_row, seq_len):
    n = x2.shape[0]
    tile = NORM_ROW_TILE
    assert seq_len % tile == 0 and tile % CONV_ROWS == 0 and CONV_EARLY_COLS % CONV_TILE == 0
    row = lambda i: (i, 0)
    fixed = lambda i: (0, 0)
    pad_col = lambda v: v[0, :BG_ROWS][:, None]
    return pl.pallas_call(
        functools.partial(_norm_proj_kernel, tiles_per_seq=seq_len // tile),
        grid=(n // tile,),
        in_specs=[pl.BlockSpec((tile, D_MODEL), row),
                  pl.BlockSpec((1, D_MODEL), fixed)]
                 + _weight_specs([(0, na), (na, 2 * DN_HEADS)], D_MODEL)
                 + [pl.BlockSpec((CONV_WIDTH, 3 * DN_WIDTH), fixed),
                    pl.BlockSpec((1, LANES), fixed),
                    pl.BlockSpec((1, LANES), fixed),
                    pl.BlockSpec((BG_ROWS, 1), fixed),
                    pl.BlockSpec((BG_ROWS, 1), fixed)],
        out_specs=[pl.BlockSpec((tile, D_MODEL), row),
                   pl.BlockSpec((tile, na), row),
                   pl.BlockSpec((tile, LANES), row),
                   pl.BlockSpec((BG_ROWS, tile), lambda i: (0, i))],
        out_shape=[jax.ShapeDtypeStruct((n, D_MODEL), BF16),
                   jax.ShapeDtypeStruct((n, na), BF16),
                   jax.ShapeDtypeStruct((n, LANES), F32),
                   jax.ShapeDtypeStruct((BG_ROWS, n), F32)],
        scratch_shapes=[pltpu.VMEM((na, D_MODEL), BF16), pltpu.VMEM((LANES, D_MODEL), BF16),
                        pltpu.VMEM((CONV_HALO, max(CONV_EARLY_COLS, LANES)), F32),
                        pltpu.VMEM((CONV_HALO + tile, CONV_TILE), F32),
                        pltpu.VMEM((CONV_HALO + tile, CONV_TILE), F32)],
        compiler_params=_params_sequential(),
        name="norm_proj",
    )(x2, norm_w, wt, wt, conv_w2, alog_row, dt_row, pad_col(alog_row), pad_col(dt_row))


REGROUP_STRIDE = 4


def _matmul_kernel(h_ref, *refs, dilation):
    n_scratch = 1 if dilation == 1 else 3
    w_refs, o_ref, w_scr = refs[:-n_scratch - 1], refs[-n_scratch - 1], refs[-n_scratch]
    _cast_weights_once(w_refs, w_scr)
    if dilation == 1:
        o_ref[...] = lax.dot_general(h_ref[...], w_scr[...], _NT, preferred_element_type=F32).astype(o_ref.dtype)
        return
    stage_ref, stage2_ref = refs[-2], refs[-1]
    s1 = min(dilation, REGROUP_STRIDE)
    s2 = dilation // s1
    rows1 = h_ref.shape[0] // s1
    per = h_ref.shape[0] // dilation
    col = 0
    for w_ref in w_refs:
        width = w_ref.shape[0]
        res = lax.dot_general(h_ref[...], w_scr[col:col + width, :], _NT, preferred_element_type=F32)
        slabs = range(col // LANES, (col + width) // LANES)
        for c in slabs:
            stage_ref[c] = res[:, (c * LANES - col):(c * LANES - col) + LANES]
        for c in slabs:
            for a in range(s1):
                first = stage_ref[c, pl.ds(a, rows1, stride=s1), :]
                if s2 == 1:
                    o_ref[0, a, :, c * LANES:(c + 1) * LANES] = first.astype(o_ref.dtype)
                else:
                    stage2_ref[c, a * rows1:(a + 1) * rows1, :] = first
        if s2 > 1:
            for c in slabs:
                for a in range(s1):
                    for b in range(s2):
                        o_ref[0, b * s1 + a, :, c * LANES:(c + 1) * LANES] = (
                            stage2_ref[c, pl.ds(a * rows1 + b, per, stride=s2), :].astype(o_ref.dtype))
        col += width


def _matmul(h, wt, row_blocks, name, dilation=1, seq_len=None):
    n, k = h.shape
    nc = sum(rows for _, rows in row_blocks)
    scratch = [pltpu.VMEM((nc, k), BF16)]
    tile = PROJ_ROW_TILE
    if dilation == 1:
        out_spec = pl.BlockSpec((tile, nc), lambda i: (i, 0))
        out_shape = jax.ShapeDtypeStruct((n, nc), BF16)
    else:
        tiles = seq_len // tile
        per = tile // dilation
        assert seq_len % tile == 0 and tile % dilation == 0 and per % 16 == 0 and nc % LANES == 0
        out_spec = pl.BlockSpec((1, dilation, per, nc), lambda i: (i // tiles, 0, i % tiles, 0))
        out_shape = jax.ShapeDtypeStruct((n // seq_len, dilation, seq_len // dilation, nc), BF16)
        assert dilation % min(dilation, REGROUP_STRIDE) == 0
        scratch += [pltpu.VMEM((nc // LANES, tile, LANES), F32)] * 2
    return pl.pallas_call(
        functools.partial(_matmul_kernel, dilation=dilation),
        grid=(n // tile,),
        in_specs=[pl.BlockSpec((tile, k), lambda i: (i, 0))] + _weight_specs(row_blocks, k),
        out_specs=out_spec,
        out_shape=out_shape,
        scratch_shapes=scratch,
        compiler_params=_params_sequential(),
        name=name,
    )(h, *([wt] * len(row_blocks)))


DN_GROUP = 8
DN_ROWS = DN_GROUP * CHUNK
DN_HALO = 16


def _deltanet_kernel(a_ref, bg_ref, bgt_ref, cw_ref, nw_ref, o_ref,
                     u_ref, wq_ref, kd_ref, ai_ref, eg_ref):
    t_len = a_ref.shape[1]
    n_groups = t_len // DN_ROWS
    ii = lax.broadcasted_iota(jnp.int32, (CHUNK, CHUNK), 0)
    jj = lax.broadcasted_iota(jnp.int32, (CHUNK, CHUNK), 1)
    causal = ii >= jj
    strict = ii > jj
    eye_f = (ii == jj).astype(F32)
    bi = lax.broadcasted_iota(jnp.int32, (DN_ROWS, DN_ROWS), 0)
    bj = lax.broadcasted_iota(jnp.int32, (DN_ROWS, DN_ROWS), 1)
    shift = int(math.log2(CHUNK))
    same_chunk = jnp.right_shift(bi, shift) == jnp.right_shift(bj, shift)
    cum_lower = (same_chunk & (bi >= bj)).astype(BF16)
    cum_upper = (same_chunk & (bi <= bj)).astype(BF16)
    problems = [(c, h) for c in range(DN_GROUP) for h in range(DN_HEADS)]
    heads = range(DN_HEADS)
    hsl = lambda h: slice(h * HEAD_DIM, (h + 1) * HEAD_DIM)
    cs = lambda c: slice(c * CHUNK, (c + 1) * CHUNK)
    ones_b = jnp.ones((HEAD_DIM, HEAD_DIM), BF16)
    sel_src = lax.broadcasted_iota(jnp.int32, (LANES, DN_WIDTH), 0)
    sel_head = jnp.right_shift(lax.broadcasted_iota(jnp.int32, (LANES, DN_WIDTH), 1), int(math.log2(HEAD_DIM)))
    sel_beta = (sel_src == sel_head).astype(BF16)
    sel_g = (sel_src == sel_head + DN_HEADS).astype(BF16)
    lane = lax.broadcasted_iota(jnp.int32, (DN_ROWS, LANES), 1)
    g_lanes = (lane >= DN_HEADS) & (lane < 2 * DN_HEADS)

    pending = []

    def defer(ref, idx, value):
        pending.append((ref, idx, value))

    def flush():
        for ref, idx, value in pending:
            ref[idx] = value
        pending.clear()

    def conv_silu(gi, r0, col0):
        cols = slice(col0, col0 + HEAD_DIM)
        rp = pl.multiple_of(jnp.maximum(r0 - DN_HALO, 0), DN_HALO)
        cur = a_ref[0, pl.ds(r0, DN_ROWS), cols].astype(F32)
        prev = a_ref[0, pl.ds(rp, DN_HALO), cols].astype(F32)
        prev = jnp.where(gi > 0, prev, 0.0)
        xw = jnp.concatenate([prev, cur], axis=0)
        conv = None
        for j in range(CONV_WIDTH):
            lo = DN_HALO - (CONV_WIDTH - 1) + j
            term = cw_ref[j:j + 1, cols] * xw[lo:lo + DN_ROWS, :]
            conv = term if conv is None else conv + term
        return _silu(conv)

    def elementwise(gi, handoff):
        r0 = pl.multiple_of(gi * DN_ROWS, DN_ROWS)
        def act_of(h):
            tiles = []
            for base in (0, DN_WIDTH, 2 * DN_WIDTH):
                col0 = base + h * HEAD_DIM
                if col0 < CONV_EARLY_COLS:
                    tiles.append(a_ref[0, pl.ds(r0, DN_ROWS), col0:col0 + HEAD_DIM].astype(F32))
                else:
                    tiles.append(conv_silu(gi, r0, col0))
            return tiles
        acts = []

        def head_norms(h):
            qh, kh, vh = act_of(h)
            qn_h = qh * (lax.rsqrt(_mxu((qh * qh).astype(BF16), ones_b) + EPS) * (HEAD_DIM ** -0.5))
            kn_h = kh * lax.rsqrt(_mxu((kh * kh).astype(BF16), ones_b) + EPS)
            acts.append((qn_h, kn_h, vh))

        head_norms(0)
        yield
        bg = bg_ref[0, pl.ds(r0, DN_ROWS), :]
        gc_all = sum(_mxu(cum_lower, piece) for piece in _split3(bg))
        gct_all = sum(_mxu(piece, cum_upper) for piece in _split3(bgt_ref[:, pl.ds(r0, DN_ROWS)]))
        gl_all = jnp.concatenate([jnp.broadcast_to(gc_all[(c + 1) * CHUNK - 1:(c + 1) * CHUNK, :], (CHUNK, LANES))
                                  for c in range(DN_GROUP)], axis=0)
        eg_all = jnp.exp(gl_all)
        for c in range(DN_GROUP):
            defer(eg_ref, (gi * DN_GROUP + c,), eg_all[c * CHUNK:c * CHUNK + 8, :])
        gc_m = jnp.where(g_lanes, gc_all, 0.0)
        gl_m = jnp.where(g_lanes, gl_all, 0.0)
        beta_rep = _mxu(bg.astype(BF16), sel_beta)
        egc_rep = _mxu(jnp.exp(gc_m).astype(BF16), sel_g)
        ekd_rep = _mxu(jnp.exp(gl_m - gc_m).astype(BF16), sel_g)

        for h in range(1, DN_HEADS):
            head_norms(h)
            yield

        qn, knb, kb, rhs = [], [], [], []
        for h in heads:
            qn_h, kn_h, vh = acts[h]
            kb_h = kn_h * beta_rep[:, hsl(h)]
            defer(kd_ref, (h, pl.ds(r0, DN_ROWS), slice(None)), (kn_h * ekd_rep[:, hsl(h)]).astype(BF16))
            qg_h = (qn_h * egc_rep[:, hsl(h)]).astype(BF16)
            for c in range(DN_GROUP):
                w0 = pl.multiple_of(2 * (r0 + c * CHUNK), 2 * CHUNK)
                defer(wq_ref, (h, pl.ds(w0 + CHUNK, CHUNK), slice(None)), qg_h[cs(c)])
            qn.append(qn_h)
            knb.append(kn_h.astype(BF16))
            kb.append(kb_h)
            rhs.append(jnp.concatenate([vh * beta_rep[:, hsl(h)], kb_h * egc_rep[:, hsl(h)]], axis=1).astype(BF16))
            yield

        qk = [lax.dot_general(jnp.concatenate([qn[h][cs(c)], kb[h][cs(c)]], axis=0).astype(BF16),
                              knb[h][cs(c)], _NT, preferred_element_type=F32) for c, h in problems]
        decay = []
        for c, h in problems:
            diff = gc_all[cs(c), DN_HEADS + h:DN_HEADS + h + 1] - gct_all[DN_HEADS + h:DN_HEADS + h + 1, cs(c)]
            decay.append(jnp.where(causal, jnp.exp(jnp.where(causal, diff, 0.0)), 0.0))
        for i, (c, h) in enumerate(problems):
            a_intra = jnp.where(causal, qk[i][:CHUNK] * decay[i], 0.0)
            defer(ai_ref, (h, pl.ds(r0 + c * CHUNK, CHUNK), slice(None)), a_intra.astype(BF16))
        handoff["neg_lower"] = [-jnp.where(strict, qk[i][CHUNK:] * decay[i], 0.0) for i in range(len(problems))]
        handoff["rhs"] = rhs

    def solve(gi, handoff):
        r0 = pl.multiple_of(gi * DN_ROWS, DN_ROWS)
        ps = handoff["neg_lower"]
        rhs = handoff["rhs"]
        tms = [eye_f + p for p in ps]
        for _ in range(shift - 1):
            pbs = [p.astype(BF16) for p in ps]
            ps = [_mxu(pb, pb) for pb in pbs]
            tms = [tm + _mxu(tm.astype(BF16), p.astype(BF16)) for tm, p in zip(tms, ps)]
            yield
        uw = [_mxu(tms[i].astype(BF16), rhs[h][cs(c)]) for i, (c, h) in enumerate(problems)]
        for i, (c, h) in enumerate(problems):
            defer(u_ref, (h, pl.ds(r0 + c * CHUNK, CHUNK), slice(None)), uw[i][:, :HEAD_DIM])
            w0 = pl.multiple_of(2 * (r0 + c * CHUNK), 2 * CHUNK)
            defer(wq_ref, (h, pl.ds(w0, CHUNK), slice(None)), uw[i][:, HEAD_DIM:].astype(BF16))

    def recur(c, state):
        r0 = pl.multiple_of(c * CHUNK, CHUNK)
        rows = pl.ds(r0, CHUNK)
        wrows = pl.ds(pl.multiple_of(2 * r0, 2 * CHUNK), 2 * CHUNK)
        eg = eg_ref[c]
        ws = [_mxu(wq_ref[h, wrows, :], state[h].astype(BF16)) for h in heads]
        vn = [(u_ref[h, rows, :] - ws[h][:CHUNK]).astype(BF16) for h in heads]
        new_state = [state[h] * eg[0:1, DN_HEADS + h:DN_HEADS + h + 1]
                     + lax.dot_general(kd_ref[h, rows, :], vn[h], _TN, preferred_element_type=F32) for h in heads]
        o = [ws[h][CHUNK:] + _mxu(ai_ref[h, rows, :], vn[h]) for h in heads]
        ms = [_mxu((x * x).astype(BF16), ones_b) * (1.0 / HEAD_DIM) for x in o]
        for h in heads:
            on = o[h] * lax.rsqrt(ms[h] + EPS) * nw_ref[...]
            z = a_ref[0, rows, 3 * DN_WIDTH + h * HEAD_DIM:3 * DN_WIDTH + (h + 1) * HEAD_DIM].astype(F32)
            defer(o_ref, (0, rows, hsl(h)), (on * _silu(z)).astype(o_ref.dtype))
        return new_state

    def pipeline_step(g_prep, g_recur, state):
        def stages():
            if g_prep is not None:
                handoff = {}
                yield from elementwise(g_prep, handoff)
                yield
                yield from solve(g_prep, handoff)

        work = stages()
        for c in range(DN_GROUP):
            if g_recur is not None:
                state = recur(g_recur * DN_GROUP + c, state)
            next(work, None)
        for _ in work:
            pass
        flush()
        return state

    zero = jnp.zeros((HEAD_DIM, HEAD_DIM), F32)
    state = pipeline_step(0, None, [zero] * DN_HEADS)
    state = list(lax.fori_loop(
        1, n_groups, lambda gi, st: tuple(pipeline_step(gi, gi - 1, list(st))), tuple(state)))
    pipeline_step(None, n_groups - 1, state)


def _deltanet(a3, bg3, bgt, conv_w2, dn_norm_row):
    b, t, wa = a3.shape
    nt = bgt.shape[0]
    return pl.pallas_call(
        _deltanet_kernel,
        grid=(b,),
        in_specs=[pl.BlockSpec((1, t, wa), lambda i: (i, 0, 0)),
                  pl.BlockSpec((1, t, LANES), lambda i: (i, 0, 0)),
                  pl.BlockSpec((nt, t), lambda i: (0, i)),
                  pl.BlockSpec((CONV_WIDTH, 3 * DN_WIDTH), lambda i: (0, 0)),
                  pl.BlockSpec((1, HEAD_DIM), lambda i: (0, 0))],
        out_specs=pl.BlockSpec((1, t, DN_WIDTH), lambda i: (i, 0, 0)),
        out_shape=jax.ShapeDtypeStruct((b, t, DN_WIDTH), BF16),
        scratch_shapes=[pltpu.VMEM((DN_HEADS, t, HEAD_DIM), F32),
                        pltpu.VMEM((DN_HEADS, 2 * t, HEAD_DIM), BF16),
                        pltpu.VMEM((DN_HEADS, t, HEAD_DIM), BF16),
                        pltpu.VMEM((DN_HEADS, t, CHUNK), BF16),
                        pltpu.VMEM((t // CHUNK, 8, LANES), F32)],
        compiler_params=pltpu.CompilerParams(dimension_semantics=("parallel",),
                                             vmem_limit_bytes=DN_VMEM_LIMIT_BYTES),
        name="deltanet",
    )(a3, bg3, bgt, conv_w2, dn_norm_row)


SWA_PREP_ROWS = 512
SWA_UNITS = 4


def _swa_kernel(s_ref, plain_ref, swapped_ref, o_ref, lse_ref, qs_ref, ks_ref, bias_ref, *dilated_scratch, dilation):
    t_len = s_ref.shape[1]
    d = dilation
    seq = t_len // d
    nblk = seq // SWA_BLOCK
    heads = range(SWA_HEADS)
    hsl = lambda h: slice(h * HEAD_DIM, (h + 1) * HEAD_DIM)
    if d > 1:
        (ostage_ref,) = dilated_scratch
    mean_b = jnp.full((HEAD_DIM, HEAD_DIM), 1.0 / HEAD_DIM, BF16)
    src_lane = lax.broadcasted_iota(jnp.int32, (HEAD_DIM, HEAD_DIM), 0)
    dst_lane = lax.broadcasted_iota(jnp.int32, (HEAD_DIM, HEAD_DIM), 1)
    swap_b = (((dst_lane < ROPE_HALF) & (src_lane == dst_lane + ROPE_HALF))
              | ((dst_lane >= ROPE_HALF) & (dst_lane < ROPE_DIM) & (src_lane == dst_lane - ROPE_HALF))
              ).astype(BF16)

    def prep(n, carry):
        r0 = pl.multiple_of(n * SWA_PREP_ROWS, SWA_PREP_ROWS)
        rows = pl.ds(r0, SWA_PREP_ROWS)
        for which, dst in ((0, qs_ref), (1, ks_ref)):
            plain = plain_ref[which, rows, :]
            swapped_tab = swapped_ref[which, rows, :]
            xb = [s_ref[0, rows, which * SWA_WIDTH + h * HEAD_DIM:which * SWA_WIDTH + (h + 1) * HEAD_DIM]
                  for h in heads]
            ms = [_mxu(x * x, mean_b) for x in xb]
            sw = [_mxu(x, swap_b) for x in xb]
            ys = [(x.astype(F32) * plain + s * swapped_tab) * lax.rsqrt(m + EPS) for x, s, m in zip(xb, sw, ms)]
            for h in heads:
                dst[rows, hsl(h)] = ys[h].astype(BF16)
        return carry

    lax.fori_loop(0, t_len // SWA_PREP_ROWS, prep, 0)

    qi = lax.broadcasted_iota(jnp.int32, (SWA_BLOCK, 2 * SWA_BLOCK), 0)
    kj = lax.broadcasted_iota(jnp.int32, (SWA_BLOCK, 2 * SWA_BLOCK), 1)
    bias_ref[0] = jnp.where(kj <= qi, 0.0, NEG_INF).astype(F32)
    bias_ref[1] = jnp.where((kj >= qi) & (kj <= qi + SWA_BLOCK), 0.0, NEG_INF).astype(F32)
    lane = lax.broadcasted_iota(jnp.int32, (SWA_BLOCK, LANES), 1)

    def attend(units):
        jobs = [(u, h) for u in range(len(units)) for h in heads]
        qrows = [pl.ds(q0, SWA_BLOCK) for q0, _, _, _, _ in units]
        krows = [pl.ds(k0, width) for _, k0, width, _, _ in units]
        s = [lax.dot_general(qs_ref[qrows[u], hsl(h)], ks_ref[krows[u], hsl(h)], _NT,
                             preferred_element_type=F32) + units[u][3] for u, h in jobs]
        m = [jnp.max(x, axis=-1, keepdims=True) for x in s]
        p = [jnp.exp(x - mx) for x, mx in zip(s, m)]
        l = [jnp.sum(x, axis=-1, keepdims=True) for x in p]
        v = [s_ref[0, krows[u], 2 * SWA_WIDTH + h * HEAD_DIM:2 * SWA_WIDTH + (h + 1) * HEAD_DIM] for u, h in jobs]
        o =[_mxu(x.astype(BF16), vh) * (1.0 / lx) for x, vh, lx in zip(p, v, l)]
        for u in range(len(units)):
            lse_tile = jnp.zeros((SWA_BLOCK, LANES), F32)
            for h in heads:
                j = u * SWA_HEADS + h
                lse_tile = jnp.where(lane == h, m[j] + jnp.log(l[j]), lse_tile)
            t0 = units[u][4]
            if d == 1:
                for h in heads:
                    o_ref[0, qrows[u], hsl(h)] = o[u * SWA_HEADS + h].astype(o_ref.dtype)
                lse_ref[0, qrows[u], :] = lse_tile
            else:
                trows = pl.ds(t0, SWA_BLOCK, stride=d)
                for h in heads:
                    ostage_ref[h, trows, :] = o[u * SWA_HEADS + h]
                lse_ref[0, trows, :] = lse_tile

    if nblk == 1:
        causal_bias = bias_ref[0][:, :SWA_BLOCK]

        def body(i, carry):
            units = []
            for u in range(SWA_UNITS):
                r = i * SWA_UNITS + u
                q0 = pl.multiple_of(r * seq, SWA_BLOCK)
                units.append((q0, q0, SWA_BLOCK, causal_bias, r))
            attend(units)
            return carry

        lax.fori_loop(0, d // SWA_UNITS, body, 0)
    else:
        pairs = nblk // SWA_UNITS

        def body(i, carry):
            r = i // pairs
            pp = i % pairs
            units = []
            for u in range(SWA_UNITS):
                n = pp * SWA_UNITS + u
                q0 = pl.multiple_of(r * seq + n * SWA_BLOCK, SWA_BLOCK)
                if u == 0:
                    first = jnp.asarray(pp == 0).astype(jnp.int32)
                    k0 = pl.multiple_of(q0 - (1 - first) * SWA_BLOCK, SWA_BLOCK)
                    bias = bias_ref[1 - first]
                else:
                    k0 = pl.multiple_of(q0 - SWA_BLOCK, SWA_BLOCK)
                    bias = bias_ref[1]
                units.append((q0, k0, 2 * SWA_BLOCK, bias, r + d * n * SWA_BLOCK))
            attend(units)
            return carry

        lax.fori_loop(0, d * pairs, body, 0)

    if d > 1:
        def emit(n, carry):
            rows = pl.ds(pl.multiple_of(n * SWA_PREP_ROWS, SWA_PREP_ROWS), SWA_PREP_ROWS)
            for h in heads:
                o_ref[0, rows, hsl(h)] = ostage_ref[h, rows, :].astype(o_ref.dtype)
            return carry
        lax.fori_loop(0, t_len // SWA_PREP_ROWS, emit, 0)


def _swa_group(s3, plain, swapped, group, dilation):
    b, t, ws = s3.shape
    assert (t // dilation) % SWA_BLOCK == 0 and t % SWA_PREP_ROWS == 0
    nblk = t // dilation // SWA_BLOCK
    assert (dilation % SWA_UNITS == 0) if nblk == 1 else (nblk % SWA_UNITS == 0)
    scratch = [pltpu.VMEM((t, SWA_WIDTH), BF16), pltpu.VMEM((t, SWA_WIDTH), BF16),
               pltpu.VMEM((2, SWA_BLOCK, 2 * SWA_BLOCK), F32)]
    if dilation > 1:
        scratch += [pltpu.VMEM((SWA_HEADS, t, LANES), F32)]
    tab_spec = pl.BlockSpec((None, 2, t, LANES), lambda i: (group, 0, 0, 0))
    return pl.pallas_call(
        functools.partial(_swa_kernel, dilation=dilation),
        grid=(b,),
        in_specs=[pl.BlockSpec((1, t, ws), lambda i: (i, 0, 0)), tab_spec, tab_spec],
        out_specs=[pl.BlockSpec((1, t, SWA_WIDTH), lambda i: (i, 0, 0)),
                   pl.BlockSpec((1, t, LANES), lambda i: (i, 0, 0))],
        out_shape=[jax.ShapeDtypeStruct((b, t, SWA_WIDTH), BF16),
                   jax.ShapeDtypeStruct((b, t, LANES), F32)],
        scratch_shapes=scratch,
        compiler_params=_params(1),
        name=f"swa_d{dilation}",
    )(s3, plain, swapped)


MERGE_ROWS = 512


def _merge_out_kernel(x_ref, h_ref, odn_ref, o0_ref, o1_ref, o2_ref, l0_ref, l1_ref, l2_ref,
                      wz_ref, wgdn_ref, wgswa_ref, wdn32_ref, wswa32_ref, wout32_ref, out_ref,
                      wzg_scr, wdn_ref, wswa_ref, wout_ref):
    _cast_weights_once([wz_ref, wgdn_ref, wgswa_ref], wzg_scr)
    _cast_weights_once([wdn32_ref], wdn_ref)
    _cast_weights_once([wswa32_ref], wswa_ref)
    _cast_weights_once([wout32_ref], wout_ref)
    zg_all = lax.dot_general(h_ref[...], wzg_scr[...], _NT, preferred_element_type=F32)
    outs = []
    for j in range(ROW_TILE // MERGE_ROWS):
        rows = slice(j * MERGE_ROWS, (j + 1) * MERGE_ROWS)
        zg = zg_all[rows, :]
        l0 = l0_ref[rows, :]
        l1 = l1_ref[rows, :]
        l2 = l2_ref[rows, :]
        m = jnp.maximum(jnp.maximum(l0, l1), l2)
        e0 = jnp.exp(l0 - m)
        e1 = jnp.exp(l1 - m)
        e2 = jnp.exp(l2 - m)
        inv = 1.0 / (e0 + e1 + e2)
        a0, a1, a2 = e0 * inv, e1 * inv, e2 * inv
        parts = []
        for h in range(SWA_HEADS):
            hs = slice(h * HEAD_DIM, (h + 1) * HEAD_DIM)
            col = slice(h, h + 1)
            oh = (a0[:, col] * o0_ref[rows, hs].astype(F32) + a1[:, col] * o1_ref[rows, hs].astype(F32)
                  + a2[:, col] * o2_ref[rows, hs].astype(F32))
            parts.append((oh * _silu(zg[:, hs])).astype(BF16))
        o_swa = jnp.concatenate(parts, axis=1)
        y_swa = _mxu(o_swa, wswa_ref[...])
        y_dn = _mxu(odn_ref[rows, :], wdn_ref[...])
        g_dn = zg[:, SWA_WIDTH:SWA_WIDTH + D_MODEL]
        g_swa = zg[:, SWA_WIDTH + D_MODEL:SWA_WIDTH + 2 * D_MODEL]
        merged = _sigmoid(g_dn) * y_dn + _sigmoid(g_swa) * y_swa
        outs.append((rows, x_ref[rows, :] + _mxu(merged.astype(BF16), wout_ref[...])))
    for rows, value in outs:
        out_ref[rows, :] = value


def _merge_out(x2, h, odn, o_list, lse_list, wt, zg_row_blocks, w_dn, w_swa, w_out):
    n = x2.shape[0]
    row = lambda w: pl.BlockSpec((ROW_TILE, w), lambda i: (i, 0))
    full = lambda a: pl.BlockSpec(a.shape, lambda i: (0, 0), pipeline_mode=pl.Buffered(1))
    nzg = sum(rows for _, rows in zg_row_blocks)
    return pl.pallas_call(
        _merge_out_kernel,
        grid=(n // ROW_TILE,),
        in_specs=[row(D_MODEL), row(D_MODEL), row(DN_WIDTH), row(SWA_WIDTH), row(SWA_WIDTH), row(SWA_WIDTH),
                  row(LANES), row(LANES), row(LANES)]
                 + _weight_specs(zg_row_blocks, D_MODEL)
                 + [full(w_dn), full(w_swa), full(w_out)],
        out_specs=row(D_MODEL),
        out_shape=jax.ShapeDtypeStruct((n, D_MODEL), F32),
        scratch_shapes=[pltpu.VMEM((nzg, D_MODEL), BF16)] + [pltpu.VMEM(w.shape, BF16) for w in (w_dn, w_swa, w_out)],
        compiler_params=_params_sequential(),
        name="merge_out",
    )(x2, h, odn, *o_list, *lse_list, wt, wt, wt, w_dn, w_swa, w_out)


def _rope_tables(t_len, q_norm_w, k_norm_w):
    j = jnp.arange(t_len, dtype=jnp.int32)
    pos = jnp.stack([((j % (t_len // d)) * d + j // (t_len // d)).astype(F32) for _, d in SWA_GROUPS])
    inv_freq = ROPE_THETA ** (-jnp.arange(0, ROPE_DIM, 2, dtype=F32) / ROPE_DIM)
    ang = pos[:, :, None] * inv_freq[None, None, :]
    cos, sin = jnp.cos(ang), jnp.sin(ang)
    tail = (N_GROUPS, t_len, HEAD_DIM - ROPE_DIM)
    cos_t = jnp.concatenate([cos, cos, jnp.ones(tail, F32)], axis=-1)
    sin_t = jnp.concatenate([-sin, sin, jnp.zeros(tail, F32)], axis=-1)
    w = jnp.stack([q_norm_w.astype(F32) * (HEAD_DIM ** -0.5), k_norm_w.astype(F32)], axis=1)
    w_swapped = jnp.concatenate([w[..., ROPE_HALF:ROPE_DIM], w[..., :ROPE_HALF], w[..., ROPE_DIM:]], axis=-1)
    return w[:, :, None, :] * cos_t[:, None], w_swapped[:, :, None, :] * sin_t[:, None]


def kernel(x, norm_w, w_in, conv_w, dn_a_log, dn_dt_bias, dn_norm_w, q_norm_w, k_norm_w,
           w_branch_dn, w_branch_swa, w_out):
    b, t, d = x.shape
    n = b * t
    layer = 0
    wt = jnp.swapaxes(w_in[layer], 0, 1)
    c_z = 4 * DN_WIDTH
    c_q = c_z + 2 * DN_HEADS
    c_k = c_q + N_GROUPS * SWA_WIDTH
    c_v = c_k + N_GROUPS * SWA_WIDTH
    c_sz = c_v + N_GROUPS * SWA_WIDTH
    c_g = c_sz + SWA_WIDTH
    grp_blocks = lambda g: [(c0 + g * SWA_WIDTH, SWA_WIDTH) for c0 in (c_q, c_k, c_v)]
    zg_blocks = [(c_sz, SWA_WIDTH), (c_g, D_MODEL), (c_g + D_MODEL, D_MODEL)]
    pad_heads = lambda v: jnp.pad(v.astype(F32), (DN_HEADS, LANES - 2 * DN_HEADS))[None, :]
    alog_row = pad_heads(dn_a_log[layer])
    dt_row = pad_heads(dn_dt_bias[layer])

    x2 = x.reshape(n, d)
    h, a, bg, bgt = _norm_proj(x2, norm_w[layer][None, :], wt, c_z, conv_w[layer][:, 0, :], alog_row, dt_row, t)

    o_dn = _deltanet(a.reshape(b, t, 4 * DN_WIDTH), bg.reshape(b, t, LANES), bgt,
                     conv_w[layer][:, 0, :], dn_norm_w[layer][None, :])

    plain, swapped = _rope_tables(t, q_norm_w[layer], k_norm_w[layer])
    o_list, lse_list = [], []
    for g, (window, dilation) in enumerate(SWA_GROUPS):
        assert window // dilation == SWA_BLOCK
        s_g = _matmul(h, wt, grp_blocks(g), f"proj_swa{g}", dilation, t).reshape(b, t, 3 * SWA_WIDTH)
        o_g, lse_g = _swa_group(s_g, plain, swapped, g, dilation)
        o_list.append(o_g.reshape(n, SWA_WIDTH))
        lse_list.append(lse_g.reshape(n, LANES))

    out = _merge_out(x2, h, o_dn.reshape(n, DN_WIDTH), o_list, lse_list, wt, zg_blocks,
                     w_branch_dn[layer], w_branch_swa[layer], w_out[layer])
    return out.reshape(b, t, d)
```

```python
import functools
import math

import jax
import jax.numpy as jnp
import numpy as np
from jax import lax
from jax.experimental import pallas as pl
from jax.experimental.pallas import tpu as pltpu

D_MODEL = 1024
HEAD_DIM = 128
DN_HEADS = 4
DN_WIDTH = DN_HEADS * HEAD_DIM
CONV_WIDTH = 4
CHUNK = 64
SWA_GROUPS = ((128, 1), (512, 4), (2048, 16))
N_GROUPS = 3
SWA_HEADS = 4
SWA_WIDTH = SWA_HEADS * HEAD_DIM
SWA_BLOCK = 128
ROPE_DIM = HEAD_DIM // 4
ROPE_HALF = ROPE_DIM // 2
ROPE_THETA = 500000.0
EPS = 1e-6
NEG_INF = -1e30

LANES = 128
VMEM_LIMIT_BYTES = 48 * 1024 * 1024
DN_VMEM_LIMIT_BYTES = 56 * 1024 * 1024
ROW_TILE = 512
NORM_ROW_TILE = 512
PROJ_ROW_TILE = 1024
BG_ROWS = 16

F32 = jnp.float32
BF16 = jnp.bfloat16
_NT = (((1,), (1,)), ((), ()))
_TN = (((0,), (0,)), ((), ()))


def _sigmoid(x):
    return 0.5 * jnp.tanh(0.5 * x) + 0.5


def _silu(x):
    return x * _sigmoid(x)


def _softplus(x):
    return jnp.maximum(x, 0.0) + jnp.log(1.0 + jnp.exp(-jnp.abs(x)))


def _params(n_axes):
    return pltpu.CompilerParams(dimension_semantics=("parallel",) * n_axes,
                                vmem_limit_bytes=VMEM_LIMIT_BYTES)


def _params_sequential():
    return pltpu.CompilerParams(dimension_semantics=("arbitrary",), vmem_limit_bytes=VMEM_LIMIT_BYTES)


def _mxu(a, b):
    return jnp.dot(a, b, preferred_element_type=F32)


def _split3(x):
    hi = x.astype(BF16)
    r1 = x - hi.astype(F32)
    mid = r1.astype(BF16)
    lo = (r1 - mid.astype(F32)).astype(BF16)
    return hi, mid, lo


def _beta_and_log_decay(ab, a_log, dt_bias, head_index):
    beta = _sigmoid(ab)
    g = -jnp.exp(a_log) * _softplus(ab + dt_bias)
    return jnp.where(head_index < DN_HEADS, beta, g)


def _weight_specs(row_blocks, k):
    return [pl.BlockSpec((pl.Element(rows), pl.Element(k)), lambda i, start=start: (start, 0),
                         pipeline_mode=pl.Buffered(1))
            for start, rows in row_blocks]


def _cast_weights_once(w_refs, w_scr):
    @pl.when(pl.program_id(0) == 0)
    def _():
        blocks = [w_ref[...] for w_ref in w_refs]
        pad = w_scr.shape[0] - sum(blk.shape[0] for blk in blocks)
        if pad:
            blocks.append(jnp.zeros((pad, w_scr.shape[1]), F32))
        w_scr[...] = (jnp.concatenate(blocks, axis=0) if len(blocks) > 1 else blocks[0]).astype(BF16)


CONV_TILE = 256
CONV_HALO = 8
CONV_ROWS = 64
CONV_EARLY_COLS = 2 * DN_WIDTH


def _norm_proj_kernel(x_ref, nw_ref, wa_ref, wab_ref, cw_ref, alog_ref, dt_ref, alogt_ref, dtt_ref,
                      h_ref, a_ref, bg_ref, bgt_ref, wa_scr, wab_scr, tail_ref, res_a, res_b, *, tiles_per_seq):
    res_scr = (res_a, res_b)
    tile_rows = x_ref.shape[0]
    _cast_weights_once([wa_ref], wa_scr)
    _cast_weights_once([wab_ref], wab_scr)
    x = x_ref[...]
    h = (x * lax.rsqrt(jnp.mean(x * x, axis=-1, keepdims=True) + EPS)) * nw_ref[...]
    hb = h.astype(BF16)
    h_ref[...] = hb
    n_conv = CONV_EARLY_COLS
    seq_start = pl.program_id(0) % tiles_per_seq == 0
    tail = jnp.where(seq_start, 0.0, tail_ref[...])
    stores = []
    tiles = list(range(0, n_conv, CONV_TILE))
    base = pl.multiple_of((pl.program_id(0) >> 20) * CONV_HALO, CONV_HALO)

    def project(k):
        c0 = tiles[k]
        buf = res_scr[k % 2]
        buf[pl.ds(base, CONV_HALO), :] = tail[:, c0:c0 + CONV_TILE]
        buf[pl.ds(base + CONV_HALO, tile_rows), :] = lax.dot_general(hb, wa_scr[c0:c0 + CONV_TILE, :], _NT,
                                                                    preferred_element_type=F32)

    if tiles:
        project(0)
    for k, c0 in enumerate(tiles):
        cols = slice(c0, c0 + CONV_TILE)
        if k + 1 < len(tiles):
            project(k + 1)
        buf = res_scr[k % 2]
        acts = []
        for rb in range(0, tile_rows, CONV_ROWS):
            blk = buf[pl.ds(base + rb, CONV_HALO + CONV_ROWS), :]
            conv = None
            for j in range(CONV_WIDTH):
                lo = CONV_HALO - (CONV_WIDTH - 1) + j
                term = cw_ref[j:j + 1, cols] * blk[lo:lo + CONV_ROWS, :]
                conv = term if conv is None else conv + term
            acts.append(_silu(conv).astype(BF16))
        stores.append((a_ref, (slice(None), cols), jnp.concatenate(acts, axis=0)))
        stores.append((tail_ref, (slice(None), cols), buf[pl.ds(base + tile_rows, CONV_HALO), :]))
    z = lax.dot_general(hb, wa_scr[n_conv:, :], _NT, preferred_element_type=F32)
    stores.append((a_ref, (slice(None), slice(n_conv, None)), z.astype(BF16)))
    for ref, idx, value in stores:
        ref[idx] = value
    ab = lax.dot_general(hb, wab_scr[...], _NT, preferred_element_type=F32)
    bg_ref[...] = _beta_and_log_decay(ab, alog_ref[...], dt_ref[...],
                                      lax.broadcasted_iota(jnp.int32, ab.shape, 1))
    abt = lax.dot_general(wab_scr[0:BG_ROWS, :], hb, _NT, preferred_element_type=F32)
    bgt_ref[...] = _beta_and_log_decay(abt, alogt_ref[...], dtt_ref[...],
                                       lax.broadcasted_iota(jnp.int32, abt.shape, 0))


def _norm_proj(x2, norm_w, wt, na, conv_w2, alog_row, dt_row, seq_len):
    n = x2.shape[0]
    tile = NORM_ROW_TILE
    assert seq_len % tile == 0 and tile % CONV_ROWS == 0 and CONV_EARLY_COLS % CONV_TILE == 0
    row = lambda i: (i, 0)
    fixed = lambda i: (0, 0)
    pad_col = lambda v: v[0, :BG_ROWS][:, None]
    return pl.pallas_call(
        functools.partial(_norm_proj_kernel, tiles_per_seq=seq_len // tile),
        grid=(n // tile,),
        in_specs=[pl.BlockSpec((tile, D_MODEL), row),
                  pl.BlockSpec((1, D_MODEL), fixed)]
                 + _weight_specs([(0, na), (na, 2 * DN_HEADS)], D_MODEL)
                 + [pl.BlockSpec((CONV_WIDTH, 3 * DN_WIDTH), fixed),
                    pl.BlockSpec((1, LANES), fixed),
                    pl.BlockSpec((1, LANES), fixed),
                    pl.BlockSpec((BG_ROWS, 1), fixed),
                    pl.BlockSpec((BG_ROWS, 1), fixed)],
        out_specs=[pl.BlockSpec((tile, D_MODEL), row),
                   pl.BlockSpec((tile, na), row),
                   pl.BlockSpec((tile, LANES), row),
                   pl.BlockSpec((BG_ROWS, tile), lambda i: (0, i))],
        out_shape=[jax.ShapeDtypeStruct((n, D_MODEL), BF16),
                   jax.ShapeDtypeStruct((n, na), BF16),
                   jax.ShapeDtypeStruct((n, LANES), F32),
                   jax.ShapeDtypeStruct((BG_ROWS, n), F32)],
        scratch_shapes=[pltpu.VMEM((na, D_MODEL), BF16), pltpu.VMEM((LANES, D_MODEL), BF16),
                        pltpu.VMEM((CONV_HALO, max(CONV_EARLY_COLS, LANES)), F32),
                        pltpu.VMEM((CONV_HALO + tile, CONV_TILE), F32),
                        pltpu.VMEM((CONV_HALO + tile, CONV_TILE), F32)],
        compiler_params=_params_sequential(),
        name="norm_proj",
    )(x2, norm_w, wt, wt, conv_w2, alog_row, dt_row, pad_col(alog_row), pad_col(dt_row))


REGROUP_STRIDE = 4


def _matmul_kernel(h_ref, *refs, dilation):
    n_scratch = 1 if dilation == 1 else 3
    w_refs, o_ref, w_scr = refs[:-n_scratch - 1], refs[-n_scratch - 1], refs[-n_scratch]
    _cast_weights_once(w_refs, w_scr)
    if dilation == 1:
        o_ref[...] = lax.dot_general(h_ref[...], w_scr[...], _NT, preferred_element_type=F32).astype(o_ref.dtype)
        return
    stage_ref, stage2_ref = refs[-2], refs[-1]
    s1 = min(dilation, REGROUP_STRIDE)
    s2 = dilation // s1
    rows1 = h_ref.shape[0] // s1
    per = h_ref.shape[0] // dilation
    col = 0
    for w_ref in w_refs:
        width = w_ref.shape[0]
        res = lax.dot_general(h_ref[...], w_scr[col:col + width, :], _NT, preferred_element_type=F32)
        slabs = range(col // LANES, (col + width) // LANES)
        for c in slabs:
            stage_ref[c] = res[:, (c * LANES - col):(c * LANES - col) + LANES]
        for c in slabs:
            for a in range(s1):
                first = stage_ref[c, pl.ds(a, rows1, stride=s1), :]
                if s2 == 1:
                    o_ref[0, a, :, c * LANES:(c + 1) * LANES] = first.astype(o_ref.dtype)
                else:
                    stage2_ref[c, a * rows1:(a + 1) * rows1, :] = first
        if s2 > 1:
            for c in slabs:
                for a in range(s1):
                    for b in range(s2):
                        o_ref[0, b * s1 + a, :, c * LANES:(c + 1) * LANES] = (
                            stage2_ref[c, pl.ds(a * rows1 + b, per, stride=s2), :].astype(o_ref.dtype))
        col += width


def _matmul(h, wt, row_blocks, name, dilation=1, seq_len=None):
    n, k = h.shape
    nc = sum(rows for _, rows in row_blocks)
    scratch = [pltpu.VMEM((nc, k), BF16)]
    tile = PROJ_ROW_TILE
    if dilation == 1:
        out_spec = pl.BlockSpec((tile, nc), lambda i: (i, 0))
        out_shape = jax.ShapeDtypeStruct((n, nc), BF16)
    else:
        tiles = seq_len // tile
        per = tile // dilation
        assert seq_len % tile == 0 and tile % dilation == 0 and per % 16 == 0 and nc % LANES == 0
        out_spec = pl.BlockSpec((1, dilation, per, nc), lambda i: (i // tiles, 0, i % tiles, 0))
        out_shape = jax.ShapeDtypeStruct((n // seq_len, dilation, seq_len // dilation, nc), BF16)
        assert dilation % min(dilation, REGROUP_STRIDE) == 0
        scratch += [pltpu.VMEM((nc // LANES, tile, LANES), F32)] * 2
    return pl.pallas_call(
        functools.partial(_matmul_kernel, dilation=dilation),
        grid=(n // tile,),
        in_specs=[pl.BlockSpec((tile, k), lambda i: (i, 0))] + _weight_specs(row_blocks, k),
        out_specs=out_spec,
        out_shape=out_shape,
        scratch_shapes=scratch,
        compiler_params=_params_sequential(),
        name=name,
    )(h, *([wt] * len(row_blocks)))


DN_GROUP = 8
DN_ROWS = DN_GROUP * CHUNK
DN_HALO = 16


def _deltanet_kernel(a_ref, bg_ref, bgt_ref, cw_ref, nw_ref, o_ref,
                     u_ref, wq_ref, kd_ref, ai_ref, eg_ref):
    t_len = a_ref.shape[1]
    n_groups = t_len // DN_ROWS
    ii = lax.broadcasted_iota(jnp.int32, (CHUNK, CHUNK), 0)
    jj = lax.broadcasted_iota(jnp.int32, (CHUNK, CHUNK), 1)
    causal = ii >= jj
    strict = ii > jj
    eye_f = (ii == jj).astype(F32)
    bi = lax.broadcasted_iota(jnp.int32, (DN_ROWS, DN_ROWS), 0)
    bj = lax.broadcasted_iota(jnp.int32, (DN_ROWS, DN_ROWS), 1)
    shift = int(math.log2(CHUNK))
    same_chunk = jnp.right_shift(bi, shift) == jnp.right_shift(bj, shift)
    cum_lower = (same_chunk & (bi >= bj)).astype(BF16)
    cum_upper = (same_chunk & (bi <= bj)).astype(BF16)
    problems = [(c, h) for c in range(DN_GROUP) for h in range(DN_HEADS)]
    heads = range(DN_HEADS)
    hsl = lambda h: slice(h * HEAD_DIM, (h + 1) * HEAD_DIM)
    cs = lambda c: slice(c * CHUNK, (c + 1) * CHUNK)
    ones_b = jnp.ones((HEAD_DIM, HEAD_DIM), BF16)
    sel_src = lax.broadcasted_iota(jnp.int32, (LANES, DN_WIDTH), 0)
    sel_head = jnp.right_shift(lax.broadcasted_iota(jnp.int32, (LANES, DN_WIDTH), 1), int(math.log2(HEAD_DIM)))
    sel_beta = (sel_src == sel_head).astype(BF16)
    sel_g = (sel_src == sel_head + DN_HEADS).astype(BF16)
    lane = lax.broadcasted_iota(jnp.int32, (DN_ROWS, LANES), 1)
    g_lanes = (lane >= DN_HEADS) & (lane < 2 * DN_HEADS)

    pending = []

    def defer(ref, idx, value):
        pending.append((ref, idx, value))

    def flush():
        for ref, idx, value in pending:
            ref[idx] = value
        pending.clear()

    def conv_silu(gi, r0, col0):
        cols = slice(col0, col0 + HEAD_DIM)
        rp = pl.multiple_of(jnp.maximum(r0 - DN_HALO, 0), DN_HALO)
        cur = a_ref[0, pl.ds(r0, DN_ROWS), cols].astype(F32)
        prev = a_ref[0, pl.ds(rp, DN_HALO), cols].astype(F32)
        prev = jnp.where(gi > 0, prev, 0.0)
        xw = jnp.concatenate([prev, cur], axis=0)
        conv = None
        for j in range(CONV_WIDTH):
            lo = DN_HALO - (CONV_WIDTH - 1) + j
            term = cw_ref[j:j + 1, cols] * xw[lo:lo + DN_ROWS, :]
            conv = term if conv is None else conv + term
        return _silu(conv)

    def elementwise(gi, handoff):
        r0 = pl.multiple_of(gi * DN_ROWS, DN_ROWS)
        def act_of(h):
            tiles = []
            for base in (0, DN_WIDTH, 2 * DN_WIDTH):
                col0 = base + h * HEAD_DIM
                if col0 < CONV_EARLY_COLS:
                    tiles.append(a_ref[0, pl.ds(r0, DN_ROWS), col0:col0 + HEAD_DIM].astype(F32))
                else:
                    tiles.append(conv_silu(gi, r0, col0))
            return tiles
        acts = []

        def head_norms(h):
            qh, kh, vh = act_of(h)
            qn_h = qh * (lax.rsqrt(_mxu((qh * qh).astype(BF16), ones_b) + EPS) * (HEAD_DIM ** -0.5))
            kn_h = kh * lax.rsqrt(_mxu((kh * kh).astype(BF16), ones_b) + EPS)
            acts.append((qn_h, kn_h, vh))

        head_norms(0)
        yield
        bg = bg_ref[0, pl.ds(r0, DN_ROWS), :]
        gc_all = sum(_mxu(cum_lower, piece) for piece in _split3(bg))
        gct_all = sum(_mxu(piece, cum_upper) for piece in _split3(bgt_ref[:, pl.ds(r0, DN_ROWS)]))
        gl_all = jnp.concatenate([jnp.broadcast_to(gc_all[(c + 1) * CHUNK - 1:(c + 1) * CHUNK, :], (CHUNK, LANES))
                                  for c in range(DN_GROUP)], axis=0)
        eg_all = jnp.exp(gl_all)
        for c in range(DN_GROUP):
            defer(eg_ref, (gi * DN_GROUP + c,), eg_all[c * CHUNK:c * CHUNK + 8, :])
        gc_m = jnp.where(g_lanes, gc_all, 0.0)
        gl_m = jnp.where(g_lanes, gl_all, 0.0)
        beta_rep = _mxu(bg.astype(BF16), sel_beta)
        egc_rep = _mxu(jnp.exp(gc_m).astype(BF16), sel_g)
        ekd_rep = _mxu(jnp.exp(gl_m - gc_m).astype(BF16), sel_g)

        for h in range(1, DN_HEADS):
            head_norms(h)
            yield

        qn, knb, kb, rhs = [], [], [], []
        for h in heads:
            qn_h, kn_h, vh = acts[h]
            kb_h = kn_h * beta_rep[:, hsl(h)]
            defer(kd_ref, (h, pl.ds(r0, DN_ROWS), slice(None)), (kn_h * ekd_rep[:, hsl(h)]).astype(BF16))
            qg_h = (qn_h * egc_rep[:, hsl(h)]).astype(BF16)
            for c in range(DN_GROUP):
                w0 = pl.multiple_of(2 * (r0 + c * CHUNK), 2 * CHUNK)
                defer(wq_ref, (h, pl.ds(w0 + CHUNK, CHUNK), slice(None)), qg_h[cs(c)])
            qn.append(qn_h)
            knb.append(kn_h.astype(BF16))
            kb.append(kb_h)
            rhs.append(jnp.concatenate([vh * beta_rep[:, hsl(h)], kb_h * egc_rep[:, hsl(h)]], axis=1).astype(BF16))
            yield

        qk = [lax.dot_general(jnp.concatenate([qn[h][cs(c)], kb[h][cs(c)]], axis=0).astype(BF16),
                              knb[h][cs(c)], _NT, preferred_element_type=F32) for c, h in problems]
        decay = []
        for c, h in problems:
            diff = gc_all[cs(c), DN_HEADS + h:DN_HEADS + h + 1] - gct_all[DN_HEADS + h:DN_HEADS + h + 1, cs(c)]
            decay.append(jnp.where(causal, jnp.exp(jnp.where(causal, diff, 0.0)), 0.0))
        for i, (c, h) in enumerate(problems):
            a_intra = jnp.where(causal, qk[i][:CHUNK] * decay[i], 0.0)
            defer(ai_ref, (h, pl.ds(r0 + c * CHUNK, CHUNK), slice(None)), a_intra.astype(BF16))
        handoff["neg_lower"] = [-jnp.where(strict, qk[i][CHUNK:] * decay[i], 0.0) for i in range(len(problems))]
        handoff["rhs"] = rhs

    def solve(gi, handoff):
        r0 = pl.multiple_of(gi * DN_ROWS, DN_ROWS)
        ps = handoff["neg_lower"]
        rhs = handoff["rhs"]
        tms = [eye_f + p for p in ps]
        for _ in range(shift - 1):
            pbs = [p.astype(BF16) for p in ps]
            ps = [_mxu(pb, pb) for pb in pbs]
            tms = [tm + _mxu(tm.astype(BF16), p.astype(BF16)) for tm, p in zip(tms, ps)]
            yield
        uw = [_mxu(tms[i].astype(BF16), rhs[h][cs(c)]) for i, (c, h) in enumerate(problems)]
        for i, (c, h) in enumerate(problems):
            defer(u_ref, (h, pl.ds(r0 + c * CHUNK, CHUNK), slice(None)), uw[i][:, :HEAD_DIM])
            w0 = pl.multiple_of(2 * (r0 + c * CHUNK), 2 * CHUNK)
            defer(wq_ref, (h, pl.ds(w0, CHUNK), slice(None)), uw[i][:, HEAD_DIM:].astype(BF16))

    def recur(c, state):
        r0 = pl.multiple_of(c * CHUNK, CHUNK)
        rows = pl.ds(r0, CHUNK)
        wrows = pl.ds(pl.multiple_of(2 * r0, 2 * CHUNK), 2 * CHUNK)
        eg = eg_ref[c]
        ws = [_mxu(wq_ref[h, wrows, :], state[h].astype(BF16)) for h in heads]
        vn = [(u_ref[h, rows, :] - ws[h][:CHUNK]).astype(BF16) for h in heads]
        new_state = [state[h] * eg[0:1, DN_HEADS + h:DN_HEADS + h + 1]
                     + lax.dot_general(kd_ref[h, rows, :], vn[h], _TN, preferred_element_type=F32) for h in heads]
        o = [ws[h][CHUNK:] + _mxu(ai_ref[h, rows, :], vn[h]) for h in heads]
        ms = [_mxu((x * x).astype(BF16), ones_b) * (1.0 / HEAD_DIM) for x in o]
        for h in heads:
            on = o[h] * lax.rsqrt(ms[h] + EPS) * nw_ref[...]
            z = a_ref[0, rows, 3 * DN_WIDTH + h * HEAD_DIM:3 * DN_WIDTH + (h + 1) * HEAD_DIM].astype(F32)
            defer(o_ref, (0, rows, hsl(h)), (on * _silu(z)).astype(o_ref.dtype))
        return new_state

    def pipeline_step(g_prep, g_recur, state):
        def stages():
            if g_prep is not None:
                handoff = {}
                yield from elementwise(g_prep, handoff)
                yield
                yield from solve(g_prep, handoff)

        work = stages()
        for c in range(DN_GROUP):
            if g_recur is not None:
                state = recur(g_recur * DN_GROUP + c, state)
            next(work, None)
        for _ in work:
            pass
        flush()
        return state

    zero = jnp.zeros((HEAD_DIM, HEAD_DIM), F32)
    state = pipeline_step(0, None, [zero] * DN_HEADS)
    state = list(lax.fori_loop(
        1, n_groups, lambda gi, st: tuple(pipeline_step(gi, gi - 1, list(st))), tuple(state)))
    pipeline_step(None, n_groups - 1, state)


def _deltanet(a3, bg3, bgt, conv_w2, dn_norm_row):
    b, t, wa = a3.shape
    nt = bgt.shape[0]
    return pl.pallas_call(
        _deltanet_kernel,
        grid=(b,),
        in_specs=[pl.BlockSpec((1, t, wa), lambda i: (i, 0, 0)),
                  pl.BlockSpec((1, t, LANES), lambda i: (i, 0, 0)),
                  pl.BlockSpec((nt, t), lambda i: (0, i)),
                  pl.BlockSpec((CONV_WIDTH, 3 * DN_WIDTH), lambda i: (0, 0)),
                  pl.BlockSpec((1, HEAD_DIM), lambda i: (0, 0))],
        out_specs=pl.BlockSpec((1, t, DN_WIDTH), lambda i: (i, 0, 0)),
        out_shape=jax.ShapeDtypeStruct((b, t, DN_WIDTH), BF16),
        scratch_shapes=[pltpu.VMEM((DN_HEADS, t, HEAD_DIM), F32),
                        pltpu.VMEM((DN_HEADS, 2 * t, HEAD_DIM), BF16),
                        pltpu.VMEM((DN_HEADS, t, HEAD_DIM), BF16),
                        pltpu.VMEM((DN_HEADS, t, CHUNK), BF16),
                        pltpu.VMEM((t // CHUNK, 8, LANES), F32)],
        compiler_params=pltpu.CompilerParams(dimension_semantics=("parallel",),
                                             vmem_limit_bytes=DN_VMEM_LIMIT_BYTES),
        name="deltanet",
    )(a3, bg3, bgt, conv_w2, dn_norm_row)


SWA_PREP_ROWS = 1024
SWA_UNITS = 4


def _swa_kernel(s_ref, plain_ref, swapped_ref, o_ref, lse_ref, qs_ref, ks_ref, bias_ref, *dilated_scratch, dilation):
    t_len = s_ref.shape[1]
    d = dilation
    seq = t_len // d
    nblk = seq // SWA_BLOCK
    heads = range(SWA_HEADS)
    hsl = lambda h: slice(h * HEAD_DIM, (h + 1) * HEAD_DIM)
    if d > 1:
        (ostage_ref,) = dilated_scratch
    mean_b = jnp.full((HEAD_DIM, HEAD_DIM), 1.0 / HEAD_DIM, BF16)
    src_lane = lax.broadcasted_iota(jnp.int32, (HEAD_DIM, HEAD_DIM), 0)
    dst_lane = lax.broadcasted_iota(jnp.int32, (HEAD_DIM, HEAD_DIM), 1)
    swap_b = (((dst_lane < ROPE_HALF) & (src_lane == dst_lane + ROPE_HALF))
              | ((dst_lane >= ROPE_HALF) & (dst_lane < ROPE_DIM) & (src_lane == dst_lane - ROPE_HALF))
              ).astype(BF16)

    def prep(n, carry):
        r0 = pl.multiple_of(n * SWA_PREP_ROWS, SWA_PREP_ROWS)
        rows = pl.ds(r0, SWA_PREP_ROWS)
        for which, dst in ((0, qs_ref), (1, ks_ref)):
            plain = plain_ref[which, rows, :]
            swapped_tab = swapped_ref[which, rows, :]
            xb = [s_ref[0, rows, which * SWA_WIDTH + h * HEAD_DIM:which * SWA_WIDTH + (h + 1) * HEAD_DIM]
                  for h in heads]
            ms = [_mxu(x * x, mean_b) for x in xb]
            sw = [_mxu(x, swap_b) for x in xb]
            ys = [(x.astype(F32) * plain + s * swapped_tab) * lax.rsqrt(m + EPS) for x, s, m in zip(xb, sw, ms)]
            for h in heads:
                dst[rows, hsl(h)] = ys[h].astype(BF16)
        return carry

    lax.fori_loop(0, t_len // SWA_PREP_ROWS, prep, 0)

    qi = lax.broadcasted_iota(jnp.int32, (SWA_BLOCK, 2 * SWA_BLOCK), 0)
    kj = lax.broadcasted_iota(jnp.int32, (SWA_BLOCK, 2 * SWA_BLOCK), 1)
    bias_ref[0] = jnp.where(kj <= qi, 0.0, NEG_INF).astype(F32)
    bias_ref[1] = jnp.where((kj >= qi) & (kj <= qi + SWA_BLOCK), 0.0, NEG_INF).astype(F32)
    lane = lax.broadcasted_iota(jnp.int32, (SWA_BLOCK, LANES), 1)

    def attend(units):
        jobs = [(u, h) for u in range(len(units)) for h in heads]
        qrows = [pl.ds(q0, SWA_BLOCK) for q0, _, _, _, _ in units]
        krows = [pl.ds(k0, width) for _, k0, width, _, _ in units]
        s = [lax.dot_general(qs_ref[qrows[u], hsl(h)], ks_ref[krows[u], hsl(h)], _NT,
                             preferred_element_type=F32) + units[u][3] for u, h in jobs]
        m = [jnp.max(x, axis=-1, keepdims=True) for x in s]
        p = [jnp.exp(x - mx) for x, mx in zip(s, m)]
        l = [jnp.sum(x, axis=-1, keepdims=True) for x in p]
        v = [s_ref[0, krows[u], 2 * SWA_WIDTH + h * HEAD_DIM:2 * SWA_WIDTH + (h + 1) * HEAD_DIM] for u, h in jobs]
        o = [_mxu(x.astype(BF16), vh) for x, vh in zip(p, v)]
        for u in range(len(units)):
            lse_tile = jnp.zeros((SWA_BLOCK, LANES), F32)
            for h in heads:
                j = u * SWA_HEADS + h
                lse_tile = jnp.where(lane == h, m[j], jnp.where(lane == SWA_HEADS + h, l[j], lse_tile))
            t0 = units[u][4]
            if d == 1:
                for h in heads:
                    o_ref[0, qrows[u], hsl(h)] = o[u * SWA_HEADS + h].astype(o_ref.dtype)
                lse_ref[0, qrows[u], :] = lse_tile
            else:
                trows = pl.ds(t0, SWA_BLOCK, stride=d)
                for h in heads:
                    ostage_ref[h, trows, :] = o[u * SWA_HEADS + h]
                lse_ref[0, trows, :] = lse_tile

    if nblk == 1:
        causal_bias = bias_ref[0][:, :SWA_BLOCK]

        def body(i, carry):
            units = []
            for u in range(SWA_UNITS):
                r = i * SWA_UNITS + u
                q0 = pl.multiple_of(r * seq, SWA_BLOCK)
                units.append((q0, q0, SWA_BLOCK, causal_bias, r))
            attend(units)
            return carry

        lax.fori_loop(0, d // SWA_UNITS, body, 0)
    else:
        pairs = nblk // SWA_UNITS

        def body(i, carry):
            r = i // pairs
            pp = i % pairs
            units = []
            for u in range(SWA_UNITS):
                n = pp * SWA_UNITS + u
                q0 = pl.multiple_of(r * seq + n * SWA_BLOCK, SWA_BLOCK)
                if u == 0:
                    first = jnp.asarray(pp == 0).astype(jnp.int32)
                    k0 = pl.multiple_of(q0 - (1 - first) * SWA_BLOCK, SWA_BLOCK)
                    bias = bias_ref[1 - first]
                else:
                    k0 = pl.multiple_of(q0 - SWA_BLOCK, SWA_BLOCK)
                    bias = bias_ref[1]
                units.append((q0, k0, 2 * SWA_BLOCK, bias, r + d * n * SWA_BLOCK))
            attend(units)
            return carry

        lax.fori_loop(0, d * pairs, body, 0)

    if d > 1:
        def emit(n, carry):
            rows = pl.ds(pl.multiple_of(n * SWA_PREP_ROWS, SWA_PREP_ROWS), SWA_PREP_ROWS)
            for h in heads:
                o_ref[0, rows, hsl(h)] = ostage_ref[h, rows, :].astype(o_ref.dtype)
            return carry
        lax.fori_loop(0, t_len // SWA_PREP_ROWS, emit, 0)


def _swa_group(s3, plain, swapped, group, dilation):
    b, t, ws = s3.shape
    assert (t // dilation) % SWA_BLOCK == 0 and t % SWA_PREP_ROWS == 0
    nblk = t // dilation // SWA_BLOCK
    assert (dilation % SWA_UNITS == 0) if nblk == 1 else (nblk % SWA_UNITS == 0)
    scratch = [pltpu.VMEM((t, SWA_WIDTH), BF16), pltpu.VMEM((t, SWA_WIDTH), BF16),
               pltpu.VMEM((2, SWA_BLOCK, 2 * SWA_BLOCK), F32)]
    if dilation > 1:
        scratch += [pltpu.VMEM((SWA_HEADS, t, LANES), F32)]
    tab_spec = pl.BlockSpec((None, 2, t, LANES), lambda i: (group, 0, 0, 0))
    return pl.pallas_call(
        functools.partial(_swa_kernel, dilation=dilation),
        grid=(b,),
        in_specs=[pl.BlockSpec((1, t, ws), lambda i: (i, 0, 0)), tab_spec, tab_spec],
        out_specs=[pl.BlockSpec((1, t, SWA_WIDTH), lambda i: (i, 0, 0)),
                   pl.BlockSpec((1, t, LANES), lambda i: (i, 0, 0))],
        out_shape=[jax.ShapeDtypeStruct((b, t, SWA_WIDTH), BF16),
                   jax.ShapeDtypeStruct((b, t, LANES), F32)],
        scratch_shapes=scratch,
        compiler_params=_params(1),
        name=f"swa_d{dilation}",
    )(s3, plain, swapped)


MERGE_ROWS = 512


def _merge_out_kernel(x_ref, h_ref, odn_ref, o0_ref, o1_ref, o2_ref, l0_ref, l1_ref, l2_ref,
                      wz_ref, wgdn_ref, wgswa_ref, wdn32_ref, wswa32_ref, wout32_ref, out_ref,
                      wzg_scr, wdn_ref, wswa_ref, wout_ref):
    _cast_weights_once([wz_ref, wgdn_ref, wgswa_ref], wzg_scr)
    _cast_weights_once([wdn32_ref], wdn_ref)
    _cast_weights_once([wswa32_ref], wswa_ref)
    _cast_weights_once([wout32_ref], wout_ref)
    zg_all = lax.dot_general(h_ref[...], wzg_scr[...], _NT, preferred_element_type=F32)
    outs = []
    for j in range(ROW_TILE // MERGE_ROWS):
        rows = slice(j * MERGE_ROWS, (j + 1) * MERGE_ROWS)
        zg = zg_all[rows, :]
        st0 = l0_ref[rows, :]
        st1 = l1_ref[rows, :]
        st2 = l2_ref[rows, :]
        m = jnp.maximum(jnp.maximum(st0, st1), st2)
        e0 = jnp.exp(st0 - m)
        e1 = jnp.exp(st1 - m)
        e2 = jnp.exp(st2 - m)
        sums = [pltpu.roll(st, LANES - SWA_HEADS, 1) for st in (st0, st1, st2)]
        inv = 1.0 / (e0 * sums[0] + e1 * sums[1] + e2 * sums[2])
        a0, a1, a2 = e0 * inv, e1 * inv, e2 * inv
        parts = []
        for h in range(SWA_HEADS):
            hs = slice(h * HEAD_DIM, (h + 1) * HEAD_DIM)
            col = slice(h, h + 1)
            oh = (a0[:, col] * o0_ref[rows, hs].astype(F32) + a1[:, col] * o1_ref[rows, hs].astype(F32)
                  + a2[:, col] * o2_ref[rows, hs].astype(F32))
            parts.append((oh * _silu(zg[:, hs])).astype(BF16))
        o_swa = jnp.concatenate(parts, axis=1)
        y_swa = _mxu(o_swa, wswa_ref[...])
        y_dn = _mxu(odn_ref[rows, :], wdn_ref[...])
        g_dn = zg[:, SWA_WIDTH:SWA_WIDTH + D_MODEL]
        g_swa = zg[:, SWA_WIDTH + D_MODEL:SWA_WIDTH + 2 * D_MODEL]
        merged = _sigmoid(g_dn) * y_dn + _sigmoid(g_swa) * y_swa
        outs.append((rows, x_ref[rows, :] + _mxu(merged.astype(BF16), wout_ref[...])))
    for rows, value in outs:
        out_ref[rows, :] = value


def _merge_out(x2, h, odn, o_list, lse_list, wt, zg_row_blocks, w_dn, w_swa, w_out):
    n = x2.shape[0]
    row = lambda w: pl.BlockSpec((ROW_TILE, w), lambda i: (i, 0))
    full = lambda a: pl.BlockSpec(a.shape, lambda i: (0, 0), pipeline_mode=pl.Buffered(1))
    nzg = sum(rows for _, rows in zg_row_blocks)
    return pl.pallas_call(
        _merge_out_kernel,
        grid=(n // ROW_TILE,),
        in_specs=[row(D_MODEL), row(D_MODEL), row(DN_WIDTH), row(SWA_WIDTH), row(SWA_WIDTH), row(SWA_WIDTH),
                  row(LANES), row(LANES), row(LANES)]
                 + _weight_specs(zg_row_blocks, D_MODEL)
                 + [full(w_dn), full(w_swa), full(w_out)],
        out_specs=row(D_MODEL),
        out_shape=jax.ShapeDtypeStruct((n, D_MODEL), F32),
        scratch_shapes=[pltpu.VMEM((nzg, D_MODEL), BF16)] + [pltpu.VMEM(w.shape, BF16) for w in (w_dn, w_swa, w_out)],
        compiler_params=_params_sequential(),
        name="merge_out",
    )(x2, h, odn, *o_list, *lse_list, wt, wt, wt, w_dn, w_swa, w_out)


def _rope_tables(t_len, q_norm_w, k_norm_w):
    j = jnp.arange(t_len, dtype=jnp.int32)
    pos = jnp.stack([((j % (t_len // d)) * d + j // (t_len // d)).astype(F32) for _, d in SWA_GROUPS])
    inv_freq = ROPE_THETA ** (-jnp.arange(0, ROPE_DIM, 2, dtype=F32) / ROPE_DIM)
    ang = pos[:, :, None] * inv_freq[None, None, :]
    cos, sin = jnp.cos(ang), jnp.sin(ang)
    tail = (N_GROUPS, t_len, HEAD_DIM - ROPE_DIM)
    cos_t = jnp.concatenate([cos, cos, jnp.ones(tail, F32)], axis=-1)
    sin_t = jnp.concatenate([-sin, sin, jnp.zeros(tail, F32)], axis=-1)
    w = jnp.stack([q_norm_w.astype(F32) * (HEAD_DIM ** -0.5), k_norm_w.astype(F32)], axis=1)
    w_swapped = jnp.concatenate([w[..., ROPE_HALF:ROPE_DIM], w[..., :ROPE_HALF], w[..., ROPE_DIM:]], axis=-1)
    return w[:, :, None, :] * cos_t[:, None], w_swapped[:, :, None, :] * sin_t[:, None]


def kernel(x, norm_w, w_in, conv_w, dn_a_log, dn_dt_bias, dn_norm_w, q_norm_w, k_norm_w,
           w_branch_dn, w_branch_swa, w_out):
    b, t, d = x.shape
    n = b * t
    layer = 0
    wt = jnp.swapaxes(w_in[layer], 0, 1)
    c_z = 4 * DN_WIDTH
    c_q = c_z + 2 * DN_HEADS
    c_k = c_q + N_GROUPS * SWA_WIDTH
    c_v = c_k + N_GROUPS * SWA_WIDTH
    c_sz = c_v + N_GROUPS * SWA_WIDTH
    c_g = c_sz + SWA_WIDTH
    grp_blocks = lambda g: [(c0 + g * SWA_WIDTH, SWA_WIDTH) for c0 in (c_q, c_k, c_v)]
    zg_blocks = [(c_sz, SWA_WIDTH), (c_g, D_MODEL), (c_g + D_MODEL, D_MODEL)]
    pad_heads = lambda v: jnp.pad(v.astype(F32), (DN_HEADS, LANES - 2 * DN_HEADS))[None, :]
    alog_row = pad_heads(dn_a_log[layer])
    dt_row = pad_heads(dn_dt_bias[layer])

    x2 = x.reshape(n, d)
    h, a, bg, bgt = _norm_proj(x2, norm_w[layer][None, :], wt, c_z, conv_w[layer][:, 0, :], alog_row, dt_row, t)

    o_dn = _deltanet(a.reshape(b, t, 4 * DN_WIDTH), bg.reshape(b, t, LANES), bgt,
                     conv_w[layer][:, 0, :], dn_norm_w[layer][None, :])

    plain, swapped = _rope_tables(t, q_norm_w[layer], k_norm_w[layer])
    o_list, lse_list = [], []
    for g, (window, dilation) in enumerate(SWA_GROUPS):
        assert window // dilation == SWA_BLOCK
        s_g = _matmul(h, wt, grp_blocks(g), f"proj_swa{g}", dilation, t).reshape(b, t, 3 * SWA_WIDTH)
        o_g, lse_g = _swa_group(s_g, plain, swapped, g, dilation)
        o_list.append(o_g.reshape(n, SWA_WIDTH))
        lse_list.append(lse_g.reshape(n, LANES))

    out = _merge_out(x2, h, o_dn.reshape(n, DN_WIDTH), o_list, lse_list, wt, zg_blocks,
                     w_branch_dn[layer], w_branch_swa[layer], w_out[layer])
    return out.reshape(b, t, d)
```

```python
import functools
import math

import jax
import jax.numpy as jnp
import numpy as np
from jax import lax
from jax.experimental import pallas as pl
from jax.experimental.pallas import tpu as pltpu

D_MODEL = 1024
HEAD_DIM = 128
DN_HEADS = 4
DN_WIDTH = DN_HEADS * HEAD_DIM
CONV_WIDTH = 4
CHUNK = 64
SWA_GROUPS = ((128, 1), (512, 4), (2048, 16))
N_GROUPS = 3
SWA_HEADS = 4
SWA_WIDTH = SWA_HEADS * HEAD_DIM
SWA_BLOCK = 128
ROPE_DIM = HEAD_DIM // 4
ROPE_HALF = ROPE_DIM // 2
ROPE_THETA = 500000.0
EPS = 1e-6
NEG_INF = -1e30

LANES = 128
VMEM_LIMIT_BYTES = 48 * 1024 * 1024
DN_VMEM_LIMIT_BYTES = 56 * 1024 * 1024
ROW_TILE = 512
NORM_ROW_TILE = 512
PROJ_ROW_TILE = 1024
BG_ROWS = 16

F32 = jnp.float32
BF16 = jnp.bfloat16
_NT = (((1,), (1,)), ((), ()))
_TN = (((0,), (0,)), ((), ()))


def _sigmoid(x):
    return 0.5 * jnp.tanh(0.5 * x) + 0.5


def _silu(x):
    return x * _sigmoid(x)


def _softplus(x):
    return jnp.maximum(x, 0.0) + jnp.log(1.0 + jnp.exp(-jnp.abs(x)))


def _params(n_axes):
    return pltpu.CompilerParams(dimension_semantics=("parallel",) * n_axes,
                                vmem_limit_bytes=VMEM_LIMIT_BYTES)


def _params_sequential():
    return pltpu.CompilerParams(dimension_semantics=("arbitrary",), vmem_limit_bytes=VMEM_LIMIT_BYTES)


def _mxu(a, b):
    return jnp.dot(a, b, preferred_element_type=F32)


def _split3(x):
    hi = x.astype(BF16)
    r1 = x - hi.astype(F32)
    mid = r1.astype(BF16)
    lo = (r1 - mid.astype(F32)).astype(BF16)
    return hi, mid, lo


def _beta_and_log_decay(ab, a_log, dt_bias, head_index):
    beta = _sigmoid(ab)
    g = -jnp.exp(a_log) * _softplus(ab + dt_bias)
    return jnp.where(head_index < DN_HEADS, beta, g)


def _weight_specs(row_blocks, k):
    return [pl.BlockSpec((pl.Element(rows), pl.Element(k)), lambda i, start=start: (start, 0),
                         pipeline_mode=pl.Buffered(1))
            for start, rows in row_blocks]


def _cast_weights_once(w_refs, w_scr):
    @pl.when(pl.program_id(0) == 0)
    def _():
        blocks = [w_ref[...] for w_ref in w_refs]
        pad = w_scr.shape[0] - sum(blk.shape[0] for blk in blocks)
        if pad:
            blocks.append(jnp.zeros((pad, w_scr.shape[1]), F32))
        w_scr[...] = (jnp.concatenate(blocks, axis=0) if len(blocks) > 1 else blocks[0]).astype(BF16)


CONV_TILE = 256
CONV_HALO = 8
CONV_ROWS = 64
CONV_EARLY_COLS = 2 * DN_WIDTH


def _norm_proj_kernel(x_ref, nw_ref, wa_ref, wab_ref, cw_ref, alog_ref, dt_ref, alogt_ref, dtt_ref,
                      h_ref, a_ref, bg_ref, bgt_ref, wa_scr, wab_scr, tail_ref, res_a, res_b, *, tiles_per_seq):
    res_scr = (res_a, res_b)
    tile_rows = x_ref.shape[0]
    _cast_weights_once([wa_ref], wa_scr)
    _cast_weights_once([wab_ref], wab_scr)
    x = x_ref[...]
    h = (x * lax.rsqrt(jnp.mean(x * x, axis=-1, keepdims=True) + EPS)) * nw_ref[...]
    hb = h.astype(BF16)
    h_ref[...] = hb
    n_conv = CONV_EARLY_COLS
    seq_start = pl.program_id(0) % tiles_per_seq == 0
    tail = jnp.where(seq_start, 0.0, tail_ref[...])
    stores = []
    tiles = list(range(0, n_conv, CONV_TILE))
    base = pl.multiple_of((pl.program_id(0) >> 20) * CONV_HALO, CONV_HALO)

    def project(k):
        c0 = tiles[k]
        buf = res_scr[k % 2]
        buf[pl.ds(base, CONV_HALO), :] = tail[:, c0:c0 + CONV_TILE]
        buf[pl.ds(base + CONV_HALO, tile_rows), :] = lax.dot_general(hb, wa_scr[c0:c0 + CONV_TILE, :], _NT,
                                                                    preferred_element_type=F32)

    if tiles:
        project(0)
    for k, c0 in enumerate(tiles):
        cols = slice(c0, c0 + CONV_TILE)
        if k + 1 < len(tiles):
            project(k + 1)
        buf = res_scr[k % 2]
        acts = []
        for rb in range(0, tile_rows, CONV_ROWS):
            blk = buf[pl.ds(base + rb, CONV_HALO + CONV_ROWS), :]
            conv = None
            for j in range(CONV_WIDTH):
                lo = CONV_HALO - (CONV_WIDTH - 1) + j
                term = cw_ref[j:j + 1, cols] * blk[lo:lo + CONV_ROWS, :]
                conv = term if conv is None else conv + term
            acts.append(_silu(conv).astype(BF16))
        stores.append((a_ref, (slice(None), cols), jnp.concatenate(acts, axis=0)))
        stores.append((tail_ref, (slice(None), cols), buf[pl.ds(base + tile_rows, CONV_HALO), :]))
    z = lax.dot_general(hb, wa_scr[n_conv:, :], _NT, preferred_element_type=F32)
    stores.append((a_ref, (slice(None), slice(n_conv, None)), z.astype(BF16)))
    for ref, idx, value in stores:
        ref[idx] = value
    ab = lax.dot_general(hb, wab_scr[...], _NT, preferred_element_type=F32)
    bg_ref[...] = _beta_and_log_decay(ab, alog_ref[...], dt_ref[...],
                                      lax.broadcasted_iota(jnp.int32, ab.shape, 1))
    abt = lax.dot_general(wab_scr[0:BG_ROWS, :], hb, _NT, preferred_element_type=F32)
    bgt_ref[...] = _beta_and_log_decay(abt, alogt_ref[...], dtt_ref[...],
                                       lax.broadcasted_iota(jnp.int32, abt.shape, 0))


def _norm_proj(x2, norm_w, wt, na, conv_w2, alog_row, dt_row, seq_len):
    n = x2.shape[0]
    tile = NORM_ROW_TILE
    assert seq_len % tile == 0 and tile % CONV_ROWS == 0 and CONV_EARLY_COLS % CONV_TILE == 0
    row = lambda i: (i, 0)
    fixed = lambda i: (0, 0)
    pad_col = lambda v: v[0, :BG_ROWS][:, None]
    return pl.pallas_call(
        functools.partial(_norm_proj_kernel, tiles_per_seq=seq_len // tile),
        grid=(n // tile,),
        in_specs=[pl.BlockSpec((tile, D_MODEL), row),
                  pl.BlockSpec((1, D_MODEL), fixed)]
                 + _weight_specs([(0, na), (na, 2 * DN_HEADS)], D_MODEL)
                 + [pl.BlockSpec((CONV_WIDTH, 3 * DN_WIDTH), fixed),
                    pl.BlockSpec((1, LANES), fixed),
                    pl.BlockSpec((1, LANES), fixed),
                    pl.BlockSpec((BG_ROWS, 1), fixed),
                    pl.BlockSpec((BG_ROWS, 1), fixed)],
        out_specs=[pl.BlockSpec((tile, D_MODEL), row),
                   pl.BlockSpec((tile, na), row),
                   pl.BlockSpec((tile, LANES), row),
                   pl.BlockSpec((BG_ROWS, tile), lambda i: (0, i))],
        out_shape=[jax.ShapeDtypeStruct((n, D_MODEL), BF16),
                   jax.ShapeDtypeStruct((n, na), BF16),
                   jax.ShapeDtypeStruct((n, LANES), F32),
                   jax.ShapeDtypeStruct((BG_ROWS, n), F32)],
        scratch_shapes=[pltpu.VMEM((na, D_MODEL), BF16), pltpu.VMEM((LANES, D_MODEL), BF16),
                        pltpu.VMEM((CONV_HALO, max(CONV_EARLY_COLS, LANES)), F32),
                        pltpu.VMEM((CONV_HALO + tile, CONV_TILE), F32),
                        pltpu.VMEM((CONV_HALO + tile, CONV_TILE), F32)],
        compiler_params=_params_sequential(),
        name="norm_proj",
    )(x2, norm_w, wt, wt, conv_w2, alog_row, dt_row, pad_col(alog_row), pad_col(dt_row))


REGROUP_STRIDE = 4


def _matmul_kernel(h_ref, *refs, dilation):
    n_scratch = 1 if dilation == 1 else 3
    w_refs, o_ref, w_scr = refs[:-n_scratch - 1], refs[-n_scratch - 1], refs[-n_scratch]
    _cast_weights_once(w_refs, w_scr)
    if dilation == 1:
        o_ref[...] = lax.dot_general(h_ref[...], w_scr[...], _NT, preferred_element_type=F32).astype(o_ref.dtype)
        return
    stage_ref, stage2_ref = refs[-2], refs[-1]
    s1 = min(dilation, REGROUP_STRIDE)
    s2 = dilation // s1
    rows1 = h_ref.shape[0] // s1
    per = h_ref.shape[0] // dilation
    col = 0
    for w_ref in w_refs:
        width = w_ref.shape[0]
        res = lax.dot_general(h_ref[...], w_scr[col:col + width, :], _NT, preferred_element_type=F32)
        slabs = range(col // LANES, (col + width) // LANES)
        for c in slabs:
            stage_ref[c] = res[:, (c * LANES - col):(c * LANES - col) + LANES]
        for c in slabs:
            for a in range(s1):
                first = stage_ref[c, pl.ds(a, rows1, stride=s1), :]
                if s2 == 1:
                    o_ref[0, a, :, c * LANES:(c + 1) * LANES] = first.astype(o_ref.dtype)
                else:
                    stage2_ref[c, a * rows1:(a + 1) * rows1, :] = first
        if s2 > 1:
            for c in slabs:
                for a in range(s1):
                    for b in range(s2):
                        o_ref[0, b * s1 + a, :, c * LANES:(c + 1) * LANES] = (
                            stage2_ref[c, pl.ds(a * rows1 + b, per, stride=s2), :].astype(o_ref.dtype))
        col += width


def _matmul(h, wt, row_blocks, name, dilation=1, seq_len=None):
    n, k = h.shape
    nc = sum(rows for _, rows in row_blocks)
    scratch = [pltpu.VMEM((nc, k), BF16)]
    tile = PROJ_ROW_TILE
    if dilation == 1:
        out_spec = pl.BlockSpec((tile, nc), lambda i: (i, 0))
        out_shape = jax.ShapeDtypeStruct((n, nc), BF16)
    else:
        tiles = seq_len // tile
        per = tile // dilation
        assert seq_len % tile == 0 and tile % dilation == 0 and per % 16 == 0 and nc % LANES == 0
        out_spec = pl.BlockSpec((1, dilation, per, nc), lambda i: (i // tiles, 0, i % tiles, 0))
        out_shape = jax.ShapeDtypeStruct((n // seq_len, dilation, seq_len // dilation, nc), BF16)
        assert dilation % min(dilation, REGROUP_STRIDE) == 0
        scratch += [pltpu.VMEM((nc // LANES, tile, LANES), F32)] * 2
    return pl.pallas_call(
        functools.partial(_matmul_kernel, dilation=dilation),
        grid=(n // tile,),
        in_specs=[pl.BlockSpec((tile, k), lambda i: (i, 0))] + _weight_specs(row_blocks, k),
        out_specs=out_spec,
        out_shape=out_shape,
        scratch_shapes=scratch,
        compiler_params=_params_sequential(),
        name=name,
    )(h, *([wt] * len(row_blocks)))


DN_GROUP = 8
DN_ROWS = DN_GROUP * CHUNK
DN_HALO = 16


def _deltanet_kernel(a_ref, bg_ref, bgt_ref, cw_ref, nw_ref, o_ref,
                     u_ref, wq_ref, kd_ref, ai_ref, eg_ref):
    t_len = a_ref.shape[1]
    n_groups = t_len // DN_ROWS
    assert 2 * CHUNK == LANES and DN_GROUP % 2 == 0
    ii = lax.broadcasted_iota(jnp.int32, (CHUNK, LANES), 0)
    lane_pair = lax.broadcasted_iota(jnp.int32, (CHUNK, LANES), 1)
    jj = lane_pair & (CHUNK - 1)
    second = lane_pair >= CHUNK
    causal = ii >= jj
    strict = ii > jj
    eye_f = (ii == jj).astype(F32)
    second_rows = lax.broadcasted_iota(jnp.int32, (LANES, LANES), 0) >= CHUNK
    second_cols = lax.broadcasted_iota(jnp.int32, (LANES, LANES), 1) >= CHUNK
    same_half = second_rows == second_cols

    def block_diag(pair):
        return jnp.where(same_half, jnp.concatenate([pair, pair], axis=0), 0)

    bi =lax.broadcasted_iota(jnp.int32, (DN_ROWS, DN_ROWS), 0)
    bj = lax.broadcasted_iota(jnp.int32, (DN_ROWS, DN_ROWS), 1)
    shift = int(math.log2(CHUNK))
    same_chunk = jnp.right_shift(bi, shift) == jnp.right_shift(bj, shift)
    cum_lower = (same_chunk & (bi >= bj)).astype(BF16)
    cum_upper = (same_chunk & (bi <= bj)).astype(BF16)
    problems = [(cp, h) for cp in range(DN_GROUP // 2) for h in range(DN_HEADS)]
    heads = range(DN_HEADS)
    hsl = lambda h: slice(h * HEAD_DIM, (h + 1) * HEAD_DIM)
    cs = lambda c: slice(c * CHUNK, (c + 1) * CHUNK)
    ones_b = jnp.ones((HEAD_DIM, HEAD_DIM), BF16)
    sel_src = lax.broadcasted_iota(jnp.int32, (LANES, DN_WIDTH), 0)
    sel_head = jnp.right_shift(lax.broadcasted_iota(jnp.int32, (LANES, DN_WIDTH), 1), int(math.log2(HEAD_DIM)))
    sel_beta = (sel_src == sel_head).astype(BF16)
    sel_g = (sel_src == sel_head + DN_HEADS).astype(BF16)
    lane = lax.broadcasted_iota(jnp.int32, (DN_ROWS, LANES), 1)
    g_lanes = (lane >= DN_HEADS) & (lane < 2 * DN_HEADS)

    pending = []

    def defer(ref, idx, value):
        pending.append((ref, idx, value))

    def flush():
        for ref, idx, value in pending:
            ref[idx] = value
        pending.clear()

    def conv_silu(gi, r0, col0):
        cols = slice(col0, col0 + HEAD_DIM)
        rp = pl.multiple_of(jnp.maximum(r0 - DN_HALO, 0), DN_HALO)
        cur = a_ref[0, pl.ds(r0, DN_ROWS), cols].astype(F32)
        prev = a_ref[0, pl.ds(rp, DN_HALO), cols].astype(F32)
        prev = jnp.where(gi > 0, prev, 0.0)
        xw = jnp.concatenate([prev, cur], axis=0)
        conv = None
        for j in range(CONV_WIDTH):
            lo = DN_HALO - (CONV_WIDTH - 1) + j
            term = cw_ref[j:j + 1, cols] * xw[lo:lo + DN_ROWS, :]
            conv = term if conv is None else conv + term
        return _silu(conv)

    def elementwise(gi, handoff):
        r0 = pl.multiple_of(gi * DN_ROWS, DN_ROWS)
        def act_of(h):
            tiles = []
            for base in (0, DN_WIDTH, 2 * DN_WIDTH):
                col0 = base + h * HEAD_DIM
                if col0 < CONV_EARLY_COLS:
                    tiles.append(a_ref[0, pl.ds(r0, DN_ROWS), col0:col0 + HEAD_DIM].astype(F32))
                else:
                    tiles.append(conv_silu(gi, r0, col0))
            return tiles
        acts = []

        def head_norms(h):
            qh, kh, vh = act_of(h)
            qn_h = qh * (lax.rsqrt(_mxu((qh * qh).astype(BF16), ones_b) + EPS) * (HEAD_DIM ** -0.5))
            kn_h = kh * lax.rsqrt(_mxu((kh * kh).astype(BF16), ones_b) + EPS)
            acts.append((qn_h, kn_h, vh))

        head_norms(0)
        yield
        bg = bg_ref[0, pl.ds(r0, DN_ROWS), :]
        gc_all = sum(_mxu(cum_lower, piece) for piece in _split3(bg))
        gct_all = sum(_mxu(piece, cum_upper) for piece in _split3(bgt_ref[:, pl.ds(r0, DN_ROWS)]))
        gl_all = jnp.concatenate([jnp.broadcast_to(gc_all[(c + 1) * CHUNK - 1:(c + 1) * CHUNK, :], (CHUNK, LANES))
                                  for c in range(DN_GROUP)], axis=0)
        eg_all = jnp.exp(gl_all)
        for c in range(DN_GROUP):
            defer(eg_ref, (gi * DN_GROUP + c,), eg_all[c * CHUNK:c * CHUNK + 8, :])
        gc_m = jnp.where(g_lanes, gc_all, 0.0)
        gl_m = jnp.where(g_lanes, gl_all, 0.0)
        beta_rep = _mxu(bg.astype(BF16), sel_beta)
        egc_rep = _mxu(jnp.exp(gc_m).astype(BF16), sel_g)
        ekd_rep = _mxu(jnp.exp(gl_m - gc_m).astype(BF16), sel_g)

        for h in range(1, DN_HEADS):
            head_norms(h)
            yield

        qn, knb, kb, rhs = [], [], [], []
        for h in heads:
            qn_h, kn_h, vh = acts[h]
            kb_h = kn_h * beta_rep[:, hsl(h)]
            defer(kd_ref, (h, pl.ds(r0, DN_ROWS), slice(None)), (kn_h * ekd_rep[:, hsl(h)]).astype(BF16))
            qg_h = (qn_h * egc_rep[:, hsl(h)]).astype(BF16)
            for c in range(DN_GROUP):
                w0 = pl.multiple_of(2 * (r0 + c * CHUNK), 2 * CHUNK)
                defer(wq_ref, (h, pl.ds(w0 + CHUNK, CHUNK), slice(None)), qg_h[cs(c)])
            qn.append(qn_h)
            knb.append(kn_h.astype(BF16))
            kb.append(kb_h)
            rhs.append(jnp.concatenate([vh * beta_rep[:, hsl(h)], kb_h * egc_rep[:, hsl(h)]], axis=1).astype(BF16))
            yield

        zeros_k = jnp.zeros((CHUNK, HEAD_DIM), BF16)
        qk = []
        for cp, h in problems:
            c1, c2 = cs(2 * cp), cs(2 * cp + 1)
            lhs = jnp.concatenate([jnp.concatenate([qn[h][c1], qn[h][c2]], axis=1),
                                   jnp.concatenate([kb[h][c1], kb[h][c2]], axis=1)], axis=0).astype(BF16)
            keys = jnp.concatenate([jnp.concatenate([knb[h][c1], zeros_k], axis=1),
                                    jnp.concatenate([zeros_k, knb[h][c2]], axis=1)], axis=0)
            qk.append(lax.dot_general(lhs, keys, _NT, preferred_element_type=F32))
        decay = []
        for cp, h in problems:
            col = slice(DN_HEADS + h, DN_HEADS + h + 1)
            gc_col = jnp.where(second, gc_all[cs(2 * cp + 1), col], gc_all[cs(2 * cp), col])
            diff = gc_col - gct_all[DN_HEADS + h:DN_HEADS + h + 1, 2 * cp * CHUNK:(2 * cp + 2) * CHUNK]
            decay.append(jnp.where(causal, jnp.exp(jnp.where(causal, diff, 0.0)), 0.0))
        for i, (cp, h) in enumerate(problems):
            a_intra = jnp.where(causal, qk[i][:CHUNK] * decay[i], 0.0)
            defer(ai_ref, (h, gi * (DN_GROUP // 2) + cp), a_intra.astype(BF16))
        handoff["neg_lower"] = [-jnp.where(strict, qk[i][CHUNK:] * decay[i], 0.0) for i in range(len(problems))]
        handoff["rhs"] = rhs

    def solve(gi, handoff):
        r0 = pl.multiple_of(gi * DN_ROWS, DN_ROWS)
        ps = handoff["neg_lower"]
        rhs = handoff["rhs"]
        tms = [eye_f + p for p in ps]
        for _ in range(shift - 1):
            pbs = [p.astype(BF16) for p in ps]
            ps = [_mxu(pb, block_diag(pb)) for pb in pbs]
            tms = [tm + _mxu(tm.astype(BF16), block_diag(p.astype(BF16))) for tm, p in zip(tms, ps)]
            yield
        zeros_r = jnp.zeros((CHUNK, 2 * HEAD_DIM), BF16)
        uw = []
        for i, (cp, h) in enumerate(problems):
            both = jnp.concatenate([jnp.concatenate([rhs[h][cs(2 * cp)], zeros_r], axis=1),
                                    jnp.concatenate([zeros_r, rhs[h][cs(2 * cp + 1)]], axis=1)], axis=0)
            uw.append(_mxu(tms[i].astype(BF16), both))
        for i, (cp, h) in enumerate(problems):
            for half in range(2):
                c = 2 * cp + half
                part = uw[i][:, half * 2 * HEAD_DIM:(half + 1) * 2 * HEAD_DIM]
                defer(u_ref, (h, pl.ds(r0 + c * CHUNK, CHUNK), slice(None)), part[:, :HEAD_DIM])
                w0 = pl.multiple_of(2 * (r0 + c * CHUNK), 2 * CHUNK)
                defer(wq_ref, (h, pl.ds(w0, CHUNK), slice(None)), part[:, HEAD_DIM:].astype(BF16))

    def recur(g, c_in_group, state):
        c = g * DN_GROUP + c_in_group
        pair = g * (DN_GROUP // 2) + c_in_group // 2
        zeros_v = jnp.zeros((CHUNK, HEAD_DIM), BF16)
        pad = (lambda vn_h: jnp.concatenate([vn_h, zeros_v], axis=0)) if c_in_group % 2 == 0 else (
            lambda vn_h: jnp.concatenate([zeros_v, vn_h], axis=0))
        r0 = pl.multiple_of(c * CHUNK, CHUNK)
        rows = pl.ds(r0, CHUNK)
        wrows = pl.ds(pl.multiple_of(2 * r0, 2 * CHUNK), 2 * CHUNK)
        eg = eg_ref[c]
        ws = [_mxu(wq_ref[h, wrows, :], state[h].astype(BF16)) for h in heads]
        vn = [(u_ref[h, rows, :] - ws[h][:CHUNK]).astype(BF16) for h in heads]
        new_state = [state[h] * eg[0:1, DN_HEADS + h:DN_HEADS + h + 1]
                     + lax.dot_general(kd_ref[h, rows, :], vn[h], _TN, preferred_element_type=F32) for h in heads]
        o = [ws[h][CHUNK:] + _mxu(ai_ref[h, pair], pad(vn[h])) for h in heads]
        ms = [_mxu((x * x).astype(BF16), ones_b) * (1.0 / HEAD_DIM) for x in o]
        for h in heads:
            on = o[h] * lax.rsqrt(ms[h] + EPS) * nw_ref[...]
            z = a_ref[0, rows, 3 * DN_WIDTH + h * HEAD_DIM:3 * DN_WIDTH + (h + 1) * HEAD_DIM].astype(F32)
            defer(o_ref, (0, rows, hsl(h)), (on * _silu(z)).astype(o_ref.dtype))
        return new_state

    def pipeline_step(g_prep, g_recur, state):
        def stages():
            if g_prep is not None:
                handoff = {}
                yield from elementwise(g_prep, handoff)
                yield
                yield from solve(g_prep, handoff)

        work = stages()
        for c in range(DN_GROUP):
            if g_recur is not None:
                state = recur(g_recur, c, state)
            next(work, None)
        for _ in work:
            pass
        flush()
        return state

    zero = jnp.zeros((HEAD_DIM, HEAD_DIM), F32)
    state = pipeline_step(0, None, [zero] * DN_HEADS)
    state = list(lax.fori_loop(
        1, n_groups, lambda gi, st: tuple(pipeline_step(gi, gi - 1, list(st))), tuple(state)))
    pipeline_step(None, n_groups - 1, state)


def _deltanet(a3, bg3, bgt, conv_w2, dn_norm_row):
    b, t, wa = a3.shape
    nt = bgt.shape[0]
    return pl.pallas_call(
        _deltanet_kernel,
        grid=(b,),
        in_specs=[pl.BlockSpec((1, t, wa), lambda i: (i, 0, 0)),
                  pl.BlockSpec((1, t, LANES), lambda i: (i, 0, 0)),
                  pl.BlockSpec((nt, t), lambda i: (0, i)),
                  pl.BlockSpec((CONV_WIDTH, 3 * DN_WIDTH), lambda i: (0, 0)),
                  pl.BlockSpec((1, HEAD_DIM), lambda i: (0, 0))],
        out_specs=pl.BlockSpec((1, t, DN_WIDTH), lambda i: (i, 0, 0)),
        out_shape=jax.ShapeDtypeStruct((b, t, DN_WIDTH), BF16),
        scratch_shapes=[pltpu.VMEM((DN_HEADS, t, HEAD_DIM), F32),
                        pltpu.VMEM((DN_HEADS, 2 * t, HEAD_DIM), BF16),
                        pltpu.VMEM((DN_HEADS, t, HEAD_DIM), BF16),
                        pltpu.VMEM((DN_HEADS, t // LANES, CHUNK, LANES), BF16),
                        pltpu.VMEM((t // CHUNK, 8, LANES), F32)],
        compiler_params=pltpu.CompilerParams(dimension_semantics=("parallel",),
                                             vmem_limit_bytes=DN_VMEM_LIMIT_BYTES),
        name="deltanet",
    )(a3, bg3, bgt, conv_w2, dn_norm_row)


SWA_PREP_ROWS = 1024
SWA_UNITS = 4


def _swa_kernel(s_ref, plain_ref, swapped_ref, o_ref, lse_ref, qs_ref, ks_ref, bias_ref, *dilated_scratch, dilation):
    t_len = s_ref.shape[1]
    d = dilation
    seq = t_len // d
    nblk = seq // SWA_BLOCK
    heads = range(SWA_HEADS)
    hsl = lambda h: slice(h * HEAD_DIM, (h + 1) * HEAD_DIM)
    if d > 1:
        (ostage_ref,) = dilated_scratch
    mean_b = jnp.full((HEAD_DIM, HEAD_DIM), 1.0 / HEAD_DIM, BF16)
    src_lane = lax.broadcasted_iota(jnp.int32, (HEAD_DIM, HEAD_DIM), 0)
    dst_lane = lax.broadcasted_iota(jnp.int32, (HEAD_DIM, HEAD_DIM), 1)
    swap_b = (((dst_lane < ROPE_HALF) & (src_lane == dst_lane + ROPE_HALF))
              | ((dst_lane >= ROPE_HALF) & (dst_lane < ROPE_DIM) & (src_lane == dst_lane - ROPE_HALF))
              ).astype(BF16)

    def prep(n, carry):
        r0 = pl.multiple_of(n * SWA_PREP_ROWS, SWA_PREP_ROWS)
        rows = pl.ds(r0, SWA_PREP_ROWS)
        for which, dst in ((0, qs_ref), (1, ks_ref)):
            plain = plain_ref[which, rows, :]
            swapped_tab = swapped_ref[which, rows, :]
            xb = [s_ref[0, rows, which * SWA_WIDTH + h * HEAD_DIM:which * SWA_WIDTH + (h + 1) * HEAD_DIM]
                  for h in heads]
            ms = [_mxu(x * x, mean_b) for x in xb]
            sw = [_mxu(x, swap_b) for x in xb]
            ys = [(x.astype(F32) * plain + s * swapped_tab) * lax.rsqrt(m + EPS) for x, s, m in zip(xb, sw, ms)]
            for h in heads:
                dst[rows, hsl(h)] = ys[h].astype(BF16)
        return carry

    lax.fori_loop(0, t_len // SWA_PREP_ROWS, prep, 0)

    qi = lax.broadcasted_iota(jnp.int32, (SWA_BLOCK, 2 * SWA_BLOCK), 0)
    kj = lax.broadcasted_iota(jnp.int32, (SWA_BLOCK, 2 * SWA_BLOCK), 1)
    bias_ref[0] = jnp.where(kj <= qi, 0.0, NEG_INF).astype(F32)
    bias_ref[1] = jnp.where((kj >= qi) & (kj <= qi + SWA_BLOCK), 0.0, NEG_INF).astype(F32)
    lane = lax.broadcasted_iota(jnp.int32, (SWA_BLOCK, LANES), 1)

    def attend(units):
        jobs = [(u, h) for u in range(len(units)) for h in heads]
        qrows = [pl.ds(q0, SWA_BLOCK) for q0, _, _, _, _ in units]
        krows = [pl.ds(k0, width) for _, k0, width, _, _ in units]
        s = [lax.dot_general(qs_ref[qrows[u], hsl(h)], ks_ref[krows[u], hsl(h)], _NT,
                             preferred_element_type=F32) + units[u][3] for u, h in jobs]
        m = [jnp.max(x, axis=-1, keepdims=True) for x in s]
        p = [jnp.exp(x - mx) for x, mx in zip(s, m)]
        l = [jnp.sum(x, axis=-1, keepdims=True) for x in p]
        v = [s_ref[0, krows[u], 2 * SWA_WIDTH + h * HEAD_DIM:2 * SWA_WIDTH + (h + 1) * HEAD_DIM] for u, h in jobs]
        o = [_mxu(x.astype(BF16), vh) for x, vh in zip(p, v)]
        for u in range(len(units)):
            lse_tile = jnp.zeros((SWA_BLOCK, LANES), F32)
            for h in heads:
                j = u * SWA_HEADS + h
                lse_tile = jnp.where(lane == h, m[j], jnp.where(lane == SWA_HEADS + h, l[j], lse_tile))
            t0 = units[u][4]
            if d == 1:
                for h in heads:
                    o_ref[0, qrows[u], hsl(h)] = o[u * SWA_HEADS + h].astype(o_ref.dtype)
                lse_ref[0, qrows[u], :] = lse_tile
            else:
                trows = pl.ds(t0, SWA_BLOCK, stride=d)
                for h in heads:
                    ostage_ref[h, trows, :] = o[u * SWA_HEADS + h]
                lse_ref[0, trows, :] = lse_tile

    if nblk == 1:
        causal_bias = bias_ref[0][:, :SWA_BLOCK]

        def body(i, carry):
            units = []
            for u in range(SWA_UNITS):
                r = i * SWA_UNITS + u
                q0 = pl.multiple_of(r * seq, SWA_BLOCK)
                units.append((q0, q0, SWA_BLOCK, causal_bias, r))
            attend(units)
            return carry

        lax.fori_loop(0, d // SWA_UNITS, body, 0)
    else:
        pairs = nblk // SWA_UNITS

        def body(i, carry):
            r = i // pairs
            pp = i % pairs
            units = []
            for u in range(SWA_UNITS):
                n = pp * SWA_UNITS + u
                q0 = pl.multiple_of(r * seq + n * SWA_BLOCK, SWA_BLOCK)
                if u == 0:
                    first = jnp.asarray(pp == 0).astype(jnp.int32)
                    k0 = pl.multiple_of(q0 - (1 - first) * SWA_BLOCK, SWA_BLOCK)
                    bias = bias_ref[1 - first]
                else:
                    k0 = pl.multiple_of(q0 - SWA_BLOCK, SWA_BLOCK)
                    bias = bias_ref[1]
                units.append((q0, k0, 2 * SWA_BLOCK, bias, r + d * n * SWA_BLOCK))
            attend(units)
            return carry

        lax.fori_loop(0, d * pairs, body, 0)

    if d > 1:
        def emit(n, carry):
            rows = pl.ds(pl.multiple_of(n * SWA_PREP_ROWS, SWA_PREP_ROWS), SWA_PREP_ROWS)
            for h in heads:
                o_ref[0, rows, hsl(h)] = ostage_ref[h, rows, :].astype(o_ref.dtype)
            return carry
        lax.fori_loop(0, t_len // SWA_PREP_ROWS, emit, 0)


def _swa_group(s3, plain, swapped, group, dilation):
    b, t, ws = s3.shape
    assert (t // dilation) % SWA_BLOCK == 0 and t % SWA_PREP_ROWS == 0
    nblk = t // dilation // SWA_BLOCK
    assert (dilation % SWA_UNITS == 0) if nblk == 1 else (nblk % SWA_UNITS == 0)
    scratch = [pltpu.VMEM((t, SWA_WIDTH), BF16), pltpu.VMEM((t, SWA_WIDTH), BF16),
               pltpu.VMEM((2, SWA_BLOCK, 2 * SWA_BLOCK), F32)]
    if dilation > 1:
        scratch += [pltpu.VMEM((SWA_HEADS, t, LANES), F32)]
    tab_spec = pl.BlockSpec((None, 2, t, LANES), lambda i: (group, 0, 0, 0))
    return pl.pallas_call(
        functools.partial(_swa_kernel, dilation=dilation),
        grid=(b,),
        in_specs=[pl.BlockSpec((1, t, ws), lambda i: (i, 0, 0)), tab_spec, tab_spec],
        out_specs=[pl.BlockSpec((1, t, SWA_WIDTH), lambda i: (i, 0, 0)),
                   pl.BlockSpec((1, t, LANES), lambda i: (i, 0, 0))],
        out_shape=[jax.ShapeDtypeStruct((b, t, SWA_WIDTH), BF16),
                   jax.ShapeDtypeStruct((b, t, LANES), F32)],
        scratch_shapes=scratch,
        compiler_params=_params(1),
        name=f"swa_d{dilation}",
    )(s3, plain, swapped)


MERGE_ROWS = 512


def _merge_out_kernel(x_ref, h_ref, odn_ref, o0_ref, o1_ref, o2_ref, l0_ref, l1_ref, l2_ref,
                      wz_ref, wgdn_ref, wgswa_ref, wdn32_ref, wswa32_ref, wout32_ref, out_ref,
                      wzg_scr, wdn_ref, wswa_ref, wout_ref):
    _cast_weights_once([wz_ref, wgdn_ref, wgswa_ref], wzg_scr)
    _cast_weights_once([wdn32_ref], wdn_ref)
    _cast_weights_once([wswa32_ref], wswa_ref)
    _cast_weights_once([wout32_ref], wout_ref)
    zg_all = lax.dot_general(h_ref[...], wzg_scr[...], _NT, preferred_element_type=F32)
    outs = []
    for j in range(ROW_TILE // MERGE_ROWS):
        rows = slice(j * MERGE_ROWS, (j + 1) * MERGE_ROWS)
        zg = zg_all[rows, :]
        st0 = l0_ref[rows, :]
        st1 = l1_ref[rows, :]
        st2 = l2_ref[rows, :]
        m = jnp.maximum(jnp.maximum(st0, st1), st2)
        e0 = jnp.exp(st0 - m)
        e1 = jnp.exp(st1 - m)
        e2 = jnp.exp(st2 - m)
        sums = [pltpu.roll(st, LANES - SWA_HEADS, 1) for st in (st0, st1, st2)]
        inv = 1.0 / (e0 * sums[0] + e1 * sums[1] + e2 * sums[2])
        a0, a1, a2 = e0 * inv, e1 * inv, e2 * inv
        parts = []
        for h in range(SWA_HEADS):
            hs = slice(h * HEAD_DIM, (h + 1) * HEAD_DIM)
            col = slice(h, h + 1)
            oh = (a0[:, col] * o0_ref[rows, hs].astype(F32) + a1[:, col] * o1_ref[rows, hs].astype(F32)
                  + a2[:, col] * o2_ref[rows, hs].astype(F32))
            parts.append((oh * _silu(zg[:, hs])).astype(BF16))
        o_swa = jnp.concatenate(parts, axis=1)
        y_swa = _mxu(o_swa, wswa_ref[...])
        y_dn = _mxu(odn_ref[rows, :], wdn_ref[...])
        g_dn = zg[:, SWA_WIDTH:SWA_WIDTH + D_MODEL]
        g_swa = zg[:, SWA_WIDTH + D_MODEL:SWA_WIDTH + 2 * D_MODEL]
        merged = _sigmoid(g_dn) * y_dn + _sigmoid(g_swa) * y_swa
        outs.append((rows, x_ref[rows, :] + _mxu(merged.astype(BF16), wout_ref[...])))
    for rows, value in outs:
        out_ref[rows, :] = value


def _merge_out(x2, h, odn, o_list, lse_list, wt, zg_row_blocks, w_dn, w_swa, w_out):
    n = x2.shape[0]
    row = lambda w: pl.BlockSpec((ROW_TILE, w), lambda i: (i, 0))
    full = lambda a: pl.BlockSpec(a.shape, lambda i: (0, 0), pipeline_mode=pl.Buffered(1))
    nzg = sum(rows for _, rows in zg_row_blocks)
    return pl.pallas_call(
        _merge_out_kernel,
        grid=(n // ROW_TILE,),
        in_specs=[row(D_MODEL), row(D_MODEL), row(DN_WIDTH), row(SWA_WIDTH), row(SWA_WIDTH), row(SWA_WIDTH),
                  row(LANES), row(LANES), row(LANES)]
                 + _weight_specs(zg_row_blocks, D_MODEL)
                 + [full(w_dn), full(w_swa), full(w_out)],
        out_specs=row(D_MODEL),
        out_shape=jax.ShapeDtypeStruct((n, D_MODEL), F32),
        scratch_shapes=[pltpu.VMEM((nzg, D_MODEL), BF16)] + [pltpu.VMEM(w.shape, BF16) for w in (w_dn, w_swa, w_out)],
        compiler_params=_params_sequential(),
        name="merge_out",
    )(x2, h, odn, *o_list, *lse_list, wt, wt, wt, w_dn, w_swa, w_out)


def _rope_tables(t_len, q_norm_w, k_norm_w):
    j = np.arange(t_len)
    pos = np.stack([(j % (t_len // d)) * d + j // (t_len // d) for _, d in SWA_GROUPS]).astype(np.float64)
    inv_freq = ROPE_THETA ** (-np.arange(0, ROPE_DIM, 2, dtype=np.float64) / ROPE_DIM)
    ang = pos[:, :, None] * inv_freq[None, None, :]
    cos, sin = np.cos(ang), np.sin(ang)
    tail = (N_GROUPS, t_len, HEAD_DIM - ROPE_DIM)
    cos_t = jnp.asarray(np.concatenate([cos, cos, np.ones(tail)], axis=-1), F32)
    sin_t = jnp.asarray(np.concatenate([-sin, sin, np.zeros(tail)], axis=-1), F32)
    w = jnp.stack([q_norm_w.astype(F32) * (HEAD_DIM ** -0.5), k_norm_w.astype(F32)], axis=1)
    w_swapped = jnp.concatenate([w[..., ROPE_HALF:ROPE_DIM], w[..., :ROPE_HALF], w[..., ROPE_DIM:]], axis=-1)
    return w[:, :, None, :] * cos_t[:, None], w_swapped[:, :, None, :] * sin_t[:, None]


def kernel(x, norm_w, w_in, conv_w, dn_a_log, dn_dt_bias, dn_norm_w, q_norm_w, k_norm_w,
           w_branch_dn, w_branch_swa, w_out):
    b, t, d = x.shape
    n = b * t
    layer = 0
    wt = jnp.swapaxes(w_in[layer], 0, 1)
    c_z = 4 * DN_WIDTH
    c_q = c_z + 2 * DN_HEADS
    c_k = c_q + N_GROUPS * SWA_WIDTH
    c_v = c_k + N_GROUPS * SWA_WIDTH
    c_sz = c_v + N_GROUPS * SWA_WIDTH
    c_g = c_sz + SWA_WIDTH
    grp_blocks = lambda g: [(c0 + g * SWA_WIDTH, SWA_WIDTH) for c0 in (c_q, c_k, c_v)]
    zg_blocks = [(c_sz, SWA_WIDTH), (c_g, D_MODEL), (c_g + D_MODEL, D_MODEL)]
    pad_heads = lambda v: jnp.pad(v.astype(F32), (DN_HEADS, LANES - 2 * DN_HEADS))[None, :]
    alog_row = pad_heads(dn_a_log[layer])
    dt_row = pad_heads(dn_dt_bias[layer])

    x2 = x.reshape(n, d)
    h, a, bg, bgt = _norm_proj(x2, norm_w[layer][None, :], wt, c_z, conv_w[layer][:, 0, :], alog_row, dt_row, t)

    o_dn = _deltanet(a.reshape(b, t, 4 * DN_WIDTH), bg.reshape(b, t, LANES), bgt,
                     conv_w[layer][:, 0, :], dn_norm_w[layer][None, :])

    plain, swapped = _rope_tables(t, q_norm_w[layer], k_norm_w[layer])
    o_list, lse_list = [], []
    for g, (window, dilation) in enumerate(SWA_GROUPS):
        assert window // dilation == SWA_BLOCK
        s_g = _matmul(h, wt, grp_blocks(g), f"proj_swa{g}", dilation, t).reshape(b, t, 3 * SWA_WIDTH)
        o_g, lse_g = _swa_group(s_g, plain, swapped, g, dilation)
        o_list.append(o_g.reshape(n, SWA_WIDTH))
        lse_list.append(lse_g.reshape(n, LANES))

    out = _merge_out(x2, h, o_dn.reshape(n, DN_WIDTH), o_list, lse_list, wt, zg_blocks,
                     w_branch_dn[layer], w_branch_swa[layer], w_out[layer])
    return out.reshape(b, t, d)
```

```python
import functools
import math

import jax
import jax.numpy as jnp
import numpy as np
from jax import lax
from jax.experimental import pallas as pl
from jax.experimental.pallas import tpu as pltpu

D_MODEL = 1024
HEAD_DIM = 128
DN_HEADS = 4
DN_WIDTH = DN_HEADS * HEAD_DIM
CONV_WIDTH = 4
CHUNK = 64
SWA_GROUPS = ((128, 1), (512, 4), (2048, 16))
N_GROUPS = 3
SWA_HEADS = 4
SWA_WIDTH = SWA_HEADS * HEAD_DIM
SWA_BLOCK = 128
ROPE_DIM = HEAD_DIM // 4
ROPE_HALF = ROPE_DIM // 2
ROPE_THETA = 500000.0
EPS = 1e-6
NEG_INF = -1e30

LANES = 128
VMEM_LIMIT_BYTES = 48 * 1024 * 1024
DN_VMEM_LIMIT_BYTES = 56 * 1024 * 1024
ROW_TILE = 512
NORM_ROW_TILE = 512
PROJ_ROW_TILE = 1024
BG_ROWS = 16

F32 = jnp.float32
BF16 = jnp.bfloat16
_NT = (((1,), (1,)), ((), ()))
_TN = (((0,), (0,)), ((), ()))


def _sigmoid(x):
    return 0.5 * jnp.tanh(0.5 * x) + 0.5


def _silu(x):
    return x * _sigmoid(x)


def _softplus(x):
    return jnp.maximum(x, 0.0) + jnp.log(1.0 + jnp.exp(-jnp.abs(x)))


def _params(n_axes):
    return pltpu.CompilerParams(dimension_semantics=("parallel",) * n_axes,
                                vmem_limit_bytes=VMEM_LIMIT_BYTES)


def _params_sequential():
    return pltpu.CompilerParams(dimension_semantics=("arbitrary",), vmem_limit_bytes=VMEM_LIMIT_BYTES)


def _mxu(a, b):
    return jnp.dot(a, b, preferred_element_type=F32)


def _split3(x):
    hi = x.astype(BF16)
    r1 = x - hi.astype(F32)
    mid = r1.astype(BF16)
    lo = (r1 - mid.astype(F32)).astype(BF16)
    return hi, mid, lo


def _beta_and_log_decay(ab, a_log, dt_bias, head_index):
    beta = _sigmoid(ab)
    g = -jnp.exp(a_log) * _softplus(ab + dt_bias)
    return jnp.where(head_index < DN_HEADS, beta, g)


def _weight_specs(row_blocks, k):
    return [pl.BlockSpec((pl.Element(rows), pl.Element(k)), lambda i, start=start: (start, 0),
                         pipeline_mode=pl.Buffered(1))
            for start, rows in row_blocks]


def _cast_weights_once(w_refs, w_scr):
    @pl.when(pl.program_id(0) == 0)
    def _():
        blocks = [w_ref[...] for w_ref in w_refs]
        pad = w_scr.shape[0] - sum(blk.shape[0] for blk in blocks)
        if pad:
            blocks.append(jnp.zeros((pad, w_scr.shape[1]), F32))
        w_scr[...] = (jnp.concatenate(blocks, axis=0) if len(blocks) > 1 else blocks[0]).astype(BF16)


CONV_TILE = 256
CONV_HALO = 8
CONV_ROWS = 64
CONV_EARLY_COLS = 2 * DN_WIDTH


def _norm_proj_kernel(x_ref, nw_ref, wa_ref, wab_ref, cw_ref, alog_ref, dt_ref, alogt_ref, dtt_ref,
                      h_ref, a_ref, bg_ref, bgt_ref, wa_scr, wab_scr, tail_ref, res_a, res_b, *, tiles_per_seq):
    res_scr = (res_a, res_b)
    tile_rows = x_ref.shape[0]
    _cast_weights_once([wa_ref], wa_scr)
    _cast_weights_once([wab_ref], wab_scr)
    x = x_ref[...]
    h = (x * lax.rsqrt(jnp.mean(x * x, axis=-1, keepdims=True) + EPS)) * nw_ref[...]
    hb = h.astype(BF16)
    h_ref[...] = hb
    n_conv = CONV_EARLY_COLS
    seq_start = pl.program_id(0) % tiles_per_seq == 0
    tail = jnp.where(seq_start, 0.0, tail_ref[...])
    stores = []
    tiles = list(range(0, n_conv, CONV_TILE))
    base = pl.multiple_of((pl.program_id(0) >> 20) * CONV_HALO, CONV_HALO)

    def project(k):
        c0 = tiles[k]
        buf = res_scr[k % 2]
        buf[pl.ds(base, CONV_HALO), :] = tail[:, c0:c0 + CONV_TILE]
        buf[pl.ds(base + CONV_HALO, tile_rows), :] = lax.dot_general(hb, wa_scr[c0:c0 + CONV_TILE, :], _NT,
                                                                    preferred_element_type=F32)

    if tiles:
        project(0)
    for k, c0 in enumerate(tiles):
        cols = slice(c0, c0 + CONV_TILE)
        if k + 1 < len(tiles):
            project(k + 1)
        buf = res_scr[k % 2]
        acts = []
        for rb in range(0, tile_rows, CONV_ROWS):
            blk = buf[pl.ds(base + rb, CONV_HALO + CONV_ROWS), :]
            conv = None
            for j in range(CONV_WIDTH):
                lo = CONV_HALO - (CONV_WIDTH - 1) + j
                term = cw_ref[j:j + 1, cols] * blk[lo:lo + CONV_ROWS, :]
                conv = term if conv is None else conv + term
            acts.append(_silu(conv).astype(BF16))
        stores.append((a_ref, (slice(None), cols), jnp.concatenate(acts, axis=0)))
        stores.append((tail_ref, (slice(None), cols), buf[pl.ds(base + tile_rows, CONV_HALO), :]))
    z = lax.dot_general(hb, wa_scr[n_conv:, :], _NT, preferred_element_type=F32)
    stores.append((a_ref, (slice(None), slice(n_conv, None)), z.astype(BF16)))
    for ref, idx, value in stores:
        ref[idx] = value
    ab = lax.dot_general(hb, wab_scr[...], _NT, preferred_element_type=F32)
    bg_ref[...] = _beta_and_log_decay(ab, alog_ref[...], dt_ref[...],
                                      lax.broadcasted_iota(jnp.int32, ab.shape, 1))
    abt = lax.dot_general(wab_scr[0:BG_ROWS, :], hb, _NT, preferred_element_type=F32)
    bgt_ref[...] = _beta_and_log_decay(abt, alogt_ref[...], dtt_ref[...],
                                       lax.broadcasted_iota(jnp.int32, abt.shape, 0))


def _norm_proj(x2, norm_w, wt, na, conv_w2, alog_row, dt_row, seq_len):
    n = x2.shape[0]
    tile = NORM_ROW_TILE
    assert seq_len % tile == 0 and tile % CONV_ROWS == 0 and CONV_EARLY_COLS % CONV_TILE == 0
    row = lambda i: (i, 0)
    fixed = lambda i: (0, 0)
    pad_col = lambda v: v[0, :BG_ROWS][:, None]
    return pl.pallas_call(
        functools.partial(_norm_proj_kernel, tiles_per_seq=seq_len // tile),
        grid=(n // tile,),
        in_specs=[pl.BlockSpec((tile, D_MODEL), row),
                  pl.BlockSpec((1, D_MODEL), fixed)]
                 + _weight_specs([(0, na), (na, 2 * DN_HEADS)], D_MODEL)
                 + [pl.BlockSpec((CONV_WIDTH, 3 * DN_WIDTH), fixed),
                    pl.BlockSpec((1, LANES), fixed),
                    pl.BlockSpec((1, LANES), fixed),
                    pl.BlockSpec((BG_ROWS, 1), fixed),
                    pl.BlockSpec((BG_ROWS, 1), fixed)],
        out_specs=[pl.BlockSpec((tile, D_MODEL), row),
                   pl.BlockSpec((tile, na), row),
                   pl.BlockSpec((tile, LANES), row),
                   pl.BlockSpec((BG_ROWS, tile), lambda i: (0, i))],
        out_shape=[jax.ShapeDtypeStruct((n, D_MODEL), BF16),
                   jax.ShapeDtypeStruct((n, na), BF16),
                   jax.ShapeDtypeStruct((n, LANES), F32),
                   jax.ShapeDtypeStruct((BG_ROWS, n), F32)],
        scratch_shapes=[pltpu.VMEM((na, D_MODEL), BF16), pltpu.VMEM((LANES, D_MODEL), BF16),
                        pltpu.VMEM((CONV_HALO, max(CONV_EARLY_COLS, LANES)), F32),
                        pltpu.VMEM((CONV_HALO + tile, CONV_TILE), F32),
                        pltpu.VMEM((CONV_HALO + tile, CONV_TILE), F32)],
        compiler_params=_params_sequential(),
        name="norm_proj",
    )(x2, norm_w, wt, wt, conv_w2, alog_row, dt_row, pad_col(alog_row), pad_col(dt_row))


REGROUP_STRIDE = 4


def _matmul_kernel(h_ref, *refs, dilation):
    n_scratch = 1 if dilation == 1 else 3
    w_refs, o_ref, w_scr = refs[:-n_scratch - 1], refs[-n_scratch - 1], refs[-n_scratch]
    _cast_weights_once(w_refs, w_scr)
    if dilation == 1:
        o_ref[...] = lax.dot_general(h_ref[...], w_scr[...], _NT, preferred_element_type=F32).astype(o_ref.dtype)
        return
    stage_ref, stage2_ref = refs[-2], refs[-1]
    s1 = min(dilation, REGROUP_STRIDE)
    s2 = dilation // s1
    rows1 = h_ref.shape[0] // s1
    per = h_ref.shape[0] // dilation
    col = 0
    for w_ref in w_refs:
        width = w_ref.shape[0]
        res = lax.dot_general(h_ref[...], w_scr[col:col + width, :], _NT, preferred_element_type=F32)
        slabs = range(col // LANES, (col + width) // LANES)
        for c in slabs:
            stage_ref[c] = res[:, (c * LANES - col):(c * LANES - col) + LANES]
        for c in slabs:
            for a in range(s1):
                first = stage_ref[c, pl.ds(a, rows1, stride=s1), :]
                if s2 == 1:
                    o_ref[0, a, :, c * LANES:(c + 1) * LANES] = first.astype(o_ref.dtype)
                else:
                    stage2_ref[c, a * rows1:(a + 1) * rows1, :] = first
        if s2 > 1:
            for c in slabs:
                for a in range(s1):
                    for b in range(s2):
                        o_ref[0, b * s1 + a, :, c * LANES:(c + 1) * LANES] = (
                            stage2_ref[c, pl.ds(a * rows1 + b, per, stride=s2), :].astype(o_ref.dtype))
        col += width


def _matmul(h, wt, row_blocks, name, dilation=1, seq_len=None):
    n, k = h.shape
    nc = sum(rows for _, rows in row_blocks)
    scratch = [pltpu.VMEM((nc, k), BF16)]
    tile = PROJ_ROW_TILE
    if dilation == 1:
        out_spec = pl.BlockSpec((tile, nc), lambda i: (i, 0))
        out_shape = jax.ShapeDtypeStruct((n, nc), BF16)
    else:
        tiles = seq_len // tile
        per = tile // dilation
        assert seq_len % tile == 0 and tile % dilation == 0 and per % 16 == 0 and nc % LANES == 0
        out_spec = pl.BlockSpec((1, dilation, per, nc), lambda i: (i // tiles, 0, i % tiles, 0))
        out_shape = jax.ShapeDtypeStruct((n // seq_len, dilation, seq_len // dilation, nc), BF16)
        assert dilation % min(dilation, REGROUP_STRIDE) == 0
        scratch += [pltpu.VMEM((nc // LANES, tile, LANES), F32)] * 2
    return pl.pallas_call(
        functools.partial(_matmul_kernel, dilation=dilation),
        grid=(n // tile,),
        in_specs=[pl.BlockSpec((tile, k), lambda i: (i, 0))] + _weight_specs(row_blocks, k),
        out_specs=out_spec,
        out_shape=out_shape,
        scratch_shapes=scratch,
        compiler_params=_params_sequential(),
        name=name,
    )(h, *([wt] * len(row_blocks)))


DN_GROUP = 8
DN_ROWS = DN_GROUP * CHUNK
DN_HALO = 16


def _deltanet_kernel(a_ref, bg_ref, bgt_ref, cw_ref, nw_ref, o_ref,
                     u_ref, wq_ref, kd_ref, ai_ref, eg_ref):
    t_len = a_ref.shape[1]
    n_groups = t_len // DN_ROWS
    assert 2 * CHUNK == LANES and DN_GROUP % 2 == 0
    ii = lax.broadcasted_iota(jnp.int32, (CHUNK, LANES), 0)
    lane_pair = lax.broadcasted_iota(jnp.int32, (CHUNK, LANES), 1)
    jj = lane_pair & (CHUNK - 1)
    second = lane_pair >= CHUNK
    causal = ii >= jj
    strict = ii > jj
    eye_f = (ii == jj).astype(F32)
    second_rows = lax.broadcasted_iota(jnp.int32, (LANES, LANES), 0) >= CHUNK
    second_cols = lax.broadcasted_iota(jnp.int32, (LANES, LANES), 1) >= CHUNK
    same_half = second_rows == second_cols

    def block_diag(pair):
        return jnp.where(same_half, jnp.concatenate([pair, pair], axis=0), 0)

    bi = lax.broadcasted_iota(jnp.int32, (DN_ROWS, DN_ROWS), 0)
    bj = lax.broadcasted_iota(jnp.int32, (DN_ROWS, DN_ROWS), 1)
    shift = int(math.log2(CHUNK))
    same_chunk = jnp.right_shift(bi, shift) == jnp.right_shift(bj, shift)
    cum_lower = (same_chunk & (bi >= bj)).astype(BF16)
    cum_upper = (same_chunk & (bi <= bj)).astype(BF16)
    problems = [(cp, h) for cp in range(DN_GROUP // 2) for h in range(DN_HEADS)]
    heads = range(DN_HEADS)
    hsl = lambda h: slice(h * HEAD_DIM, (h + 1) * HEAD_DIM)
    cs = lambda c: slice(c * CHUNK, (c + 1) * CHUNK)
    ones_b = jnp.ones((HEAD_DIM, HEAD_DIM), BF16)
    sel_src = lax.broadcasted_iota(jnp.int32, (LANES, DN_WIDTH), 0)
    sel_head = jnp.right_shift(lax.broadcasted_iota(jnp.int32, (LANES, DN_WIDTH), 1), int(math.log2(HEAD_DIM)))
    sel_beta = (sel_src == sel_head).astype(BF16)
    sel_g = (sel_src == sel_head + DN_HEADS).astype(BF16)
    lane = lax.broadcasted_iota(jnp.int32, (DN_ROWS, LANES), 1)
    g_lanes = (lane >= DN_HEADS) & (lane < 2 * DN_HEADS)

    pending = []

    def defer(ref, idx, value):
        pending.append((ref, idx, value))

    def flush():
        for ref, idx, value in pending:
            ref[idx] = value
        pending.clear()

    def conv_silu(gi, r0, col0):
        cols = slice(col0, col0 + HEAD_DIM)
        rp = pl.multiple_of(jnp.maximum(r0 - DN_HALO, 0), DN_HALO)
        cur = a_ref[0, pl.ds(r0, DN_ROWS), cols].astype(F32)
        prev = a_ref[0, pl.ds(rp, DN_HALO), cols].astype(F32)
        prev = jnp.where(gi > 0, prev, 0.0)
        xw = jnp.concatenate([prev, cur], axis=0)
        conv = None
        for j in range(CONV_WIDTH):
            lo = DN_HALO - (CONV_WIDTH - 1) + j
            term = cw_ref[j:j + 1, cols] * xw[lo:lo + DN_ROWS, :]
            conv = term if conv is None else conv + term
        return _silu(conv)

    def elementwise(gi, handoff):
        r0 = pl.multiple_of(gi * DN_ROWS, DN_ROWS)
        def act_of(h):
            tiles = []
            for base in (0, DN_WIDTH, 2 * DN_WIDTH):
                col0 = base + h * HEAD_DIM
                if col0 < CONV_EARLY_COLS:
                    tiles.append(a_ref[0, pl.ds(r0, DN_ROWS), col0:col0 + HEAD_DIM].astype(F32))
                else:
                    tiles.append(conv_silu(gi, r0, col0))
            return tiles
        acts = []

        def head_norms(h):
            qh, kh, vh = act_of(h)
            qn_h = qh * (lax.rsqrt(_mxu((qh * qh).astype(BF16), ones_b) + EPS) * (HEAD_DIM ** -0.5))
            kn_h = kh * lax.rsqrt(_mxu((kh * kh).astype(BF16), ones_b) + EPS)
            acts.append((qn_h, kn_h, vh))

        head_norms(0)
        yield
        bg = bg_ref[0, pl.ds(r0, DN_ROWS), :]
        gc_all = sum(_mxu(cum_lower, piece) for piece in _split3(bg))
        gct_all = sum(_mxu(piece, cum_upper) for piece in _split3(bgt_ref[:, pl.ds(r0, DN_ROWS)]))
        gl_all = jnp.concatenate([jnp.broadcast_to(gc_all[(c + 1) * CHUNK - 1:(c + 1) * CHUNK, :], (CHUNK, LANES))
                                  for c in range(DN_GROUP)], axis=0)
        eg_all = jnp.exp(gl_all)
        for c in range(DN_GROUP):
            defer(eg_ref, (gi * DN_GROUP + c,), eg_all[c * CHUNK:c * CHUNK + 8, :])
        gc_m = jnp.where(g_lanes, gc_all, 0.0)
        gl_m = jnp.where(g_lanes, gl_all, 0.0)
        beta_rep = _mxu(bg.astype(BF16), sel_beta)
        egc_rep = _mxu(jnp.exp(gc_m).astype(BF16), sel_g)
        ekd_rep = _mxu(jnp.exp(gl_m - gc_m).astype(BF16), sel_g)

        for h in range(1, DN_HEADS):
            head_norms(h)
            yield

        qn, knb, kb, rhs = [], [], [], []
        for h in heads:
            qn_h, kn_h, vh = acts[h]
            kb_h = kn_h * beta_rep[:, hsl(h)]
            defer(kd_ref, (h, pl.ds(r0, DN_ROWS), slice(None)), (kn_h * ekd_rep[:, hsl(h)]).astype(BF16))
            qg_h = (qn_h * egc_rep[:, hsl(h)]).astype(BF16)
            for c in range(DN_GROUP):
                w0 = pl.multiple_of(2 * (r0 + c * CHUNK), 2 * CHUNK)
                defer(wq_ref, (h, pl.ds(w0 + CHUNK, CHUNK), slice(None)), qg_h[cs(c)])
            qn.append(qn_h)
            knb.append(kn_h.astype(BF16))
            kb.append(kb_h)
            rhs.append(jnp.concatenate([vh * beta_rep[:, hsl(h)], kb_h * egc_rep[:, hsl(h)]], axis=1).astype(BF16))
            yield

        zeros_k = jnp.zeros((CHUNK, HEAD_DIM), BF16)
        qk = []
        for cp, h in problems:
            c1, c2 = cs(2 * cp), cs(2 * cp + 1)
            lhs = jnp.concatenate([jnp.concatenate([qn[h][c1], qn[h][c2]], axis=1),
                                   jnp.concatenate([kb[h][c1], kb[h][c2]], axis=1)], axis=0).astype(BF16)
            keys = jnp.concatenate([jnp.concatenate([knb[h][c1], zeros_k], axis=1),
                                    jnp.concatenate([zeros_k, knb[h][c2]], axis=1)], axis=0)
            qk.append(lax.dot_general(lhs, keys, _NT, preferred_element_type=F32))
        decay = []
        for cp, h in problems:
            col = slice(DN_HEADS + h, DN_HEADS + h + 1)
            gc_col = jnp.where(second, gc_all[cs(2 * cp + 1), col], gc_all[cs(2 * cp), col])
            diff = gc_col - gct_all[DN_HEADS + h:DN_HEADS + h + 1, 2 * cp * CHUNK:(2 * cp + 2) * CHUNK]
            decay.append(jnp.where(causal, jnp.exp(jnp.where(causal, diff, 0.0)), 0.0))
        for i, (cp, h) in enumerate(problems):
            a_intra = jnp.where(causal, qk[i][:CHUNK] * decay[i], 0.0)
            defer(ai_ref, (h, gi * (DN_GROUP // 2) + cp), a_intra.astype(BF16))
        handoff["neg_lower"] = [-jnp.where(strict, qk[i][CHUNK:] * decay[i], 0.0) for i in range(len(problems))]
        handoff["rhs"] = rhs

    def solve(gi, handoff):
        r0 = pl.multiple_of(gi * DN_ROWS, DN_ROWS)
        ps = handoff["neg_lower"]
        rhs = handoff["rhs"]
        tms = [eye_f + p for p in ps]
        for _ in range(shift - 1):
            pbs = [p.astype(BF16) for p in ps]
            ps = [_mxu(pb, block_diag(pb)) for pb in pbs]
            tms = [tm + _mxu(tm.astype(BF16), block_diag(p.astype(BF16))) for tm, p in zip(tms, ps)]
            yield
        zeros_r = jnp.zeros((CHUNK, 2 * HEAD_DIM), BF16)
        uw = []
        for i, (cp, h) in enumerate(problems):
            both = jnp.concatenate([jnp.concatenate([rhs[h][cs(2 * cp)], zeros_r], axis=1),
                                    jnp.concatenate([zeros_r, rhs[h][cs(2 * cp + 1)]], axis=1)], axis=0)
            uw.append(_mxu(tms[i].astype(BF16), both))
        for i, (cp, h) in enumerate(problems):
            for half in range(2):
                c = 2 * cp + half
                part = uw[i][:, half * 2 * HEAD_DIM:(half + 1) * 2 * HEAD_DIM]
                defer(u_ref, (h, pl.ds(r0 + c * CHUNK, CHUNK), slice(None)), part[:, :HEAD_DIM])
                w0 = pl.multiple_of(2 * (r0 + c * CHUNK), 2 * CHUNK)
                defer(wq_ref, (h, pl.ds(w0, CHUNK), slice(None)), part[:, HEAD_DIM:].astype(BF16))

    def recur(g, c_in_group, state):
        c = g * DN_GROUP + c_in_group
        pair = g * (DN_GROUP // 2) + c_in_group // 2
        zeros_v = jnp.zeros((CHUNK, HEAD_DIM), BF16)
        pad = (lambda vn_h: jnp.concatenate([vn_h, zeros_v], axis=0)) if c_in_group % 2 == 0 else (
            lambda vn_h: jnp.concatenate([zeros_v, vn_h], axis=0))
        r0 = pl.multiple_of(c * CHUNK, CHUNK)
        rows = pl.ds(r0, CHUNK)
        wrows = pl.ds(pl.multiple_of(2 * r0, 2 * CHUNK), 2 * CHUNK)
        eg = eg_ref[c]
        ws = [_mxu(wq_ref[h, wrows, :], state[h].astype(BF16)) for h in heads]
        vn = [(u_ref[h, rows, :] - ws[h][:CHUNK]).astype(BF16) for h in heads]
        new_state = [state[h] * eg[0:1, DN_HEADS + h:DN_HEADS + h + 1]
                     + lax.dot_general(kd_ref[h, rows, :], vn[h], _TN, preferred_element_type=F32) for h in heads]
        o = [ws[h][CHUNK:] + _mxu(ai_ref[h, pair], pad(vn[h])) for h in heads]
        ms = [_mxu((x * x).astype(BF16), ones_b) * (1.0 / HEAD_DIM) for x in o]
        for h in heads:
            on = o[h] * lax.rsqrt(ms[h] + EPS) * nw_ref[...]
            z = a_ref[0, rows, 3 * DN_WIDTH + h * HEAD_DIM:3 * DN_WIDTH + (h + 1) * HEAD_DIM].astype(F32)
            defer(o_ref, (0, rows, hsl(h)), (on * _silu(z)).astype(o_ref.dtype))
        return new_state

    def pipeline_step(g_prep, g_recur, state):
        def stages():
            if g_prep is not None:
                handoff = {}
                yield from elementwise(g_prep, handoff)
                yield
                yield from solve(g_prep, handoff)

        work = stages()
        for c in range(DN_GROUP):
            if g_recur is not None:
                state = recur(g_recur, c, state)
            next(work, None)
        for _ in work:
            pass
        flush()
        return state

    zero = jnp.zeros((HEAD_DIM, HEAD_DIM), F32)
    state = pipeline_step(0, None, [zero] * DN_HEADS)
    state = list(lax.fori_loop(
        1, n_groups, lambda gi, st: tuple(pipeline_step(gi, gi - 1, list(st))), tuple(state)))
    pipeline_step(None, n_groups - 1, state)


def _deltanet(a3, bg3, bgt, conv_w2, dn_norm_row):
    b, t, wa = a3.shape
    nt = bgt.shape[0]
    return pl.pallas_call(
        _deltanet_kernel,
        grid=(b,),
        in_specs=[pl.BlockSpec((1, t, wa), lambda i: (i, 0, 0)),
                  pl.BlockSpec((1, t, LANES), lambda i: (i, 0, 0)),
                  pl.BlockSpec((nt, t), lambda i: (0, i)),
                  pl.BlockSpec((CONV_WIDTH, 3 * DN_WIDTH), lambda i: (0, 0)),
                  pl.BlockSpec((1, HEAD_DIM), lambda i: (0, 0))],
        out_specs=pl.BlockSpec((1, t, DN_WIDTH), lambda i: (i, 0, 0)),
        out_shape=jax.ShapeDtypeStruct((b, t, DN_WIDTH), BF16),
        scratch_shapes=[pltpu.VMEM((DN_HEADS, t, HEAD_DIM), F32),
                        pltpu.VMEM((DN_HEADS, 2 * t, HEAD_DIM), BF16),
                        pltpu.VMEM((DN_HEADS, t, HEAD_DIM), BF16),
                        pltpu.VMEM((DN_HEADS, t // LANES, CHUNK, LANES), BF16),
                        pltpu.VMEM((t // CHUNK, 8, LANES), F32)],
        compiler_params=pltpu.CompilerParams(dimension_semantics=("parallel",),
                                             vmem_limit_bytes=DN_VMEM_LIMIT_BYTES),
        name="deltanet",
    )(a3, bg3, bgt, conv_w2, dn_norm_row)


SWA_PREP_ROWS = 1024
SWA_UNITS = 4


def _swa_kernel(s_ref, plain_ref, swapped_ref, o_ref, stats_ref, qs_ref, ks_ref, bias_ref, *dilated_scratch, dilation):
    t_len = s_ref.shape[1]
    d = dilation
    seq = t_len // d
    nblk = seq // SWA_BLOCK
    heads = range(SWA_HEADS)
    hsl = lambda h: slice(h * HEAD_DIM, (h + 1) * HEAD_DIM)
    if d > 1:
        (ostage_ref,) = dilated_scratch
    mean_b = jnp.full((HEAD_DIM, HEAD_DIM), 1.0 / HEAD_DIM, BF16)
    src_lane = lax.broadcasted_iota(jnp.int32, (HEAD_DIM, HEAD_DIM), 0)
    dst_lane = lax.broadcasted_iota(jnp.int32, (HEAD_DIM, HEAD_DIM), 1)
    swap_b = (((dst_lane < ROPE_HALF) & (src_lane == dst_lane + ROPE_HALF))
              | ((dst_lane >= ROPE_HALF) & (dst_lane < ROPE_DIM) & (src_lane == dst_lane - ROPE_HALF))
              ).astype(BF16)

    def prep(n, carry):
        r0 = pl.multiple_of(n * SWA_PREP_ROWS, SWA_PREP_ROWS)
        rows = pl.ds(r0, SWA_PREP_ROWS)
        for which, dst in ((0, qs_ref), (1, ks_ref)):
            plain = plain_ref[which, rows, :]
            swapped_tab = swapped_ref[which, rows, :]
            xb = [s_ref[0, rows, which * SWA_WIDTH + h * HEAD_DIM:which * SWA_WIDTH + (h + 1) * HEAD_DIM]
                  for h in heads]
            ms = [_mxu(x * x, mean_b) for x in xb]
            sw = [_mxu(x, swap_b) for x in xb]
            ys = [(x.astype(F32) * plain + s * swapped_tab) * lax.rsqrt(m + EPS) for x, s, m in zip(xb, sw, ms)]
            for h in heads:
                dst[rows, hsl(h)] = ys[h].astype(BF16)
        return carry

    lax.fori_loop(0, t_len // SWA_PREP_ROWS, prep, 0)

    qi = lax.broadcasted_iota(jnp.int32, (SWA_BLOCK, 2 * SWA_BLOCK), 0)
    kj = lax.broadcasted_iota(jnp.int32, (SWA_BLOCK, 2 * SWA_BLOCK), 1)
    bias_ref[0] = jnp.where(kj <= qi, 0.0, NEG_INF).astype(F32)
    bias_ref[1] = jnp.where((kj >= qi) & (kj <= qi + SWA_BLOCK), 0.0, NEG_INF).astype(F32)
    lane = lax.broadcasted_iota(jnp.int32, (SWA_BLOCK, LANES), 1)

    def attend(units):
        jobs = [(u, h) for u in range(len(units)) for h in heads]
        qrows = [pl.ds(q0, SWA_BLOCK) for q0, _, _, _, _ in units]
        krows = [pl.ds(k0, width) for _, k0, width, _, _ in units]
        s = [lax.dot_general(qs_ref[qrows[u], hsl(h)], ks_ref[krows[u], hsl(h)], _NT,
                             preferred_element_type=F32) + units[u][3] for u, h in jobs]
        m = [jnp.max(x, axis=-1, keepdims=True) for x in s]
        p = [jnp.exp(x - mx) for x, mx in zip(s, m)]
        l = [jnp.sum(x, axis=-1, keepdims=True) for x in p]
        v = [s_ref[0, krows[u], 2 * SWA_WIDTH + h * HEAD_DIM:2 * SWA_WIDTH + (h + 1) * HEAD_DIM] for u, h in jobs]
        o = [_mxu(x.astype(BF16), vh) for x, vh in zip(p, v)]
        for u in range(len(units)):
            stats_tile = jnp.zeros((SWA_BLOCK, LANES), F32)
            for h in heads:
                j = u * SWA_HEADS + h
                stats_tile = jnp.where(lane == h, m[j], jnp.where(lane == SWA_HEADS + h, l[j], stats_tile))
            t0 = units[u][4]
            if d == 1:
                for h in heads:
                    o_ref[0, qrows[u], hsl(h)] = o[u * SWA_HEADS + h].astype(o_ref.dtype)
                stats_ref[0, qrows[u], :] = stats_tile
            else:
                trows = pl.ds(t0, SWA_BLOCK, stride=d)
                for h in heads:
                    ostage_ref[h, trows, :] = o[u * SWA_HEADS + h]
                stats_ref[0, trows, :] = stats_tile

    if nblk == 1:
        causal_bias = bias_ref[0][:, :SWA_BLOCK]

        def body(i, carry):
            units = []
            for u in range(SWA_UNITS):
                r = i * SWA_UNITS + u
                q0 = pl.multiple_of(r * seq, SWA_BLOCK)
                units.append((q0, q0, SWA_BLOCK, causal_bias, r))
            attend(units)
            return carry

        lax.fori_loop(0, d // SWA_UNITS, body, 0)
    else:
        pairs = nblk // SWA_UNITS

        def body(i, carry):
            r = i // pairs
            pp = i % pairs
            units = []
            for u in range(SWA_UNITS):
                n = pp * SWA_UNITS + u
                q0 = pl.multiple_of(r * seq + n * SWA_BLOCK, SWA_BLOCK)
                if u == 0:
                    first = jnp.asarray(pp == 0).astype(jnp.int32)
                    k0 = pl.multiple_of(q0 - (1 - first) * SWA_BLOCK, SWA_BLOCK)
                    bias = bias_ref[1 - first]
                else:
                    k0 = pl.multiple_of(q0 - SWA_BLOCK, SWA_BLOCK)
                    bias = bias_ref[1]
                units.append((q0, k0, 2 * SWA_BLOCK, bias, r + d * n * SWA_BLOCK))
            attend(units)
            return carry

        lax.fori_loop(0, d * pairs, body, 0)

    if d > 1:
        def emit(n, carry):
            rows = pl.ds(pl.multiple_of(n * SWA_PREP_ROWS, SWA_PREP_ROWS), SWA_PREP_ROWS)
            for h in heads:
                o_ref[0, rows, hsl(h)] = ostage_ref[h, rows, :].astype(o_ref.dtype)
            return carry
        lax.fori_loop(0, t_len // SWA_PREP_ROWS, emit, 0)


def _swa_group(s3, plain, swapped, group, dilation):
    b, t, ws = s3.shape
    assert (t // dilation) % SWA_BLOCK == 0 and t % SWA_PREP_ROWS == 0
    nblk = t // dilation // SWA_BLOCK
    assert (dilation % SWA_UNITS == 0) if nblk == 1 else (nblk % SWA_UNITS == 0)
    scratch = [pltpu.VMEM((t, SWA_WIDTH), BF16), pltpu.VMEM((t, SWA_WIDTH), BF16),
               pltpu.VMEM((2, SWA_BLOCK, 2 * SWA_BLOCK), F32)]
    if dilation > 1:
        scratch += [pltpu.VMEM((SWA_HEADS, t, LANES), F32)]
    tab_spec = pl.BlockSpec((None, 2, t, LANES), lambda i: (group, 0, 0, 0))
    return pl.pallas_call(
        functools.partial(_swa_kernel, dilation=dilation),
        grid=(b,),
        in_specs=[pl.BlockSpec((1, t, ws), lambda i: (i, 0, 0)), tab_spec, tab_spec],
        out_specs=[pl.BlockSpec((1, t, SWA_WIDTH), lambda i: (i, 0, 0)),
                   pl.BlockSpec((1, t, LANES), lambda i: (i, 0, 0))],
        out_shape=[jax.ShapeDtypeStruct((b, t, SWA_WIDTH), BF16),
                   jax.ShapeDtypeStruct((b, t, LANES), F32)],
        scratch_shapes=scratch,
        compiler_params=_params(1),
        name=f"swa_d{dilation}",
    )(s3, plain, swapped)


MERGE_ROWS = 512


def _merge_out_kernel(x_ref, h_ref, odn_ref, o0_ref, o1_ref, o2_ref, st0_ref, st1_ref, st2_ref,
                      wz_ref, wgdn_ref, wgswa_ref, wdn32_ref, wswa32_ref, wout32_ref, out_ref,
                      wzg_scr, wdn_ref, wswa_ref, wout_ref):
    _cast_weights_once([wz_ref, wgdn_ref, wgswa_ref], wzg_scr)
    _cast_weights_once([wdn32_ref], wdn_ref)
    _cast_weights_once([wswa32_ref], wswa_ref)
    _cast_weights_once([wout32_ref], wout_ref)
    zg_all = lax.dot_general(h_ref[...], wzg_scr[...], _NT, preferred_element_type=F32)
    outs = []
    for j in range(ROW_TILE // MERGE_ROWS):
        rows = slice(j * MERGE_ROWS, (j + 1) * MERGE_ROWS)
        zg = zg_all[rows, :]
        st0 = st0_ref[rows, :]
        st1 = st1_ref[rows, :]
        st2 = st2_ref[rows, :]
        m = jnp.maximum(jnp.maximum(st0, st1), st2)
        e0 = jnp.exp(st0 - m)
        e1 = jnp.exp(st1 - m)
        e2 = jnp.exp(st2 - m)
        sums = [pltpu.roll(st, LANES - SWA_HEADS, 1) for st in (st0, st1, st2)]
        inv = 1.0 / (e0 * sums[0] + e1 * sums[1] + e2 * sums[2])
        a0, a1, a2 = e0 * inv, e1 * inv, e2 * inv
        parts = []
        for h in range(SWA_HEADS):
            hs = slice(h * HEAD_DIM, (h + 1) * HEAD_DIM)
            col = slice(h, h + 1)
            oh = (a0[:, col] * o0_ref[rows, hs].astype(F32) + a1[:, col] * o1_ref[rows, hs].astype(F32)
                  + a2[:, col] * o2_ref[rows, hs].astype(F32))
            parts.append((oh * _silu(zg[:, hs])).astype(BF16))
        o_swa = jnp.concatenate(parts, axis=1)
        y_swa = _mxu(o_swa, wswa_ref[...])
        y_dn = _mxu(odn_ref[rows, :], wdn_ref[...])
        g_dn = zg[:, SWA_WIDTH:SWA_WIDTH + D_MODEL]
        g_swa = zg[:, SWA_WIDTH + D_MODEL:SWA_WIDTH + 2 * D_MODEL]
        merged = _sigmoid(g_dn) * y_dn + _sigmoid(g_swa) * y_swa
        outs.append((rows, x_ref[rows, :] + _mxu(merged.astype(BF16), wout_ref[...])))
    for rows, value in outs:
        out_ref[rows, :] = value


def _merge_out(x2, h, odn, o_list, stats_list, wt, zg_row_blocks, w_dn, w_swa, w_out):
    n = x2.shape[0]
    row = lambda w: pl.BlockSpec((ROW_TILE, w), lambda i: (i, 0))
    full = lambda a: pl.BlockSpec(a.shape, lambda i: (0, 0), pipeline_mode=pl.Buffered(1))
    nzg = sum(rows for _, rows in zg_row_blocks)
    return pl.pallas_call(
        _merge_out_kernel,
        grid=(n // ROW_TILE,),
        in_specs=[row(D_MODEL), row(D_MODEL), row(DN_WIDTH), row(SWA_WIDTH), row(SWA_WIDTH), row(SWA_WIDTH),
                  row(LANES), row(LANES), row(LANES)]
                 + _weight_specs(zg_row_blocks, D_MODEL)
                 + [full(w_dn), full(w_swa), full(w_out)],
        out_specs=row(D_MODEL),
        out_shape=jax.ShapeDtypeStruct((n, D_MODEL), F32),
        scratch_shapes=[pltpu.VMEM((nzg, D_MODEL), BF16)] + [pltpu.VMEM(w.shape, BF16) for w in (w_dn, w_swa, w_out)],
        compiler_params=_params_sequential(),
        name="merge_out",
    )(x2, h, odn, *o_list, *stats_list, wt, wt, wt, w_dn, w_swa, w_out)


def _rope_tables(t_len, q_norm_w, k_norm_w):
    j = np.arange(t_len)
    pos = np.stack([(j % (t_len // d)) * d + j // (t_len // d) for _, d in SWA_GROUPS]).astype(np.float64)
    inv_freq = ROPE_THETA ** (-np.arange(0, ROPE_DIM, 2, dtype=np.float64) / ROPE_DIM)
    ang = pos[:, :, None] * inv_freq[None, None, :]
    cos, sin = np.cos(ang), np.sin(ang)
    tail = (N_GROUPS, t_len, HEAD_DIM - ROPE_DIM)
    cos_t = jnp.asarray(np.concatenate([cos, cos, np.ones(tail)], axis=-1), F32)
    sin_t = jnp.asarray(np.concatenate([-sin, sin, np.zeros(tail)], axis=-1), F32)
    w = jnp.stack([q_norm_w.astype(F32) * (HEAD_DIM ** -0.5), k_norm_w.astype(F32)], axis=1)
    w_swapped = jnp.concatenate([w[..., ROPE_HALF:ROPE_DIM], w[..., :ROPE_HALF], w[..., ROPE_DIM:]], axis=-1)
    return w[:, :, None, :] * cos_t[:, None], w_swapped[:, :, None, :] * sin_t[:, None]


def kernel(x, norm_w, w_in, conv_w, dn_a_log, dn_dt_bias, dn_norm_w, q_norm_w, k_norm_w,
           w_branch_dn, w_branch_swa, w_out):
    b, t, d = x.shape
    n = b * t
    layer = 0
    wt = jnp.swapaxes(w_in[layer], 0, 1)
    c_z = 4 * DN_WIDTH
    c_q = c_z + 2 * DN_HEADS
    c_k = c_q + N_GROUPS * SWA_WIDTH
    c_v = c_k + N_GROUPS * SWA_WIDTH
    c_sz = c_v + N_GROUPS * SWA_WIDTH
    c_g = c_sz + SWA_WIDTH
    grp_blocks = lambda g: [(c0 + g * SWA_WIDTH, SWA_WIDTH) for c0 in (c_q, c_k, c_v)]
    zg_blocks = [(c_sz, SWA_WIDTH), (c_g, D_MODEL), (c_g + D_MODEL, D_MODEL)]
    pad_heads = lambda v: jnp.pad(v.astype(F32), (DN_HEADS, LANES - 2 * DN_HEADS))[None, :]
    alog_row = pad_heads(dn_a_log[layer])
    dt_row = pad_heads(dn_dt_bias[layer])

    x2 = x.reshape(n, d)
    h, a, bg, bgt = _norm_proj(x2, norm_w[layer][None, :], wt, c_z, conv_w[layer][:, 0, :], alog_row, dt_row, t)

    o_dn = _deltanet(a.reshape(b, t, 4 * DN_WIDTH), bg.reshape(b, t, LANES), bgt,
                     conv_w[layer][:, 0, :], dn_norm_w[layer][None, :])

    plain, swapped = _rope_tables(t, q_norm_w[layer], k_norm_w[layer])
    o_list, stats_list = [], []
    for g, (window, dilation) in enumerate(SWA_GROUPS):
        assert window // dilation == SWA_BLOCK
        s_g = _matmul(h, wt, grp_blocks(g), f"proj_swa{g}", dilation, t).reshape(b, t, 3 * SWA_WIDTH)
        o_g, stats_g = _swa_group(s_g, plain, swapped, g, dilation)
        o_list.append(o_g.reshape(n, SWA_WIDTH))
        stats_list.append(stats_g.reshape(n, LANES))

    out = _merge_out(x2, h, o_dn.reshape(n, DN_WIDTH), o_list, stats_list, wt, zg_blocks,
                     w_branch_dn[layer], w_branch_swa[layer], w_out[layer])
    return out.reshape(b, t, d)
```

```python
import functools
import math

import jax
import jax.numpy as jnp
import numpy as np
from jax import lax
from jax.experimental import pallas as pl
from jax.experimental.pallas import tpu as pltpu

D_MODEL = 1024
HEAD_DIM = 128
DN_HEADS = 4
DN_WIDTH = DN_HEADS * HEAD_DIM
CONV_WIDTH = 4
CHUNK = 64
SWA_GROUPS = ((128, 1), (512, 4), (2048, 16))
N_GROUPS = 3
SWA_HEADS = 4
SWA_WIDTH = SWA_HEADS * HEAD_DIM
SWA_BLOCK = 128
ROPE_DIM = HEAD_DIM // 4
ROPE_HALF = ROPE_DIM // 2
ROPE_THETA = 500000.0
EPS = 1e-6
NEG_INF = -1e30

LANES = 128
VMEM_LIMIT_BYTES = 48 * 1024 * 1024
DN_VMEM_LIMIT_BYTES = 56 * 1024 * 1024
ROW_TILE = 512
NORM_ROW_TILE = 512
PROJ_ROW_TILE = 1024
BG_ROWS = 16

F32 = jnp.float32
BF16 = jnp.bfloat16
_NT = (((1,), (1,)), ((), ()))
_TN = (((0,), (0,)), ((), ()))


def _sigmoid(x):
    return 0.5 * jnp.tanh(0.5 * x) + 0.5


def _silu(x):
    return x * _sigmoid(x)


def _softplus(x):
    return jnp.maximum(x, 0.0) + jnp.log(1.0 + jnp.exp(-jnp.abs(x)))


def _params(n_axes):
    return pltpu.CompilerParams(dimension_semantics=("parallel",) * n_axes,
                                vmem_limit_bytes=VMEM_LIMIT_BYTES)


def _params_sequential():
    return pltpu.CompilerParams(dimension_semantics=("arbitrary",), vmem_limit_bytes=VMEM_LIMIT_BYTES)


def _mxu(a, b):
    return jnp.dot(a, b, preferred_element_type=F32)


def _split3(x):
    hi = x.astype(BF16)
    r1 = x - hi.astype(F32)
    mid = r1.astype(BF16)
    lo = (r1 - mid.astype(F32)).astype(BF16)
    return hi, mid, lo


def _beta_and_log_decay(ab, a_log, dt_bias, head_index):
    beta = _sigmoid(ab)
    g = -jnp.exp(a_log) * _softplus(ab + dt_bias)
    return jnp.where(head_index < DN_HEADS, beta, g)


def _weight_specs(row_blocks, k):
    return [pl.BlockSpec((pl.Element(rows), pl.Element(k)), lambda i, start=start: (start, 0),
                         pipeline_mode=pl.Buffered(1))
            for start, rows in row_blocks]


def _cast_weights_once(w_refs, w_scr):
    @pl.when(pl.program_id(0) == 0)
    def _():
        blocks = [w_ref[...] for w_ref in w_refs]
        pad = w_scr.shape[0] - sum(blk.shape[0] for blk in blocks)
        if pad:
            blocks.append(jnp.zeros((pad, w_scr.shape[1]), F32))
        w_scr[...] = (jnp.concatenate(blocks, axis=0) if len(blocks) > 1 else blocks[0]).astype(BF16)


CONV_TILE = 256
CONV_HALO = 8
CONV_ROWS = 64
CONV_EARLY_COLS = 2 * DN_WIDTH


def _norm_proj_kernel(x_ref, nw_ref, wa_ref, wab_ref, cw_ref, alog_ref, dt_ref, alogt_ref, dtt_ref,
                      h_ref, a_ref, bg_ref, bgt_ref, wa_scr, wab_scr, tail_ref, res_a, res_b, *, tiles_per_seq):
    res_scr = (res_a, res_b)
    tile_rows = x_ref.shape[0]
    _cast_weights_once([wa_ref], wa_scr)
    _cast_weights_once([wab_ref], wab_scr)
    x = x_ref[...]
    h = (x * lax.rsqrt(jnp.mean(x * x, axis=-1, keepdims=True) + EPS)) * nw_ref[...]
    hb = h.astype(BF16)
    h_ref[...] = hb
    n_conv = CONV_EARLY_COLS
    seq_start = pl.program_id(0) % tiles_per_seq == 0
    tail = jnp.where(seq_start, 0.0, tail_ref[...])
    stores = []
    tiles = list(range(0, n_conv, CONV_TILE))
    base = pl.multiple_of((pl.program_id(0) >> 20) * CONV_HALO, CONV_HALO)

    def project(k):
        c0 = tiles[k]
        buf = res_scr[k % 2]
        buf[pl.ds(base, CONV_HALO), :] = tail[:, c0:c0 + CONV_TILE]
        buf[pl.ds(base + CONV_HALO, tile_rows), :] = lax.dot_general(hb, wa_scr[c0:c0 + CONV_TILE, :], _NT,
                                                                    preferred_element_type=F32)

    if tiles:
        project(0)
    for k, c0 in enumerate(tiles):
        cols = slice(c0, c0 + CONV_TILE)
        if k + 1 < len(tiles):
            project(k + 1)
        buf = res_scr[k % 2]
        acts = []
        for rb in range(0, tile_rows, CONV_ROWS):
            blk = buf[pl.ds(base + rb, CONV_HALO + CONV_ROWS), :]
            conv = None
            for j in range(CONV_WIDTH):
                lo = CONV_HALO - (CONV_WIDTH - 1) + j
                term = cw_ref[j:j + 1, cols] * blk[lo:lo + CONV_ROWS, :]
                conv = term if conv is None else conv + term
            acts.append(_silu(conv).astype(BF16))
        stores.append((a_ref, (slice(None), cols), jnp.concatenate(acts, axis=0)))
        stores.append((tail_ref, (slice(None), cols), buf[pl.ds(base + tile_rows, CONV_HALO), :]))
    z = lax.dot_general(hb, wa_scr[n_conv:, :], _NT, preferred_element_type=F32)
    stores.append((a_ref, (slice(None), slice(n_conv, None)), z.astype(BF16)))
    for ref, idx, value in stores:
        ref[idx] = value
    ab = lax.dot_general(hb, wab_scr[...], _NT, preferred_element_type=F32)
    bg_ref[...] = _beta_and_log_decay(ab, alog_ref[...], dt_ref[...],
                                      lax.broadcasted_iota(jnp.int32, ab.shape, 1))
    abt = lax.dot_general(wab_scr[0:BG_ROWS, :], hb, _NT, preferred_element_type=F32)
    bgt_ref[...] = _beta_and_log_decay(abt, alogt_ref[...], dtt_ref[...],
                                       lax.broadcasted_iota(jnp.int32, abt.shape, 0))


def _norm_proj(x2, norm_w, wt, na, conv_w2, alog_row, dt_row, seq_len):
    n = x2.shape[0]
    tile = NORM_ROW_TILE
    assert seq_len % tile == 0 and tile % CONV_ROWS == 0 and CONV_EARLY_COLS % CONV_TILE == 0
    row = lambda i: (i, 0)
    fixed = lambda i: (0, 0)
    pad_col = lambda v: v[0, :BG_ROWS][:, None]
    return pl.pallas_call(
        functools.partial(_norm_proj_kernel, tiles_per_seq=seq_len // tile),
        grid=(n // tile,),
        in_specs=[pl.BlockSpec((tile, D_MODEL), row),
                  pl.BlockSpec((1, D_MODEL), fixed)]
                 + _weight_specs([(0, na), (na, 2 * DN_HEADS)], D_MODEL)
                 + [pl.BlockSpec((CONV_WIDTH, 3 * DN_WIDTH), fixed),
                    pl.BlockSpec((1, LANES), fixed),
                    pl.BlockSpec((1, LANES), fixed),
                    pl.BlockSpec((BG_ROWS, 1), fixed),
                    pl.BlockSpec((BG_ROWS, 1), fixed)],
        out_specs=[pl.BlockSpec((tile, D_MODEL), row),
                   pl.BlockSpec((tile, na), row),
                   pl.BlockSpec((tile, LANES), row),
                   pl.BlockSpec((BG_ROWS, tile), lambda i: (0, i))],
        out_shape=[jax.ShapeDtypeStruct((n, D_MODEL), BF16),
                   jax.ShapeDtypeStruct((n, na), BF16),
                   jax.ShapeDtypeStruct((n, LANES), F32),
                   jax.ShapeDtypeStruct((BG_ROWS, n), F32)],
        scratch_shapes=[pltpu.VMEM((na, D_MODEL), BF16), pltpu.VMEM((LANES, D_MODEL), BF16),
                        pltpu.VMEM((CONV_HALO, max(CONV_EARLY_COLS, LANES)), F32),
                        pltpu.VMEM((CONV_HALO + tile, CONV_TILE), F32),
                        pltpu.VMEM((CONV_HALO + tile, CONV_TILE), F32)],
        compiler_params=_params_sequential(),
        name="norm_proj",
    )(x2, norm_w, wt, wt, conv_w2, alog_row, dt_row, pad_col(alog_row), pad_col(dt_row))


REGROUP_STRIDE = 4


def _matmul_kernel(h_ref, *refs, dilation):
    n_scratch = 1 if dilation == 1 else 3
    w_refs, o_ref, w_scr = refs[:-n_scratch - 1], refs[-n_scratch - 1], refs[-n_scratch]
    _cast_weights_once(w_refs, w_scr)
    if dilation == 1:
        o_ref[...] = lax.dot_general(h_ref[...], w_scr[...], _NT, preferred_element_type=F32).astype(o_ref.dtype)
        return
    stage_ref, stage2_ref = refs[-2], refs[-1]
    s1 = min(dilation, REGROUP_STRIDE)
    s2 = dilation // s1
    rows1 = h_ref.shape[0] // s1
    per = h_ref.shape[0] // dilation
    col = 0
    for w_ref in w_refs:
        width = w_ref.shape[0]
        res = lax.dot_general(h_ref[...], w_scr[col:col + width, :], _NT, preferred_element_type=F32)
        slabs = range(col // LANES, (col + width) // LANES)
        for c in slabs:
            stage_ref[c] = res[:, (c * LANES - col):(c * LANES - col) + LANES]
        for c in slabs:
            for a in range(s1):
                first = stage_ref[c, pl.ds(a, rows1, stride=s1), :]
                if s2 == 1:
                    o_ref[0, a, :, c * LANES:(c + 1) * LANES] = first.astype(o_ref.dtype)
                else:
                    stage2_ref[c, a * rows1:(a + 1) * rows1, :] = first
        if s2 > 1:
            for c in slabs:
                for a in range(s1):
                    for b in range(s2):
                        o_ref[0, b * s1 + a, :, c * LANES:(c + 1) * LANES] = (
                            stage2_ref[c, pl.ds(a * rows1 + b, per, stride=s2), :].astype(o_ref.dtype))
        col += width


def _matmul(h, wt, row_blocks, name, dilation=1, seq_len=None):
    n, k = h.shape
    nc = sum(rows for _, rows in row_blocks)
    scratch = [pltpu.VMEM((nc, k), BF16)]
    tile = PROJ_ROW_TILE
    if dilation == 1:
        out_spec = pl.BlockSpec((tile, nc), lambda i: (i, 0))
        out_shape = jax.ShapeDtypeStruct((n, nc), BF16)
    else:
        tiles = seq_len // tile
        per = tile // dilation
        assert seq_len % tile == 0 and tile % dilation == 0 and per % 16 == 0 and nc % LANES == 0
        out_spec = pl.BlockSpec((1, dilation, per, nc), lambda i: (i // tiles, 0, i % tiles, 0))
        out_shape = jax.ShapeDtypeStruct((n // seq_len, dilation, seq_len // dilation, nc), BF16)
        assert dilation % min(dilation, REGROUP_STRIDE) == 0
        scratch += [pltpu.VMEM((nc // LANES, tile, LANES), F32)] * 2
    return pl.pallas_call(
        functools.partial(_matmul_kernel, dilation=dilation),
        grid=(n // tile,),
        in_specs=[pl.BlockSpec((tile, k), lambda i: (i, 0))] + _weight_specs(row_blocks, k),
        out_specs=out_spec,
        out_shape=out_shape,
        scratch_shapes=scratch,
        compiler_params=_params_sequential(),
        name=name,
    )(h, *([wt] * len(row_blocks)))


DN_GROUP = 8
DN_ROWS = DN_GROUP * CHUNK
DN_HALO = 16


def _deltanet_kernel(a_ref, bg_ref, bgt_ref, cw_ref, nw_ref, o_ref,
                     u_ref, wq_ref, kd_ref, ai_ref, eg_ref):
    t_len = a_ref.shape[1]
    n_groups = t_len // DN_ROWS
    assert 2 * CHUNK == LANES and DN_GROUP % 2 == 0
    ii = lax.broadcasted_iota(jnp.int32, (CHUNK, LANES), 0)
    lane_pair = lax.broadcasted_iota(jnp.int32, (CHUNK, LANES), 1)
    jj = lane_pair & (CHUNK - 1)
    second = lane_pair >= CHUNK
    causal = ii >= jj
    strict = ii > jj
    eye_f = (ii == jj).astype(F32)
    second_rows = lax.broadcasted_iota(jnp.int32, (LANES, LANES), 0) >= CHUNK
    second_cols = lax.broadcasted_iota(jnp.int32, (LANES, LANES), 1) >= CHUNK
    same_half = second_rows == second_cols

    def block_diag(pair):
        return jnp.where(same_half, jnp.concatenate([pair, pair], axis=0), 0)

    bi = lax.broadcasted_iota(jnp.int32, (DN_ROWS, DN_ROWS), 0)
    bj = lax.broadcasted_iota(jnp.int32, (DN_ROWS, DN_ROWS), 1)
    shift = int(math.log2(CHUNK))
    same_chunk = jnp.right_shift(bi, shift) == jnp.right_shift(bj, shift)
    cum_lower = (same_chunk & (bi >= bj)).astype(BF16)
    cum_upper = (same_chunk & (bi <= bj)).astype(BF16)
    problems = [(cp, h) for cp in range(DN_GROUP // 2) for h in range(DN_HEADS)]
    heads = range(DN_HEADS)
    hsl = lambda h: slice(h * HEAD_DIM, (h + 1) * HEAD_DIM)
    cs = lambda c: slice(c * CHUNK, (c + 1) * CHUNK)
    ones_b = jnp.ones((HEAD_DIM, HEAD_DIM), BF16)
    sel_src = lax.broadcasted_iota(jnp.int32, (LANES, DN_WIDTH), 0)
    sel_head = jnp.right_shift(lax.broadcasted_iota(jnp.int32, (LANES, DN_WIDTH), 1), int(math.log2(HEAD_DIM)))
    sel_beta = (sel_src == sel_head).astype(BF16)
    sel_g = (sel_src == sel_head + DN_HEADS).astype(BF16)
    lane = lax.broadcasted_iota(jnp.int32, (DN_ROWS, LANES), 1)
    g_lanes = (lane >= DN_HEADS) & (lane < 2 * DN_HEADS)

    pending = []

    def defer(ref, idx, value):
        pending.append((ref, idx, value))

    def flush():
        for ref, idx, value in pending:
            ref[idx] = value
        pending.clear()

    def conv_silu(gi, r0, col0):
        cols = slice(col0, col0 + HEAD_DIM)
        rp = pl.multiple_of(jnp.maximum(r0 - DN_HALO, 0), DN_HALO)
        cur = a_ref[0, pl.ds(r0, DN_ROWS), cols].astype(F32)
        prev = a_ref[0, pl.ds(rp, DN_HALO), cols].astype(F32)
        prev = jnp.where(gi > 0, prev, 0.0)
        xw = jnp.concatenate([prev, cur], axis=0)
        conv = None
        for j in range(CONV_WIDTH):
            lo = DN_HALO - (CONV_WIDTH - 1) + j
            term = cw_ref[j:j + 1, cols] * xw[lo:lo + DN_ROWS, :]
            conv = term if conv is None else conv + term
        return _silu(conv)

    def elementwise(gi, handoff):
        r0 = pl.multiple_of(gi * DN_ROWS, DN_ROWS)
        def act_of(h):
            tiles = []
            for base in (0, DN_WIDTH, 2 * DN_WIDTH):
                col0 = base + h * HEAD_DIM
                if col0 < CONV_EARLY_COLS:
                    tiles.append(a_ref[0, pl.ds(r0, DN_ROWS), col0:col0 + HEAD_DIM].astype(F32))
                else:
                    tiles.append(conv_silu(gi, r0, col0))
            return tiles
        acts = []

        def head_norms(h):
            qh, kh, vh = act_of(h)
            qn_h = qh * (lax.rsqrt(_mxu((qh * qh).astype(BF16), ones_b) + EPS) * (HEAD_DIM ** -0.5))
            kn_h = kh * lax.rsqrt(_mxu((kh * kh).astype(BF16), ones_b) + EPS)
            acts.append((qn_h, kn_h, vh))

        head_norms(0)
        yield
        bg = bg_ref[0, pl.ds(r0, DN_ROWS), :]
        gc_all = sum(_mxu(cum_lower, piece) for piece in _split3(bg))
        gct_all = sum(_mxu(piece, cum_upper) for piece in _split3(bgt_ref[:, pl.ds(r0, DN_ROWS)]))
        gl_all = jnp.concatenate([jnp.broadcast_to(gc_all[(c + 1) * CHUNK - 1:(c + 1) * CHUNK, :], (CHUNK, LANES))
                                  for c in range(DN_GROUP)], axis=0)
        eg_all = jnp.exp(gl_all)
        for c in range(DN_GROUP):
            defer(eg_ref, (gi * DN_GROUP + c,), eg_all[c * CHUNK:c * CHUNK + 8, :])
        gc_m = jnp.where(g_lanes, gc_all, 0.0)
        gl_m = jnp.where(g_lanes, gl_all, 0.0)
        beta_rep = _mxu(bg.astype(BF16), sel_beta)
        egc_rep = _mxu(jnp.exp(gc_m).astype(BF16), sel_g)
        ekd_rep = _mxu(jnp.exp(gl_m - gc_m).astype(BF16), sel_g)

        for h in range(1, DN_HEADS):
            head_norms(h)
            yield

        qn, knb, kb, rhs = [], [], [], []
        for h in heads:
            qn_h, kn_h, vh = acts[h]
            kb_h = kn_h * beta_rep[:, hsl(h)]
            defer(kd_ref, (h, pl.ds(r0, DN_ROWS), slice(None)), (kn_h * ekd_rep[:, hsl(h)]).astype(BF16))
            qg_h = (qn_h * egc_rep[:, hsl(h)]).astype(BF16)
            for c in range(DN_GROUP):
                w0 = pl.multiple_of(2 * (r0 + c * CHUNK), 2 * CHUNK)
                defer(wq_ref, (h, pl.ds(w0 + CHUNK, CHUNK), slice(None)), qg_h[cs(c)])
            qn.append(qn_h)
            knb.append(kn_h.astype(BF16))
            kb.append(kb_h)
            rhs.append(jnp.concatenate([vh * beta_rep[:, hsl(h)], kb_h * egc_rep[:, hsl(h)]], axis=1).astype(BF16))
            yield

        zeros_k = jnp.zeros((CHUNK, HEAD_DIM), BF16)
        qk = []
        for cp, h in problems:
            c1, c2 = cs(2 * cp), cs(2 * cp + 1)
            lhs = jnp.concatenate([jnp.concatenate([qn[h][c1], qn[h][c2]], axis=1),
                                   jnp.concatenate([kb[h][c1], kb[h][c2]], axis=1)], axis=0).astype(BF16)
            keys = jnp.concatenate([jnp.concatenate([knb[h][c1], zeros_k], axis=1),
                                    jnp.concatenate([zeros_k, knb[h][c2]], axis=1)], axis=0)
            qk.append(lax.dot_general(lhs, keys, _NT, preferred_element_type=F32))
        decay = []
        for cp, h in problems:
            col = slice(DN_HEADS + h, DN_HEADS + h + 1)
            gc_col = jnp.where(second, gc_all[cs(2 * cp + 1), col], gc_all[cs(2 * cp), col])
            diff = gc_col - gct_all[DN_HEADS + h:DN_HEADS + h + 1, 2 * cp * CHUNK:(2 * cp + 2) * CHUNK]
            decay.append(jnp.where(causal, jnp.exp(jnp.where(causal, diff, 0.0)), 0.0))
        for i, (cp, h) in enumerate(problems):
            a_intra = jnp.where(causal, qk[i][:CHUNK] * decay[i], 0.0)
            defer(ai_ref, (h, gi * (DN_GROUP // 2) + cp), a_intra.astype(BF16))
        handoff["neg_lower"] = [-jnp.where(strict, qk[i][CHUNK:] * decay[i], 0.0) for i in range(len(problems))]
        handoff["rhs"] = rhs

    def solve(gi, handoff):
        r0 = pl.multiple_of(gi * DN_ROWS, DN_ROWS)
        ps = handoff["neg_lower"]
        rhs = handoff["rhs"]
        tms = [eye_f + p for p in ps]
        pbs = [p.astype(BF16) for p in ps]
        ps = [_mxu(pb, block_diag(pb)) for pb in pbs]
        yield
        for _ in range(shift - 2):
            pbs = [p.astype(BF16) for p in ps]
            both = [_mxu(jnp.concatenate([pb, tm.astype(BF16)], axis=0), block_diag(pb)) for pb, tm in zip(pbs, tms)]
            ps = [r[:CHUNK] for r in both]
            tms = [tm + r[CHUNK:] for tm, r in zip(tms, both)]
            yield
        tms = [tm + _mxu(tm.astype(BF16), block_diag(p.astype(BF16))) for tm, p in zip(tms, ps)]
        zeros_r = jnp.zeros((CHUNK, 2 * HEAD_DIM), BF16)
        uw = []
        for i, (cp, h) in enumerate(problems):
            both = jnp.concatenate([jnp.concatenate([rhs[h][cs(2 * cp)], zeros_r], axis=1),
                                    jnp.concatenate([zeros_r, rhs[h][cs(2 * cp + 1)]], axis=1)], axis=0)
            uw.append(_mxu(tms[i].astype(BF16), both))
        for i, (cp, h) in enumerate(problems):
            for half in range(2):
                c = 2 * cp + half
                part = uw[i][:, half * 2 * HEAD_DIM:(half + 1) * 2 * HEAD_DIM]
                defer(u_ref, (h, pl.ds(r0 + c * CHUNK, CHUNK), slice(None)), part[:, :HEAD_DIM])
                w0 = pl.multiple_of(2 * (r0 + c * CHUNK), 2 * CHUNK)
                defer(wq_ref, (h, pl.ds(w0, CHUNK), slice(None)), part[:, HEAD_DIM:].astype(BF16))

    def recur(g, c_in_group, state):
        c = g * DN_GROUP + c_in_group
        pair = g * (DN_GROUP // 2) + c_in_group // 2
        zeros_v = jnp.zeros((CHUNK, HEAD_DIM), BF16)
        pad = (lambda vn_h: jnp.concatenate([vn_h, zeros_v], axis=0)) if c_in_group % 2 == 0 else (
            lambda vn_h: jnp.concatenate([zeros_v, vn_h], axis=0))
        r0 = pl.multiple_of(c * CHUNK, CHUNK)
        rows = pl.ds(r0, CHUNK)
        wrows = pl.ds(pl.multiple_of(2 * r0, 2 * CHUNK), 2 * CHUNK)
        eg = eg_ref[c]
        ws = [_mxu(wq_ref[h, wrows, :], state[h].astype(BF16)) for h in heads]
        vn = [(u_ref[h, rows, :] - ws[h][:CHUNK]).astype(BF16) for h in heads]
        new_state = [state[h] * eg[0:1, DN_HEADS + h:DN_HEADS + h + 1]
                     + lax.dot_general(kd_ref[h, rows, :], vn[h], _TN, preferred_element_type=F32) for h in heads]
        o = [ws[h][CHUNK:] + _mxu(ai_ref[h, pair], pad(vn[h])) for h in heads]
        ms = [_mxu((x * x).astype(BF16), ones_b) * (1.0 / HEAD_DIM) for x in o]
        for h in heads:
            on = o[h] * lax.rsqrt(ms[h] + EPS) * nw_ref[...]
            z = a_ref[0, rows, 3 * DN_WIDTH + h * HEAD_DIM:3 * DN_WIDTH + (h + 1) * HEAD_DIM].astype(F32)
            defer(o_ref, (0, rows, hsl(h)), (on * _silu(z)).astype(o_ref.dtype))
        return new_state

    def pipeline_step(g_prep, g_recur, state):
        def stages():
            if g_prep is not None:
                handoff = {}
                yield from elementwise(g_prep, handoff)
                yield
                yield from solve(g_prep, handoff)

        work = stages()
        for c in range(DN_GROUP):
            if g_recur is not None:
                state = recur(g_recur, c, state)
            next(work, None)
        for _ in work:
            pass
        flush()
        return state

    zero = jnp.zeros((HEAD_DIM, HEAD_DIM), F32)
    state = pipeline_step(0, None, [zero] * DN_HEADS)
    state = list(lax.fori_loop(
        1, n_groups, lambda gi, st: tuple(pipeline_step(gi, gi - 1, list(st))), tuple(state)))
    pipeline_step(None, n_groups - 1, state)


def _deltanet(a3, bg3, bgt, conv_w2, dn_norm_row):
    b, t, wa = a3.shape
    nt = bgt.shape[0]
    return pl.pallas_call(
        _deltanet_kernel,
        grid=(b,),
        in_specs=[pl.BlockSpec((1, t, wa), lambda i: (i, 0, 0)),
                  pl.BlockSpec((1, t, LANES), lambda i: (i, 0, 0)),
                  pl.BlockSpec((nt, t), lambda i: (0, i)),
                  pl.BlockSpec((CONV_WIDTH, 3 * DN_WIDTH), lambda i: (0, 0)),
                  pl.BlockSpec((1, HEAD_DIM), lambda i: (0, 0))],
        out_specs=pl.BlockSpec((1, t, DN_WIDTH), lambda i: (i, 0, 0)),
        out_shape=jax.ShapeDtypeStruct((b, t, DN_WIDTH), BF16),
        scratch_shapes=[pltpu.VMEM((DN_HEADS, t, HEAD_DIM), F32),
                        pltpu.VMEM((DN_HEADS, 2 * t, HEAD_DIM), BF16),
                        pltpu.VMEM((DN_HEADS, t, HEAD_DIM), BF16),
                        pltpu.VMEM((DN_HEADS, t // LANES, CHUNK, LANES), BF16),
                        pltpu.VMEM((t // CHUNK, 8, LANES), F32)],
        compiler_params=pltpu.CompilerParams(dimension_semantics=("parallel",),
                                             vmem_limit_bytes=DN_VMEM_LIMIT_BYTES),
        name="deltanet",
    )(a3, bg3, bgt, conv_w2, dn_norm_row)


SWA_PREP_ROWS = 1024
SWA_UNITS = 4


def _swa_kernel(s_ref, plain_ref, swapped_ref, o_ref, stats_ref, qs_ref, ks_ref, bias_ref, *dilated_scratch, dilation):
    t_len = s_ref.shape[1]
    d = dilation
    seq = t_len // d
    nblk = seq // SWA_BLOCK
    heads = range(SWA_HEADS)
    hsl = lambda h: slice(h * HEAD_DIM, (h + 1) * HEAD_DIM)
    if d > 1:
        (ostage_ref,) = dilated_scratch
    mean_b = jnp.full((HEAD_DIM, HEAD_DIM), 1.0 / HEAD_DIM, BF16)
    src_lane = lax.broadcasted_iota(jnp.int32, (HEAD_DIM, HEAD_DIM), 0)
    dst_lane = lax.broadcasted_iota(jnp.int32, (HEAD_DIM, HEAD_DIM), 1)
    swap_b = (((dst_lane < ROPE_HALF) & (src_lane == dst_lane + ROPE_HALF))
              | ((dst_lane >= ROPE_HALF) & (dst_lane < ROPE_DIM) & (src_lane == dst_lane - ROPE_HALF))
              ).astype(BF16)

    def prep(n, carry):
        r0 = pl.multiple_of(n * SWA_PREP_ROWS, SWA_PREP_ROWS)
        rows = pl.ds(r0, SWA_PREP_ROWS)
        for which, dst in ((0, qs_ref), (1, ks_ref)):
            plain = plain_ref[which, rows, :]
            swapped_tab = swapped_ref[which, rows, :]
            xb = [s_ref[0, rows, which * SWA_WIDTH + h * HEAD_DIM:which * SWA_WIDTH + (h + 1) * HEAD_DIM]
                  for h in heads]
            ms = [_mxu(x * x, mean_b) for x in xb]
            sw = [_mxu(x, swap_b) for x in xb]
            ys = [(x.astype(F32) * plain + s * swapped_tab) * lax.rsqrt(m + EPS) for x, s, m in zip(xb, sw, ms)]
            for h in heads:
                dst[rows, hsl(h)] = ys[h].astype(BF16)
        return carry

    lax.fori_loop(0, t_len // SWA_PREP_ROWS, prep, 0)

    qi = lax.broadcasted_iota(jnp.int32, (SWA_BLOCK, 2 * SWA_BLOCK), 0)
    kj = lax.broadcasted_iota(jnp.int32, (SWA_BLOCK, 2 * SWA_BLOCK), 1)
    bias_ref[0] = jnp.where(kj <= qi, 0.0, NEG_INF).astype(F32)
    bias_ref[1] = jnp.where((kj >= qi) & (kj <= qi + SWA_BLOCK), 0.0, NEG_INF).astype(F32)
    lane = lax.broadcasted_iota(jnp.int32, (SWA_BLOCK, LANES), 1)

    def attend(units):
        jobs = [(u, h) for u in range(len(units)) for h in heads]
        qrows = [pl.ds(q0, SWA_BLOCK) for q0, _, _, _, _ in units]
        krows = [pl.ds(k0, width) for _, k0, width, _, _ in units]
        s = [lax.dot_general(qs_ref[qrows[u], hsl(h)], ks_ref[krows[u], hsl(h)], _NT,
                             preferred_element_type=F32) + units[u][3] for u, h in jobs]
        m = [jnp.max(x, axis=-1, keepdims=True) for x in s]
        p = [jnp.exp(x - mx) for x, mx in zip(s, m)]
        l = [jnp.sum(x, axis=-1, keepdims=True) for x in p]
        v = [s_ref[0, krows[u], 2 * SWA_WIDTH + h * HEAD_DIM:2 * SWA_WIDTH + (h + 1) * HEAD_DIM] for u, h in jobs]
        o = [_mxu(x.astype(BF16), vh) for x, vh in zip(p, v)]
        for u in range(len(units)):
            stats_tile = jnp.zeros((SWA_BLOCK, LANES), F32)
            for h in heads:
                j = u * SWA_HEADS + h
                stats_tile = jnp.where(lane == h, m[j], jnp.where(lane == SWA_HEADS + h, l[j], stats_tile))
            t0 = units[u][4]
            if d == 1:
                for h in heads:
                    o_ref[0, qrows[u], hsl(h)] = o[u * SWA_HEADS + h].astype(o_ref.dtype)
                stats_ref[0, qrows[u], :] = stats_tile
            else:
                trows = pl.ds(t0, SWA_BLOCK, stride=d)
                for h in heads:
                    ostage_ref[h, trows, :] = o[u * SWA_HEADS + h]
                stats_ref[0, trows, :] = stats_tile

    if nblk == 1:
        causal_bias = bias_ref[0][:, :SWA_BLOCK]

        def body(i, carry):
            units = []
            for u in range(SWA_UNITS):
                r = i * SWA_UNITS + u
                q0 = pl.multiple_of(r * seq, SWA_BLOCK)
                units.append((q0, q0, SWA_BLOCK, causal_bias, r))
            attend(units)
            return carry

        lax.fori_loop(0, d // SWA_UNITS, body, 0)
    else:
        pairs = nblk // SWA_UNITS

        def body(i, carry):
            r = i // pairs
            pp = i % pairs
            units = []
            for u in range(SWA_UNITS):
                n = pp * SWA_UNITS + u
                q0 = pl.multiple_of(r * seq + n * SWA_BLOCK, SWA_BLOCK)
                if u == 0:
                    first = jnp.asarray(pp == 0).astype(jnp.int32)
                    k0 = pl.multiple_of(q0 - (1 - first) * SWA_BLOCK, SWA_BLOCK)
                    bias = bias_ref[1 - first]
                else:
                    k0 = pl.multiple_of(q0 - SWA_BLOCK, SWA_BLOCK)
                    bias = bias_ref[1]
                units.append((q0, k0, 2 * SWA_BLOCK, bias, r + d * n * SWA_BLOCK))
            attend(units)
            return carry

        lax.fori_loop(0, d * pairs, body, 0)

    if d > 1:
        def emit(n, carry):
            rows = pl.ds(pl.multiple_of(n * SWA_PREP_ROWS, SWA_PREP_ROWS), SWA_PREP_ROWS)
            for h in heads:
                o_ref[0, rows, hsl(h)] = ostage_ref[h, rows, :].astype(o_ref.dtype)
            return carry
        lax.fori_loop(0, t_len // SWA_PREP_ROWS, emit, 0)


def _swa_group(s3, plain, swapped, group, dilation):
    b, t, ws = s3.shape
    assert (t // dilation) % SWA_BLOCK == 0 and t % SWA_PREP_ROWS == 0
    nblk = t // dilation // SWA_BLOCK
    assert (dilation % SWA_UNITS == 0) if nblk == 1 else (nblk % SWA_UNITS == 0)
    scratch = [pltpu.VMEM((t, SWA_WIDTH), BF16), pltpu.VMEM((t, SWA_WIDTH), BF16),
               pltpu.VMEM((2, SWA_BLOCK, 2 * SWA_BLOCK), F32)]
    if dilation > 1:
        scratch += [pltpu.VMEM((SWA_HEADS, t, LANES), F32)]
    tab_spec = pl.BlockSpec((None, 2, t, LANES), lambda i: (group, 0, 0, 0))
    return pl.pallas_call(
        functools.partial(_swa_kernel, dilation=dilation),
        grid=(b,),
        in_specs=[pl.BlockSpec((1, t, ws), lambda i: (i, 0, 0)), tab_spec, tab_spec],
        out_specs=[pl.BlockSpec((1, t, SWA_WIDTH), lambda i: (i, 0, 0)),
                   pl.BlockSpec((1, t, LANES), lambda i: (i, 0, 0))],
        out_shape=[jax.ShapeDtypeStruct((b, t, SWA_WIDTH), BF16),
                   jax.ShapeDtypeStruct((b, t, LANES), F32)],
        scratch_shapes=scratch,
        compiler_params=_params(1),
        name=f"swa_d{dilation}",
    )(s3, plain, swapped)


MERGE_ROWS = 512


def _merge_out_kernel(x_ref, h_ref, odn_ref, o0_ref, o1_ref, o2_ref, st0_ref, st1_ref, st2_ref,
                      wz_ref, wgdn_ref, wgswa_ref, wdn32_ref, wswa32_ref, wout32_ref, out_ref,
                      wzg_scr, wdn_ref, wswa_ref, wout_ref):
    _cast_weights_once([wz_ref, wgdn_ref, wgswa_ref], wzg_scr)
    _cast_weights_once([wdn32_ref], wdn_ref)
    _cast_weights_once([wswa32_ref], wswa_ref)
    _cast_weights_once([wout32_ref], wout_ref)
    zg_all = lax.dot_general(h_ref[...], wzg_scr[...], _NT, preferred_element_type=F32)
    outs = []
    for j in range(ROW_TILE // MERGE_ROWS):
        rows = slice(j * MERGE_ROWS, (j + 1) * MERGE_ROWS)
        zg = zg_all[rows, :]
        st0 = st0_ref[rows, :]
        st1 = st1_ref[rows, :]
        st2 = st2_ref[rows, :]
        m = jnp.maximum(jnp.maximum(st0, st1), st2)
        e0 = jnp.exp(st0 - m)
        e1 = jnp.exp(st1 - m)
        e2 = jnp.exp(st2 - m)
        sums = [pltpu.roll(st, LANES - SWA_HEADS, 1) for st in (st0, st1, st2)]
        inv = 1.0 / (e0 * sums[0] + e1 * sums[1] + e2 * sums[2])
        a0, a1, a2 = e0 * inv, e1 * inv, e2 * inv
        parts = []
        for h in range(SWA_HEADS):
            hs = slice(h * HEAD_DIM, (h + 1) * HEAD_DIM)
            col = slice(h, h + 1)
            oh = (a0[:, col] * o0_ref[rows, hs].astype(F32) + a1[:, col] * o1_ref[rows, hs].astype(F32)
                  + a2[:, col] * o2_ref[rows, hs].astype(F32))
            parts.append((oh * _silu(zg[:, hs])).astype(BF16))
        o_swa = jnp.concatenate(parts, axis=1)
        y_swa = _mxu(o_swa, wswa_ref[...])
        y_dn = _mxu(odn_ref[rows, :], wdn_ref[...])
        g_dn = zg[:, SWA_WIDTH:SWA_WIDTH + D_MODEL]
        g_swa = zg[:, SWA_WIDTH + D_MODEL:SWA_WIDTH + 2 * D_MODEL]
        merged = _sigmoid(g_dn) * y_dn + _sigmoid(g_swa) * y_swa
        outs.append((rows, x_ref[rows, :] + _mxu(merged.astype(BF16), wout_ref[...])))
    for rows, value in outs:
        out_ref[rows, :] = value


def _merge_out(x2, h, odn, o_list, stats_list, wt, zg_row_blocks, w_dn, w_swa, w_out):
    n = x2.shape[0]
    row = lambda w: pl.BlockSpec((ROW_TILE, w), lambda i: (i, 0))
    full = lambda a: pl.BlockSpec(a.shape, lambda i: (0, 0), pipeline_mode=pl.Buffered(1))
    nzg = sum(rows for _, rows in zg_row_blocks)
    return pl.pallas_call(
        _merge_out_kernel,
        grid=(n // ROW_TILE,),
        in_specs=[row(D_MODEL), row(D_MODEL), row(DN_WIDTH), row(SWA_WIDTH), row(SWA_WIDTH), row(SWA_WIDTH),
                  row(LANES), row(LANES), row(LANES)]
                 + _weight_specs(zg_row_blocks, D_MODEL)
                 + [full(w_dn), full(w_swa), full(w_out)],
        out_specs=row(D_MODEL),
        out_shape=jax.ShapeDtypeStruct((n, D_MODEL), F32),
        scratch_shapes=[pltpu.VMEM((nzg, D_MODEL), BF16)] + [pltpu.VMEM(w.shape, BF16) for w in (w_dn, w_swa, w_out)],
        compiler_params=_params_sequential(),
        name="merge_out",
    )(x2, h, odn, *o_list, *stats_list, wt, wt, wt, w_dn, w_swa, w_out)


def _rope_tables(t_len, q_norm_w, k_norm_w):
    j = np.arange(t_len)
    pos = np.stack([(j % (t_len // d)) * d + j // (t_len // d) for _, d in SWA_GROUPS]).astype(np.float64)
    inv_freq = ROPE_THETA ** (-np.arange(0, ROPE_DIM, 2, dtype=np.float64) / ROPE_DIM)
    ang = pos[:, :, None] * inv_freq[None, None, :]
    cos, sin = np.cos(ang), np.sin(ang)
    tail = (N_GROUPS, t_len, HEAD_DIM - ROPE_DIM)
    cos_t = jnp.asarray(np.concatenate([cos, cos, np.ones(tail)], axis=-1), F32)
    sin_t = jnp.asarray(np.concatenate([-sin, sin, np.zeros(tail)], axis=-1), F32)
    w = jnp.stack([q_norm_w.astype(F32) * (HEAD_DIM ** -0.5), k_norm_w.astype(F32)], axis=1)
    w_swapped = jnp.concatenate([w[..., ROPE_HALF:ROPE_DIM], w[..., :ROPE_HALF], w[..., ROPE_DIM:]], axis=-1)
    return w[:, :, None, :] * cos_t[:, None], w_swapped[:, :, None, :] * sin_t[:, None]


def kernel(x, norm_w, w_in, conv_w, dn_a_log, dn_dt_bias, dn_norm_w, q_norm_w, k_norm_w,
           w_branch_dn, w_branch_swa, w_out):
    b, t, d = x.shape
    n = b * t
    layer = 0
    wt = jnp.swapaxes(w_in[layer], 0, 1)
    c_z = 4 * DN_WIDTH
    c_q = c_z + 2 * DN_HEADS
    c_k = c_q + N_GROUPS * SWA_WIDTH
    c_v = c_k + N_GROUPS * SWA_WIDTH
    c_sz = c_v + N_GROUPS * SWA_WIDTH
    c_g = c_sz + SWA_WIDTH
    grp_blocks = lambda g: [(c0 + g * SWA_WIDTH, SWA_WIDTH) for c0 in (c_q, c_k, c_v)]
    zg_blocks = [(c_sz, SWA_WIDTH), (c_g, D_MODEL), (c_g + D_MODEL, D_MODEL)]
    pad_heads = lambda v: jnp.pad(v.astype(F32), (DN_HEADS, LANES - 2 * DN_HEADS))[None, :]
    alog_row = pad_heads(dn_a_log[layer])
    dt_row = pad_heads(dn_dt_bias[layer])

    x2 = x.reshape(n, d)
    h, a, bg, bgt = _norm_proj(x2, norm_w[layer][None, :], wt, c_z, conv_w[layer][:, 0, :], alog_row, dt_row, t)

    o_dn = _deltanet(a.reshape(b, t, 4 * DN_WIDTH), bg.reshape(b, t, LANES), bgt,
                     conv_w[layer][:, 0, :], dn_norm_w[layer][None, :])

    plain, swapped = _rope_tables(t, q_norm_w[layer], k_norm_w[layer])
    o_list, stats_list = [], []
    for g, (window, dilation) in enumerate(SWA_GROUPS):
        assert window // dilation == SWA_BLOCK
        s_g = _matmul(h, wt, grp_blocks(g), f"proj_swa{g}", dilation, t).reshape(b, t, 3 * SWA_WIDTH)
        o_g, stats_g = _swa_group(s_g, plain, swapped, g, dilation)
        o_list.append(o_g.reshape(n, SWA_WIDTH))
        stats_list.append(stats_g.reshape(n, LANES))

    out = _merge_out(x2, h, o_dn.reshape(n, DN_WIDTH), o_list, stats_list, wt, zg_blocks,
                     w_branch_dn[layer], w_branch_swa[layer], w_out[layer])
    return out.reshape(b, t, d)
```

```python
import functools
import math

import jax
import jax.numpy as jnp
import numpy as np
from jax import lax
from jax.experimental import pallas as pl
from jax.experimental.pallas import tpu as pltpu

D_MODEL = 1024
HEAD_DIM = 128
DN_HEADS = 4
DN_WIDTH = DN_HEADS * HEAD_DIM
CONV_WIDTH = 4
CHUNK = 64
SWA_GROUPS = ((128, 1), (512, 4), (2048, 16))
N_GROUPS = 3
SWA_HEADS = 4
SWA_WIDTH = SWA_HEADS * HEAD_DIM
SWA_BLOCK = 128
ROPE_DIM = HEAD_DIM // 4
ROPE_HALF = ROPE_DIM // 2
ROPE_THETA = 500000.0
EPS = 1e-6
NEG_INF = -1e30

LANES = 128
VMEM_LIMIT_BYTES = 48 * 1024 * 1024
DN_VMEM_LIMIT_BYTES = 56 * 1024 * 1024
ROW_TILE = 512
NORM_ROW_TILE = 512
PROJ_ROW_TILE = 1024
BG_ROWS = 16

F32 = jnp.float32
BF16 = jnp.bfloat16
_NT = (((1,), (1,)), ((), ()))
_TN = (((0,), (0,)), ((), ()))


def _sigmoid(x):
    return 0.5 * jnp.tanh(0.5 * x) + 0.5


def _silu(x):
    return x * _sigmoid(x)


def _softplus(x):
    return jnp.maximum(x, 0.0) + jnp.log(1.0 + jnp.exp(-jnp.abs(x)))


def _params(n_axes):
    return pltpu.CompilerParams(dimension_semantics=("parallel",) * n_axes,
                                vmem_limit_bytes=VMEM_LIMIT_BYTES)


def _params_sequential():
    return pltpu.CompilerParams(dimension_semantics=("arbitrary",), vmem_limit_bytes=VMEM_LIMIT_BYTES)


def _mxu(a, b):
    return jnp.dot(a, b, preferred_element_type=F32)


def _split3(x):
    hi = x.astype(BF16)
    r1 = x - hi.astype(F32)
    mid = r1.astype(BF16)
    lo = (r1 - mid.astype(F32)).astype(BF16)
    return hi, mid, lo


def _beta_and_log_decay(ab, a_log, dt_bias, head_index):
    beta = _sigmoid(ab)
    g = -jnp.exp(a_log) * _softplus(ab + dt_bias)
    return jnp.where(head_index < DN_HEADS, beta, g)


def _weight_specs(row_blocks, k):
    return [pl.BlockSpec((pl.Element(rows), pl.Element(k)), lambda i, start=start: (start, 0),
                         pipeline_mode=pl.Buffered(1))
            for start, rows in row_blocks]


def _cast_weights_once(w_refs, w_scr):
    @pl.when(pl.program_id(0) == 0)
    def _():
        blocks = [w_ref[...] for w_ref in w_refs]
        pad = w_scr.shape[0] - sum(blk.shape[0] for blk in blocks)
        if pad:
            blocks.append(jnp.zeros((pad, w_scr.shape[1]), F32))
        w_scr[...] = (jnp.concatenate(blocks, axis=0) if len(blocks) > 1 else blocks[0]).astype(BF16)


CONV_TILE = 256
CONV_HALO = 8
CONV_ROWS = 64
CONV_EARLY_COLS = 2 * DN_WIDTH


def _norm_proj_kernel(x_ref, nw_ref, wa_ref, wab_ref, cw_ref, alog_ref, dt_ref, alogt_ref, dtt_ref,
                      h_ref, a_ref, bg_ref, bgt_ref, wa_scr, wab_scr, tail_ref, res_a, res_b, *, tiles_per_seq):
    res_scr = (res_a, res_b)
    tile_rows = x_ref.shape[0]
    _cast_weights_once([wa_ref], wa_scr)
    _cast_weights_once([wab_ref], wab_scr)
    x = x_ref[...]
    h = (x * lax.rsqrt(jnp.mean(x * x, axis=-1, keepdims=True) + EPS)) * nw_ref[...]
    hb = h.astype(BF16)
    h_ref[...] = hb
    n_conv = CONV_EARLY_COLS
    seq_start = pl.program_id(0) % tiles_per_seq == 0
    tail = jnp.where(seq_start, 0.0, tail_ref[...])
    stores = []
    tiles = list(range(0, n_conv, CONV_TILE))
    base = pl.multiple_of((pl.program_id(0) >> 20) * CONV_HALO, CONV_HALO)

    def project(k):
        c0 = tiles[k]
        buf = res_scr[k % 2]
        buf[pl.ds(base, CONV_HALO), :] = tail[:, c0:c0 + CONV_TILE]
        buf[pl.ds(base + CONV_HALO, tile_rows), :] = lax.dot_general(hb, wa_scr[c0:c0 + CONV_TILE, :], _NT,
                                                                    preferred_element_type=F32)

    if tiles:
        project(0)
    for k, c0 in enumerate(tiles):
        cols = slice(c0, c0 + CONV_TILE)
        if k + 1 < len(tiles):
            project(k + 1)
        buf = res_scr[k % 2]
        acts = []
        for rb in range(0, tile_rows, CONV_ROWS):
            blk = buf[pl.ds(base + rb, CONV_HALO + CONV_ROWS), :]
            conv = None
            for j in range(CONV_WIDTH):
                lo = CONV_HALO - (CONV_WIDTH - 1) + j
                term = cw_ref[j:j + 1, cols] * blk[lo:lo + CONV_ROWS, :]
                conv = term if conv is None else conv + term
            acts.append(_silu(conv).astype(BF16))
        stores.append((a_ref, (slice(None), cols), jnp.concatenate(acts, axis=0)))
        stores.append((tail_ref, (slice(None), cols), buf[pl.ds(base + tile_rows, CONV_HALO), :]))
    z = lax.dot_general(hb, wa_scr[n_conv:, :], _NT, preferred_element_type=F32)
    stores.append((a_ref, (slice(None), slice(n_conv, None)), z.astype(BF16)))
    for ref, idx, value in stores:
        ref[idx] = value
    ab = lax.dot_general(hb, wab_scr[...], _NT, preferred_element_type=F32)
    bg_ref[...] = _beta_and_log_decay(ab, alog_ref[...], dt_ref[...],
                                      lax.broadcasted_iota(jnp.int32, ab.shape, 1))
    abt = lax.dot_general(wab_scr[0:BG_ROWS, :], hb, _NT, preferred_element_type=F32)
    bgt_ref[...] = _beta_and_log_decay(abt, alogt_ref[...], dtt_ref[...],
                                       lax.broadcasted_iota(jnp.int32, abt.shape, 0))


def _norm_proj(x2, norm_w, wt, na, conv_w2, alog_row, dt_row, seq_len):
    n = x2.shape[0]
    tile = NORM_ROW_TILE
    assert seq_len % tile == 0 and tile % CONV_ROWS == 0 and CONV_EARLY_COLS % CONV_TILE == 0
    row = lambda i: (i, 0)
    fixed = lambda i: (0, 0)
    pad_col = lambda v: v[0, :BG_ROWS][:, None]
    return pl.pallas_call(
        functools.partial(_norm_proj_kernel, tiles_per_seq=seq_len // tile),
        grid=(n // tile,),
        in_specs=[pl.BlockSpec((tile, D_MODEL), row),
                  pl.BlockSpec((1, D_MODEL), fixed)]
                 + _weight_specs([(0, na), (na, 2 * DN_HEADS)], D_MODEL)
                 + [pl.BlockSpec((CONV_WIDTH, 3 * DN_WIDTH), fixed),
                    pl.BlockSpec((1, LANES), fixed),
                    pl.BlockSpec((1, LANES), fixed),
                    pl.BlockSpec((BG_ROWS, 1), fixed),
                    pl.BlockSpec((BG_ROWS, 1), fixed)],
        out_specs=[pl.BlockSpec((tile, D_MODEL), row),
                   pl.BlockSpec((tile, na), row),
                   pl.BlockSpec((tile, LANES), row),
                   pl.BlockSpec((BG_ROWS, tile), lambda i: (0, i))],
        out_shape=[jax.ShapeDtypeStruct((n, D_MODEL), BF16),
                   jax.ShapeDtypeStruct((n, na), BF16),
                   jax.ShapeDtypeStruct((n, LANES), F32),
                   jax.ShapeDtypeStruct((BG_ROWS, n), F32)],
        scratch_shapes=[pltpu.VMEM((na, D_MODEL), BF16), pltpu.VMEM((LANES, D_MODEL), BF16),
                        pltpu.VMEM((CONV_HALO, max(CONV_EARLY_COLS, LANES)), F32),
                        pltpu.VMEM((CONV_HALO + tile, CONV_TILE), F32),
                        pltpu.VMEM((CONV_HALO + tile, CONV_TILE), F32)],
        compiler_params=_params_sequential(),
        name="norm_proj",
    )(x2, norm_w, wt, wt, conv_w2, alog_row, dt_row, pad_col(alog_row), pad_col(dt_row))


REGROUP_STRIDE = 4


def _matmul_kernel(h_ref, *refs, dilation):
    n_scratch = 1 if dilation == 1 else 3
    w_refs, o_ref, w_scr = refs[:-n_scratch - 1], refs[-n_scratch - 1], refs[-n_scratch]
    _cast_weights_once(w_refs, w_scr)
    if dilation == 1:
        o_ref[...] = lax.dot_general(h_ref[...], w_scr[...], _NT, preferred_element_type=F32).astype(o_ref.dtype)
        return
    stage_ref, stage2_ref = refs[-2], refs[-1]
    s1 = min(dilation, REGROUP_STRIDE)
    s2 = dilation // s1
    rows1 = h_ref.shape[0] // s1
    per = h_ref.shape[0] // dilation
    col = 0
    for w_ref in w_refs:
        width = w_ref.shape[0]
        res = lax.dot_general(h_ref[...], w_scr[col:col + width, :], _NT, preferred_element_type=F32)
        slabs = range(col // LANES, (col + width) // LANES)
        for c in slabs:
            stage_ref[c] = res[:, (c * LANES - col):(c * LANES - col) + LANES]
        for c in slabs:
            for a in range(s1):
                first = stage_ref[c, pl.ds(a, rows1, stride=s1), :]
                if s2 == 1:
                    o_ref[0, a, :, c * LANES:(c + 1) * LANES] = first.astype(o_ref.dtype)
                else:
                    stage2_ref[c, a * rows1:(a + 1) * rows1, :] = first
        if s2 > 1:
            for c in slabs:
                for a in range(s1):
                    for b in range(s2):
                        o_ref[0, b * s1 + a, :, c * LANES:(c + 1) * LANES] = (
                            stage2_ref[c, pl.ds(a * rows1 + b, per, stride=s2), :].astype(o_ref.dtype))
        col += width


def _matmul(h, wt, row_blocks, name, dilation=1, seq_len=None):
    n, k = h.shape
    nc = sum(rows for _, rows in row_blocks)
    scratch = [pltpu.VMEM((nc, k), BF16)]
    tile = PROJ_ROW_TILE
    if dilation == 1:
        out_spec = pl.BlockSpec((tile, nc), lambda i: (i, 0))
        out_shape = jax.ShapeDtypeStruct((n, nc), BF16)
    else:
        tiles = seq_len // tile
        per = tile // dilation
        assert seq_len % tile == 0 and tile % dilation == 0 and per % 16 == 0 and nc % LANES == 0
        out_spec = pl.BlockSpec((1, dilation, per, nc), lambda i: (i // tiles, 0, i % tiles, 0))
        out_shape = jax.ShapeDtypeStruct((n // seq_len, dilation, seq_len // dilation, nc), BF16)
        assert dilation % min(dilation, REGROUP_STRIDE) == 0
        scratch += [pltpu.VMEM((nc // LANES, tile, LANES), F32)] * 2
    return pl.pallas_call(
        functools.partial(_matmul_kernel, dilation=dilation),
        grid=(n // tile,),
        in_specs=[pl.BlockSpec((tile, k), lambda i: (i, 0))] + _weight_specs(row_blocks, k),
        out_specs=out_spec,
        out_shape=out_shape,
        scratch_shapes=scratch,
        compiler_params=_params_sequential(),
        name=name,
    )(h, *([wt] * len(row_blocks)))


DN_GROUP = 8
DN_ROWS = DN_GROUP * CHUNK
DN_HALO = 16


def _deltanet_kernel(a_ref, bg_ref, bgt_ref, cw_ref, nw_ref, o_ref,
                     u_ref, wq_ref, kd_ref, ai_ref, eg_ref):
    t_len = a_ref.shape[1]
    n_groups = t_len // DN_ROWS
    assert 2 * CHUNK == LANES and DN_GROUP % 2 == 0
    ii = lax.broadcasted_iota(jnp.int32, (CHUNK, LANES), 0)
    lane_pair = lax.broadcasted_iota(jnp.int32, (CHUNK, LANES), 1)
    jj = lane_pair & (CHUNK - 1)
    second = lane_pair >= CHUNK
    causal = ii >= jj
    strict = ii > jj
    eye_f = (ii == jj).astype(F32)
    second_rows = lax.broadcasted_iota(jnp.int32, (LANES, LANES), 0) >= CHUNK
    second_cols = lax.broadcasted_iota(jnp.int32, (LANES, LANES), 1) >= CHUNK
    same_half = second_rows == second_cols

    def block_diag(pair):
        return jnp.where(same_half, jnp.concatenate([pair, pair], axis=0), 0)

    bi = lax.broadcasted_iota(jnp.int32, (DN_ROWS, DN_ROWS), 0)
    bj = lax.broadcasted_iota(jnp.int32, (DN_ROWS, DN_ROWS), 1)
    shift = int(math.log2(CHUNK))
    same_chunk = jnp.right_shift(bi, shift) == jnp.right_shift(bj, shift)
    cum_lower = (same_chunk & (bi >= bj)).astype(BF16)
    cum_upper = (same_chunk & (bi <= bj)).astype(BF16)
    problems = [(cp, h) for cp in range(DN_GROUP // 2) for h in range(DN_HEADS)]
    heads = range(DN_HEADS)
    hsl = lambda h: slice(h * HEAD_DIM, (h + 1) * HEAD_DIM)
    cs = lambda c: slice(c * CHUNK, (c + 1) * CHUNK)
    ones_b = jnp.ones((HEAD_DIM, HEAD_DIM), BF16)
    sel_src = lax.broadcasted_iota(jnp.int32, (LANES, DN_WIDTH), 0)
    sel_head = jnp.right_shift(lax.broadcasted_iota(jnp.int32, (LANES, DN_WIDTH), 1), int(math.log2(HEAD_DIM)))
    sel_beta = (sel_src == sel_head).astype(BF16)
    sel_g = (sel_src == sel_head + DN_HEADS).astype(BF16)
    lane = lax.broadcasted_iota(jnp.int32, (DN_ROWS, LANES), 1)
    g_lanes = (lane >= DN_HEADS) & (lane < 2 * DN_HEADS)

    pending = []

    def defer(ref, idx, value):
        pending.append((ref, idx, value))

    def flush():
        for ref, idx, value in pending:
            ref[idx] = value
        pending.clear()

    def conv_silu(gi, r0, col0):
        cols = slice(col0, col0 + HEAD_DIM)
        rp = pl.multiple_of(jnp.maximum(r0 - DN_HALO, 0), DN_HALO)
        cur = a_ref[0, pl.ds(r0, DN_ROWS), cols].astype(F32)
        prev = a_ref[0, pl.ds(rp, DN_HALO), cols].astype(F32)
        prev = jnp.where(gi > 0, prev, 0.0)
        xw = jnp.concatenate([prev, cur], axis=0)
        conv = None
        for j in range(CONV_WIDTH):
            lo = DN_HALO - (CONV_WIDTH - 1) + j
            term = cw_ref[j:j + 1, cols] * xw[lo:lo + DN_ROWS, :]
            conv = term if conv is None else conv + term
        return _silu(conv)

    def elementwise(gi, handoff):
        r0 = pl.multiple_of(gi * DN_ROWS, DN_ROWS)
        def act_of(h):
            tiles = []
            for base in (0, DN_WIDTH, 2 * DN_WIDTH):
                col0 = base + h * HEAD_DIM
                if col0 < CONV_EARLY_COLS:
                    tiles.append(a_ref[0, pl.ds(r0, DN_ROWS), col0:col0 + HEAD_DIM].astype(F32))
                else:
                    tiles.append(conv_silu(gi, r0, col0))
            return tiles
        acts = []

        def head_norms(h):
            qh, kh, vh = act_of(h)
            qn_h = qh * (lax.rsqrt(_mxu((qh * qh).astype(BF16), ones_b) + EPS) * (HEAD_DIM ** -0.5))
            kn_h = kh * lax.rsqrt(_mxu((kh * kh).astype(BF16), ones_b) + EPS)
            acts.append((qn_h, kn_h, vh))

        head_norms(0)
        yield
        bg = bg_ref[0, pl.ds(r0, DN_ROWS), :]
        gc_all = sum(_mxu(cum_lower, piece) for piece in _split3(bg))
        gct_all = sum(_mxu(piece, cum_upper) for piece in _split3(bgt_ref[:, pl.ds(r0, DN_ROWS)]))
        gl_all = jnp.concatenate([jnp.broadcast_to(gc_all[(c + 1) * CHUNK - 1:(c + 1) * CHUNK, :], (CHUNK, LANES))
                                  for c in range(DN_GROUP)], axis=0)
        eg_all = jnp.exp(gl_all)
        for c in range(DN_GROUP):
            defer(eg_ref, (gi * DN_GROUP + c,), eg_all[c * CHUNK:c * CHUNK + 8, :])
        gc_m = jnp.where(g_lanes, gc_all, 0.0)
        gl_m = jnp.where(g_lanes, gl_all, 0.0)
        beta_rep = _mxu(bg.astype(BF16), sel_beta)
        egc_rep = _mxu(jnp.exp(gc_m).astype(BF16), sel_g)
        ekd_rep = _mxu(jnp.exp(gl_m - gc_m).astype(BF16), sel_g)

        for h in range(1, DN_HEADS):
            head_norms(h)
            yield

        qn, knb, kb, rhs = [], [], [], []
        for h in heads:
            qn_h, kn_h, vh = acts[h]
            kb_h = kn_h * beta_rep[:, hsl(h)]
            defer(kd_ref, (h, pl.ds(r0, DN_ROWS), slice(None)), (kn_h * ekd_rep[:, hsl(h)]).astype(BF16))
            qg_h = (qn_h * egc_rep[:, hsl(h)]).astype(BF16)
            for c in range(DN_GROUP):
                w0 = pl.multiple_of(2 * (r0 + c * CHUNK), 2 * CHUNK)
                defer(wq_ref, (h, pl.ds(w0 + CHUNK, CHUNK), slice(None)), qg_h[cs(c)])
            qn.append(qn_h)
            knb.append(kn_h.astype(BF16))
            kb.append(kb_h)
            rhs.append(jnp.concatenate([vh * beta_rep[:, hsl(h)], kb_h * egc_rep[:, hsl(h)]], axis=1).astype(BF16))
            yield

        zeros_k = jnp.zeros((CHUNK, HEAD_DIM), BF16)
        qk = []
        for cp, h in problems:
            c1, c2 = cs(2 * cp), cs(2 * cp + 1)
            lhs = jnp.concatenate([jnp.concatenate([qn[h][c1], qn[h][c2]], axis=1),
                                   jnp.concatenate([kb[h][c1], kb[h][c2]], axis=1)], axis=0).astype(BF16)
            keys = jnp.concatenate([jnp.concatenate([knb[h][c1], zeros_k], axis=1),
                                    jnp.concatenate([zeros_k, knb[h][c2]], axis=1)], axis=0)
            qk.append(lax.dot_general(lhs, keys, _NT, preferred_element_type=F32))
        decay = []
        for cp, h in problems:
            col = slice(DN_HEADS + h, DN_HEADS + h + 1)
            gc_col = jnp.where(second, gc_all[cs(2 * cp + 1), col], gc_all[cs(2 * cp), col])
            diff = gc_col - gct_all[DN_HEADS + h:DN_HEADS + h + 1, 2 * cp * CHUNK:(2 * cp + 2) * CHUNK]
            decay.append(jnp.where(causal, jnp.exp(jnp.where(causal, diff, 0.0)), 0.0))
        for i, (cp, h) in enumerate(problems):
            a_intra = jnp.where(causal, qk[i][:CHUNK] * decay[i], 0.0)
            defer(ai_ref, (h, gi * (DN_GROUP // 2) + cp), a_intra.astype(BF16))
        handoff["neg_lower"] = [-jnp.where(strict, qk[i][CHUNK:] * decay[i], 0.0) for i in range(len(problems))]
        handoff["rhs"] = rhs

    def solve(gi, handoff):
        r0 = pl.multiple_of(gi * DN_ROWS, DN_ROWS)
        ps = handoff["neg_lower"]
        rhs = handoff["rhs"]
        tms = [eye_f + p for p in ps]
        pbs = [p.astype(BF16) for p in ps]
        ps = [_mxu(pb, block_diag(pb)) for pb in pbs]
        yield
        for _ in range(shift - 2):
            pbs = [p.astype(BF16) for p in ps]
            both = [_mxu(jnp.concatenate([pb, tm.astype(BF16)], axis=0), block_diag(pb)) for pb, tm in zip(pbs, tms)]
            ps = [r[:CHUNK] for r in both]
            tms = [tm + r[CHUNK:] for tm, r in zip(tms, both)]
            yield
        tms = [tm + _mxu(tm.astype(BF16), block_diag(p.astype(BF16))) for tm, p in zip(tms, ps)]
        zeros_r = jnp.zeros((CHUNK, 2 * HEAD_DIM), BF16)
        uw = []
        for i, (cp, h) in enumerate(problems):
            both = jnp.concatenate([jnp.concatenate([rhs[h][cs(2 * cp)], zeros_r], axis=1),
                                    jnp.concatenate([zeros_r, rhs[h][cs(2 * cp + 1)]], axis=1)], axis=0)
            uw.append(_mxu(tms[i].astype(BF16), both))
        for i, (cp, h) in enumerate(problems):
            for half in range(2):
                c = 2 * cp + half
                part = uw[i][:, half * 2 * HEAD_DIM:(half + 1) * 2 * HEAD_DIM]
                defer(u_ref, (h, pl.ds(r0 + c * CHUNK, CHUNK), slice(None)), part[:, :HEAD_DIM])
                w0 = pl.multiple_of(2 * (r0 + c * CHUNK), 2 * CHUNK)
                defer(wq_ref, (h, pl.ds(w0, CHUNK), slice(None)), part[:, HEAD_DIM:].astype(BF16))

    def recur(g, c_in_group, state):
        c = g * DN_GROUP + c_in_group
        pair = g * (DN_GROUP // 2) + c_in_group // 2
        zeros_v = jnp.zeros((CHUNK, HEAD_DIM), BF16)
        pad = (lambda vn_h: jnp.concatenate([vn_h, zeros_v], axis=0)) if c_in_group % 2 == 0 else (
            lambda vn_h: jnp.concatenate([zeros_v, vn_h], axis=0))
        r0 = pl.multiple_of(c * CHUNK, CHUNK)
        rows = pl.ds(r0, CHUNK)
        wrows = pl.ds(pl.multiple_of(2 * r0, 2 * CHUNK), 2 * CHUNK)
        eg = eg_ref[c]
        ws = [_mxu(wq_ref[h, wrows, :], state[h].astype(BF16)) for h in heads]
        vn = [(u_ref[h, rows, :] - ws[h][:CHUNK]).astype(BF16) for h in heads]
        new_state = [state[h] * eg[0:1, DN_HEADS + h:DN_HEADS + h + 1]
                     + lax.dot_general(kd_ref[h, rows, :], vn[h], _TN, preferred_element_type=F32) for h in heads]
        o = [ws[h][CHUNK:] + _mxu(ai_ref[h, pair], pad(vn[h])) for h in heads]
        ms = [_mxu((x * x).astype(BF16), ones_b) * (1.0 / HEAD_DIM) for x in o]
        for h in heads:
            on = o[h] * lax.rsqrt(ms[h] + EPS) * nw_ref[...]
            z = a_ref[0, rows, 3 * DN_WIDTH + h * HEAD_DIM:3 * DN_WIDTH + (h + 1) * HEAD_DIM].astype(F32)
            defer(o_ref, (0, rows, hsl(h)), (on * _silu(z)).astype(o_ref.dtype))
        return new_state

    def pipeline_step(g_prep, g_recur, state):
        def stages():
            if g_prep is not None:
                handoff = {}
                yield from elementwise(g_prep, handoff)
                yield
                yield from solve(g_prep, handoff)

        work = stages()
        for c in range(DN_GROUP):
            if g_recur is not None:
                state = recur(g_recur, c, state)
            next(work, None)
        for _ in work:
            pass
        flush()
        return state

    zero = jnp.zeros((HEAD_DIM, HEAD_DIM), F32)
    state = pipeline_step(0, None, [zero] * DN_HEADS)
    state = list(lax.fori_loop(
        1, n_groups, lambda gi, st: tuple(pipeline_step(gi, gi - 1, list(st))), tuple(state)))
    pipeline_step(None, n_groups - 1, state)


def _deltanet(a3, bg3, bgt, conv_w2, dn_norm_row):
    b, t, wa = a3.shape
    nt = bgt.shape[0]
    return pl.pallas_call(
        _deltanet_kernel,
        grid=(b,),
        in_specs=[pl.BlockSpec((1, t, wa), lambda i: (i, 0, 0)),
                  pl.BlockSpec((1, t, LANES), lambda i: (i, 0, 0)),
                  pl.BlockSpec((nt, t), lambda i: (0, i)),
                  pl.BlockSpec((CONV_WIDTH, 3 * DN_WIDTH), lambda i: (0, 0)),
                  pl.BlockSpec((1, HEAD_DIM), lambda i: (0, 0))],
        out_specs=pl.BlockSpec((1, t, DN_WIDTH), lambda i: (i, 0, 0)),
        out_shape=jax.ShapeDtypeStruct((b, t, DN_WIDTH), BF16),
        scratch_shapes=[pltpu.VMEM((DN_HEADS, t, HEAD_DIM), F32),
                        pltpu.VMEM((DN_HEADS, 2 * t, HEAD_DIM), BF16),
                        pltpu.VMEM((DN_HEADS, t, HEAD_DIM), BF16),
                        pltpu.VMEM((DN_HEADS, t // LANES, CHUNK, LANES), BF16),
                        pltpu.VMEM((t // CHUNK, 8, LANES), F32)],
        compiler_params=pltpu.CompilerParams(dimension_semantics=("parallel",),
                                             vmem_limit_bytes=DN_VMEM_LIMIT_BYTES),
        name="deltanet",
    )(a3, bg3, bgt, conv_w2, dn_norm_row)


SWA_PREP_ROWS = 1024
SWA_UNITS = 4


def _swa_kernel(s_ref, plain_ref, swapped_ref, o_ref, stats_ref, qs_ref, ks_ref, bias_ref, *dilated_scratch, dilation):
    t_len = s_ref.shape[1]
    d = dilation
    seq = t_len // d
    nblk = seq // SWA_BLOCK
    heads = range(SWA_HEADS)
    hsl = lambda h: slice(h * HEAD_DIM, (h + 1) * HEAD_DIM)
    if d > 1:
        (ostage_ref,) = dilated_scratch
    mean_b = jnp.full((HEAD_DIM, HEAD_DIM), 1.0 / HEAD_DIM, BF16)
    src_lane = lax.broadcasted_iota(jnp.int32, (HEAD_DIM, HEAD_DIM), 0)
    dst_lane = lax.broadcasted_iota(jnp.int32, (HEAD_DIM, HEAD_DIM), 1)
    swap_b = (((dst_lane < ROPE_HALF) & (src_lane == dst_lane + ROPE_HALF))
              | ((dst_lane >= ROPE_HALF) & (dst_lane < ROPE_DIM) & (src_lane == dst_lane - ROPE_HALF))
              ).astype(BF16)

    def prep(n, carry):
        r0 = pl.multiple_of(n * SWA_PREP_ROWS, SWA_PREP_ROWS)
        rows = pl.ds(r0, SWA_PREP_ROWS)
        for which, dst in ((0, qs_ref), (1, ks_ref)):
            plain = plain_ref[which, rows, :]
            swapped_tab = swapped_ref[which, rows, :]
            xb = [s_ref[0, rows, which * SWA_WIDTH + h * HEAD_DIM:which * SWA_WIDTH + (h + 1) * HEAD_DIM]
                  for h in heads]
            ms = [_mxu(x * x, mean_b) for x in xb]
            sw = [_mxu(x, swap_b) for x in xb]
            ys = [(x.astype(F32) * plain + s * swapped_tab) * lax.rsqrt(m + EPS) for x, s, m in zip(xb, sw, ms)]
            for h in heads:
                dst[rows, hsl(h)] = ys[h].astype(BF16)
        return carry

    lax.fori_loop(0, t_len // SWA_PREP_ROWS, prep, 0)

    qi = lax.broadcasted_iota(jnp.int32, (SWA_BLOCK, 2 * SWA_BLOCK), 0)
    kj = lax.broadcasted_iota(jnp.int32, (SWA_BLOCK, 2 * SWA_BLOCK), 1)
    bias_ref[0] = jnp.where(kj <= qi, 0.0, NEG_INF).astype(F32)
    bias_ref[1] = jnp.where((kj >= qi) & (kj <= qi + SWA_BLOCK), 0.0, NEG_INF).astype(F32)
    lane = lax.broadcasted_iota(jnp.int32, (SWA_BLOCK, LANES), 1)

    def attend(units):
        jobs = [(u, h) for u in range(len(units)) for h in heads]
        qrows = [pl.ds(q0, SWA_BLOCK) for q0, _, _, _, _ in units]
        krows = [pl.ds(k0, width) for _, k0, width, _, _ in units]
        s = [lax.dot_general(qs_ref[qrows[u], hsl(h)], ks_ref[krows[u], hsl(h)], _NT,
                             preferred_element_type=F32) + units[u][3] for u, h in jobs]
        m = [jnp.max(x, axis=-1, keepdims=True) for x in s]
        p = [jnp.exp(x - mx) for x, mx in zip(s, m)]
        l = [jnp.sum(x, axis=-1, keepdims=True) for x in p]
        v = [s_ref[0, krows[u], 2 * SWA_WIDTH + h * HEAD_DIM:2 * SWA_WIDTH + (h + 1) * HEAD_DIM] for u, h in jobs]
        o = [_mxu(x.astype(BF16), vh) for x, vh in zip(p, v)]
        for u in range(len(units)):
            stats_tile = jnp.zeros((SWA_BLOCK, LANES), F32)
            for h in heads:
                j = u * SWA_HEADS + h
                stats_tile = jnp.where(lane == h, m[j], jnp.where(lane == SWA_HEADS + h, l[j], stats_tile))
            t0 = units[u][4]
            if d == 1:
                for h in heads:
                    o_ref[0, qrows[u], hsl(h)] = o[u * SWA_HEADS + h].astype(o_ref.dtype)
                stats_ref[0, qrows[u], :] = stats_tile
            else:
                trows = pl.ds(t0, SWA_BLOCK, stride=d)
                for h in heads:
                    ostage_ref[h, trows, :] = o[u * SWA_HEADS + h]
                stats_ref[0, trows, :] = stats_tile

    if nblk == 1:
        causal_bias = bias_ref[0][:, :SWA_BLOCK]

        def body(i, carry):
            units = []
            for u in range(SWA_UNITS):
                r = i * SWA_UNITS + u
                q0 = pl.multiple_of(r * seq, SWA_BLOCK)
                units.append((q0, q0, SWA_BLOCK, causal_bias, r))
            attend(units)
            return carry

        lax.fori_loop(0, d // SWA_UNITS, body, 0, unroll=True)
    else:
        pairs = nblk // SWA_UNITS

        def body(i, carry):
            r = i // pairs
            pp = i % pairs
            units = []
            for u in range(SWA_UNITS):
                n = pp * SWA_UNITS + u
                q0 = pl.multiple_of(r * seq + n * SWA_BLOCK, SWA_BLOCK)
                if u == 0:
                    first = jnp.asarray(pp == 0).astype(jnp.int32)
                    k0 = pl.multiple_of(q0 - (1 - first) * SWA_BLOCK, SWA_BLOCK)
                    bias = bias_ref[1 - first]
                else:
                    k0 = pl.multiple_of(q0 - SWA_BLOCK, SWA_BLOCK)
                    bias = bias_ref[1]
                units.append((q0, k0, 2 * SWA_BLOCK, bias, r + d * n * SWA_BLOCK))
            attend(units)
            return carry

        lax.fori_loop(0, d * pairs, body, 0, unroll=True)

    if d > 1:
        def emit(n, carry):
            rows = pl.ds(pl.multiple_of(n * SWA_PREP_ROWS, SWA_PREP_ROWS), SWA_PREP_ROWS)
            for h in heads:
                o_ref[0, rows, hsl(h)] = ostage_ref[h, rows, :].astype(o_ref.dtype)
            return carry
        lax.fori_loop(0, t_len // SWA_PREP_ROWS, emit, 0)


def _swa_group(s3, plain, swapped, group, dilation):
    b, t, ws = s3.shape
    assert (t // dilation) % SWA_BLOCK == 0 and t % SWA_PREP_ROWS == 0
    nblk = t // dilation // SWA_BLOCK
    assert (dilation % SWA_UNITS == 0) if nblk == 1 else (nblk % SWA_UNITS == 0)
    scratch = [pltpu.VMEM((t, SWA_WIDTH), BF16), pltpu.VMEM((t, SWA_WIDTH), BF16),
               pltpu.VMEM((2, SWA_BLOCK, 2 * SWA_BLOCK), F32)]
    if dilation > 1:
        scratch += [pltpu.VMEM((SWA_HEADS, t, LANES), F32)]
    tab_spec = pl.BlockSpec((None, 2, t, LANES), lambda i: (group, 0, 0, 0))
    return pl.pallas_call(
        functools.partial(_swa_kernel, dilation=dilation),
        grid=(b,),
        in_specs=[pl.BlockSpec((1, t, ws), lambda i: (i, 0, 0)), tab_spec, tab_spec],
        out_specs=[pl.BlockSpec((1, t, SWA_WIDTH), lambda i: (i, 0, 0)),
                   pl.BlockSpec((1, t, LANES), lambda i: (i, 0, 0))],
        out_shape=[jax.ShapeDtypeStruct((b, t, SWA_WIDTH), BF16),
                   jax.ShapeDtypeStruct((b, t, LANES), F32)],
        scratch_shapes=scratch,
        compiler_params=_params(1),
        name=f"swa_d{dilation}",
    )(s3, plain, swapped)


MERGE_ROWS = 512


def _merge_out_kernel(x_ref, h_ref, odn_ref, o0_ref, o1_ref, o2_ref, st0_ref, st1_ref, st2_ref,
                      wz_ref, wgdn_ref, wgswa_ref, wdn32_ref, wswa32_ref, wout32_ref, out_ref,
                      wzg_scr, wdn_ref, wswa_ref, wout_ref):
    _cast_weights_once([wz_ref, wgdn_ref, wgswa_ref], wzg_scr)
    _cast_weights_once([wdn32_ref], wdn_ref)
    _cast_weights_once([wswa32_ref], wswa_ref)
    _cast_weights_once([wout32_ref], wout_ref)
    zg_all = lax.dot_general(h_ref[...], wzg_scr[...], _NT, preferred_element_type=F32)
    outs = []
    for j in range(ROW_TILE // MERGE_ROWS):
        rows = slice(j * MERGE_ROWS, (j + 1) * MERGE_ROWS)
        zg = zg_all[rows, :]
        st0 = st0_ref[rows, :]
        st1 = st1_ref[rows, :]
        st2 = st2_ref[rows, :]
        m = jnp.maximum(jnp.maximum(st0, st1), st2)
        e0 = jnp.exp(st0 - m)
        e1 = jnp.exp(st1 - m)
        e2 = jnp.exp(st2 - m)
        sums = [pltpu.roll(st, LANES - SWA_HEADS, 1) for st in (st0, st1, st2)]
        inv = 1.0 / (e0 * sums[0] + e1 * sums[1] + e2 * sums[2])
        a0, a1, a2 = e0 * inv, e1 * inv, e2 * inv
        parts = []
        for h in range(SWA_HEADS):
            hs = slice(h * HEAD_DIM, (h + 1) * HEAD_DIM)
            col = slice(h, h + 1)
            oh = (a0[:, col] * o0_ref[rows, hs].astype(F32) + a1[:, col] * o1_ref[rows, hs].astype(F32)
                  + a2[:, col] * o2_ref[rows, hs].astype(F32))
            parts.append((oh * _silu(zg[:, hs])).astype(BF16))
        o_swa = jnp.concatenate(parts, axis=1)
        y_swa = _mxu(o_swa, wswa_ref[...])
        y_dn = _mxu(odn_ref[rows, :], wdn_ref[...])
        g_dn = zg[:, SWA_WIDTH:SWA_WIDTH + D_MODEL]
        g_swa = zg[:, SWA_WIDTH + D_MODEL:SWA_WIDTH + 2 * D_MODEL]
        merged = _sigmoid(g_dn) * y_dn + _sigmoid(g_swa) * y_swa
        outs.append((rows, x_ref[rows, :] + _mxu(merged.astype(BF16), wout_ref[...])))
    for rows, value in outs:
        out_ref[rows, :] = value


def _merge_out(x2, h, odn, o_list, stats_list, wt, zg_row_blocks, w_dn, w_swa, w_out):
    n = x2.shape[0]
    row = lambda w: pl.BlockSpec((ROW_TILE, w), lambda i: (i, 0))
    full = lambda a: pl.BlockSpec(a.shape, lambda i: (0, 0), pipeline_mode=pl.Buffered(1))
    nzg = sum(rows for _, rows in zg_row_blocks)
    return pl.pallas_call(
        _merge_out_kernel,
        grid=(n // ROW_TILE,),
        in_specs=[row(D_MODEL), row(D_MODEL), row(DN_WIDTH), row(SWA_WIDTH), row(SWA_WIDTH), row(SWA_WIDTH),
                  row(LANES), row(LANES), row(LANES)]
                 + _weight_specs(zg_row_blocks, D_MODEL)
                 + [full(w_dn), full(w_swa), full(w_out)],
        out_specs=row(D_MODEL),
        out_shape=jax.ShapeDtypeStruct((n, D_MODEL), F32),
        scratch_shapes=[pltpu.VMEM((nzg, D_MODEL), BF16)] + [pltpu.VMEM(w.shape, BF16) for w in (w_dn, w_swa, w_out)],
        compiler_params=_params_sequential(),
        name="merge_out",
    )(x2, h, odn, *o_list, *stats_list, wt, wt, wt, w_dn, w_swa, w_out)


def _rope_tables(t_len, q_norm_w, k_norm_w):
    j = np.arange(t_len)
    pos = np.stack([(j % (t_len // d)) * d + j // (t_len // d) for _, d in SWA_GROUPS]).astype(np.float64)
    inv_freq = ROPE_THETA ** (-np.arange(0, ROPE_DIM, 2, dtype=np.float64) / ROPE_DIM)
    ang = pos[:, :, None] * inv_freq[None, None, :]
    cos, sin = np.cos(ang), np.sin(ang)
    tail = (N_GROUPS, t_len, HEAD_DIM - ROPE_DIM)
    cos_t = jnp.asarray(np.concatenate([cos, cos, np.ones(tail)], axis=-1), F32)
    sin_t = jnp.asarray(np.concatenate([-sin, sin, np.zeros(tail)], axis=-1), F32)
    w = jnp.stack([q_norm_w.astype(F32) * (HEAD_DIM ** -0.5), k_norm_w.astype(F32)], axis=1)
    w_swapped = jnp.concatenate([w[..., ROPE_HALF:ROPE_DIM], w[..., :ROPE_HALF], w[..., ROPE_DIM:]], axis=-1)
    return w[:, :, None, :] * cos_t[:, None], w_swapped[:, :, None, :] * sin_t[:, None]


def kernel(x, norm_w, w_in, conv_w, dn_a_log, dn_dt_bias, dn_norm_w, q_norm_w, k_norm_w,
           w_branch_dn, w_branch_swa, w_out):
    b, t, d = x.shape
    n = b * t
    layer = 0
    wt = jnp.swapaxes(w_in[layer], 0, 1)
    c_z = 4 * DN_WIDTH
    c_q = c_z + 2 * DN_HEADS
    c_k = c_q + N_GROUPS * SWA_WIDTH
    c_v = c_k + N_GROUPS * SWA_WIDTH
    c_sz = c_v + N_GROUPS * SWA_WIDTH
    c_g = c_sz + SWA_WIDTH
    grp_blocks = lambda g: [(c0 + g * SWA_WIDTH, SWA_WIDTH) for c0 in (c_q, c_k, c_v)]
    zg_blocks = [(c_sz, SWA_WIDTH), (c_g, D_MODEL), (c_g + D_MODEL, D_MODEL)]
    pad_heads = lambda v: jnp.pad(v.astype(F32), (DN_HEADS, LANES - 2 * DN_HEADS))[None, :]
    alog_row = pad_heads(dn_a_log[layer])
    dt_row = pad_heads(dn_dt_bias[layer])

    x2 = x.reshape(n, d)
    h, a, bg, bgt = _norm_proj(x2, norm_w[layer][None, :], wt, c_z, conv_w[layer][:, 0, :], alog_row, dt_row, t)

    o_dn = _deltanet(a.reshape(b, t, 4 * DN_WIDTH), bg.reshape(b, t, LANES), bgt,
                     conv_w[layer][:, 0, :], dn_norm_w[layer][None, :])

    plain, swapped = _rope_tables(t, q_norm_w[layer], k_norm_w[layer])
    o_list, stats_list = [], []
    for g, (window, dilation) in enumerate(SWA_GROUPS):
        assert window // dilation == SWA_BLOCK
        s_g = _matmul(h, wt, grp_blocks(g), f"proj_swa{g}", dilation, t).reshape(b, t, 3 * SWA_WIDTH)
        o_g, stats_g = _swa_group(s_g, plain, swapped, g, dilation)
        o_list.append(o_g.reshape(n, SWA_WIDTH))
        stats_list.append(stats_g.reshape(n, LANES))

    out = _merge_out(x2, h, o_dn.reshape(n, DN_WIDTH), o_list, stats_list, wt, zg_blocks,
                     w_branch_dn[layer], w_branch_swa[layer], w_out[layer])
    return out.reshape(b, t, d)
```

```python
import functools
import math

import jax
import jax.numpy as jnp
import numpy as np
from jax import lax
from jax.experimental import pallas as pl
from jax.experimental.pallas import tpu as pltpu

D_MODEL = 1024
HEAD_DIM = 128
DN_HEADS = 4
DN_WIDTH = DN_HEADS * HEAD_DIM
CONV_WIDTH = 4
CHUNK = 64
SWA_GROUPS = ((128, 1), (512, 4), (2048, 16))
N_GROUPS = 3
SWA_HEADS = 4
SWA_WIDTH = SWA_HEADS * HEAD_DIM
SWA_BLOCK = 128
ROPE_DIM = HEAD_DIM // 4
ROPE_HALF = ROPE_DIM // 2
ROPE_THETA = 500000.0
EPS = 1e-6
NEG_INF = -1e30

LANES = 128
VMEM_LIMIT_BYTES = 48 * 1024 * 1024
DN_VMEM_LIMIT_BYTES = 56 * 1024 * 1024
ROW_TILE = 512
NORM_ROW_TILE = 512
PROJ_ROW_TILE = 1024
BG_ROWS = 16

F32 = jnp.float32
BF16 = jnp.bfloat16
_NT = (((1,), (1,)), ((), ()))
_TN = (((0,), (0,)), ((), ()))


def _sigmoid(x):
    return 0.5 * jnp.tanh(0.5 * x) + 0.5


def _silu(x):
    return x * _sigmoid(x)


def _softplus(x):
    return jnp.maximum(x, 0.0) + jnp.log(1.0 + jnp.exp(-jnp.abs(x)))


def _params(n_axes):
    return pltpu.CompilerParams(dimension_semantics=("parallel",) * n_axes,
                                vmem_limit_bytes=VMEM_LIMIT_BYTES)


def _params_sequential():
    return pltpu.CompilerParams(dimension_semantics=("arbitrary",), vmem_limit_bytes=VMEM_LIMIT_BYTES)


def _mxu(a, b):
    return jnp.dot(a, b, preferred_element_type=F32)


def _split3(x):
    hi = x.astype(BF16)
    r1 = x - hi.astype(F32)
    mid = r1.astype(BF16)
    lo = (r1 - mid.astype(F32)).astype(BF16)
    return hi, mid, lo


def _beta_and_log_decay(ab, a_log, dt_bias, head_index):
    beta = _sigmoid(ab)
    g = -jnp.exp(a_log) * _softplus(ab + dt_bias)
    return jnp.where(head_index < DN_HEADS, beta, g)


def _weight_specs(row_blocks, k):
    return [pl.BlockSpec((pl.Element(rows), pl.Element(k)), lambda i, start=start: (start, 0),
                         pipeline_mode=pl.Buffered(1))
            for start, rows in row_blocks]


def _cast_weights_once(w_refs, w_scr):
    @pl.when(pl.program_id(0) == 0)
    def _():
        blocks = [w_ref[...] for w_ref in w_refs]
        pad = w_scr.shape[0] - sum(blk.shape[0] for blk in blocks)
        if pad:
            blocks.append(jnp.zeros((pad, w_scr.shape[1]), F32))
        w_scr[...] = (jnp.concatenate(blocks, axis=0) if len(blocks) > 1 else blocks[0]).astype(BF16)


CONV_TILE = 256
CONV_HALO = 8
CONV_ROWS = 64
CONV_EARLY_COLS = 2 * DN_WIDTH


def _norm_proj_kernel(x_ref, nw_ref, wa_ref, wab_ref, cw_ref, alog_ref, dt_ref, alogt_ref, dtt_ref,
                      h_ref, a_ref, bg_ref, bgt_ref, wa_scr, wab_scr, tail_ref, res_a, res_b, *, tiles_per_seq):
    res_scr = (res_a, res_b)
    tile_rows = x_ref.shape[0]
    _cast_weights_once([wa_ref], wa_scr)
    _cast_weights_once([wab_ref], wab_scr)
    x = x_ref[...]
    h = (x * lax.rsqrt(jnp.mean(x * x, axis=-1, keepdims=True) + EPS)) * nw_ref[...]
    hb = h.astype(BF16)
    h_ref[...] = hb
    n_conv = CONV_EARLY_COLS
    seq_start = pl.program_id(0) % tiles_per_seq == 0
    tail = jnp.where(seq_start, 0.0, tail_ref[...])
    stores = []
    tiles = list(range(0, n_conv, CONV_TILE))
    base = pl.multiple_of((pl.program_id(0) >> 20) * CONV_HALO, CONV_HALO)

    def project(k):
        c0 = tiles[k]
        buf = res_scr[k % 2]
        buf[pl.ds(base, CONV_HALO), :] = tail[:, c0:c0 + CONV_TILE]
        buf[pl.ds(base + CONV_HALO, tile_rows), :] = lax.dot_general(hb, wa_scr[c0:c0 + CONV_TILE, :], _NT,
                                                                    preferred_element_type=F32)

    if tiles:
        project(0)
    for k, c0 in enumerate(tiles):
        cols = slice(c0, c0 + CONV_TILE)
        if k + 1 < len(tiles):
            project(k + 1)
        buf = res_scr[k % 2]
        acts = []
        for rb in range(0, tile_rows, CONV_ROWS):
            blk = buf[pl.ds(base + rb, CONV_HALO + CONV_ROWS), :]
            conv = None
            for j in range(CONV_WIDTH):
                lo = CONV_HALO - (CONV_WIDTH - 1) + j
                term = cw_ref[j:j + 1, cols] * blk[lo:lo + CONV_ROWS, :]
                conv = term if conv is None else conv + term
            acts.append(_silu(conv).astype(BF16))
        stores.append((a_ref, (slice(None), cols), jnp.concatenate(acts, axis=0)))
        stores.append((tail_ref, (slice(None), cols), buf[pl.ds(base + tile_rows, CONV_HALO), :]))
    z = lax.dot_general(hb, wa_scr[n_conv:, :], _NT, preferred_element_type=F32)
    stores.append((a_ref, (slice(None), slice(n_conv, None)), z.astype(BF16)))
    for ref, idx, value in stores:
        ref[idx] = value
    ab = lax.dot_general(hb, wab_scr[...], _NT, preferred_element_type=F32)
    bg_ref[...] = _beta_and_log_decay(ab, alog_ref[...], dt_ref[...],
                                      lax.broadcasted_iota(jnp.int32, ab.shape, 1))
    abt = lax.dot_general(wab_scr[0:BG_ROWS, :], hb, _NT, preferred_element_type=F32)
    bgt_ref[...] = _beta_and_log_decay(abt, alogt_ref[...], dtt_ref[...],
                                       lax.broadcasted_iota(jnp.int32, abt.shape, 0))


def _norm_proj(x2, norm_w, wt, na, conv_w2, alog_row, dt_row, seq_len):
    n = x2.shape[0]
    tile = NORM_ROW_TILE
    assert seq_len % tile == 0 and tile % CONV_ROWS == 0 and CONV_EARLY_COLS % CONV_TILE == 0
    row = lambda i: (i, 0)
    fixed = lambda i: (0, 0)
    pad_col = lambda v: v[0, :BG_ROWS][:, None]
    return pl.pallas_call(
        functools.partial(_norm_proj_kernel, tiles_per_seq=seq_len // tile),
        grid=(n // tile,),
        in_specs=[pl.BlockSpec((tile, D_MODEL), row),
                  pl.BlockSpec((1, D_MODEL), fixed)]
                 + _weight_specs([(0, na), (na, 2 * DN_HEADS)], D_MODEL)
                 + [pl.BlockSpec((CONV_WIDTH, 3 * DN_WIDTH), fixed),
                    pl.BlockSpec((1, LANES), fixed),
                    pl.BlockSpec((1, LANES), fixed),
                    pl.BlockSpec((BG_ROWS, 1), fixed),
                    pl.BlockSpec((BG_ROWS, 1), fixed)],
        out_specs=[pl.BlockSpec((tile, D_MODEL), row),
                   pl.BlockSpec((tile, na), row),
                   pl.BlockSpec((tile, LANES), row),
                   pl.BlockSpec((BG_ROWS, tile), lambda i: (0, i))],
        out_shape=[jax.ShapeDtypeStruct((n, D_MODEL), BF16),
                   jax.ShapeDtypeStruct((n, na), BF16),
                   jax.ShapeDtypeStruct((n, LANES), F32),
                   jax.ShapeDtypeStruct((BG_ROWS, n), F32)],
        scratch_shapes=[pltpu.VMEM((na, D_MODEL), BF16), pltpu.VMEM((LANES, D_MODEL), BF16),
                        pltpu.VMEM((CONV_HALO, max(CONV_EARLY_COLS, LANES)), F32),
                        pltpu.VMEM((CONV_HALO + tile, CONV_TILE), F32),
                        pltpu.VMEM((CONV_HALO + tile, CONV_TILE), F32)],
        compiler_params=_params_sequential(),
        name="norm_proj",
    )(x2, norm_w, wt, wt, conv_w2, alog_row, dt_row, pad_col(alog_row), pad_col(dt_row))


REGROUP_STRIDE = 4


def _matmul_kernel(h_ref, *refs, dilation):
    n_scratch = 1 if dilation == 1 else 3
    w_refs, o_ref, w_scr = refs[:-n_scratch - 1], refs[-n_scratch - 1], refs[-n_scratch]
    _cast_weights_once(w_refs, w_scr)
    if dilation == 1:
        o_ref[...] = lax.dot_general(h_ref[...], w_scr[...], _NT, preferred_element_type=F32).astype(o_ref.dtype)
        return
    stage_ref, stage2_ref = refs[-2], refs[-1]
    s1 = min(dilation, REGROUP_STRIDE)
    s2 = dilation // s1
    rows1 = h_ref.shape[0] // s1
    per = h_ref.shape[0] // dilation
    col = 0
    for w_ref in w_refs:
        width = w_ref.shape[0]
        res = lax.dot_general(h_ref[...], w_scr[col:col + width, :], _NT, preferred_element_type=F32)
        slabs = range(col // LANES, (col + width) // LANES)
        for c in slabs:
            stage_ref[c] = res[:, (c * LANES - col):(c * LANES - col) + LANES]
        for c in slabs:
            for a in range(s1):
                first = stage_ref[c, pl.ds(a, rows1, stride=s1), :]
                if s2 == 1:
                    o_ref[0, a, :, c * LANES:(c + 1) * LANES] = first.astype(o_ref.dtype)
                else:
                    stage2_ref[c, a * rows1:(a + 1) * rows1, :] = first
        if s2 > 1:
            for c in slabs:
                for a in range(s1):
                    for b in range(s2):
                        o_ref[0, b * s1 + a, :, c * LANES:(c + 1) * LANES] = (
                            stage2_ref[c, pl.ds(a * rows1 + b, per, stride=s2), :].astype(o_ref.dtype))
        col += width


def _matmul(h, wt, row_blocks, name, dilation=1, seq_len=None):
    n, k = h.shape
    nc = sum(rows for _, rows in row_blocks)
    scratch = [pltpu.VMEM((nc, k), BF16)]
    tile = PROJ_ROW_TILE
    if dilation == 1:
        out_spec = pl.BlockSpec((tile, nc), lambda i: (i, 0))
        out_shape = jax.ShapeDtypeStruct((n, nc), BF16)
    else:
        tiles = seq_len // tile
        per = tile // dilation
        assert seq_len % tile == 0 and tile % dilation == 0 and per % 16 == 0 and nc % LANES == 0
        out_spec = pl.BlockSpec((1, dilation, per, nc), lambda i: (i // tiles, 0, i % tiles, 0))
        out_shape = jax.ShapeDtypeStruct((n // seq_len, dilation, seq_len // dilation, nc), BF16)
        assert dilation % min(dilation, REGROUP_STRIDE) == 0
        scratch += [pltpu.VMEM((nc // LANES, tile, LANES), F32)] * 2
    return pl.pallas_call(
        functools.partial(_matmul_kernel, dilation=dilation),
        grid=(n // tile,),
        in_specs=[pl.BlockSpec((tile, k), lambda i: (i, 0))] + _weight_specs(row_blocks, k),
        out_specs=out_spec,
        out_shape=out_shape,
        scratch_shapes=scratch,
        compiler_params=_params_sequential(),
        name=name,
    )(h, *([wt] * len(row_blocks)))


DN_GROUP = 8
DN_ROWS = DN_GROUP * CHUNK
DN_HALO = 16


def _deltanet_kernel(a_ref, bg_ref, bgt_ref, cw_ref, nw_ref, o_ref,
                     u_ref, wq_ref, kd_ref, ai_ref, eg_ref):
    t_len = a_ref.shape[1]
    n_groups = t_len // DN_ROWS
    assert 2 * CHUNK == LANES and DN_GROUP % 2 == 0
    ii = lax.broadcasted_iota(jnp.int32, (CHUNK, LANES), 0)
    lane_pair = lax.broadcasted_iota(jnp.int32, (CHUNK, LANES), 1)
    jj = lane_pair & (CHUNK - 1)
    second = lane_pair >= CHUNK
    causal = ii >= jj
    strict = ii > jj
    eye_f = (ii == jj).astype(F32)
    second_rows = lax.broadcasted_iota(jnp.int32, (LANES, LANES), 0) >= CHUNK
    second_cols = lax.broadcasted_iota(jnp.int32, (LANES, LANES), 1) >= CHUNK
    same_half = second_rows == second_cols

    def block_diag(pair):
        return jnp.where(same_half, jnp.concatenate([pair, pair], axis=0), 0)

    bi = lax.broadcasted_iota(jnp.int32, (DN_ROWS, DN_ROWS), 0)
    bj = lax.broadcasted_iota(jnp.int32, (DN_ROWS, DN_ROWS), 1)
    shift = int(math.log2(CHUNK))
    same_chunk = jnp.right_shift(bi, shift) == jnp.right_shift(bj, shift)
    cum_lower = (same_chunk & (bi >= bj)).astype(BF16)
    cum_upper = (same_chunk & (bi <= bj)).astype(BF16)
    problems = [(cp, h) for cp in range(DN_GROUP // 2) for h in range(DN_HEADS)]
    heads = range(DN_HEADS)
    hsl = lambda h: slice(h * HEAD_DIM, (h + 1) * HEAD_DIM)
    cs = lambda c: slice(c * CHUNK, (c + 1) * CHUNK)
    ones_b = jnp.ones((HEAD_DIM, HEAD_DIM), BF16)
    sel_src = lax.broadcasted_iota(jnp.int32, (LANES, DN_WIDTH), 0)
    sel_head = jnp.right_shift(lax.broadcasted_iota(jnp.int32, (LANES, DN_WIDTH), 1), int(math.log2(HEAD_DIM)))
    sel_beta = (sel_src == sel_head).astype(BF16)
    sel_g = (sel_src == sel_head + DN_HEADS).astype(BF16)
    lane = lax.broadcasted_iota(jnp.int32, (DN_ROWS, LANES), 1)
    g_lanes = (lane >= DN_HEADS) & (lane < 2 * DN_HEADS)

    pending = []

    def defer(ref, idx, value):
        pending.append((ref, idx, value))

    def flush():
        for ref, idx, value in pending:
            ref[idx] = value
        pending.clear()

    def conv_silu(gi, r0, col0):
        cols = slice(col0, col0 + HEAD_DIM)
        rp = pl.multiple_of(jnp.maximum(r0 - DN_HALO, 0), DN_HALO)
        cur = a_ref[0, pl.ds(r0, DN_ROWS), cols].astype(F32)
        prev = a_ref[0, pl.ds(rp, DN_HALO), cols].astype(F32)
        prev = jnp.where(gi > 0, prev, 0.0)
        xw = jnp.concatenate([prev, cur], axis=0)
        conv = None
        for j in range(CONV_WIDTH):
            lo = DN_HALO - (CONV_WIDTH - 1) + j
            term = cw_ref[j:j + 1, cols] * xw[lo:lo + DN_ROWS, :]
            conv = term if conv is None else conv + term
        return _silu(conv)

    def elementwise(gi, handoff):
        r0 = pl.multiple_of(gi * DN_ROWS, DN_ROWS)
        def act_of(h):
            tiles = []
            for base in (0, DN_WIDTH, 2 * DN_WIDTH):
                col0 = base + h * HEAD_DIM
                if col0 < CONV_EARLY_COLS:
                    tiles.append(a_ref[0, pl.ds(r0, DN_ROWS), col0:col0 + HEAD_DIM].astype(F32))
                else:
                    tiles.append(conv_silu(gi, r0, col0))
            return tiles
        acts = []

        def head_norms(h):
            qh, kh, vh = act_of(h)
            qn_h = qh * (lax.rsqrt(_mxu((qh * qh).astype(BF16), ones_b) + EPS) * (HEAD_DIM ** -0.5))
            kn_h = kh * lax.rsqrt(_mxu((kh * kh).astype(BF16), ones_b) + EPS)
            acts.append((qn_h, kn_h, vh))

        head_norms(0)
        yield
        bg = bg_ref[0, pl.ds(r0, DN_ROWS), :]
        gc_all = sum(_mxu(cum_lower, piece) for piece in _split3(bg))
        gct_all = sum(_mxu(piece, cum_upper) for piece in _split3(bgt_ref[:, pl.ds(r0, DN_ROWS)]))
        gl_all = jnp.concatenate([jnp.broadcast_to(gc_all[(c + 1) * CHUNK - 1:(c + 1) * CHUNK, :], (CHUNK, LANES))
                                  for c in range(DN_GROUP)], axis=0)
        eg_all = jnp.exp(gl_all)
        for c in range(DN_GROUP):
            defer(eg_ref, (gi * DN_GROUP + c,), eg_all[c * CHUNK:c * CHUNK + 8, :])
        gc_m = jnp.where(g_lanes, gc_all, 0.0)
        gl_m = jnp.where(g_lanes, gl_all, 0.0)
        beta_rep = _mxu(bg.astype(BF16), sel_beta)
        egc_rep = _mxu(jnp.exp(gc_m).astype(BF16), sel_g)
        ekd_rep = _mxu(jnp.exp(gl_m - gc_m).astype(BF16), sel_g)

        for h in range(1, DN_HEADS):
            head_norms(h)
            yield

        qn, knb, kb, rhs = [], [], [], []
        for h in heads:
            qn_h, kn_h, vh = acts[h]
            kb_h = kn_h * beta_rep[:, hsl(h)]
            defer(kd_ref, (h, pl.ds(r0, DN_ROWS), slice(None)), (kn_h * ekd_rep[:, hsl(h)]).astype(BF16))
            qg_h = (qn_h * egc_rep[:, hsl(h)]).astype(BF16)
            for c in range(DN_GROUP):
                w0 = pl.multiple_of(2 * (r0 + c * CHUNK), 2 * CHUNK)
                defer(wq_ref, (h, pl.ds(w0 + CHUNK, CHUNK), slice(None)), qg_h[cs(c)])
            qn.append(qn_h)
            knb.append(kn_h.astype(BF16))
            kb.append(kb_h)
            rhs.append(jnp.concatenate([vh * beta_rep[:, hsl(h)], kb_h * egc_rep[:, hsl(h)]], axis=1).astype(BF16))
            yield

        zeros_k = jnp.zeros((CHUNK, HEAD_DIM), BF16)
        qk = []
        for cp, h in problems:
            c1, c2 = cs(2 * cp), cs(2 * cp + 1)
            lhs = jnp.concatenate([jnp.concatenate([qn[h][c1], qn[h][c2]], axis=1),
                                   jnp.concatenate([kb[h][c1], kb[h][c2]], axis=1)], axis=0).astype(BF16)
            keys = jnp.concatenate([jnp.concatenate([knb[h][c1], zeros_k], axis=1),
                                    jnp.concatenate([zeros_k, knb[h][c2]], axis=1)], axis=0)
            qk.append(lax.dot_general(lhs, keys, _NT, preferred_element_type=F32))
        decay = []
        for cp, h in problems:
            col = slice(DN_HEADS + h, DN_HEADS + h + 1)
            gc_col = jnp.where(second, gc_all[cs(2 * cp + 1), col], gc_all[cs(2 * cp), col])
            diff = gc_col - gct_all[DN_HEADS + h:DN_HEADS + h + 1, 2 * cp * CHUNK:(2 * cp + 2) * CHUNK]
            decay.append(jnp.where(causal, jnp.exp(jnp.where(causal, diff, 0.0)), 0.0))
        for i, (cp, h) in enumerate(problems):
            a_intra = jnp.where(causal, qk[i][:CHUNK] * decay[i], 0.0)
            defer(ai_ref, (h, gi * (DN_GROUP // 2) + cp), a_intra.astype(BF16))
        handoff["neg_lower"] = [-jnp.where(strict, qk[i][CHUNK:] * decay[i], 0.0) for i in range(len(problems))]
        handoff["rhs"] = rhs

    def solve(gi, handoff):
        r0 = pl.multiple_of(gi * DN_ROWS, DN_ROWS)
        ps = handoff["neg_lower"]
        rhs = handoff["rhs"]
        tms = [eye_f + p for p in ps]
        pbs = [p.astype(BF16) for p in ps]
        ps = [_mxu(pb, block_diag(pb)) for pb in pbs]
        yield
        for _ in range(shift - 2):
            pbs = [p.astype(BF16) for p in ps]
            both = [_mxu(jnp.concatenate([pb, tm.astype(BF16)], axis=0), block_diag(pb)) for pb, tm in zip(pbs, tms)]
            ps = [r[:CHUNK] for r in both]
            tms = [tm + r[CHUNK:] for tm, r in zip(tms, both)]
            yield
        tms = [tm + _mxu(tm.astype(BF16), block_diag(p.astype(BF16))) for tm, p in zip(tms, ps)]
        zeros_r = jnp.zeros((CHUNK, 2 * HEAD_DIM), BF16)
        uw = []
        for i, (cp, h) in enumerate(problems):
            both = jnp.concatenate([jnp.concatenate([rhs[h][cs(2 * cp)], zeros_r], axis=1),
                                    jnp.concatenate([zeros_r, rhs[h][cs(2 * cp + 1)]], axis=1)], axis=0)
            uw.append(_mxu(tms[i].astype(BF16), both))
        for i, (cp, h) in enumerate(problems):
            for half in range(2):
                c = 2 * cp + half
                part = uw[i][:, half * 2 * HEAD_DIM:(half + 1) * 2 * HEAD_DIM]
                defer(u_ref, (h, pl.ds(r0 + c * CHUNK, CHUNK), slice(None)), part[:, :HEAD_DIM])
                w0 = pl.multiple_of(2 * (r0 + c * CHUNK), 2 * CHUNK)
                defer(wq_ref, (h, pl.ds(w0, CHUNK), slice(None)), part[:, HEAD_DIM:].astype(BF16))

    def recur(g, c_in_group, state):
        c = g * DN_GROUP + c_in_group
        pair = g * (DN_GROUP // 2) + c_in_group // 2
        zeros_v = jnp.zeros((CHUNK, HEAD_DIM), BF16)
        pad = (lambda vn_h: jnp.concatenate([vn_h, zeros_v], axis=0)) if c_in_group % 2 == 0 else (
            lambda vn_h: jnp.concatenate([zeros_v, vn_h], axis=0))
        r0 = pl.multiple_of(c * CHUNK, CHUNK)
        rows = pl.ds(r0, CHUNK)
        wrows = pl.ds(pl.multiple_of(2 * r0, 2 * CHUNK), 2 * CHUNK)
        eg = eg_ref[c]
        ws = [_mxu(wq_ref[h, wrows, :], state[h].astype(BF16)) for h in heads]
        vn = [(u_ref[h, rows, :] - ws[h][:CHUNK]).astype(BF16) for h in heads]
        new_state = [state[h] * eg[0:1, DN_HEADS + h:DN_HEADS + h + 1]
                     + lax.dot_general(kd_ref[h, rows, :], vn[h], _TN, preferred_element_type=F32) for h in heads]
        o = [ws[h][CHUNK:] + _mxu(ai_ref[h, pair], pad(vn[h])) for h in heads]
        ms = [_mxu((x * x).astype(BF16), ones_b) * (1.0 / HEAD_DIM) for x in o]
        for h in heads:
            on = o[h] * lax.rsqrt(ms[h] + EPS) * nw_ref[...]
            z = a_ref[0, rows, 3 * DN_WIDTH + h * HEAD_DIM:3 * DN_WIDTH + (h + 1) * HEAD_DIM].astype(F32)
            defer(o_ref, (0, rows, hsl(h)), (on * _silu(z)).astype(o_ref.dtype))
        return new_state

    def pipeline_step(g_prep, g_recur, state):
        def stages():
            if g_prep is not None:
                handoff = {}
                yield from elementwise(g_prep, handoff)
                yield
                yield from solve(g_prep, handoff)

        work = stages()
        for c in range(DN_GROUP):
            if g_recur is not None:
                state = recur(g_recur, c, state)
            next(work, None)
        for _ in work:
            pass
        flush()
        return state

    zero = jnp.zeros((HEAD_DIM, HEAD_DIM), F32)
    state = pipeline_step(0, None, [zero] * DN_HEADS)
    state = list(lax.fori_loop(
        1, n_groups, lambda gi, st: tuple(pipeline_step(gi, gi - 1, list(st))), tuple(state), unroll=True))
    pipeline_step(None, n_groups - 1, state)


def _deltanet(a3, bg3, bgt, conv_w2, dn_norm_row):
    b, t, wa = a3.shape
    nt = bgt.shape[0]
    return pl.pallas_call(
        _deltanet_kernel,
        grid=(b,),
        in_specs=[pl.BlockSpec((1, t, wa), lambda i: (i, 0, 0)),
                  pl.BlockSpec((1, t, LANES), lambda i: (i, 0, 0)),
                  pl.BlockSpec((nt, t), lambda i: (0, i)),
                  pl.BlockSpec((CONV_WIDTH, 3 * DN_WIDTH), lambda i: (0, 0)),
                  pl.BlockSpec((1, HEAD_DIM), lambda i: (0, 0))],
        out_specs=pl.BlockSpec((1, t, DN_WIDTH), lambda i: (i, 0, 0)),
        out_shape=jax.ShapeDtypeStruct((b, t, DN_WIDTH), BF16),
        scratch_shapes=[pltpu.VMEM((DN_HEADS, t, HEAD_DIM), F32),
                        pltpu.VMEM((DN_HEADS, 2 * t, HEAD_DIM), BF16),
                        pltpu.VMEM((DN_HEADS, t, HEAD_DIM), BF16),
                        pltpu.VMEM((DN_HEADS, t // LANES, CHUNK, LANES), BF16),
                        pltpu.VMEM((t // CHUNK, 8, LANES), F32)],
        compiler_params=pltpu.CompilerParams(dimension_semantics=("parallel",),
                                             vmem_limit_bytes=DN_VMEM_LIMIT_BYTES),
        name="deltanet",
    )(a3, bg3, bgt, conv_w2, dn_norm_row)


SWA_PREP_ROWS = 1024
SWA_UNITS = 4


def _swa_kernel(s_ref, plain_ref, swapped_ref, o_ref, stats_ref, qs_ref, ks_ref, bias_ref, *dilated_scratch, dilation):
    t_len = s_ref.shape[1]
    d = dilation
    seq = t_len // d
    nblk = seq // SWA_BLOCK
    heads = range(SWA_HEADS)
    hsl = lambda h: slice(h * HEAD_DIM, (h + 1) * HEAD_DIM)
    if d > 1:
        (ostage_ref,) = dilated_scratch
    mean_b = jnp.full((HEAD_DIM, HEAD_DIM), 1.0 / HEAD_DIM, BF16)
    src_lane = lax.broadcasted_iota(jnp.int32, (HEAD_DIM, HEAD_DIM), 0)
    dst_lane = lax.broadcasted_iota(jnp.int32, (HEAD_DIM, HEAD_DIM), 1)
    swap_b = (((dst_lane < ROPE_HALF) & (src_lane == dst_lane + ROPE_HALF))
              | ((dst_lane >= ROPE_HALF) & (dst_lane < ROPE_DIM) & (src_lane == dst_lane - ROPE_HALF))
              ).astype(BF16)

    def prep(n, carry):
        r0 = pl.multiple_of(n * SWA_PREP_ROWS, SWA_PREP_ROWS)
        rows = pl.ds(r0, SWA_PREP_ROWS)
        for which, dst in ((0, qs_ref), (1, ks_ref)):
            plain = plain_ref[which, rows, :]
            swapped_tab = swapped_ref[which, rows, :]
            xb = [s_ref[0, rows, which * SWA_WIDTH + h * HEAD_DIM:which * SWA_WIDTH + (h + 1) * HEAD_DIM]
                  for h in heads]
            ms = [_mxu(x * x, mean_b) for x in xb]
            sw = [_mxu(x, swap_b) for x in xb]
            ys = [(x.astype(F32) * plain + s * swapped_tab) * lax.rsqrt(m + EPS) for x, s, m in zip(xb, sw, ms)]
            for h in heads:
                dst[rows, hsl(h)] = ys[h].astype(BF16)
        return carry

    lax.fori_loop(0, t_len // SWA_PREP_ROWS, prep, 0)

    qi = lax.broadcasted_iota(jnp.int32, (SWA_BLOCK, 2 * SWA_BLOCK), 0)
    kj = lax.broadcasted_iota(jnp.int32, (SWA_BLOCK, 2 * SWA_BLOCK), 1)
    bias_ref[0] = jnp.where(kj <= qi, 0.0, NEG_INF).astype(F32)
    bias_ref[1] = jnp.where((kj >= qi) & (kj <= qi + SWA_BLOCK), 0.0, NEG_INF).astype(F32)
    lane = lax.broadcasted_iota(jnp.int32, (SWA_BLOCK, LANES), 1)

    def attend(units):
        jobs = [(u, h) for u in range(len(units)) for h in heads]
        qrows = [pl.ds(q0, SWA_BLOCK) for q0, _, _, _, _ in units]
        krows = [pl.ds(k0, width) for _, k0, width, _, _ in units]
        s = [lax.dot_general(qs_ref[qrows[u], hsl(h)], ks_ref[krows[u], hsl(h)], _NT,
                             preferred_element_type=F32) + units[u][3] for u, h in jobs]
        m = [jnp.max(x, axis=-1, keepdims=True) for x in s]
        p = [jnp.exp(x - mx) for x, mx in zip(s, m)]
        l = [jnp.sum(x, axis=-1, keepdims=True) for x in p]
        v = [s_ref[0, krows[u], 2 * SWA_WIDTH + h * HEAD_DIM:2 * SWA_WIDTH + (h + 1) * HEAD_DIM] for u, h in jobs]
        o = [_mxu(x.astype(BF16), vh) for x, vh in zip(p, v)]
        for u in range(len(units)):
            stats_tile = jnp.zeros((SWA_BLOCK, LANES), F32)
            for h in heads:
                j = u * SWA_HEADS + h
                stats_tile = jnp.where(lane == h, m[j], jnp.where(lane == SWA_HEADS + h, l[j], stats_tile))
            t0 = units[u][4]
            if d == 1:
                for h in heads:
                    o_ref[0, qrows[u], hsl(h)] = o[u * SWA_HEADS + h].astype(o_ref.dtype)
                stats_ref[0, qrows[u], :] = stats_tile
            else:
                trows = pl.ds(t0, SWA_BLOCK, stride=d)
                for h in heads:
                    ostage_ref[h, trows, :] = o[u * SWA_HEADS + h]
                stats_ref[0, trows, :] = stats_tile

    if nblk == 1:
        causal_bias = bias_ref[0][:, :SWA_BLOCK]

        def body(i, carry):
            units = []
            for u in range(SWA_UNITS):
                r = i * SWA_UNITS + u
                q0 = pl.multiple_of(r * seq, SWA_BLOCK)
                units.append((q0, q0, SWA_BLOCK, causal_bias, r))
            attend(units)
            return carry

        lax.fori_loop(0, d // SWA_UNITS, body, 0, unroll=True)
    else:
        pairs = nblk // SWA_UNITS

        def body(i, carry):
            r = i // pairs
            pp = i % pairs
            units = []
            for u in range(SWA_UNITS):
                n = pp * SWA_UNITS + u
                q0 = pl.multiple_of(r * seq + n * SWA_BLOCK, SWA_BLOCK)
                if u == 0:
                    first = jnp.asarray(pp == 0).astype(jnp.int32)
                    k0 = pl.multiple_of(q0 - (1 - first) * SWA_BLOCK, SWA_BLOCK)
                    bias = bias_ref[1 - first]
                else:
                    k0 = pl.multiple_of(q0 - SWA_BLOCK, SWA_BLOCK)
                    bias = bias_ref[1]
                units.append((q0, k0, 2 * SWA_BLOCK, bias, r + d * n * SWA_BLOCK))
            attend(units)
            return carry

        lax.fori_loop(0, d * pairs, body, 0, unroll=True)

    if d > 1:
        def emit(n, carry):
            rows = pl.ds(pl.multiple_of(n * SWA_PREP_ROWS, SWA_PREP_ROWS), SWA_PREP_ROWS)
            for h in heads:
                o_ref[0, rows, hsl(h)] = ostage_ref[h, rows, :].astype(o_ref.dtype)
            return carry
        lax.fori_loop(0, t_len // SWA_PREP_ROWS, emit, 0)


def _swa_group(s3, plain, swapped, group, dilation):
    b, t, ws = s3.shape
    assert (t // dilation) % SWA_BLOCK == 0 and t % SWA_PREP_ROWS == 0
    nblk = t // dilation // SWA_BLOCK
    assert (dilation % SWA_UNITS == 0) if nblk == 1 else (nblk % SWA_UNITS == 0)
    scratch = [pltpu.VMEM((t, SWA_WIDTH), BF16), pltpu.VMEM((t, SWA_WIDTH), BF16),
               pltpu.VMEM((2, SWA_BLOCK, 2 * SWA_BLOCK), F32)]
    if dilation > 1:
        scratch += [pltpu.VMEM((SWA_HEADS, t, LANES), F32)]
    tab_spec = pl.BlockSpec((None, 2, t, LANES), lambda i: (group, 0, 0, 0))
    return pl.pallas_call(
        functools.partial(_swa_kernel, dilation=dilation),
        grid=(b,),
        in_specs=[pl.BlockSpec((1, t, ws), lambda i: (i, 0, 0)), tab_spec, tab_spec],
        out_specs=[pl.BlockSpec((1, t, SWA_WIDTH), lambda i: (i, 0, 0)),
                   pl.BlockSpec((1, t, LANES), lambda i: (i, 0, 0))],
        out_shape=[jax.ShapeDtypeStruct((b, t, SWA_WIDTH), BF16),
                   jax.ShapeDtypeStruct((b, t, LANES), F32)],
        scratch_shapes=scratch,
        compiler_params=_params(1),
        name=f"swa_d{dilation}",
    )(s3, plain, swapped)


MERGE_ROWS = 512


def _merge_out_kernel(x_ref, h_ref, odn_ref, o0_ref, o1_ref, o2_ref, st0_ref, st1_ref, st2_ref,
                      wz_ref, wgdn_ref, wgswa_ref, wdn32_ref, wswa32_ref, wout32_ref, out_ref,
                      wzg_scr, wdn_ref, wswa_ref, wout_ref):
    _cast_weights_once([wz_ref, wgdn_ref, wgswa_ref], wzg_scr)
    _cast_weights_once([wdn32_ref], wdn_ref)
    _cast_weights_once([wswa32_ref], wswa_ref)
    _cast_weights_once([wout32_ref], wout_ref)
    zg_all = lax.dot_general(h_ref[...], wzg_scr[...], _NT, preferred_element_type=F32)
    outs = []
    for j in range(ROW_TILE // MERGE_ROWS):
        rows = slice(j * MERGE_ROWS, (j + 1) * MERGE_ROWS)
        zg = zg_all[rows, :]
        st0 = st0_ref[rows, :]
        st1 = st1_ref[rows, :]
        st2 = st2_ref[rows, :]
        m = jnp.maximum(jnp.maximum(st0, st1), st2)
        e0 = jnp.exp(st0 - m)
        e1 = jnp.exp(st1 - m)
        e2 = jnp.exp(st2 - m)
        sums = [pltpu.roll(st, LANES - SWA_HEADS, 1) for st in (st0, st1, st2)]
        inv = 1.0 / (e0 * sums[0] + e1 * sums[1] + e2 * sums[2])
        a0, a1, a2 = e0 * inv, e1 * inv, e2 * inv
        parts = []
        for h in range(SWA_HEADS):
            hs = slice(h * HEAD_DIM, (h + 1) * HEAD_DIM)
            col = slice(h, h + 1)
            oh = (a0[:, col] * o0_ref[rows, hs].astype(F32) + a1[:, col] * o1_ref[rows, hs].astype(F32)
                  + a2[:, col] * o2_ref[rows, hs].astype(F32))
            parts.append((oh * _silu(zg[:, hs])).astype(BF16))
        o_swa = jnp.concatenate(parts, axis=1)
        y_swa = _mxu(o_swa, wswa_ref[...])
        y_dn = _mxu(odn_ref[rows, :], wdn_ref[...])
        g_dn = zg[:, SWA_WIDTH:SWA_WIDTH + D_MODEL]
        g_swa = zg[:, SWA_WIDTH + D_MODEL:SWA_WIDTH + 2 * D_MODEL]
        merged = _sigmoid(g_dn) * y_dn + _sigmoid(g_swa) * y_swa
        outs.append((rows, x_ref[rows, :] + _mxu(merged.astype(BF16), wout_ref[...])))
    for rows, value in outs:
        out_ref[rows, :] = value


def _merge_out(x2, h, odn, o_list, stats_list, wt, zg_row_blocks, w_dn, w_swa, w_out):
    n = x2.shape[0]
    row = lambda w: pl.BlockSpec((ROW_TILE, w), lambda i: (i, 0))
    full = lambda a: pl.BlockSpec(a.shape, lambda i: (0, 0), pipeline_mode=pl.Buffered(1))
    nzg = sum(rows for _, rows in zg_row_blocks)
    return pl.pallas_call(
        _merge_out_kernel,
        grid=(n // ROW_TILE,),
        in_specs=[row(D_MODEL), row(D_MODEL), row(DN_WIDTH), row(SWA_WIDTH), row(SWA_WIDTH), row(SWA_WIDTH),
                  row(LANES), row(LANES), row(LANES)]
                 + _weight_specs(zg_row_blocks, D_MODEL)
                 + [full(w_dn), full(w_swa), full(w_out)],
        out_specs=row(D_MODEL),
        out_shape=jax.ShapeDtypeStruct((n, D_MODEL), F32),
        scratch_shapes=[pltpu.VMEM((nzg, D_MODEL), BF16)] + [pltpu.VMEM(w.shape, BF16) for w in (w_dn, w_swa, w_out)],
        compiler_params=_params_sequential(),
        name="merge_out",
    )(x2, h, odn, *o_list, *stats_list, wt, wt, wt, w_dn, w_swa, w_out)


def _rope_tables(t_len, q_norm_w, k_norm_w):
    j = np.arange(t_len)
    pos = np.stack([(j % (t_len // d)) * d + j // (t_len // d) for _, d in SWA_GROUPS]).astype(np.float64)
    inv_freq = ROPE_THETA ** (-np.arange(0, ROPE_DIM, 2, dtype=np.float64) / ROPE_DIM)
    ang = pos[:, :, None] * inv_freq[None, None, :]
    cos, sin = np.cos(ang), np.sin(ang)
    tail = (N_GROUPS, t_len, HEAD_DIM - ROPE_DIM)
    cos_t = jnp.asarray(np.concatenate([cos, cos, np.ones(tail)], axis=-1), F32)
    sin_t = jnp.asarray(np.concatenate([-sin, sin, np.zeros(tail)], axis=-1), F32)
    w = jnp.stack([q_norm_w.astype(F32) * (HEAD_DIM ** -0.5), k_norm_w.astype(F32)], axis=1)
    w_swapped = jnp.concatenate([w[..., ROPE_HALF:ROPE_DIM], w[..., :ROPE_HALF], w[..., ROPE_DIM:]], axis=-1)
    return w[:, :, None, :] * cos_t[:, None], w_swapped[:, :, None, :] * sin_t[:, None]


def kernel(x, norm_w, w_in, conv_w, dn_a_log, dn_dt_bias, dn_norm_w, q_norm_w, k_norm_w,
           w_branch_dn, w_branch_swa, w_out):
    b, t, d = x.shape
    n = b * t
    layer = 0
    wt = jnp.swapaxes(w_in[layer], 0, 1)
    c_z = 4 * DN_WIDTH
    c_q = c_z + 2 * DN_HEADS
    c_k = c_q + N_GROUPS * SWA_WIDTH
    c_v = c_k + N_GROUPS * SWA_WIDTH
    c_sz = c_v + N_GROUPS * SWA_WIDTH
    c_g = c_sz + SWA_WIDTH
    grp_blocks = lambda g: [(c0 + g * SWA_WIDTH, SWA_WIDTH) for c0 in (c_q, c_k, c_v)]
    zg_blocks = [(c_sz, SWA_WIDTH), (c_g, D_MODEL), (c_g + D_MODEL, D_MODEL)]
    pad_heads = lambda v: jnp.pad(v.astype(F32), (DN_HEADS, LANES - 2 * DN_HEADS))[None, :]
    alog_row = pad_heads(dn_a_log[layer])
    dt_row = pad_heads(dn_dt_bias[layer])

    x2 = x.reshape(n, d)
    h, a, bg, bgt = _norm_proj(x2, norm_w[layer][None, :], wt, c_z, conv_w[layer][:, 0, :], alog_row, dt_row, t)

    o_dn = _deltanet(a.reshape(b, t, 4 * DN_WIDTH), bg.reshape(b, t, LANES), bgt,
                     conv_w[layer][:, 0, :], dn_norm_w[layer][None, :])

    plain, swapped = _rope_tables(t, q_norm_w[layer], k_norm_w[layer])
    o_list, stats_list = [], []
    for g, (window, dilation) in enumerate(SWA_GROUPS):
        assert window // dilation == SWA_BLOCK
        s_g = _matmul(h, wt, grp_blocks(g), f"proj_swa{g}", dilation, t).reshape(b, t, 3 * SWA_WIDTH)
        o_g, stats_g = _swa_group(s_g, plain, swapped, g, dilation)
        o_list.append(o_g.reshape(n, SWA_WIDTH))
        stats_list.append(stats_g.reshape(n, LANES))

    out = _merge_out(x2, h, o_dn.reshape(n, DN_WIDTH), o_list, stats_list, wt, zg_blocks,
                     w_branch_dn[layer], w_branch_swa[layer], w_out[layer])
    return out.reshape(b, t, d)
```
